```python
import math
import jax, jax.numpy as jnp
from jax import lax
import numpy as np

D_MODEL = 1024
BATCH = 16
SEQ = 4096
DEPTH = 4

CTX_LEN = 256
GRID_W = 64
EPS = 1e-6

LRU_WIDTH = 1024
LRU_BLOCKS = 8
LRU_BLOCK_W = LRU_WIDTH // LRU_BLOCKS
LRU_CONV = 4
LRU_C = 8.0
ATT_HEADS = 8
ATT_KV_HEADS = 2
HEAD_DIM = 128
ATT_WIDTH = ATT_HEADS * HEAD_DIM
KV_WIDTH = ATT_KV_HEADS * HEAD_DIM
WINDOW = 128
BLOCK = 128
ROPE_BASE = 10000.0
HY_WIDTH = 1024
HY_ORDER = 2
HY_CONV = 3
HY_EMB = 33
HY_FILTER_DIM = 64
HY_DECAY_TARGET = 1e-2
HY_FAST_PCT = 0.3
HY_SLOW_PCT = 1.5
RET_HEADS = 8
RET_DK = 128
RET_DV = 128
RET_WIDTH = RET_HEADS * RET_DV
RET_CHUNK = 128

EVEN_IN = 2 * LRU_WIDTH + ATT_WIDTH + 2 * KV_WIDTH + ATT_WIDTH
EVEN_MIX = LRU_WIDTH + ATT_WIDTH
ODD_IN = (HY_ORDER + 1) * HY_WIDTH + HY_WIDTH + 2 * RET_HEADS * RET_DK + 2 * RET_WIDTH
ODD_MIX = HY_WIDTH + RET_WIDTH
N_EVEN = (DEPTH + 1) // 2
N_ODD = DEPTH // 2

kernel_name = 'hybrid_lru_swa_hyena_retention_dit'

F32 = jnp.float32


def _split(t, sizes):
    return jnp.split(t, np.cumsum(sizes)[:-1].tolist(), axis=-1)


def _flip(t):
    return jnp.flip(t, axis=1)


def _ident(t):
    return t


def rmsnorm(x, g):
    x32 = x.astype(F32)
    y = x32 * lax.rsqrt(jnp.mean(x32 * x32, axis=-1, keepdims=True) + EPS)
    return (y * g.astype(F32)).astype(x.dtype)


def depthwise_conv(x, w, b, left):
    K = w.shape[0]
    L = x.shape[1]
    xp = jnp.pad(x, ((0, 0), (left, K - 1 - left), (0, 0)))
    y = xp[:, 0:L] * w[0]
    for kk in range(1, K):
        y = y + xp[:, kk:kk + L] * w[kk]
    return y + b


def axial_rope_angles(seq):
    n_rows = seq // GRID_W
    row = jnp.repeat(jnp.arange(n_rows, dtype=F32), GRID_W)
    col = jnp.tile(jnp.arange(GRID_W, dtype=F32), n_rows)
    half = HEAD_DIM // 2
    inv = ROPE_BASE ** (-jnp.arange(0, half, 2, dtype=F32) / half)
    return row[:, None] * inv, col[:, None] * inv


def rope1d(x, ang):
    n = ang.shape[-1]
    cos = jnp.cos(ang)[:, None, :]
    sin = jnp.sin(ang)[:, None, :]
    x1, x2 = x[..., :n], x[..., n:]
    return jnp.concatenate([x1 * cos - x2 * sin, x2 * cos + x1 * sin], axis=-1)


def rope_2d(x, ang_r, ang_c):
    half = HEAD_DIM // 2
    y = jnp.concatenate([rope1d(x[..., :half], ang_r), rope1d(x[..., half:], ang_c)], axis=-1)
    return y.astype(x.dtype)


def _lin_combine(e1, e2):
    a1, b1 = e1
    a2, b2 = e2
    return a1 * a2, a2 * b1 + b2


def linear_scan(a, b, h0):
    a_cum, b_cum = lax.associative_scan(_lin_combine, (a, b), axis=1)
    return a_cum * h0[:, None] + b_cum


def blockdiag(x, w):
    Bn, L, W = x.shape
    xb = x.reshape(Bn, L, LRU_BLOCKS, LRU_BLOCK_W)
    return jnp.einsum('blnc,ncd->blnd', xb, w).reshape(Bn, L, W)


def rglru_coeffs(u, wa, ba, wx, bx, lam):
    r = jax.nn.sigmoid(blockdiag(u, wa) + ba)
    i = jax.nn.sigmoid(blockdiag(u, wx) + bx)
    log_a = -LRU_C * r * jax.nn.softplus(-lam)
    a = jnp.exp(log_a)
    b = jnp.sqrt(-jnp.expm1(2.0 * log_a)) * (i * u)
    return a.astype(F32), b.astype(F32)


def rglru_mixer(xa, xac, conv_w, conv_b, wa, ba, wx, bx, lam):
    left = LRU_CONV // 2
    u = depthwise_conv(xa, conv_w, conv_b, left).astype(F32)
    uc = depthwise_conv(xac, conv_w, conv_b, left).astype(F32)
    zero = jnp.zeros((uc.shape[0], LRU_WIDTH), F32)
    y = 0.0
    yc = 0.0
    for d in range(2):
        fl = _flip if d == 1 else _ident
        a, b = rglru_coeffs(fl(uc), wa[d], ba[d], wx[d], bx[d], lam[d])
        hc = linear_scan(a, b, zero)
        a, b = rglru_coeffs(fl(u), wa[d], ba[d], wx[d], bx[d], lam[d])
        hl = linear_scan(a, b, hc[:, -1])
        y = y + fl(hl)
        yc = yc + fl(hc)
    return y.astype(xa.dtype), yc.astype(xa.dtype)


def softmax_with_sink(scores, sink):
    m = sink
    for s in scores:
        m = jnp.maximum(m, jnp.max(s, axis=-1, keepdims=True))
    ps = [jnp.exp(s - m) for s in scores]
    denom = jnp.exp(sink - m)
    for p in ps:
        denom = denom + jnp.sum(p, axis=-1, keepdims=True)
    return [p / denom for p in ps]


def window_attention(q, k, v, kc, vc, sink):
    Bn, S = q.shape[0], q.shape[1]
    nb = S // BLOCK
    G = ATT_HEADS // ATT_KV_HEADS
    scale = HEAD_DIM ** -0.5
    qb = q.reshape(Bn, nb, BLOCK, ATT_KV_HEADS, G, HEAD_DIM)

    def band(t):
        tp = jnp.pad(t, ((0, 0), (BLOCK, BLOCK), (0, 0), (0, 0)))
        tp = tp.reshape(Bn, nb + 2, BLOCK, ATT_KV_HEADS, HEAD_DIM)
        return jnp.concatenate([tp[:, :-2], tp[:, 1:-1], tp[:, 2:]], axis=2)

    kb, vb = band(k), band(v)
    qi = jnp.arange(BLOCK)
    kj = jnp.arange(3 * BLOCK) - BLOCK
    rel = qi[:, None] - kj[None, :]
    kpos = jnp.arange(nb)[:, None] * BLOCK + kj[None, :]
    mask = (jnp.abs(rel) <= WINDOW)[None] & ((kpos >= 0) & (kpos < S))[:, None, :]
    s_loc = jnp.einsum('bnihgd,bnjhd->bnhgij', qb, kb).astype(F32) * scale
    s_loc = jnp.where(mask[None, :, None, None], s_loc, -jnp.inf)
    s_ctx = jnp.einsum('bnihgd,bchd->bnhgic', qb, kc).astype(F32) * scale
    sink_b = sink.reshape(ATT_KV_HEADS, G).astype(F32)[None, None, :, :, None, None]
    p_loc, p_ctx = softmax_with_sink([s_loc, s_ctx], sink_b)
    o = (jnp.einsum('bnhgij,bnjhd->bnihgd', p_loc, vb.astype(F32))
         + jnp.einsum('bnhgic,bchd->bnihgd', p_ctx, vc.astype(F32)))
    return o.reshape(Bn, S, ATT_WIDTH).astype(q.dtype)


def context_attention(qc, kc, vc, sink):
    Bn, Cn = qc.shape[0], qc.shape[1]
    G = ATT_HEADS // ATT_KV_HEADS
    qg = qc.reshape(Bn, Cn, ATT_KV_HEADS, G, HEAD_DIM)
    s = jnp.einsum('bihgd,bjhd->bhgij', qg, kc).astype(F32) * (HEAD_DIM ** -0.5)
    sink_b = sink.reshape(ATT_KV_HEADS, G).astype(F32)[None, :, :, None, None]
    (p,) = softmax_with_sink([s], sink_b)
    o = jnp.einsum('bhgij,bjhd->bihgd', p, vc.astype(F32))
    return o.reshape(Bn, Cn, ATT_WIDTH).astype(qc.dtype)


def hyena_filters(L, w1, b1, w2, b2, w3, freq):
    t = jnp.linspace(0.0, 1.0, L, dtype=F32)[:, None]
    bands = (HY_EMB - 1) // 2
    w = 2.0 * math.pi * jnp.arange(L, dtype=F32)[:, None] / L
    f = jnp.linspace(1e-4, bands - 1, bands, dtype=F32)[None]
    z = jnp.concatenate([t, jnp.cos(f * w), -jnp.sin(f * w)], axis=-1)
    hid = jnp.sin(freq * (z @ w1 + b1))
    hid = jnp.sin(freq * (hid @ w2 + b2))
    filt = (hid @ w3).astype(F32)
    max_decay = math.log(HY_DECAY_TARGET) / HY_FAST_PCT
    min_decay = math.log(HY_DECAY_TARGET) / HY_SLOW_PCT
    deltas = jnp.linspace(min_decay, max_decay, HY_WIDTH, dtype=F32)
    decay = jnp.exp(-t * jnp.abs(deltas))
    return filt.reshape(L, HY_ORDER, 2, HY_WIDTH) * decay[:, None, None, :]


def fft_conv_bidir(u, h_fwd, h_bwd, bias):
    L = u.shape[1]
    kern = jnp.concatenate([h_fwd, jnp.zeros((1, HY_WIDTH), F32), jnp.flip(h_bwd[1:], axis=0)], axis=0)
    kf = jnp.fft.rfft(kern, n=2 * L, axis=0)
    uf = jnp.fft.rfft(u.astype(F32), n=2 * L, axis=1)
    y = jnp.fft.irfft(uf * kf[None], n=2 * L, axis=1)[:, :L]
    return y + u.astype(F32) * bias.astype(F32)


def hyena_mixer(z, conv_w, conv_b, w1, b1, w2, b2, w3, freq, bias):
    L = z.shape[1]
    z = depthwise_conv(z, conv_w, conv_b, HY_CONV // 2)
    parts = jnp.split(z, HY_ORDER + 1, axis=-1)
    filt = hyena_filters(L, w1, b1, w2, b2, w3, freq)
    y = parts[0].astype(F32)
    for o in range(HY_ORDER):
        y = parts[o + 1].astype(F32) * fft_conv_bidir(y, filt[:, o, 0], filt[:, o, 1], bias[o])
    return y.astype(z.dtype)


def retention_dir(q, k, v, log_gamma, state0, include_diag):
    Bn, L, H, dk = q.shape
    dv = v.shape[-1]
    n = L // RET_CHUNK
    idx = jnp.arange(RET_CHUNK, dtype=F32)
    diff = idx[:, None] - idx[None, :]
    mask = (diff >= 0) if include_diag else (diff > 0)
    inner = jnp.where(mask, jnp.exp(jnp.maximum(diff, 0.0)[None] * log_gamma[:, None, None]), 0.0)
    q_dec = jnp.exp((idx[:, None] + 1.0) * log_gamma[None])
    k_dec = jnp.exp((RET_CHUNK - 1.0 - idx[:, None]) * log_gamma[None])
    c_dec = jnp.exp(RET_CHUNK * log_gamma)

    def chunks(t):
        return jnp.moveaxis(t.reshape(Bn, n, RET_CHUNK, H, t.shape[-1]), 1, 0)

    def step(state, qkv):
        qj, kj, vj = qkv
        att = jnp.einsum('bihd,bjhd->bhij', qj, kj) * inner
        o = (jnp.einsum('bhij,bjhe->bihe', att, vj)
             + jnp.einsum('bihd,bhde->bihe', qj, state) * q_dec[None, :, :, None])
        state = state * c_dec[None, :, None, None] + jnp.einsum('bjhd,bjhe->bhde', kj * k_dec[None, :, :, None], vj)
        return state, o

    state, o = lax.scan(step, state0, (chunks(q), chunks(k), chunks(v)))
    return jnp.moveaxis(o, 0, 1).reshape(Bn, L, H, dv), state


def _head_rms(o):
    return o * lax.rsqrt(jnp.mean(o * o, axis=-1, keepdims=True) + EPS)


def retention_mixer(q, k, v, qc, kc, vc, log_gamma):
    Bn, S = q.shape[0], q.shape[1]
    Cn = qc.shape[1]
    kscale = RET_DK ** -0.5
    q, k, v = q.astype(F32), k.astype(F32) * kscale, v.astype(F32)
    qc, kc, vc = qc.astype(F32), kc.astype(F32) * kscale, vc.astype(F32)
    zero = jnp.zeros((Bn, RET_HEADS, RET_DK, RET_DV), F32)
    o = 0.0
    oc = 0.0
    for d in range(2):
        fl = _flip if d == 1 else _ident
        o_c, s_c = retention_dir(fl(qc), fl(kc), fl(vc), log_gamma[d], zero, d == 0)
        o_l, _ = retention_dir(fl(q), fl(k), fl(v), log_gamma[d], s_c, d == 0)
        o = o + fl(o_l)
        oc = oc + fl(o_c)
    return _head_rms(o).reshape(Bn, S, RET_WIDTH), _head_rms(oc).reshape(Bn, Cn, RET_WIDTH)


def even_mixer(h, hc, w_in, w_out, conv_w, conv_b, wa, ba, wx, bx, lam, sink, ang_r, ang_c, need_ctx):
    Bn, S = h.shape[0], h.shape[1]
    Cn = hc.shape[1]
    sizes = (LRU_WIDTH, LRU_WIDTH, ATT_WIDTH, KV_WIDTH, KV_WIDTH, ATT_WIDTH)
    xa, ga, q, k, v, gb = _split(h @ w_in, sizes)
    xac, gac, qc, kc, vc, gbc = _split(hc @ w_in, sizes)
    ya, yac = rglru_mixer(xa, xac, conv_w, conv_b, wa, ba, wx, bx, lam)
    q = rope_2d(q.reshape(Bn, S, ATT_HEADS, HEAD_DIM), ang_r, ang_c)
    k = rope_2d(k.reshape(Bn, S, ATT_KV_HEADS, HEAD_DIM), ang_r, ang_c)
    v = v.reshape(Bn, S, ATT_KV_HEADS, HEAD_DIM)
    kc = kc.reshape(Bn, Cn, ATT_KV_HEADS, HEAD_DIM)
    vc = vc.reshape(Bn, Cn, ATT_KV_HEADS, HEAD_DIM)
    yb = window_attention(q, k, v, kc, vc, sink)
    y = jnp.concatenate([ya * jax.nn.silu(ga), yb * jax.nn.silu(gb)], axis=-1) @ w_out
    if not need_ctx:
        return y, None
    ybc = context_attention(qc.reshape(Bn, Cn, ATT_HEADS, HEAD_DIM), kc, vc, sink)
    yc = jnp.concatenate([yac * jax.nn.silu(gac), ybc * jax.nn.silu(gbc)], axis=-1) @ w_out
    return y, yc


def odd_mixer(h, hc, w_in, w_out, hy_conv_w, hy_conv_b, hy_w1, hy_b1, hy_w2, hy_b2, hy_w3, hy_freq,
              hy_bias, decay_logit, need_ctx):
    Bn, S = h.shape[0], h.shape[1]
    Cn = hc.shape[1]
    sizes = ((HY_ORDER + 1) * HY_WIDTH, HY_WIDTH, RET_HEADS * RET_DK, RET_HEADS * RET_DK, RET_WIDTH, RET_WIDTH)
    z, gh, q, k, v, gd = _split(h @ w_in, sizes)
    zc, ghc, qc, kc, vc, gdc = _split(hc @ w_in, sizes)
    hyp = (hy_conv_w, hy_conv_b, hy_w1, hy_b1, hy_w2, hy_b2, hy_w3, hy_freq, hy_bias)
    yh = hyena_mixer(z, *hyp)
    log_gamma = -jax.nn.softplus(-decay_logit.astype(F32))
    yd, ydc = retention_mixer(
        q.reshape(Bn, S, RET_HEADS, RET_DK), k.reshape(Bn, S, RET_HEADS, RET_DK),
        v.reshape(Bn, S, RET_HEADS, RET_DV), qc.reshape(Bn, Cn, RET_HEADS, RET_DK),
        kc.reshape(Bn, Cn, RET_HEADS, RET_DK), vc.reshape(Bn, Cn, RET_HEADS, RET_DV), log_gamma)
    yd = yd.astype(h.dtype)
    y = jnp.concatenate([yh * jax.nn.silu(gh), yd * jax.nn.silu(gd)], axis=-1) @ w_out
    if not need_ctx:
        return y, None
    yhc = hyena_mixer(zc, *hyp)
    yc = jnp.concatenate([yhc * jax.nn.silu(ghc), ydc.astype(h.dtype) * jax.nn.silu(gdc)], axis=-1) @ w_out
    return y, yc


def setup_inputs(seed: int = 0) -> dict:
    key = jax.random.key(seed)
    ks = iter(jax.random.split(key, 40))

    def nrm(shape, s):
        return jax.random.normal(next(ks), shape, F32) * s

    NE, NO = N_EVEN, N_ODD
    x = nrm((BATCH, SEQ, D_MODEL), 1.0)
    c = nrm((BATCH, D_MODEL), 1.0)
    ctx = nrm((BATCH, CTX_LEN, D_MODEL), 1.0)
    c_ctx = nrm((D_MODEL,), 1.0)
    mod_w = nrm((DEPTH, D_MODEL, 3 * D_MODEL), 0.5 * D_MODEL ** -0.5)
    mod_b = nrm((DEPTH, 3 * D_MODEL), 0.01)
    norm_pre = 1.0 + nrm((DEPTH, D_MODEL), 0.05)
    norm_post = 1.0 + nrm((DEPTH, D_MODEL), 0.05)
    ev_w_in = nrm((NE, D_MODEL, EVEN_IN), D_MODEL ** -0.5)
    ev_w_out = nrm((NE, EVEN_MIX, D_MODEL), EVEN_MIX ** -0.5)
    lru_conv_w = nrm((NE, LRU_CONV, LRU_WIDTH), LRU_CONV ** -0.5)
    lru_conv_b = nrm((NE, LRU_WIDTH), 0.01)
    lru_wa = nrm((NE, 2, LRU_BLOCKS, LRU_BLOCK_W, LRU_BLOCK_W), LRU_BLOCK_W ** -0.5)
    lru_ba = nrm((NE, 2, LRU_WIDTH), 0.01)
    lru_wx = nrm((NE, 2, LRU_BLOCKS, LRU_BLOCK_W, LRU_BLOCK_W), LRU_BLOCK_W ** -0.5)
    lru_bx = nrm((NE, 2, LRU_WIDTH), 0.01)
    a0 = jax.random.uniform(next(ks), (NE, 2, LRU_WIDTH), F32, 0.9, 0.999)
    lru_lambda = jnp.log(a0) - jnp.log1p(-a0)
    attn_sink = nrm((NE, ATT_HEADS), 1.0)
    od_w_in = nrm((NO, D_MODEL, ODD_IN), D_MODEL ** -0.5)
    od_w_out = nrm((NO, ODD_MIX, D_MODEL), ODD_MIX ** -0.5)
    hy_conv_w = nrm((NO, HY_CONV, (HY_ORDER + 1) * HY_WIDTH), HY_CONV ** -0.5)
    hy_conv_b = nrm((NO, (HY_ORDER + 1) * HY_WIDTH), 0.01)
    hy_w1 = nrm((NO, HY_EMB, HY_FILTER_DIM), HY_EMB ** -0.5)
    hy_b1 = nrm((NO, HY_FILTER_DIM), 0.1)
    hy_w2 = nrm((NO, HY_FILTER_DIM, HY_FILTER_DIM), HY_FILTER_DIM ** -0.5)
    hy_b2 = nrm((NO, HY_FILTER_DIM), 0.1)
    hy_w3 = nrm((NO, HY_FILTER_DIM, HY_ORDER * 2 * HY_WIDTH), 0.1 * HY_FILTER_DIM ** -0.5)
    hy_freq = 1.0 + nrm((NO, HY_FILTER_DIM), 0.1)
    hy_bias = nrm((NO, HY_ORDER, HY_WIDTH), 1.0)
    base_logit = jnp.log(2.0 ** (5.0 + jnp.arange(RET_HEADS, dtype=F32)) - 1.0)
    ret_decay_logit = base_logit[None, None] + nrm((NO, 2, RET_HEADS), 0.1)
    return {'x': x, 'c': c, 'ctx': ctx, 'c_ctx': c_ctx, 'mod_w': mod_w, 'mod_b': mod_b,
            'norm_pre': norm_pre, 'norm_post': norm_post, 'ev_w_in': ev_w_in, 'ev_w_out': ev_w_out,
            'lru_conv_w': lru_conv_w, 'lru_conv_b': lru_conv_b, 'lru_wa': lru_wa, 'lru_ba': lru_ba,
            'lru_wx': lru_wx, 'lru_bx': lru_bx, 'lru_lambda': lru_lambda, 'attn_sink': attn_sink,
            'od_w_in': od_w_in, 'od_w_out': od_w_out, 'hy_conv_w': hy_conv_w, 'hy_conv_b': hy_conv_b,
            'hy_w1': hy_w1, 'hy_b1': hy_b1, 'hy_w2': hy_w2, 'hy_b2': hy_b2, 'hy_w3': hy_w3,
            'hy_freq': hy_freq, 'hy_bias': hy_bias, 'ret_decay_logit': ret_decay_logit}


def reference(x, c, ctx, c_ctx, mod_w, mod_b, norm_pre, norm_post, ev_w_in, ev_w_out, lru_conv_w,
              lru_conv_b, lru_wa, lru_ba, lru_wx, lru_bx, lru_lambda, attn_sink, od_w_in, od_w_out,
              hy_conv_w, hy_conv_b, hy_w1, hy_b1, hy_w2, hy_b2, hy_w3, hy_freq, hy_bias, ret_decay_logit):
    S = x.shape[1]
    ang_r, ang_c = axial_rope_angles(S)
    sc = jax.nn.silu(c)
    scc = jax.nn.silu(c_ctx)
    for l in range(DEPTH):
        need_ctx = l < DEPTH - 1
        shift, scale, gate = jnp.split(sc @ mod_w[l] + mod_b[l], 3, axis=-1)
        shift_c, scale_c, gate_c = jnp.split(scc @ mod_w[l] + mod_b[l], 3, axis=-1)
        h = rmsnorm(x, norm_pre[l]) * (1.0 + scale[:, None]) + shift[:, None]
        hc = rmsnorm(ctx, norm_pre[l]) * (1.0 + scale_c) + shift_c
        if l % 2 == 0:
            e = l // 2
            y, yc = even_mixer(h, hc, ev_w_in[e], ev_w_out[e], lru_conv_w[e], lru_conv_b[e], lru_wa[e],
                               lru_ba[e], lru_wx[e], lru_bx[e], lru_lambda[e], attn_sink[e], ang_r, ang_c,
                               need_ctx)
        else:
            o = l // 2
            y, yc = odd_mixer(h, hc, od_w_in[o], od_w_out[o], hy_conv_w[o], hy_conv_b[o], hy_w1[o],
                              hy_b1[o], hy_w2[o], hy_b2[o], hy_w3[o], hy_freq[o], hy_bias[o],
                              ret_decay_logit[o], need_ctx)
        x = x + gate[:, None] * rmsnorm(y, norm_post[l])
        if need_ctx:
            ctx = ctx + gate_c * rmsnorm(yc, norm_post[l])
    return x
```

```python
import functools
import math

import numpy as np
import jax
import jax.numpy as jnp
from jax import lax
from jax.experimental import pallas as pl
from jax.experimental.pallas import tpu as pltpu

F32 = jnp.float32
BF16 = jnp.bfloat16
HIGHEST = lax.Precision.HIGHEST

EPS = 1e-6
GRID_W = 64
LANES = 128
LRU_BLOCK_W = 128
LRU_CONV = 4
LRU_C = 8.0
LRU_CHUNK = 64
ATT_HEADS = 8
ATT_KV_HEADS = 2
HEAD_DIM = 128
BLOCK = 128
ROPE_BASE = 10000.0
HY_ORDER = 2
HY_EMB = 33
HY_EMB_PAD = 40
HY_DECAY_TARGET = 1e-2
HY_FAST_PCT = 0.3
HY_SLOW_PCT = 1.5
RET_HEADS = 8
RET_DK = 128
RET_CHUNK = 128

VMEM_LIMIT = 56 * 1024 * 1024
NEG = -1e30


def _params(sem, vmem=VMEM_LIMIT):
    return pltpu.CompilerParams(dimension_semantics=sem, vmem_limit_bytes=vmem)


def _sigmoid(v):
    return 0.5 * (jnp.tanh(0.5 * v) + 1.0)


def _silu(v):
    return v * _sigmoid(v)


def _softplus(v):
    return jnp.maximum(v, 0.0) + jnp.log(1.0 + jnp.exp(-jnp.abs(v)))


def _dot(a, b, **kw):
    return jnp.dot(a, b, preferred_element_type=F32, **kw)


def _dot_nt(a, b):
    return lax.dot_general(a, b, (((1,), (1,)), ((), ())), preferred_element_type=F32)


def _dot_tn(a, b):
    return lax.dot_general(a, b, (((0,), (0,)), ((), ())), preferred_element_type=F32)


def _mod_kernel(c_ref, w_ref, b_ref, o_ref):
    s = _silu(c_ref[...])
    o_ref[0] = _dot(s, w_ref[0], precision=HIGHEST) + b_ref[0]


def _modulation(crows, mod_w, mod_b):
    depth, d, n3 = mod_w.shape
    r = crows.shape[0]
    tn = 1024
    return pl.pallas_call(
        _mod_kernel,
        grid=(depth, n3 // tn),
        in_specs=[pl.BlockSpec((r, d), lambda l, j: (0, 0)),
                  pl.BlockSpec((1, d, tn), lambda l, j: (l, 0, j)),
                  pl.BlockSpec((1, 1, tn), lambda l, j: (l, 0, j))],
        out_specs=pl.BlockSpec((1, r, tn), lambda l, j: (l, 0, j)),
        out_shape=jax.ShapeDtypeStruct((depth, r, n3), F32),
        compiler_params=_params(("parallel", "parallel")),
        name="modulation",
    )(crows, mod_w, mod_b.reshape(depth, 1, n3))


def _inproj_kernel(x_ref, g_ref, sc_ref, sh_ref, w_ref, o_ref, h_ref):
    @pl.when(pl.program_id(1) == 0)
    def _():
        x = x_ref[...]
        y = x * lax.rsqrt(jnp.mean(x * x, axis=-1, keepdims=True) + EPS) * g_ref[...]
        h_ref[...] = (y * (1.0 + sc_ref[0]) + sh_ref[0]).astype(BF16)

    o_ref[...] = _dot(h_ref[...], w_ref[...]).astype(o_ref.dtype)


def _in_proj(x2, g, scale, shift, rows_per_group, w, out_dtype, tm, tn):
    m, d = x2.shape
    n = w.shape[1]
    tpg = rows_per_group // tm
    return pl.pallas_call(
        _inproj_kernel,
        grid=(m // tm, n // tn),
        in_specs=[pl.BlockSpec((tm, d), lambda i, j: (i, 0)),
                  pl.BlockSpec((1, d), lambda i, j: (0, 0)),
                  pl.BlockSpec((1, 1, d), lambda i, j: (i // tpg, 0, 0)),
                  pl.BlockSpec((1, 1, d), lambda i, j: (i // tpg, 0, 0)),
                  pl.BlockSpec((d, tn), lambda i, j: (0, j))],
        out_specs=pl.BlockSpec((tm, tn), lambda i, j: (i, j)),
        out_shape=jax.ShapeDtypeStruct((m, n), out_dtype),
        scratch_shapes=[pltpu.VMEM((tm, d), BF16)],
        compiler_params=_params(("parallel", "arbitrary")),
        name="in_proj",
    )(x2, g, scale, shift, w)


def _outproj_kernel(a_ref, ga_ref, b_ref, w_ref, x_ref, g_ref, gate_ref, o_ref):
    wa = w_ref.shape[0] // 2
    a = (a_ref[...].astype(F32) * _silu(ga_ref[...].astype(F32))).astype(BF16)
    y = _dot(a, w_ref[0:wa, :]) + _dot(b_ref[...], w_ref[wa:, :])
    yn = y * lax.rsqrt(jnp.mean(y * y, axis=-1, keepdims=True) + EPS) * g_ref[...]
    o_ref[...] = x_ref[...] + gate_ref[0] * yn


def _out_proj(a, a_col, ga, ga_col, b, b_col, w, x2, g, gate, rows_per_group, tm):
    m, d = x2.shape
    wa = w.shape[0] // 2
    tpg = rows_per_group // tm
    return pl.pallas_call(
        _outproj_kernel,
        grid=(m // tm,),
        in_specs=[pl.BlockSpec((tm, wa), lambda i: (i, a_col)),
                  pl.BlockSpec((tm, wa), lambda i: (i, ga_col)),
                  pl.BlockSpec((tm, wa), lambda i: (i, b_col)),
                  pl.BlockSpec(w.shape, lambda i: (0, 0)),
                  pl.BlockSpec((tm, d), lambda i: (i, 0)),
                  pl.BlockSpec((1, d), lambda i: (0, 0)),
                  pl.BlockSpec((1, 1, d), lambda i: (i // tpg, 0, 0))],
        out_specs=pl.BlockSpec((tm, d), lambda i: (i, 0)),
        out_shape=jax.ShapeDtypeStruct((m, d), F32),
        compiler_params=_params(("parallel",)),
        name="out_proj",
    )(a, ga, b, w, x2, g, gate)


def _lru_kernel(xa_ref, xac_ref, cw_ref, cb_ref, wg_ref, bg_ref, lam_ref, y_ref, yc_ref,
                xp_ref, xcp_ref, hf_ref, ab_ref, bb_ref, hfc_ref, abc_ref, bbc_ref):
    s_len = xa_ref.shape[0]
    c_len = xac_ref.shape[0]
    tc = LRU_CHUNK
    zeros8 = jnp.zeros((8, LANES), F32)
    xp_ref[0:8, :] = zeros8
    xp_ref[s_len + 8:s_len + 16, :] = zeros8
    xp_ref[8:s_len + 8, :] = xa_ref[...]
    xcp_ref[0:8, :] = zeros8
    xcp_ref[c_len + 8:c_len + 16, :] = zeros8
    xcp_ref[8:c_len + 8, :] = xac_ref[...]

    c8 = -LRU_C * _softplus(-lam_ref[...])
    row = lax.broadcasted_iota(jnp.int32, (tc, LANES), 0)
    left = LRU_CONV // 2

    def coeffs(src_ref, t0):
        u = cb_ref[...] + cw_ref[0:1, :] * src_ref[pl.ds(t0 + 8 - left, tc), :]
        for k in range(1, LRU_CONV):
            u = u + cw_ref[k:k + 1, :] * src_ref[pl.ds(t0 + 8 - left + k, tc), :]
        g = _dot(u.astype(BF16), wg_ref[0]) + bg_ref[0]
        out = []
        for d in range(2):
            r = _sigmoid(g[:, (2 * d) * LANES:(2 * d + 1) * LANES])
            i = _sigmoid(g[:, (2 * d + 1) * LANES:(2 * d + 2) * LANES])
            a = jnp.exp(c8[d:d + 1, :] * r)
            out.append((a, jnp.sqrt(1.0 - a * a) * (i * u)))
        return out

    def scan_fwd(a, b, carry):
        s = 1
        while s < tc:
            m = row >= s
            b = jnp.where(m, a * pltpu.roll(b, s, 0) + b, b)
            a = jnp.where(m, a * pltpu.roll(a, s, 0), a)
            s *= 2
        h = a * carry + b
        return h, h[tc - 1:tc, :]

    def scan_bwd(a, b, carry):
        s = 1
        while s < tc:
            m = row < tc - s
            b = jnp.where(m, a * pltpu.roll(b, tc - s, 0) + b, b)
            a = jnp.where(m, a * pltpu.roll(a, tc - s, 0), a)
            s *= 2
        h = a * carry + b
        return h, h[0:1, :]

    def fwd_pass(src_ref, n, hf_r, ab_r, bb_r, carry):
        def body(c, carry):
            t0 = pl.multiple_of(c * tc, tc)
            (af, bf), (ab, bb) = coeffs(src_ref, t0)
            h, carry = scan_fwd(af, bf, carry)
            hf_r[pl.ds(t0, tc), :] = h
            ab_r[pl.ds(t0, tc), :] = ab
            bb_r[pl.ds(t0, tc), :] = bb
            return carry
        return lax.fori_loop(0, n // tc, body, carry)

    def bwd_pass(n, hf_r, ab_r, bb_r, out_r, carry):
        def body(c, carry):
            t0 = pl.multiple_of((n // tc - 1 - c) * tc, tc)
            h, carry = scan_bwd(ab_r[pl.ds(t0, tc), :], bb_r[pl.ds(t0, tc), :], carry)
            out_r[pl.ds(t0, tc), :] = (hf_r[pl.ds(t0, tc), :] + h).astype(out_r.dtype)
            return carry
        return lax.fori_loop(0, n // tc, body, carry)

    zero = jnp.zeros((1, LANES), F32)
    carry = fwd_pass(xcp_ref, c_len, hfc_ref, abc_ref, bbc_ref, zero)
    fwd_pass(xp_ref, s_len, hf_ref, ab_ref, bb_ref, carry)
    carry = bwd_pass(c_len, hfc_ref, abc_ref, bbc_ref, yc_ref, zero)
    bwd_pass(s_len, hf_ref, ab_ref, bb_ref, y_ref, carry)


def _lru(xa2, xac2, bsz, conv_w, conv_b, wg, bg, lam):
    w = xa2.shape[1]
    s_len = xa2.shape[0] // bsz
    c_len = xac2.shape[0] // bsz
    nblk = w // LANES
    seq = lambda n: pl.BlockSpec((n, LANES), lambda b, j: (b, j))
    return pl.pallas_call(
        _lru_kernel,
        grid=(bsz, nblk),
        in_specs=[seq(s_len), seq(c_len),
                  pl.BlockSpec((LRU_CONV, LANES), lambda b, j: (0, j)),
                  pl.BlockSpec((1, LANES), lambda b, j: (0, j)),
                  pl.BlockSpec((1, LANES, 4 * LANES), lambda b, j: (j, 0, 0)),
                  pl.BlockSpec((1, 1, 4 * LANES), lambda b, j: (j, 0, 0)),
                  pl.BlockSpec((2, LANES), lambda b, j: (0, j))],
        out_specs=[seq(s_len), seq(c_len)],
        out_shape=[jax.ShapeDtypeStruct(xa2.shape, BF16), jax.ShapeDtypeStruct(xac2.shape, BF16)],
        scratch_shapes=[pltpu.VMEM((s_len + 16, LANES), F32), pltpu.VMEM((c_len + 16, LANES), F32),
                        pltpu.VMEM((s_len, LANES), F32), pltpu.VMEM((s_len, LANES), F32),
                        pltpu.VMEM((s_len, LANES), F32), pltpu.VMEM((c_len, LANES), F32),
                        pltpu.VMEM((c_len, LANES), F32), pltpu.VMEM((c_len, LANES), F32)],
        compiler_params=_params(("parallel", "parallel")),
        name="rglru",
    )(xa2, xac2, conv_w, conv_b, wg, bg, lam)


def _rope(v, cos, sin, lane):
    swapped = jnp.where((lane & 63) < 32, pltpu.roll(v, LANES - 32, 1), pltpu.roll(v, 32, 1))
    return v * cos + swapped * sin


def _attn_kernel(sink_ref, q_ref, gb_ref, k_ref, v_ref, kc_ref, vc_ref, cos_ref, sin_ref, o_ref):
    qb = pl.program_id(1)
    nb = pl.num_programs(1)
    group = ATT_HEADS // ATT_KV_HEADS
    scale = HEAD_DIM ** -0.5
    lane = lax.broadcasted_iota(jnp.int32, (BLOCK, LANES), 1)

    def blk(i):
        return pl.ds(pl.multiple_of(i * BLOCK, BLOCK), BLOCK)

    ip = jnp.maximum(qb - 1, 0)
    inx = jnp.minimum(qb + 1, nb - 1)
    cos_o, sin_o = cos_ref[blk(qb), :], sin_ref[blk(qb), :]
    cos_p, sin_p = cos_ref[blk(ip), :], sin_ref[blk(ip), :]
    cos_n, sin_n = cos_ref[blk(inx), :], sin_ref[blk(inx), :]

    rows = group * BLOCK
    ncol = 3 * BLOCK + kc_ref.shape[0]
    ri = lax.broadcasted_iota(jnp.int32, (rows, ncol), 0) & (BLOCK - 1)
    ci = lax.broadcasted_iota(jnp.int32, (rows, ncol), 1)
    has_prev = jnp.where(qb > 0, 0.0, NEG)
    has_next = jnp.where(qb < nb - 1, 0.0, NEG)
    b_prev = jnp.where(ci >= ri, has_prev, NEG)
    b_next = jnp.where(ci - 2 * BLOCK <= ri, has_next, NEG)
    bias = jnp.where(ci < BLOCK, b_prev,
                     jnp.where(ci < 2 * BLOCK, 0.0, jnp.where(ci < 3 * BLOCK, b_next, 0.0)))

    for h in range(ATT_KV_HEADS):
        ksl = slice(h * HEAD_DIM, (h + 1) * HEAD_DIM)
        kp = _rope(k_ref[blk(ip), ksl].astype(F32), cos_p, sin_p, lane)
        ko = _rope(k_ref[blk(qb), ksl].astype(F32), cos_o, sin_o, lane)
        kn = _rope(k_ref[blk(inx), ksl].astype(F32), cos_n, sin_n, lane)
        kcat = jnp.concatenate([kp.astype(BF16), ko.astype(BF16), kn.astype(BF16), kc_ref[:, ksl]], axis=0)
        vcat = jnp.concatenate([v_ref[blk(ip), ksl], v_ref[blk(qb), ksl], v_ref[blk(inx), ksl],
                                vc_ref[:, ksl]], axis=0)
        qs, sinks = [], []
        for g in range(group):
            hh = h * group + g
            qh = q_ref[:, hh * HEAD_DIM:(hh + 1) * HEAD_DIM].astype(F32)
            qs.append((_rope(qh, cos_o, sin_o, lane) * scale).astype(BF16))
            sinks.append(jnp.full((BLOCK, 1), sink_ref[hh], F32))
        q4 = jnp.concatenate(qs, axis=0)
        sk = jnp.concatenate(sinks, axis=0)
        s = _dot_nt(q4, kcat) + bias
        m = jnp.maximum(jnp.max(s, axis=-1, keepdims=True), sk)
        p = jnp.exp(s - m)
        denom = jnp.exp(sk - m) + jnp.sum(p, axis=-1, keepdims=True)
        o = _dot(p.astype(BF16), vcat) / denom
        for g in range(group):
            hh = h * group + g
            hs = slice(hh * HEAD_DIM, (hh + 1) * HEAD_DIM)
            gate = _silu(gb_ref[:, hs].astype(F32))
            o_ref[:, hs] = (o[g * BLOCK:(g + 1) * BLOCK, :] * gate).astype(o_ref.dtype)


def _attention(sink, proj, projc, bsz, cols, cos_t, sin_t):
    s_len = proj.shape[0] // bsz
    c_len = projc.shape[0] // bsz
    nb = s_len // BLOCK
    aw = ATT_HEADS * HEAD_DIM
    kw = ATT_KV_HEADS * HEAD_DIM
    return pl.pallas_call(
        _attn_kernel,
        grid=(bsz, nb),
        in_specs=[pl.BlockSpec(memory_space=pltpu.SMEM),
                  pl.BlockSpec((BLOCK, aw), lambda b, i: (b * nb + i, cols["q"])),
                  pl.BlockSpec((BLOCK, aw), lambda b, i: (b * nb + i, cols["gb"])),
                  pl.BlockSpec((s_len, kw), lambda b, i: (b, cols["k"])),
                  pl.BlockSpec((s_len, kw), lambda b, i: (b, cols["v"])),
                  pl.BlockSpec((c_len, kw), lambda b, i: (b, cols["k"])),
                  pl.BlockSpec((c_len, kw), lambda b, i: (b, cols["v"])),
                  pl.BlockSpec((s_len, LANES), lambda b, i: (0, 0)),
                  pl.BlockSpec((s_len, LANES), lambda b, i: (0, 0))],
        out_specs=pl.BlockSpec((BLOCK, aw), lambda b, i: (b * nb + i, 0)),
        out_shape=jax.ShapeDtypeStruct((proj.shape[0], aw), BF16),
        compiler_params=_params(("parallel", "arbitrary")),
        name="window_attention",
    )(sink, proj, proj, proj, proj, projc, projc, cos_t, sin_t)


def _ctx_attn_kernel(sink_ref, q_ref, gb_ref, k_ref, v_ref, o_ref):
    group = ATT_HEADS // ATT_KV_HEADS
    scale = HEAD_DIM ** -0.5
    n = q_ref.shape[0]
    for h in range(ATT_KV_HEADS):
        ksl = slice(h * HEAD_DIM, (h + 1) * HEAD_DIM)
        qs, sinks = [], []
        for g in range(group):
            hh = h * group + g
            qs.append((q_ref[:, hh * HEAD_DIM:(hh + 1) * HEAD_DIM].astype(F32) * scale).astype(BF16))
            sinks.append(jnp.full((n, 1), sink_ref[hh], F32))
        q4 = jnp.concatenate(qs, axis=0)
        sk = jnp.concatenate(sinks, axis=0)
        s = _dot_nt(q4, k_ref[:, ksl])
        m = jnp.maximum(jnp.max(s, axis=-1, keepdims=True), sk)
        p = jnp.exp(s - m)
        denom = jnp.exp(sk - m) + jnp.sum(p, axis=-1, keepdims=True)
        o = _dot(p.astype(BF16), v_ref[:, ksl]) / denom
        for g in range(group):
            hh = h * group + g
            hs = slice(hh * HEAD_DIM, (hh + 1) * HEAD_DIM)
            o_ref[:, hs] = (o[g * n:(g + 1) * n, :] * _silu(gb_ref[:, hs].astype(F32))).astype(o_ref.dtype)


def _ctx_attention(sink, projc, bsz, cols):
    c_len = projc.shape[0] // bsz
    aw = ATT_HEADS * HEAD_DIM
    kw = ATT_KV_HEADS * HEAD_DIM
    return pl.pallas_call(
        _ctx_attn_kernel,
        grid=(bsz,),
        in_specs=[pl.BlockSpec(memory_space=pltpu.SMEM),
                  pl.BlockSpec((c_len, aw), lambda b: (b, cols["q"])),
                  pl.BlockSpec((c_len, aw), lambda b: (b, cols["gb"])),
                  pl.BlockSpec((c_len, kw), lambda b: (b, cols["k"])),
                  pl.BlockSpec((c_len, kw), lambda b: (b, cols["v"]))],
        out_specs=pl.BlockSpec((c_len, aw), lambda b: (b, 0)),
        out_shape=jax.ShapeDtypeStruct((projc.shape[0], aw), BF16),
        compiler_params=_params(("parallel",)),
        name="context_attention",
    )(sink, projc, projc, projc, projc)


def _ret_kernel(dl_ref, q_ref, k_ref, v_ref, gd_ref, qc_ref, kc_ref, vc_ref, gdc_ref, o_ref, oc_ref,
                acc_ref, accc_ref):
    c = RET_CHUNK
    s_len = q_ref.shape[0]
    c_len = qc_ref.shape[0]
    lg = -_softplus(-dl_ref[0])
    lgf, lgb = lg[0:1, :], lg[1:2, :]
    ri = lax.broadcasted_iota(jnp.int32, (c, c), 0)
    ci = lax.broadcasted_iota(jnp.int32, (c, c), 1)
    diff = (ri - ci).astype(F32)
    dmat = jnp.where(ri >= ci, jnp.exp(jnp.maximum(diff, 0.0) * lgf), jnp.exp(jnp.maximum(-diff, 0.0) * lgb))
    idx = ri.astype(F32)
    qdec_f = jnp.exp((idx + 1.0) * lgf)
    kdec_f = jnp.exp((c - 1.0 - idx) * lgf)
    qdec_b = jnp.exp((c - idx) * lgb)
    kdec_b = jnp.exp(idx * lgb)
    cdec_f = jnp.exp(c * lgf)
    cdec_b = jnp.exp(c * lgb)

    def sweep_f(qr, kr, vr, acc, n, state):
        def body(j, state):
            sl = pl.ds(pl.multiple_of(j * c, c), c)
            q, k, v = qr[sl, :], kr[sl, :], vr[sl, :]
            att = (_dot_nt(q, k) * dmat).astype(BF16)
            qd = (q.astype(F32) * qdec_f).astype(BF16)
            acc[sl, :] = _dot(att, v) + _dot(qd, state.astype(BF16))
            kd = (k.astype(F32) * kdec_f).astype(BF16)
            return state * cdec_f + _dot_tn(kd, v)
        return lax.fori_loop(0, n // c, body, state)

    def sweep_b(qr, kr, vr, gr, acc, outr, n, state):
        def body(jj, state):
            sl = pl.ds(pl.multiple_of((n // c - 1 - jj) * c, c), c)
            q, k, v = qr[sl, :], kr[sl, :], vr[sl, :]
            qd = (q.astype(F32) * qdec_b).astype(BF16)
            o = (acc[sl, :] + _dot(qd, state.astype(BF16))) * (RET_DK ** -0.5)
            o = o * lax.rsqrt(jnp.mean(o * o, axis=-1, keepdims=True) + EPS)
            outr[sl, :] = (o * _silu(gr[sl, :].astype(F32))).astype(outr.dtype)
            kd = (k.astype(F32) * kdec_b).astype(BF16)
            return state * cdec_b + _dot_tn(kd, v)
        return lax.fori_loop(0, n // c, body, state)

    zero = jnp.zeros((RET_DK, LANES), F32)
    st = sweep_f(qc_ref, kc_ref, vc_ref, accc_ref, c_len, zero)
    sweep_f(q_ref, k_ref, v_ref, acc_ref, s_len, st)
    st = sweep_b(qc_ref, kc_ref, vc_ref, gdc_ref, accc_ref, oc_ref, c_len, zero)
    sweep_b(q_ref, k_ref, v_ref, gd_ref, acc_ref, o_ref, s_len, st)


def _retention(dl, proj, projc, bsz, cols):
    s_len = proj.shape[0] // bsz
    c_len = projc.shape[0] // bsz
    hb = lambda n, col: pl.BlockSpec((n, LANES), lambda b, h: (b, col + h))
    return pl.pallas_call(
        _ret_kernel,
        grid=(bsz, RET_HEADS),
        in_specs=[pl.BlockSpec((1, 2, LANES), lambda b, h: (h, 0, 0)),
                  hb(s_len, cols["q"]), hb(s_len, cols["k"]), hb(s_len, cols["v"]), hb(s_len, cols["gd"]),
                  hb(c_len, cols["q"]), hb(c_len, cols["k"]), hb(c_len, cols["v"]), hb(c_len, cols["gd"])],
        out_specs=[pl.BlockSpec((s_len, LANES), lambda b, h: (b, h)),
                   pl.BlockSpec((c_len, LANES), lambda b, h: (b, h))],
        out_shape=[jax.ShapeDtypeStruct((proj.shape[0], RET_HEADS * LANES), BF16),
                   jax.ShapeDtypeStruct((projc.shape[0], RET_HEADS * LANES), BF16)],
        scratch_shapes=[pltpu.VMEM((s_len, LANES), F32), pltpu.VMEM((c_len, LANES), F32)],
        compiler_params=_params(("parallel", "parallel")),
        name="retention",
    )(dl, proj, proj, proj, proj, projc, projc, projc, projc)


def _hy_hid_kernel(z_ref, w1_ref, b1_ref, w2_ref, b2_ref, f_ref, o_ref):
    f = f_ref[...]
    h = jnp.sin(f * (_dot(z_ref[...], w1_ref[...], precision=HIGHEST) + b1_ref[...]))
    o_ref[...] = jnp.sin(f * (_dot(h, w2_ref[...], precision=HIGHEST) + b2_ref[...]))


def _hy_hidden(zfull, w1p, b1, w2, b2, freq):
    n, e = zfull.shape
    fd = w2.shape[0]
    tr = min(n, 1024)
    full = lambda shp: pl.BlockSpec(shp, lambda i: (0, 0))
    return pl.pallas_call(
        _hy_hid_kernel,
        grid=(n // tr,),
        in_specs=[pl.BlockSpec((tr, e), lambda i: (i, 0)), full((e, fd)), full((1, fd)), full((fd, fd)),
                  full((1, fd)), full((1, fd))],
        out_specs=pl.BlockSpec((tr, fd), lambda i: (i, 0)),
        out_shape=jax.ShapeDtypeStruct((n, fd), F32),
        compiler_params=_params(("parallel",)),
        name="hyena_filter_mlp",
    )(zfull, w1p, b1, w2, b2, freq)


def _hy_filt_kernel(hid_ref, w3f_ref, w3b_ref, tn_ref, dl_ref, o_ref):
    half = hid_ref.shape[0] // 2
    decay = jnp.exp(-tn_ref[...] * jnp.abs(dl_ref[...]))
    top = _dot(hid_ref[0:half, :], w3f_ref[...], precision=HIGHEST)
    bot = _dot(hid_ref[half:, :], w3b_ref[...], precision=HIGHEST)
    row = lax.broadcasted_iota(jnp.int32, bot.shape, 0)
    bot = jnp.where(row == 0, 0.0, bot)
    o_ref[0, 0:half, :] = top * decay[0:half, :]
    o_ref[0, half:, :] = bot * decay[half:, :]


def _hy_filters(hid, w3, tn_full, deltas):
    n, fd = hid.shape
    wch = deltas.shape[1]
    nsl = wch // LANES
    return pl.pallas_call(
        _hy_filt_kernel,
        grid=(HY_ORDER, nsl),
        in_specs=[pl.BlockSpec((n, fd), lambda o, j: (0, 0)),
                  pl.BlockSpec((fd, LANES), lambda o, j: (0, o * 2 * nsl + j)),
                  pl.BlockSpec((fd, LANES), lambda o, j: (0, o * 2 * nsl + nsl + j)),
                  pl.BlockSpec((n, LANES), lambda o, j: (0, 0)),
                  pl.BlockSpec((1, LANES), lambda o, j: (0, j))],
        out_specs=pl.BlockSpec((1, n, LANES), lambda o, j: (o, 0, j)),
        out_shape=jax.ShapeDtypeStruct((HY_ORDER, n, wch), F32),
        compiler_params=_params(("parallel", "parallel")),
        name="hyena_filter",
    )(hid, w3, w3, tn_full, deltas)


def _hy_kfft_kernel(kern_ref, f1_ref, f2_ref, o_ref, g_ref, *, n1, n2):
    def stage1(i2, _):
        x = kern_ref[pl.ds(i2, n1, stride=n2), :]
        g_ref[pl.ds(pl.multiple_of(i2 * 2 * n1, 2 * n1), 2 * n1), :] = _dot(f1_ref[i2], x, precision=HIGHEST)
        return 0
    lax.fori_loop(0, n2, stage1, 0)

    def stage2(k1, _):
        x = jnp.concatenate([g_ref[pl.ds(k1, n2, stride=2 * n1), :],
                             g_ref[pl.ds(n1 + k1, n2, stride=2 * n1), :]], axis=0)
        o_ref[pl.ds(pl.multiple_of(k1 * 2 * n2, 2 * n2), 2 * n2), :] = _dot(f2_ref[...], x, precision=HIGHEST)
        return 0
    lax.fori_loop(0, n1, stage2, 0)


def _hy_kfft(kern, f1k, f2, n1, n2):
    orders, n, wch = kern.shape
    nsl = wch // LANES
    return pl.pallas_call(
        functools.partial(_hy_kfft_kernel, n1=n1, n2=n2),
        grid=(orders, nsl),
        in_specs=[pl.BlockSpec((None, n, LANES), lambda o, j: (o, 0, j)),
                  pl.BlockSpec(f1k.shape, lambda o, j: (0, 0, 0)),
                  pl.BlockSpec(f2.shape, lambda o, j: (0, 0))],
        out_specs=pl.BlockSpec((None, None, 2 * n, LANES), lambda o, j: (o, j, 0, 0)),
        out_shape=jax.ShapeDtypeStruct((orders, nsl, 2 * n, LANES), F32),
        scratch_shapes=[pltpu.VMEM((2 * n, LANES), F32)],
        compiler_params=_params(("parallel", "parallel")),
        name="hyena_filter_fft",
    )(kern, f1k, f2)


def _hy_prep_kernel(z_ref, w_ref, b_ref, o_ref):
    z = z_ref[...]
    n = z.shape[0]
    row = lax.broadcasted_iota(jnp.int32, z.shape, 0)
    zm = jnp.where(row == 0, 0.0, pltpu.roll(z, 1, 0))
    zp = jnp.where(row == n - 1, 0.0, pltpu.roll(z, n - 1, 0))
    o_ref[...] = b_ref[...] + w_ref[0:1, :] * zm + w_ref[1:2, :] * z + w_ref[2:3, :] * zp


def _hy_prep(z2, bsz, conv_w, conv_b):
    m, wch = z2.shape
    l_len = m // bsz
    return pl.pallas_call(
        _hy_prep_kernel,
        grid=(bsz, wch // LANES),
        in_specs=[pl.BlockSpec((l_len, LANES), lambda b, j: (b, j)),
                  pl.BlockSpec((3, LANES), lambda b, j: (0, j)),
                  pl.BlockSpec((1, LANES), lambda b, j: (0, j))],
        out_specs=pl.BlockSpec((l_len, LANES), lambda b, j: (b, j)),
        out_shape=jax.ShapeDtypeStruct(z2.shape, F32),
        compiler_params=_params(("parallel", "parallel")),
        name="hyena_short_conv",
    )(z2, conv_w, conv_b)


def _pack2(re, im):
    r = lax.bitcast_convert_type(re, jnp.uint32) + jnp.uint32(0x8000)
    i = lax.bitcast_convert_type(im, jnp.uint32) + jnp.uint32(0x8000)
    return (r & jnp.uint32(0xFFFF0000)) | (i >> 16)


def _unpack2(w):
    re = lax.bitcast_convert_type(w & jnp.uint32(0xFFFF0000), F32)
    im = lax.bitcast_convert_type(w << 16, F32)
    return jnp.concatenate([re, im], axis=0).astype(BF16)


def _hy_conv_kernel(y_ref, x_ref, kh_ref, bias_ref, f1_ref, f2_ref, f2i_ref, f1i_ref, o_ref, g_ref, h_ref,
                    *, n1, n2):
    h1 = n1 // 2
    l_len = h1 * n2
    bias = bias_ref[...]

    def stage1(i2, _):
        x = jnp.concatenate([y_ref[pl.ds(i2, h1, stride=n2), :],
                             y_ref[pl.ds(l_len + i2, h1, stride=n2), :]], axis=0).astype(BF16)
        a = _dot(f1_ref[i2], x)
        g_ref[pl.ds(pl.multiple_of(i2 * n1, n1), n1), :] = _pack2(a[0:n1, :], a[n1:, :])
        return 0
    lax.fori_loop(0, n2, stage1, 0)

    def stage23(k1, _):
        x = _unpack2(g_ref[pl.ds(k1, n2, stride=n1), :])
        yh = _dot(f2_ref[...], x)
        base = pl.multiple_of(k1 * 2 * n2, 2 * n2)
        kr = kh_ref[pl.ds(base, n2), :]
        ki = kh_ref[pl.ds(base + n2, n2), :]
        yr, yi = yh[0:n2, :], yh[n2:, :]
        z = jnp.concatenate([yr * kr - yi * ki, yr * ki + yi * kr], axis=0).astype(BF16)
        c = _dot(f2i_ref[...], z)
        h_ref[pl.ds(pl.multiple_of(k1 * n2, n2), n2), :] = _pack2(c[0:n2, :], c[n2:, :])
        return 0
    lax.fori_loop(0, n1, stage23, 0)

    def stage4(i2, _):
        x = _unpack2(h_ref[pl.ds(i2, n1, stride=n2), :])
        yc = _dot(f1i_ref[i2], x)
        for half in range(2):
            sl = pl.ds(half * l_len + i2, h1, stride=n2)
            o_ref[sl, :] = x_ref[sl, :] * (yc[half * h1:(half + 1) * h1, :] + bias * y_ref[sl, :])
        return 0
    lax.fori_loop(0, n2, stage4, 0)


def _hy_conv(ysrc, ycol, xsrc, xcol, khat, order, bias, mats, bsz, n1, n2):
    l_len = ysrc.shape[0] // bsz
    nsl = khat.shape[1]
    f1, f2, f2i, f1i = mats
    once = pl.Buffered(1)
    cst2 = lambda a: pl.BlockSpec(a.shape, lambda j, p: (0, 0), pipeline_mode=once)
    cst3 = lambda a: pl.BlockSpec(a.shape, lambda j, p: (0, 0, 0), pipeline_mode=once)
    return pl.pallas_call(
        functools.partial(_hy_conv_kernel, n1=n1, n2=n2),
        grid=(nsl, bsz // 2),
        in_specs=[pl.BlockSpec((2 * l_len, LANES), lambda j, p: (p, ycol + j)),
                  pl.BlockSpec((2 * l_len, LANES), lambda j, p: (p, xcol + j)),
                  pl.BlockSpec((None, None, khat.shape[2], LANES), lambda j, p: (order, j, 0, 0),
                               pipeline_mode=once),
                  pl.BlockSpec((1, LANES), lambda j, p: (0, j)),
                  cst3(f1), cst2(f2), cst2(f2i), cst3(f1i)],
        out_specs=pl.BlockSpec((2 * l_len, LANES), lambda j, p: (p, j)),
        out_shape=jax.ShapeDtypeStruct((ysrc.shape[0], nsl * LANES), F32),
        scratch_shapes=[pltpu.VMEM((n1 * n2, LANES), jnp.uint32), pltpu.VMEM((n1 * n2, LANES), jnp.uint32)],
        compiler_params=_params(("parallel", "arbitrary")),
        name="hyena_long_conv",
    )(ysrc, xsrc, khat, bias[order][None], f1, f2, f2i, f1i)


def _dft_tables(l_len):
    n = 2 * l_len
    n2 = 128 if l_len >= 1024 else 32
    n1 = n // n2
    h1 = n1 // 2
    i1 = np.arange(n1)[None, None, :]
    k1 = np.arange(n1)[None, :, None]
    i2 = np.arange(n2)[:, None, None]
    ph = 2 * np.pi * (i1 * k1 / n1 + i2 * k1 / n)
    c, s = np.cos(ph), np.sin(ph)
    ch, sh = c[:, :, :h1], s[:, :, :h1]
    f1 = np.concatenate([np.concatenate([ch, sh], 2), np.concatenate([-sh, ch], 2)], 1)
    f1k = np.concatenate([c, -s], 1)
    ct, st = np.swapaxes(ch, 1, 2), np.swapaxes(sh, 1, 2)
    f1i = np.concatenate([np.concatenate([ct, -st], 2), np.concatenate([st, ct], 2)], 1) / n
    a = np.arange(n2)
    ph2 = 2 * np.pi * np.outer(a, a) / n2
    c2, s2 = np.cos(ph2), np.sin(ph2)
    f2 = np.block([[c2, s2], [-s2, c2]])
    f2i = np.block([[c2, -s2], [s2, c2]])
    bf = lambda m: jnp.asarray(m, dtype=F32).astype(BF16)
    return n1, n2, (bf(f1), bf(f2), bf(f2i), bf(f1i)), jnp.asarray(f1k, F32), jnp.asarray(f2, F32)


def _filter_positions(l_len, width):
    lag = np.concatenate([np.arange(l_len), l_len - np.arange(l_len)]).astype(np.float64)
    t = lag / (l_len - 1)
    bands = (HY_EMB - 1) // 2
    w = 2.0 * np.pi * lag / l_len
    f = np.linspace(1e-4, bands - 1, bands)[None]
    z = np.concatenate([t[:, None], np.cos(f * w[:, None]), -np.sin(f * w[:, None])], axis=-1)
    z = np.pad(z, ((0, 0), (0, HY_EMB_PAD - HY_EMB)))
    tn = np.repeat(t[:, None], LANES, axis=1)
    max_decay = math.log(HY_DECAY_TARGET) / HY_FAST_PCT
    min_decay = math.log(HY_DECAY_TARGET) / HY_SLOW_PCT
    deltas = np.linspace(min_decay, max_decay, width)[None]
    return jnp.asarray(z, F32), jnp.asarray(tn, F32), jnp.asarray(deltas, F32)


def _rope_tables(seq):
    n_rows = seq // GRID_W
    row = np.repeat(np.arange(n_rows), GRID_W).astype(np.float64)
    col = np.tile(np.arange(GRID_W), n_rows).astype(np.float64)
    half = HEAD_DIM // 2
    inv = ROPE_BASE ** (-np.arange(0, half, 2, dtype=np.float64) / half)
    ar, ac = row[:, None] * inv, col[:, None] * inv
    cos = np.concatenate([np.cos(ar), np.cos(ar), np.cos(ac), np.cos(ac)], axis=1)
    sin = np.concatenate([-np.sin(ar), np.sin(ar), -np.sin(ac), np.sin(ac)], axis=1)
    return jnp.asarray(cos, F32), jnp.asarray(sin, F32)


def _hyena_spectra(l_len, width, w1, b1, w2, b2, w3, freq, tables):
    n1, n2, _, f1k, f2f = tables
    zfull, tn_full, deltas = _filter_positions(l_len, width)
    w1p = jnp.pad(w1, ((0, HY_EMB_PAD - HY_EMB), (0, 0)))
    hid = _hy_hidden(zfull, w1p, b1[None], w2, b2[None], freq[None])
    kern = _hy_filters(hid, w3, tn_full, deltas)
    return _hy_kfft(kern, f1k, f2f, n1, n2)


def _hyena_seq(zc, bsz, khat, bias, tables):
    n1, n2, mats, _, _ = tables
    nsl = khat.shape[1]
    y1 = _hy_conv(zc, 0, zc, nsl, khat, 0, bias, mats, bsz, n1, n2)
    return _hy_conv(y1, 0, zc, 2 * nsl, khat, 1, bias, mats, bsz, n1, n2)


def kernel(x, c, ctx, c_ctx, mod_w, mod_b, norm_pre, norm_post, ev_w_in, ev_w_out, lru_conv_w, lru_conv_b, lru_wa, lru_ba, lru_wx, lru_bx, lru_lambda, attn_sink, od_w_in, od_w_out, hy_conv_w, hy_conv_b, hy_w1, hy_b1, hy_w2, hy_b2, hy_w3, hy_freq, hy_bias, ret_decay_logit):
    bsz, s_len, d = x.shape
    c_len = ctx.shape[1]
    depth = mod_w.shape[0]
    assert bsz % 2 == 0 and bsz <= 16 and s_len % 1024 == 0 and c_len % 128 == 0

    crows = jnp.concatenate([c, c_ctx[None], jnp.zeros((24 - bsz - 1, d), F32)], axis=0)
    mod = _modulation(crows, mod_w, mod_b)

    cos_t, sin_t = _rope_tables(s_len)
    tab_l = _dft_tables(s_len)
    tab_c = _dft_tables(c_len)

    x2 = x.reshape(bsz * s_len, d)
    ctx2 = ctx.reshape(bsz * c_len, d)
    tm = 1024

    for l in range(depth):
        need_ctx = l < depth - 1
        shift, scale, gate = (mod[l, :bsz, i * d:(i + 1) * d].reshape(bsz, 1, d) for i in range(3))
        shift_c, scale_c, gate_c = (mod[l, bsz:bsz + 1, i * d:(i + 1) * d].reshape(1, 1, d) for i in range(3))
        g_pre = norm_pre[l][None]
        g_post = norm_post[l][None]

        def proj(w, dtype, tn):
            wb = w.astype(BF16)
            p = _in_proj(x2, g_pre, scale, shift, s_len, wb, dtype, tm, tn)
            pc = _in_proj(ctx2, g_pre, scale_c, shift_c, bsz * c_len, wb, dtype, tm, tn)
            return p, pc

        if l % 2 == 0:
            e = l // 2
            w_in = ev_w_in[e]
            xa, xac = proj(w_in[:, 0:1024], F32, 1024)
            w_rest = jnp.concatenate([w_in[:, 1024:2048], w_in[:, 2048:3072], w_in[:, 3584:4608],
                                      w_in[:, 3072:3584]], axis=1)
            pr, prc = proj(w_rest, BF16, 1792)
            cols = {"q": 1, "gb": 2, "k": 12, "v": 13}
            nblk = lru_wa.shape[2]
            wg = jnp.stack([lru_wa[e, 0], lru_wx[e, 0], lru_wa[e, 1], lru_wx[e, 1]], axis=1)
            wg = jnp.transpose(wg, (0, 2, 1, 3)).reshape(nblk, LRU_BLOCK_W, 4 * LRU_BLOCK_W).astype(BF16)
            bg = jnp.stack([lru_ba[e, 0], lru_bx[e, 0], lru_ba[e, 1], lru_bx[e, 1]], axis=0)
            bg = jnp.transpose(bg.reshape(4, nblk, LRU_BLOCK_W), (1, 0, 2)).reshape(nblk, 1, 4 * LRU_BLOCK_W)
            ya, yac = _lru(xa, xac, bsz, lru_conv_w[e], lru_conv_b[e][None], wg, bg, lru_lambda[e])
            yb = _attention(attn_sink[e], pr, prc, bsz, cols, cos_t, sin_t)
            w_out = ev_w_out[e].astype(BF16)
            x2_new = _out_proj(ya, 0, pr, 0, yb, 0, w_out, x2, g_post, gate, s_len, tm)
            if need_ctx:
                ybc = _ctx_attention(attn_sink[e], prc, bsz, cols)
                ctx2 = _out_proj(yac, 0, prc, 0, ybc, 0, w_out, ctx2, g_post, gate_c, bsz * c_len, tm)
            x2 = x2_new
        else:
            o = l // 2
            w_in = od_w_in[o]
            z, zc = proj(w_in[:, 0:3072], F32, 1024)
            pr, prc = proj(w_in[:, 3072:8192], BF16, 2560)
            cols = {"q": 8, "k": 16, "v": 24, "gd": 32}
            width = hy_bias.shape[2]
            fargs = (hy_w1[o], hy_b1[o], hy_w2[o], hy_b2[o], hy_w3[o], hy_freq[o])
            khat = _hyena_spectra(s_len, width, *fargs, tab_l)
            zconv = _hy_prep(z, bsz, hy_conv_w[o], hy_conv_b[o][None])
            yh = _hyena_seq(zconv, bsz, khat, hy_bias[o], tab_l)
            dl = jnp.broadcast_to(jnp.transpose(ret_decay_logit[o])[:, :, None], (RET_HEADS, 2, LANES))
            yd, ydc = _retention(dl, pr, prc, bsz, cols)
            w_out = od_w_out[o].astype(BF16)
            x2_new = _out_proj(yh, 0, pr, 0, yd, 0, w_out, x2, g_post, gate, s_len, tm)
            if need_ctx:
                khat_c = _hyena_spectra(c_len, width, *fargs, tab_c)
                zcconv = _hy_prep(zc, bsz, hy_conv_w[o], hy_conv_b[o][None])
                yhc = _hyena_seq(zcconv, bsz, khat_c, hy_bias[o], tab_c)
                ctx2 = _out_proj(yhc, 0, prc, 0, ydc, 0, w_out, ctx2, g_post, gate_c, bsz * c_len, tm)
            x2 = x2_new
    return x2.reshape(bsz, s_len, d)
```

```python
import functools
import math

import numpy as np
import jax
import jax.numpy as jnp
from jax import lax
from jax.experimental import pallas as pl
from jax.experimental.pallas import tpu as pltpu

F32 = jnp.float32
BF16 = jnp.bfloat16
HIGHEST = lax.Precision.HIGHEST

EPS = 1e-6
GRID_W = 64
LANES = 128
LRU_BLOCK_W = 128
LRU_CONV = 4
LRU_C = 8.0
LRU_CHUNK = 64
ATT_HEADS = 8
ATT_KV_HEADS = 2
HEAD_DIM = 128
BLOCK = 128
ROPE_BASE = 10000.0
HY_ORDER = 2
HY_EMB = 33
HY_EMB_PAD = 40
HY_DECAY_TARGET = 1e-2
HY_FAST_PCT = 0.3
HY_SLOW_PCT = 1.5
HY_UNROLL = 4
RET_HEADS = 8
RET_DK = 128
RET_CHUNK = 128

VMEM_LIMIT = 56 * 1024 * 1024
NEG = -1e30


def _params(sem, vmem=VMEM_LIMIT):
    return pltpu.CompilerParams(dimension_semantics=sem, vmem_limit_bytes=vmem)


def _sigmoid(v):
    return 0.5 * (jnp.tanh(0.5 * v) + 1.0)


def _silu(v):
    return v * _sigmoid(v)


def _softplus(v):
    return jnp.maximum(v, 0.0) + jnp.log(1.0 + jnp.exp(-jnp.abs(v)))


def _dot(a, b, **kw):
    return jnp.dot(a, b, preferred_element_type=F32, **kw)


def _dot_nt(a, b):
    return lax.dot_general(a, b, (((1,), (1,)), ((), ())), preferred_element_type=F32)


def _dot_tn(a, b):
    return lax.dot_general(a, b, (((0,), (0,)), ((), ())), preferred_element_type=F32)


def _mod_kernel(c_ref, w_ref, b_ref, o_ref):
    s = _silu(c_ref[...])
    o_ref[0] = _dot(s, w_ref[0], precision=HIGHEST) + b_ref[0]


def _modulation(crows, mod_w, mod_b):
    depth, d, n3 = mod_w.shape
    r = crows.shape[0]
    tn = 1024
    return pl.pallas_call(
        _mod_kernel,
        grid=(depth, n3 // tn),
        in_specs=[pl.BlockSpec((r, d), lambda l, j: (0, 0)),
                  pl.BlockSpec((1, d, tn), lambda l, j: (l, 0, j)),
                  pl.BlockSpec((1, 1, tn), lambda l, j: (l, 0, j))],
        out_specs=pl.BlockSpec((1, r, tn), lambda l, j: (l, 0, j)),
        out_shape=jax.ShapeDtypeStruct((depth, r, n3), F32),
        compiler_params=_params(("parallel", "parallel")),
        name="modulation",
    )(crows, mod_w, mod_b.reshape(depth, 1, n3))


def _inproj_kernel(x_ref, g_ref, sc_ref, sh_ref, w_ref, o_ref, h_ref):
    @pl.when(pl.program_id(1) == 0)
    def _():
        x = x_ref[...]
        y = x * lax.rsqrt(jnp.mean(x * x, axis=-1, keepdims=True) + EPS) * g_ref[...]
        h_ref[...] = (y * (1.0 + sc_ref[0]) + sh_ref[0]).astype(BF16)

    o_ref[...] = _dot(h_ref[...], w_ref[...]).astype(o_ref.dtype)


def _in_proj(x2, g, scale, shift, rows_per_group, w, out_dtype, tm, tn):
    m, d = x2.shape
    n = w.shape[1]
    tpg = rows_per_group // tm
    return pl.pallas_call(
        _inproj_kernel,
        grid=(m // tm, n // tn),
        in_specs=[pl.BlockSpec((tm, d), lambda i, j: (i, 0)),
                  pl.BlockSpec((1, d), lambda i, j: (0, 0)),
                  pl.BlockSpec((1, 1, d), lambda i, j: (i // tpg, 0, 0)),
                  pl.BlockSpec((1, 1, d), lambda i, j: (i // tpg, 0, 0)),
                  pl.BlockSpec((d, tn), lambda i, j: (0, j))],
        out_specs=pl.BlockSpec((tm, tn), lambda i, j: (i, j)),
        out_shape=jax.ShapeDtypeStruct((m, n), out_dtype),
        scratch_shapes=[pltpu.VMEM((tm, d), BF16)],
        compiler_params=_params(("parallel", "arbitrary")),
        name="in_proj",
    )(x2, g, scale, shift, w)


def _outproj_kernel(a_ref, ga_ref, b_ref, w_ref, x_ref, g_ref, gate_ref, o_ref):
    wa = w_ref.shape[0] // 2
    a = (a_ref[...].astype(F32) * _silu(ga_ref[...].astype(F32))).astype(BF16)
    y = _dot(a, w_ref[0:wa, :]) + _dot(b_ref[...], w_ref[wa:, :])
    yn = y * lax.rsqrt(jnp.mean(y * y, axis=-1, keepdims=True) + EPS) * g_ref[...]
    o_ref[...] = x_ref[...] + gate_ref[0] * yn


def _out_proj(a, a_col, ga, ga_col, b, b_col, w, x2, g, gate, rows_per_group, tm):
    m, d = x2.shape
    wa = w.shape[0] // 2
    tpg = rows_per_group // tm
    return pl.pallas_call(
        _outproj_kernel,
        grid=(m // tm,),
        in_specs=[pl.BlockSpec((tm, wa), lambda i: (i, a_col)),
                  pl.BlockSpec((tm, wa), lambda i: (i, ga_col)),
                  pl.BlockSpec((tm, wa), lambda i: (i, b_col)),
                  pl.BlockSpec(w.shape, lambda i: (0, 0)),
                  pl.BlockSpec((tm, d), lambda i: (i, 0)),
                  pl.BlockSpec((1, d), lambda i: (0, 0)),
                  pl.BlockSpec((1, 1, d), lambda i: (i // tpg, 0, 0))],
        out_specs=pl.BlockSpec((tm, d), lambda i: (i, 0)),
        out_shape=jax.ShapeDtypeStruct((m, d), F32),
        compiler_params=_params(("parallel",)),
        name="out_proj",
    )(a, ga, b, w, x2, g, gate)


def _lru_kernel(xa_ref, xac_ref, cw_ref, cb_ref, wg_ref, bg_ref, lam_ref, y_ref, yc_ref,
                xp_ref, xcp_ref, hf_ref, ab_ref, bb_ref, hfc_ref, abc_ref, bbc_ref):
    s_len = xa_ref.shape[0]
    c_len = xac_ref.shape[0]
    tc = LRU_CHUNK
    zeros8 = jnp.zeros((8, LANES), F32)
    xp_ref[0:8, :] = zeros8
    xp_ref[s_len + 8:s_len + 16, :] = zeros8
    xp_ref[8:s_len + 8, :] = xa_ref[...]
    xcp_ref[0:8, :] = zeros8
    xcp_ref[c_len + 8:c_len + 16, :] = zeros8
    xcp_ref[8:c_len + 8, :] = xac_ref[...]

    c8 = -LRU_C * _softplus(-lam_ref[...])
    row = lax.broadcasted_iota(jnp.int32, (tc, LANES), 0)
    left = LRU_CONV // 2

    def coeffs(src_ref, t0):
        u = cb_ref[...] + cw_ref[0:1, :] * src_ref[pl.ds(t0 + 8 - left, tc), :]
        for k in range(1, LRU_CONV):
            u = u + cw_ref[k:k + 1, :] * src_ref[pl.ds(t0 + 8 - left + k, tc), :]
        g = _dot(u.astype(BF16), wg_ref[0]) + bg_ref[0]
        out = []
        for d in range(2):
            r = _sigmoid(g[:, (2 * d) * LANES:(2 * d + 1) * LANES])
            i = _sigmoid(g[:, (2 * d + 1) * LANES:(2 * d + 2) * LANES])
            a = jnp.exp(c8[d:d + 1, :] * r)
            out.append((a, jnp.sqrt(1.0 - a * a) * (i * u)))
        return out

    def scan_fwd(a, b, carry):
        s = 1
        while s < tc:
            m = row >= s
            b = jnp.where(m, a * pltpu.roll(b, s, 0) + b, b)
            a = jnp.where(m, a * pltpu.roll(a, s, 0), a)
            s *= 2
        h = a * carry + b
        return h, h[tc - 1:tc, :]

    def scan_bwd(a, b, carry):
        s = 1
        while s < tc:
            m = row < tc - s
            b = jnp.where(m, a * pltpu.roll(b, tc - s, 0) + b, b)
            a = jnp.where(m, a * pltpu.roll(a, tc - s, 0), a)
            s *= 2
        h = a * carry + b
        return h, h[0:1, :]

    def fwd_pass(src_ref, n, hf_r, ab_r, bb_r, carry):
        def body(c, carry):
            t0 = pl.multiple_of(c * tc, tc)
            (af, bf), (ab, bb) = coeffs(src_ref, t0)
            h, carry = scan_fwd(af, bf, carry)
            hf_r[pl.ds(t0, tc), :] = h
            ab_r[pl.ds(t0, tc), :] = ab
            bb_r[pl.ds(t0, tc), :] = bb
            return carry
        return lax.fori_loop(0, n // tc, body, carry, unroll=2)

    def bwd_pass(n, hf_r, ab_r, bb_r, out_r, carry):
        def body(c, carry):
            t0 = pl.multiple_of((n // tc - 1 - c) * tc, tc)
            h, carry = scan_bwd(ab_r[pl.ds(t0, tc), :], bb_r[pl.ds(t0, tc), :], carry)
            out_r[pl.ds(t0, tc), :] = (hf_r[pl.ds(t0, tc), :] + h).astype(out_r.dtype)
            return carry
        return lax.fori_loop(0, n // tc, body, carry)

    zero = jnp.zeros((1, LANES), F32)
    carry = fwd_pass(xcp_ref, c_len, hfc_ref, abc_ref, bbc_ref, zero)
    fwd_pass(xp_ref, s_len, hf_ref, ab_ref, bb_ref, carry)
    carry = bwd_pass(c_len, hfc_ref, abc_ref, bbc_ref, yc_ref, zero)
    bwd_pass(s_len, hf_ref, ab_ref, bb_ref, y_ref, carry)


def _lru(xa2, xac2, bsz, conv_w, conv_b, wg, bg, lam):
    w = xa2.shape[1]
    s_len = xa2.shape[0] // bsz
    c_len = xac2.shape[0] // bsz
    nblk = w // LANES
    seq = lambda n: pl.BlockSpec((n, LANES), lambda b, j: (b, j))
    return pl.pallas_call(
        _lru_kernel,
        grid=(bsz, nblk),
        in_specs=[seq(s_len), seq(c_len),
                  pl.BlockSpec((LRU_CONV, LANES), lambda b, j: (0, j)),
                  pl.BlockSpec((1, LANES), lambda b, j: (0, j)),
                  pl.BlockSpec((1, LANES, 4 * LANES), lambda b, j: (j, 0, 0)),
                  pl.BlockSpec((1, 1, 4 * LANES), lambda b, j: (j, 0, 0)),
                  pl.BlockSpec((2, LANES), lambda b, j: (0, j))],
        out_specs=[seq(s_len), seq(c_len)],
        out_shape=[jax.ShapeDtypeStruct(xa2.shape, BF16), jax.ShapeDtypeStruct(xac2.shape, BF16)],
        scratch_shapes=[pltpu.VMEM((s_len + 16, LANES), F32), pltpu.VMEM((c_len + 16, LANES), F32),
                        pltpu.VMEM((s_len, LANES), F32), pltpu.VMEM((s_len, LANES), F32),
                        pltpu.VMEM((s_len, LANES), F32), pltpu.VMEM((c_len, LANES), F32),
                        pltpu.VMEM((c_len, LANES), F32), pltpu.VMEM((c_len, LANES), F32)],
        compiler_params=_params(("parallel", "parallel")),
        name="rglru",
    )(xa2, xac2, conv_w, conv_b, wg, bg, lam)


def _rope(v, cos, sin, lane):
    swapped = jnp.where((lane & 63) < 32, pltpu.roll(v, LANES - 32, 1), pltpu.roll(v, 32, 1))
    return v * cos + swapped * sin


def _attn_kernel(sink_ref, q_ref, gb_ref, k_ref, v_ref, kc_ref, vc_ref, cos_ref, sin_ref, o_ref):
    qb = pl.program_id(1)
    nb = pl.num_programs(1)
    group = ATT_HEADS // ATT_KV_HEADS
    scale = HEAD_DIM ** -0.5
    lane = lax.broadcasted_iota(jnp.int32, (BLOCK, LANES), 1)

    def blk(i):
        return pl.ds(pl.multiple_of(i * BLOCK, BLOCK), BLOCK)

    ip = jnp.maximum(qb - 1, 0)
    inx = jnp.minimum(qb + 1, nb - 1)
    cos_o, sin_o = cos_ref[blk(qb), :], sin_ref[blk(qb), :]
    cos_p, sin_p = cos_ref[blk(ip), :], sin_ref[blk(ip), :]
    cos_n, sin_n = cos_ref[blk(inx), :], sin_ref[blk(inx), :]

    rows = group * BLOCK
    ncol = 3 * BLOCK + kc_ref.shape[0]
    ri = lax.broadcasted_iota(jnp.int32, (rows, ncol), 0) & (BLOCK - 1)
    ci = lax.broadcasted_iota(jnp.int32, (rows, ncol), 1)
    has_prev = jnp.where(qb > 0, 0.0, NEG)
    has_next = jnp.where(qb < nb - 1, 0.0, NEG)
    b_prev = jnp.where(ci >= ri, has_prev, NEG)
    b_next = jnp.where(ci - 2 * BLOCK <= ri, has_next, NEG)
    bias = jnp.where(ci < BLOCK, b_prev,
                     jnp.where(ci < 2 * BLOCK, 0.0, jnp.where(ci < 3 * BLOCK, b_next, 0.0)))

    for h in range(ATT_KV_HEADS):
        ksl = slice(h * HEAD_DIM, (h + 1) * HEAD_DIM)
        kp = _rope(k_ref[blk(ip), ksl].astype(F32), cos_p, sin_p, lane)
        ko = _rope(k_ref[blk(qb), ksl].astype(F32), cos_o, sin_o, lane)
        kn = _rope(k_ref[blk(inx), ksl].astype(F32), cos_n, sin_n, lane)
        kcat = jnp.concatenate([kp.astype(BF16), ko.astype(BF16), kn.astype(BF16), kc_ref[:, ksl]], axis=0)
        vcat = jnp.concatenate([v_ref[blk(ip), ksl], v_ref[blk(qb), ksl], v_ref[blk(inx), ksl],
                                vc_ref[:, ksl]], axis=0)
        qs, sinks = [], []
        for g in range(group):
            hh = h * group + g
            qh = q_ref[:, hh * HEAD_DIM:(hh + 1) * HEAD_DIM].astype(F32)
            qs.append((_rope(qh, cos_o, sin_o, lane) * scale).astype(BF16))
            sinks.append(jnp.full((BLOCK, 1), sink_ref[hh], F32))
        q4 = jnp.concatenate(qs, axis=0)
        sk = jnp.concatenate(sinks, axis=0)
        s = _dot_nt(q4, kcat) + bias
        m = jnp.maximum(jnp.max(s, axis=-1, keepdims=True), sk)
        p = jnp.exp(s - m)
        denom = jnp.exp(sk - m) + jnp.sum(p, axis=-1, keepdims=True)
        o = _dot(p.astype(BF16), vcat) / denom
        for g in range(group):
            hh = h * group + g
            hs = slice(hh * HEAD_DIM, (hh + 1) * HEAD_DIM)
            gate = _silu(gb_ref[:, hs].astype(F32))
            o_ref[:, hs] = (o[g * BLOCK:(g + 1) * BLOCK, :] * gate).astype(o_ref.dtype)


def _attention(sink, proj, projc, bsz, cols, cos_t, sin_t):
    s_len = proj.shape[0] // bsz
    c_len = projc.shape[0] // bsz
    nb = s_len // BLOCK
    aw = ATT_HEADS * HEAD_DIM
    kw = ATT_KV_HEADS * HEAD_DIM
    return pl.pallas_call(
        _attn_kernel,
        grid=(bsz, nb),
        in_specs=[pl.BlockSpec(memory_space=pltpu.SMEM),
                  pl.BlockSpec((BLOCK, aw), lambda b, i: (b * nb + i, cols["q"])),
                  pl.BlockSpec((BLOCK, aw), lambda b, i: (b * nb + i, cols["gb"])),
                  pl.BlockSpec((s_len, kw), lambda b, i: (b, cols["k"])),
                  pl.BlockSpec((s_len, kw), lambda b, i: (b, cols["v"])),
                  pl.BlockSpec((c_len, kw), lambda b, i: (b, cols["k"])),
                  pl.BlockSpec((c_len, kw), lambda b, i: (b, cols["v"])),
                  pl.BlockSpec((s_len, LANES), lambda b, i: (0, 0)),
                  pl.BlockSpec((s_len, LANES), lambda b, i: (0, 0))],
        out_specs=pl.BlockSpec((BLOCK, aw), lambda b, i: (b * nb + i, 0)),
        out_shape=jax.ShapeDtypeStruct((proj.shape[0], aw), BF16),
        compiler_params=_params(("parallel", "arbitrary")),
        name="window_attention",
    )(sink, proj, proj, proj, proj, projc, projc, cos_t, sin_t)


def _ctx_attn_kernel(sink_ref, q_ref, gb_ref, k_ref, v_ref, o_ref):
    group = ATT_HEADS // ATT_KV_HEADS
    scale = HEAD_DIM ** -0.5
    n = q_ref.shape[0]
    for h in range(ATT_KV_HEADS):
        ksl = slice(h * HEAD_DIM, (h + 1) * HEAD_DIM)
        qs, sinks = [], []
        for g in range(group):
            hh = h * group + g
            qs.append((q_ref[:, hh * HEAD_DIM:(hh + 1) * HEAD_DIM].astype(F32) * scale).astype(BF16))
            sinks.append(jnp.full((n, 1), sink_ref[hh], F32))
        q4 = jnp.concatenate(qs, axis=0)
        sk = jnp.concatenate(sinks, axis=0)
        s = _dot_nt(q4, k_ref[:, ksl])
        m = jnp.maximum(jnp.max(s, axis=-1, keepdims=True), sk)
        p = jnp.exp(s - m)
        denom = jnp.exp(sk - m) + jnp.sum(p, axis=-1, keepdims=True)
        o = _dot(p.astype(BF16), v_ref[:, ksl]) / denom
        for g in range(group):
            hh = h * group + g
            hs = slice(hh * HEAD_DIM, (hh + 1) * HEAD_DIM)
            o_ref[:, hs] = (o[g * n:(g + 1) * n, :] * _silu(gb_ref[:, hs].astype(F32))).astype(o_ref.dtype)


def _ctx_attention(sink, projc, bsz, cols):
    c_len = projc.shape[0] // bsz
    aw = ATT_HEADS * HEAD_DIM
    kw = ATT_KV_HEADS * HEAD_DIM
    return pl.pallas_call(
        _ctx_attn_kernel,
        grid=(bsz,),
        in_specs=[pl.BlockSpec(memory_space=pltpu.SMEM),
                  pl.BlockSpec((c_len, aw), lambda b: (b, cols["q"])),
                  pl.BlockSpec((c_len, aw), lambda b: (b, cols["gb"])),
                  pl.BlockSpec((c_len, kw), lambda b: (b, cols["k"])),
                  pl.BlockSpec((c_len, kw), lambda b: (b, cols["v"]))],
        out_specs=pl.BlockSpec((c_len, aw), lambda b: (b, 0)),
        out_shape=jax.ShapeDtypeStruct((projc.shape[0], aw), BF16),
        compiler_params=_params(("parallel",)),
        name="context_attention",
    )(sink, projc, projc, projc, projc)


def _ret_kernel(dl_ref, q_ref, k_ref, v_ref, gd_ref, qc_ref, kc_ref, vc_ref, gdc_ref, o_ref, oc_ref,
                acc_ref, accc_ref, u_ref, uc_ref, st_ref, stc_ref):
    c = RET_CHUNK
    s_len = q_ref.shape[0]
    c_len = qc_ref.shape[0]
    lg = -_softplus(-dl_ref[0])
    lgf, lgb = lg[0:1, :], lg[1:2, :]
    ri = lax.broadcasted_iota(jnp.int32, (c, c), 0)
    ci = lax.broadcasted_iota(jnp.int32, (c, c), 1)
    diff = (ri - ci).astype(F32)
    dmat = jnp.where(ri >= ci, jnp.exp(jnp.maximum(diff, 0.0) * lgf), jnp.exp(jnp.maximum(-diff, 0.0) * lgb))
    idx = ri.astype(F32)
    qdec_f = jnp.exp((idx + 1.0) * lgf)
    kdec_f = jnp.exp((c - 1.0 - idx) * lgf)
    qdec_b = jnp.exp((c - idx) * lgb)
    kdec_b = jnp.exp(idx * lgb)
    cdec_f = jnp.exp(c * lgf)
    cdec_b = jnp.exp(c * lgb)

    def chunk(j):
        return pl.ds(pl.multiple_of(j * c, c), c)

    def intra(qr, kr, vr, acc, u, n):
        def body(j, _):
            q, k, v = qr[chunk(j), :], kr[chunk(j), :], vr[chunk(j), :]
            att = (_dot_nt(q, k) * dmat).astype(BF16)
            acc[chunk(j), :] = _dot(att, v)
            kf = k.astype(F32)
            kcat = jnp.concatenate([(kf * kdec_f).astype(BF16), (kf * kdec_b).astype(BF16)], axis=1)
            u[j] = _dot_tn(kcat, v)
            return 0
        lax.fori_loop(0, n // c, body, 0, unroll=min(4, n // c))

    def states(u, st, n, sf, sb):
        nch = n // c

        def fbody(j, s):
            st[j, 0:c, :] = s.astype(BF16)
            return s * cdec_f + u[j, 0:c, :]
        sf = lax.fori_loop(0, nch, fbody, sf)

        def bbody(jj, s):
            j = nch - 1 - jj
            st[j, c:2 * c, :] = s.astype(BF16)
            return s * cdec_b + u[j, c:2 * c, :]
        sb = lax.fori_loop(0, nch, bbody, sb)
        return sf, sb

    def cross(qr, gr, acc, st, outr, n):
        def body(j, _):
            qf = qr[chunk(j), :].astype(F32)
            qcat = jnp.concatenate([(qf * qdec_f).astype(BF16), (qf * qdec_b).astype(BF16)], axis=1)
            o = (acc[chunk(j), :] + _dot(qcat, st[j])) * (RET_DK ** -0.5)
            o = o * lax.rsqrt(jnp.mean(o * o, axis=-1, keepdims=True) + EPS)
            outr[chunk(j), :] = (o * _silu(gr[chunk(j), :].astype(F32))).astype(outr.dtype)
            return 0
        lax.fori_loop(0, n // c, body, 0, unroll=min(4, n // c))

    intra(qc_ref, kc_ref, vc_ref, accc_ref, uc_ref, c_len)
    intra(q_ref, k_ref, v_ref, acc_ref, u_ref, s_len)
    zero = jnp.zeros((RET_DK, LANES), F32)
    sf, sb = states(uc_ref, stc_ref, c_len, zero, zero)
    states(u_ref, st_ref, s_len, sf, sb)
    cross(qc_ref, gdc_ref, accc_ref, stc_ref, oc_ref, c_len)
    cross(q_ref, gd_ref, acc_ref, st_ref, o_ref, s_len)


def _retention(dl, proj, projc, bsz, cols):
    s_len = proj.shape[0] // bsz
    c_len = projc.shape[0] // bsz
    hb = lambda n, col: pl.BlockSpec((n, LANES), lambda b, h: (b, col + h))
    return pl.pallas_call(
        _ret_kernel,
        grid=(bsz, RET_HEADS),
        in_specs=[pl.BlockSpec((1, 2, LANES), lambda b, h: (h, 0, 0)),
                  hb(s_len, cols["q"]), hb(s_len, cols["k"]), hb(s_len, cols["v"]), hb(s_len, cols["gd"]),
                  hb(c_len, cols["q"]), hb(c_len, cols["k"]), hb(c_len, cols["v"]), hb(c_len, cols["gd"])],
        out_specs=[pl.BlockSpec((s_len, LANES), lambda b, h: (b, h)),
                   pl.BlockSpec((c_len, LANES), lambda b, h: (b, h))],
        out_shape=[jax.ShapeDtypeStruct((proj.shape[0], RET_HEADS * LANES), BF16),
                   jax.ShapeDtypeStruct((projc.shape[0], RET_HEADS * LANES), BF16)],
        scratch_shapes=[pltpu.VMEM((s_len, LANES), F32), pltpu.VMEM((c_len, LANES), F32),
                        pltpu.VMEM((s_len // RET_CHUNK, 2 * RET_CHUNK, LANES), F32),
                        pltpu.VMEM((c_len // RET_CHUNK, 2 * RET_CHUNK, LANES), F32),
                        pltpu.VMEM((s_len // RET_CHUNK, 2 * RET_CHUNK, LANES), BF16),
                        pltpu.VMEM((c_len // RET_CHUNK, 2 * RET_CHUNK, LANES), BF16)],
        compiler_params=_params(("parallel", "parallel")),
        name="retention",
    )(dl, proj, proj, proj, proj, projc, projc, projc, projc)


def _hy_hid_kernel(z_ref, w1_ref, b1_ref, w2_ref, b2_ref, f_ref, o_ref):
    f = f_ref[...]
    h = jnp.sin(f * (_dot(z_ref[...], w1_ref[...], precision=HIGHEST) + b1_ref[...]))
    o_ref[...] = jnp.sin(f * (_dot(h, w2_ref[...], precision=HIGHEST) + b2_ref[...]))


def _hy_hidden(zfull, w1p, b1, w2, b2, freq):
    n, e = zfull.shape
    fd = w2.shape[0]
    tr = min(n, 1024)
    full = lambda shp: pl.BlockSpec(shp, lambda i: (0, 0))
    return pl.pallas_call(
        _hy_hid_kernel,
        grid=(n // tr,),
        in_specs=[pl.BlockSpec((tr, e), lambda i: (i, 0)), full((e, fd)), full((1, fd)), full((fd, fd)),
                  full((1, fd)), full((1, fd))],
        out_specs=pl.BlockSpec((tr, fd), lambda i: (i, 0)),
        out_shape=jax.ShapeDtypeStruct((n, fd), F32),
        compiler_params=_params(("parallel",)),
        name="hyena_filter_mlp",
    )(zfull, w1p, b1, w2, b2, freq)


def _hy_filt_kernel(hid_ref, w3f_ref, w3b_ref, tn_ref, dl_ref, o_ref):
    half = hid_ref.shape[0] // 2
    decay = jnp.exp(-tn_ref[...] * jnp.abs(dl_ref[...]))
    top = _dot(hid_ref[0:half, :], w3f_ref[...], precision=HIGHEST)
    bot = _dot(hid_ref[half:, :], w3b_ref[...], precision=HIGHEST)
    row = lax.broadcasted_iota(jnp.int32, bot.shape, 0)
    bot = jnp.where(row == 0, 0.0, bot)
    o_ref[0, 0:half, :] = top * decay[0:half, :]
    o_ref[0, half:, :] = bot * decay[half:, :]


def _hy_filters(hid, w3, tn_full, deltas):
    n, fd = hid.shape
    wch = deltas.shape[1]
    nsl = wch // LANES
    return pl.pallas_call(
        _hy_filt_kernel,
        grid=(HY_ORDER, nsl),
        in_specs=[pl.BlockSpec((n, fd), lambda o, j: (0, 0)),
                  pl.BlockSpec((fd, LANES), lambda o, j: (0, o * 2 * nsl + j)),
                  pl.BlockSpec((fd, LANES), lambda o, j: (0, o * 2 * nsl + nsl + j)),
                  pl.BlockSpec((n, LANES), lambda o, j: (0, 0)),
                  pl.BlockSpec((1, LANES), lambda o, j: (0, j))],
        out_specs=pl.BlockSpec((1, n, LANES), lambda o, j: (o, 0, j)),
        out_shape=jax.ShapeDtypeStruct((HY_ORDER, n, wch), F32),
        compiler_params=_params(("parallel", "parallel")),
        name="hyena_filter",
    )(hid, w3, w3, tn_full, deltas)


def _hy_kfft_kernel(kern_ref, f1_ref, f2_ref, o_ref, g_ref, *, n1, n2):
    def stage1(i2, _):
        x = kern_ref[pl.ds(i2, n1, stride=n2), :]
        g_ref[pl.ds(pl.multiple_of(i2 * 2 * n1, 2 * n1), 2 * n1), :] = _dot(f1_ref[i2], x, precision=HIGHEST)
        return 0
    lax.fori_loop(0, n2, stage1, 0, unroll=HY_UNROLL)

    def stage2(k1, _):
        x = jnp.concatenate([g_ref[pl.ds(k1, n2, stride=2 * n1), :],
                             g_ref[pl.ds(n1 + k1, n2, stride=2 * n1), :]], axis=0)
        o_ref[pl.ds(pl.multiple_of(k1 * 2 * n2, 2 * n2), 2 * n2), :] = _dot(f2_ref[...], x, precision=HIGHEST)
        return 0
    lax.fori_loop(0, n1, stage2, 0, unroll=HY_UNROLL // 2)


def _hy_kfft(kern, f1k, f2, n1, n2):
    orders, n, wch = kern.shape
    nsl = wch // LANES
    return pl.pallas_call(
        functools.partial(_hy_kfft_kernel, n1=n1, n2=n2),
        grid=(orders, nsl),
        in_specs=[pl.BlockSpec((None, n, LANES), lambda o, j: (o, 0, j)),
                  pl.BlockSpec(f1k.shape, lambda o, j: (0, 0, 0)),
                  pl.BlockSpec(f2.shape, lambda o, j: (0, 0))],
        out_specs=pl.BlockSpec((None, None, 2 * n, LANES), lambda o, j: (o, j, 0, 0)),
        out_shape=jax.ShapeDtypeStruct((orders, nsl, 2 * n, LANES), F32),
        scratch_shapes=[pltpu.VMEM((2 * n, LANES), F32)],
        compiler_params=_params(("parallel", "parallel")),
        name="hyena_filter_fft",
    )(kern, f1k, f2)


def _hy_prep_kernel(z_ref, w_ref, b_ref, o_ref):
    z = z_ref[...]
    n = z.shape[0]
    row = lax.broadcasted_iota(jnp.int32, z.shape, 0)
    zm = jnp.where(row == 0, 0.0, pltpu.roll(z, 1, 0))
    zp = jnp.where(row == n - 1, 0.0, pltpu.roll(z, n - 1, 0))
    o_ref[...] = b_ref[...] + w_ref[0:1, :] * zm + w_ref[1:2, :] * z + w_ref[2:3, :] * zp


def _hy_prep(z2, bsz, conv_w, conv_b):
    m, wch = z2.shape
    l_len = m // bsz
    return pl.pallas_call(
        _hy_prep_kernel,
        grid=(bsz, wch // LANES),
        in_specs=[pl.BlockSpec((l_len, LANES), lambda b, j: (b, j)),
                  pl.BlockSpec((3, LANES), lambda b, j: (0, j)),
                  pl.BlockSpec((1, LANES), lambda b, j: (0, j))],
        out_specs=pl.BlockSpec((l_len, LANES), lambda b, j: (b, j)),
        out_shape=jax.ShapeDtypeStruct(z2.shape, F32),
        compiler_params=_params(("parallel", "parallel")),
        name="hyena_short_conv",
    )(z2, conv_w, conv_b)


def _pack2(re, im):
    r = lax.bitcast_convert_type(re, jnp.uint32) + jnp.uint32(0x8000)
    i = lax.bitcast_convert_type(im, jnp.uint32) + jnp.uint32(0x8000)
    return (r & jnp.uint32(0xFFFF0000)) | (i >> 16)


def _unpack2(w):
    re = lax.bitcast_convert_type(w & jnp.uint32(0xFFFF0000), F32)
    im = lax.bitcast_convert_type(w << 16, F32)
    return jnp.concatenate([re, im], axis=0).astype(BF16)


def _hy_conv_kernel(y_ref, x_ref, kh_ref, bias_ref, f1_ref, f2_ref, f2i_ref, f1i_ref, o_ref, g_ref, h_ref,
                    *, n1, n2):
    h1 = n1 // 2
    l_len = h1 * n2
    bias = bias_ref[...]

    def stage1(i2, _):
        x = jnp.concatenate([y_ref[pl.ds(i2, h1, stride=n2), :],
                             y_ref[pl.ds(l_len + i2, h1, stride=n2), :]], axis=0).astype(BF16)
        a = _dot(f1_ref[i2], x)
        g_ref[pl.ds(pl.multiple_of(i2 * n1, n1), n1), :] = _pack2(a[0:n1, :], a[n1:, :])
        return 0
    lax.fori_loop(0, n2, stage1, 0, unroll=2 * HY_UNROLL)

    def stage23(k1, _):
        x = _unpack2(g_ref[pl.ds(k1, n2, stride=n1), :])
        yh = _dot(f2_ref[...], x)
        base = pl.multiple_of(k1 * 2 * n2, 2 * n2)
        kr = kh_ref[pl.ds(base, n2), :]
        ki = kh_ref[pl.ds(base + n2, n2), :]
        yr, yi = yh[0:n2, :], yh[n2:, :]
        z = jnp.concatenate([yr * kr - yi * ki, yr * ki + yi * kr], axis=0).astype(BF16)
        c = _dot(f2i_ref[...], z)
        h_ref[pl.ds(pl.multiple_of(k1 * n2, n2), n2), :] = _pack2(c[0:n2, :], c[n2:, :])
        return 0
    lax.fori_loop(0, n1, stage23, 0, unroll=HY_UNROLL)

    def stage4(i2, _):
        x = _unpack2(h_ref[pl.ds(i2, n1, stride=n2), :])
        yc = _dot(f1i_ref[i2], x)
        for half in range(2):
            sl = pl.ds(half * l_len + i2, h1, stride=n2)
            o_ref[sl, :] = x_ref[sl, :] * (yc[half * h1:(half + 1) * h1, :] + bias * y_ref[sl, :])
        return 0
    lax.fori_loop(0, n2, stage4, 0, unroll=HY_UNROLL)


def _hy_conv(ysrc, ycol, xsrc, xcol, khat, order, bias, mats, bsz, n1, n2):
    l_len = ysrc.shape[0] // bsz
    nsl = khat.shape[1]
    f1, f2, f2i, f1i = mats
    once = pl.Buffered(1)
    cst2 = lambda a: pl.BlockSpec(a.shape, lambda j, p: (0, 0), pipeline_mode=once)
    cst3 = lambda a: pl.BlockSpec(a.shape, lambda j, p: (0, 0, 0), pipeline_mode=once)
    return pl.pallas_call(
        functools.partial(_hy_conv_kernel, n1=n1, n2=n2),
        grid=(nsl, bsz // 2),
        in_specs=[pl.BlockSpec((2 * l_len, LANES), lambda j, p: (p, ycol + j)),
                  pl.BlockSpec((2 * l_len, LANES), lambda j, p: (p, xcol + j)),
                  pl.BlockSpec((None, None, khat.shape[2], LANES), lambda j, p: (order, j, 0, 0),
                               pipeline_mode=once),
                  pl.BlockSpec((1, LANES), lambda j, p: (0, j)),
                  cst3(f1), cst2(f2), cst2(f2i), cst3(f1i)],
        out_specs=pl.BlockSpec((2 * l_len, LANES), lambda j, p: (p, j)),
        out_shape=jax.ShapeDtypeStruct((ysrc.shape[0], nsl * LANES), F32),
        scratch_shapes=[pltpu.VMEM((n1 * n2, LANES), jnp.uint32), pltpu.VMEM((n1 * n2, LANES), jnp.uint32)],
        compiler_params=_params(("parallel", "arbitrary")),
        name="hyena_long_conv",
    )(ysrc, xsrc, khat, bias[order][None], f1, f2, f2i, f1i)


def _dft_tables(l_len):
    n = 2 * l_len
    n2 = 128 if l_len >= 1024 else 32
    n1 = n // n2
    h1 = n1 // 2
    i1 = np.arange(n1)[None, None, :]
    k1 = np.arange(n1)[None, :, None]
    i2 = np.arange(n2)[:, None, None]
    ph = 2 * np.pi * (i1 * k1 / n1 + i2 * k1 / n)
    c, s = np.cos(ph), np.sin(ph)
    ch, sh = c[:, :, :h1], s[:, :, :h1]
    f1 = np.concatenate([np.concatenate([ch, sh], 2), np.concatenate([-sh, ch], 2)], 1)
    f1k = np.concatenate([c, -s], 1)
    ct, st = np.swapaxes(ch, 1, 2), np.swapaxes(sh, 1, 2)
    f1i = np.concatenate([np.concatenate([ct, -st], 2), np.concatenate([st, ct], 2)], 1) / n
    a = np.arange(n2)
    ph2 = 2 * np.pi * np.outer(a, a) / n2
    c2, s2 = np.cos(ph2), np.sin(ph2)
    f2 = np.block([[c2, s2], [-s2, c2]])
    f2i = np.block([[c2, -s2], [s2, c2]])
    bf = lambda m: jnp.asarray(m, dtype=F32).astype(BF16)
    return n1, n2, (bf(f1), bf(f2), bf(f2i), bf(f1i)), jnp.asarray(f1k, F32), jnp.asarray(f2, F32)


def _filter_positions(l_len, width):
    lag = np.concatenate([np.arange(l_len), l_len - np.arange(l_len)]).astype(np.float64)
    t = lag / (l_len - 1)
    bands = (HY_EMB - 1) // 2
    w = 2.0 * np.pi * lag / l_len
    f = np.linspace(1e-4, bands - 1, bands)[None]
    z = np.concatenate([t[:, None], np.cos(f * w[:, None]), -np.sin(f * w[:, None])], axis=-1)
    z = np.pad(z, ((0, 0), (0, HY_EMB_PAD - HY_EMB)))
    tn = np.repeat(t[:, None], LANES, axis=1)
    max_decay = math.log(HY_DECAY_TARGET) / HY_FAST_PCT
    min_decay = math.log(HY_DECAY_TARGET) / HY_SLOW_PCT
    deltas = np.linspace(min_decay, max_decay, width)[None]
    return jnp.asarray(z, F32), jnp.asarray(tn, F32), jnp.asarray(deltas, F32)


def _rope_tables(seq):
    n_rows = seq // GRID_W
    row = np.repeat(np.arange(n_rows), GRID_W).astype(np.float64)
    col = np.tile(np.arange(GRID_W), n_rows).astype(np.float64)
    half = HEAD_DIM // 2
    inv = ROPE_BASE ** (-np.arange(0, half, 2, dtype=np.float64) / half)
    ar, ac = row[:, None] * inv, col[:, None] * inv
    cos = np.concatenate([np.cos(ar), np.cos(ar), np.cos(ac), np.cos(ac)], axis=1)
    sin = np.concatenate([-np.sin(ar), np.sin(ar), -np.sin(ac), np.sin(ac)], axis=1)
    return jnp.asarray(cos, F32), jnp.asarray(sin, F32)


def _hyena_spectra(l_len, width, w1, b1, w2, b2, w3, freq, tables):
    n1, n2, _, f1k, f2f = tables
    zfull, tn_full, deltas = _filter_positions(l_len, width)
    w1p = jnp.pad(w1, ((0, HY_EMB_PAD - HY_EMB), (0, 0)))
    hid = _hy_hidden(zfull, w1p, b1[None], w2, b2[None], freq[None])
    kern = _hy_filters(hid, w3, tn_full, deltas)
    return _hy_kfft(kern, f1k, f2f, n1, n2)


def _hyena_seq(zc, bsz, khat, bias, tables):
    n1, n2, mats, _, _ = tables
    nsl = khat.shape[1]
    y1 = _hy_conv(zc, 0, zc, nsl, khat, 0, bias, mats, bsz, n1, n2)
    return _hy_conv(y1, 0, zc, 2 * nsl, khat, 1, bias, mats, bsz, n1, n2)


def kernel(x, c, ctx, c_ctx, mod_w, mod_b, norm_pre, norm_post, ev_w_in, ev_w_out, lru_conv_w, lru_conv_b, lru_wa, lru_ba, lru_wx, lru_bx, lru_lambda, attn_sink, od_w_in, od_w_out, hy_conv_w, hy_conv_b, hy_w1, hy_b1, hy_w2, hy_b2, hy_w3, hy_freq, hy_bias, ret_decay_logit):
    bsz, s_len, d = x.shape
    c_len = ctx.shape[1]
    depth = mod_w.shape[0]
    assert bsz % 2 == 0 and bsz <= 16 and s_len % 1024 == 0 and c_len % 128 == 0

    crows = jnp.concatenate([c, c_ctx[None], jnp.zeros((24 - bsz - 1, d), F32)], axis=0)
    mod = _modulation(crows, mod_w, mod_b)

    cos_t, sin_t = _rope_tables(s_len)
    tab_l = _dft_tables(s_len)
    tab_c = _dft_tables(c_len)

    x2 = x.reshape(bsz * s_len, d)
    ctx2 = ctx.reshape(bsz * c_len, d)
    tm = 1024

    for l in range(depth):
        need_ctx = l < depth - 1
        shift, scale, gate = (mod[l, :bsz, i * d:(i + 1) * d].reshape(bsz, 1, d) for i in range(3))
        shift_c, scale_c, gate_c = (mod[l, bsz:bsz + 1, i * d:(i + 1) * d].reshape(1, 1, d) for i in range(3))
        g_pre = norm_pre[l][None]
        g_post = norm_post[l][None]

        def proj(w, dtype, tn):
            wb = w.astype(BF16)
            p = _in_proj(x2, g_pre, scale, shift, s_len, wb, dtype, tm, tn)
            pc = _in_proj(ctx2, g_pre, scale_c, shift_c, bsz * c_len, wb, dtype, tm, tn)
            return p, pc

        if l % 2 == 0:
            e = l // 2
            w_in = ev_w_in[e]
            xa, xac = proj(w_in[:, 0:1024], F32, 1024)
            w_rest = jnp.concatenate([w_in[:, 1024:2048], w_in[:, 2048:3072], w_in[:, 3584:4608],
                                      w_in[:, 3072:3584]], axis=1)
            pr, prc = proj(w_rest, BF16, 1792)
            cols = {"q": 1, "gb": 2, "k": 12, "v": 13}
            nblk = lru_wa.shape[2]
            wg = jnp.stack([lru_wa[e, 0], lru_wx[e, 0], lru_wa[e, 1], lru_wx[e, 1]], axis=1)
            wg = jnp.transpose(wg, (0, 2, 1, 3)).reshape(nblk, LRU_BLOCK_W, 4 * LRU_BLOCK_W).astype(BF16)
            bg = jnp.stack([lru_ba[e, 0], lru_bx[e, 0], lru_ba[e, 1], lru_bx[e, 1]], axis=0)
            bg = jnp.transpose(bg.reshape(4, nblk, LRU_BLOCK_W), (1, 0, 2)).reshape(nblk, 1, 4 * LRU_BLOCK_W)
            ya, yac = _lru(xa, xac, bsz, lru_conv_w[e], lru_conv_b[e][None], wg, bg, lru_lambda[e])
            yb = _attention(attn_sink[e], pr, prc, bsz, cols, cos_t, sin_t)
            w_out = ev_w_out[e].astype(BF16)
            x2_new = _out_proj(ya, 0, pr, 0, yb, 0, w_out, x2, g_post, gate, s_len, tm)
            if need_ctx:
                ybc = _ctx_attention(attn_sink[e], prc, bsz, cols)
                ctx2 = _out_proj(yac, 0, prc, 0, ybc, 0, w_out, ctx2, g_post, gate_c, bsz * c_len, tm)
            x2 = x2_new
        else:
            o = l // 2
            w_in = od_w_in[o]
            z, zc = proj(w_in[:, 0:3072], F32, 1024)
            pr, prc = proj(w_in[:, 3072:8192], BF16, 2560)
            cols = {"q": 8, "k": 16, "v": 24, "gd": 32}
            width = hy_bias.shape[2]
            fargs = (hy_w1[o], hy_b1[o], hy_w2[o], hy_b2[o], hy_w3[o], hy_freq[o])
            khat = _hyena_spectra(s_len, width, *fargs, tab_l)
            zconv = _hy_prep(z, bsz, hy_conv_w[o], hy_conv_b[o][None])
            yh = _hyena_seq(zconv, bsz, khat, hy_bias[o], tab_l)
            dl = jnp.broadcast_to(jnp.transpose(ret_decay_logit[o])[:, :, None], (RET_HEADS, 2, LANES))
            yd, ydc = _retention(dl, pr, prc, bsz, cols)
            w_out = od_w_out[o].astype(BF16)
            x2_new = _out_proj(yh, 0, pr, 0, yd, 0, w_out, x2, g_post, gate, s_len, tm)
            if need_ctx:
                khat_c = _hyena_spectra(c_len, width, *fargs, tab_c)
                zcconv = _hy_prep(zc, bsz, hy_conv_w[o], hy_conv_b[o][None])
                yhc = _hyena_seq(zcconv, bsz, khat_c, hy_bias[o], tab_c)
                ctx2 = _out_proj(yhc, 0, prc, 0, ydc, 0, w_out, ctx2, g_post, gate_c, bsz * c_len, tm)
            x2 = x2_new
    return x2.reshape(bsz, s_len, d)
```

```python
import functools
import math

import numpy as np
import jax
import jax.numpy as jnp
from jax import lax
from jax.experimental import pallas as pl
from jax.experimental.pallas import tpu as pltpu

F32 = jnp.float32
BF16 = jnp.bfloat16
HIGHEST = lax.Precision.HIGHEST

EPS = 1e-6
GRID_W = 64
LANES = 128
LRU_BLOCK_W = 128
LRU_CONV = 4
LRU_C = 8.0
LRU_SEG = 16
LRU_POS = 16
LRU_TINY = 1e-30
ATT_HEADS = 8
ATT_KV_HEADS = 2
HEAD_DIM = 128
BLOCK = 128
ROPE_BASE = 10000.0
HY_ORDER = 2
HY_EMB = 33
HY_EMB_PAD = 40
HY_DECAY_TARGET = 1e-2
HY_FAST_PCT = 0.3
HY_SLOW_PCT = 1.5
HY_UNROLL = 4
HY_PITCH_PAD = 8
RET_HEADS = 8
RET_DK = 128
RET_CHUNK = 128

VMEM_LIMIT = 56 * 1024 * 1024
NEG = -1e30


def _params(sem, vmem=VMEM_LIMIT):
    return pltpu.CompilerParams(dimension_semantics=sem, vmem_limit_bytes=vmem)


def _sigmoid(v):
    return 0.5 * (jnp.tanh(0.5 * v) + 1.0)


def _silu(v):
    return v * _sigmoid(v)


def _softplus(v):
    return jnp.maximum(v, 0.0) + jnp.log(1.0 + jnp.exp(-jnp.abs(v)))


def _dot(a, b, **kw):
    return jnp.dot(a, b, preferred_element_type=F32, **kw)


def _dot_nt(a, b):
    return lax.dot_general(a, b, (((1,), (1,)), ((), ())), preferred_element_type=F32)


def _dot_tn(a, b):
    return lax.dot_general(a, b, (((0,), (0,)), ((), ())), preferred_element_type=F32)


def _mod_kernel(c_ref, w_ref, b_ref, o_ref):
    s = _silu(c_ref[...])
    o_ref[0] = _dot(s, w_ref[0], precision=HIGHEST) + b_ref[0]


def _modulation(crows, mod_w, mod_b):
    depth, d, n3 = mod_w.shape
    r = crows.shape[0]
    tn = 1024
    return pl.pallas_call(
        _mod_kernel,
        grid=(depth, n3 // tn),
        in_specs=[pl.BlockSpec((r, d), lambda l, j: (0, 0)),
                  pl.BlockSpec((1, d, tn), lambda l, j: (l, 0, j)),
                  pl.BlockSpec((1, 1, tn), lambda l, j: (l, 0, j))],
        out_specs=pl.BlockSpec((1, r, tn), lambda l, j: (l, 0, j)),
        out_shape=jax.ShapeDtypeStruct((depth, r, n3), F32),
        compiler_params=_params(("parallel", "parallel")),
        name="modulation",
    )(crows, mod_w, mod_b.reshape(depth, 1, n3))


def _inproj_kernel(x_ref, g_ref, sc_ref, sh_ref, w_ref, o_ref, h_ref):
    @pl.when(pl.program_id(1) == 0)
    def _():
        x = x_ref[...]
        y = x * lax.rsqrt(jnp.mean(x * x, axis=-1, keepdims=True) + EPS) * g_ref[...]
        h_ref[...] = (y * (1.0 + sc_ref[0]) + sh_ref[0]).astype(BF16)

    o_ref[...] = _dot(h_ref[...], w_ref[...]).astype(o_ref.dtype)


def _in_proj(x2, g, scale, shift, rows_per_group, w, out_dtype, tm, tn):
    m, d = x2.shape
    n = w.shape[1]
    tpg = rows_per_group // tm
    return pl.pallas_call(
        _inproj_kernel,
        grid=(m // tm, n // tn),
        in_specs=[pl.BlockSpec((tm, d), lambda i, j: (i, 0)),
                  pl.BlockSpec((1, d), lambda i, j: (0, 0)),
                  pl.BlockSpec((1, 1, d), lambda i, j: (i // tpg, 0, 0)),
                  pl.BlockSpec((1, 1, d), lambda i, j: (i // tpg, 0, 0)),
                  pl.BlockSpec((d, tn), lambda i, j: (0, j))],
        out_specs=pl.BlockSpec((tm, tn), lambda i, j: (i, j)),
        out_shape=jax.ShapeDtypeStruct((m, n), out_dtype),
        scratch_shapes=[pltpu.VMEM((tm, d), BF16)],
        compiler_params=_params(("parallel", "arbitrary")),
        name="in_proj",
    )(x2, g, scale, shift, w)


def _outproj_kernel(a_ref, ga_ref, b_ref, w_ref, x_ref, g_ref, gate_ref, o_ref):
    wa = w_ref.shape[0] // 2
    a = (a_ref[...].astype(F32) * _silu(ga_ref[...].astype(F32))).astype(BF16)
    y = _dot(a, w_ref[0:wa, :]) + _dot(b_ref[...], w_ref[wa:, :])
    yn = y * lax.rsqrt(jnp.mean(y * y, axis=-1, keepdims=True) + EPS) * g_ref[...]
    o_ref[...] = x_ref[...] + gate_ref[0] * yn


def _out_proj(a, a_col, ga, ga_col, b, b_col, w, x2, g, gate, rows_per_group, tm):
    m, d = x2.shape
    wa = w.shape[0] // 2
    tpg = rows_per_group // tm
    return pl.pallas_call(
        _outproj_kernel,
        grid=(m // tm,),
        in_specs=[pl.BlockSpec((tm, wa), lambda i: (i, a_col)),
                  pl.BlockSpec((tm, wa), lambda i: (i, ga_col)),
                  pl.BlockSpec((tm, wa), lambda i: (i, b_col)),
                  pl.BlockSpec(w.shape, lambda i: (0, 0)),
                  pl.BlockSpec((tm, d), lambda i: (i, 0)),
                  pl.BlockSpec((1, d), lambda i: (0, 0)),
                  pl.BlockSpec((1, 1, d), lambda i: (i // tpg, 0, 0))],
        out_specs=pl.BlockSpec((tm, d), lambda i: (i, 0)),
        out_shape=jax.ShapeDtypeStruct((m, d), F32),
        compiler_params=_params(("parallel",)),
        name="out_proj",
    )(a, ga, b, w, x2, g, gate)


def _lru_kernel(xa_ref, xac_ref, cw_ref, cb_ref, wg_ref, bg_ref, lam_ref, y_ref, yc_ref,
                xp_ref, hf_ref, pf_ref, hb_ref, pb_ref):
    nseg = LRU_SEG
    npos = LRU_POS
    c8 = -LRU_C * _softplus(-lam_ref[...])
    left = LRU_CONV // 2
    cw = [cw_ref[k:k + 1, :] for k in range(LRU_CONV)]
    cb = cb_ref[...]
    zeros8 = jnp.zeros((8, LANES), F32)
    zeros_s = jnp.zeros((nseg, LANES), F32)
    ones_s = jnp.ones((nseg, LANES), F32)

    def run(src_ref, out_ref, n, c0f, c0b):
        seg = n // nseg
        nblk = seg // npos
        pitch = seg + 24
        for j in range(nseg):
            lo, hi = j * seg - 8, (j + 1) * seg + 8
            if lo < 0:
                xp_ref[j * pitch:j * pitch + 8, :] = zeros8
            if hi > n:
                xp_ref[j * pitch + seg + 8:j * pitch + seg + 16, :] = zeros8
            lo_c, hi_c = max(lo, 0), min(hi, n)
            xp_ref[j * pitch + (lo_c - lo):j * pitch + (hi_c - lo), :] = src_ref[lo_c:hi_c, :]

        def gather(q):
            return jnp.concatenate([xp_ref[pl.ds(8 + q + c * 8 * pitch, 8, stride=pitch), :]
                                    for c in range(nseg // 8)], axis=0)

        def blk(i):
            return pl.ds(pl.multiple_of(i * npos * nseg, npos * nseg), npos * nseg)

        def pos(v, p):
            return v[p * nseg:(p + 1) * nseg, :]

        def fwd_body(i, carry):
            h, pc = carry
            p0 = i * npos
            xs = [gather(p0 + q - left) for q in range(npos + LRU_CONV - 1)]
            us = []
            for p in range(npos):
                u = cb + cw[0] * xs[p]
                for k in range(1, LRU_CONV):
                    u = u + cw[k] * xs[p + k]
                us.append(u)
            u = jnp.concatenate(us, axis=0)
            g = _dot(u.astype(BF16), wg_ref[0]) + bg_ref[0]
            coef = []
            for d in range(2):
                r = _sigmoid(g[:, (2 * d) * LANES:(2 * d + 1) * LANES])
                gi = _sigmoid(g[:, (2 * d + 1) * LANES:(2 * d + 2) * LANES])
                a = jnp.exp(c8[d:d + 1, :] * r)
                om = 1.0 - a * a
                coef.append((a, om * lax.rsqrt(jnp.maximum(om, LRU_TINY)) * (gi * u)))
            (af, bf), (ab, bb) = coef
            hb_ref[blk(i), :] = ab
            pb_ref[blk(i), :] = bb
            hs, ps = [], []
            for p in range(npos):
                a_p = pos(af, p)
                h = a_p * h + pos(bf, p)
                pc = a_p * pc
                hs.append(h)
                ps.append(pc)
            hf_ref[blk(i), :] = jnp.concatenate(hs, axis=0)
            pf_ref[blk(i), :] = jnp.concatenate(ps, axis=0)
            return h, pc

        hef, pef = lax.fori_loop(0, nblk, fwd_body, (zeros_s, ones_s))

        def bwd_body(ii, carry):
            h, pc = carry
            i = nblk - 1 - ii
            ab = hb_ref[blk(i), :]
            bb = pb_ref[blk(i), :]
            hs, ps = [None] * npos, [None] * npos
            for p in reversed(range(npos)):
                a_p = pos(ab, p)
                h = a_p * h + pos(bb, p)
                pc = a_p * pc
                hs[p] = h
                ps[p] = pc
            hb_ref[blk(i), :] = jnp.concatenate(hs, axis=0)
            pb_ref[blk(i), :] = jnp.concatenate(ps, axis=0)
            return h, pc

        heb, peb = lax.fori_loop(0, nblk, bwd_body, (zeros_s, ones_s))

        rows_f, c = [], c0f
        for j in range(nseg):
            rows_f.append(c)
            c = hef[j:j + 1, :] + pef[j:j + 1, :] * c
        final_f = c
        rows_b, c = [None] * nseg, c0b
        for j in reversed(range(nseg)):
            rows_b[j] = c
            c = heb[j:j + 1, :] + peb[j:j + 1, :] * c
        final_b = c
        cf = jnp.concatenate(rows_f, axis=0)
        cbk = jnp.concatenate(rows_b, axis=0)

        def out_body(p, _):
            rs = pl.ds(pl.multiple_of(p * nseg, nseg), nseg)
            y = hf_ref[rs, :] + pf_ref[rs, :] * cf + hb_ref[rs, :] + pb_ref[rs, :] * cbk
            for c in range(nseg // 8):
                out_ref[pl.ds(p + c * 8 * seg, 8, stride=seg), :] = y[c * 8:(c + 1) * 8, :]
            return 0
        lax.fori_loop(0, seg, out_body, 0, unroll=8)
        return final_f, final_b

    zero = jnp.zeros((1, LANES), F32)
    ff, fb = run(xac_ref, yc_ref, xac_ref.shape[0], zero, zero)
    run(xa_ref, y_ref, xa_ref.shape[0], ff, fb)


def _lru(xa2, xac2, bsz, conv_w, conv_b, wg, bg, lam):
    w = xa2.shape[1]
    s_len = xa2.shape[0] // bsz
    c_len = xac2.shape[0] // bsz
    nblk = w // LANES
    seq = lambda n: pl.BlockSpec((n, LANES), lambda b, j: (b, j))
    return pl.pallas_call(
        _lru_kernel,
        grid=(bsz, nblk),
        in_specs=[seq(s_len), seq(c_len),
                  pl.BlockSpec((LRU_CONV, LANES), lambda b, j: (0, j)),
                  pl.BlockSpec((1, LANES), lambda b, j: (0, j)),
                  pl.BlockSpec((1, LANES, 4 * LANES), lambda b, j: (j, 0, 0)),
                  pl.BlockSpec((1, 1, 4 * LANES), lambda b, j: (j, 0, 0)),
                  pl.BlockSpec((2, LANES), lambda b, j: (0, j))],
        out_specs=[seq(s_len), seq(c_len)],
        out_shape=[jax.ShapeDtypeStruct(xa2.shape, F32), jax.ShapeDtypeStruct(xac2.shape, F32)],
        scratch_shapes=[pltpu.VMEM((s_len + 24 * LRU_SEG, LANES), F32)] + [pltpu.VMEM((s_len, LANES), F32)] * 4,
        compiler_params=_params(("parallel", "parallel")),
        name="rglru",
    )(xa2, xac2, conv_w, conv_b, wg, bg, lam)


def _rope(v, cos, sin, lane):
    swapped = jnp.where((lane & 63) < 32, pltpu.roll(v, LANES - 32, 1), pltpu.roll(v, 32, 1))
    return v * cos + swapped * sin


def _attn_kernel(sink_ref, q_ref, gb_ref, k_ref, v_ref, kc_ref, vc_ref, cos_ref, sin_ref, bias_ref, o_ref):
    qb = pl.program_id(1)
    nb = pl.num_programs(1)
    group = ATT_HEADS // ATT_KV_HEADS
    scale = HEAD_DIM ** -0.5
    lane = lax.broadcasted_iota(jnp.int32, (BLOCK, LANES), 1)

    def blk(i):
        return pl.ds(pl.multiple_of(i * BLOCK, BLOCK), BLOCK)

    ip = jnp.maximum(qb - 1, 0)
    inx = jnp.minimum(qb + 1, nb - 1)
    cos_o, sin_o = cos_ref[blk(qb), :], sin_ref[blk(qb), :]
    cos_p, sin_p = cos_ref[blk(ip), :], sin_ref[blk(ip), :]
    cos_n, sin_n = cos_ref[blk(inx), :], sin_ref[blk(inx), :]

    bias = jnp.concatenate([bias_ref[0]] * group, axis=0)

    for h in range(ATT_KV_HEADS):
        ksl = slice(h * HEAD_DIM, (h + 1) * HEAD_DIM)
        kp = _rope(k_ref[blk(ip), ksl].astype(F32), cos_p, sin_p, lane)
        ko = _rope(k_ref[blk(qb), ksl].astype(F32), cos_o, sin_o, lane)
        kn = _rope(k_ref[blk(inx), ksl].astype(F32), cos_n, sin_n, lane)
        kcat = jnp.concatenate([kp.astype(BF16), ko.astype(BF16), kn.astype(BF16), kc_ref[:, ksl]], axis=0)
        vcat = jnp.concatenate([v_ref[blk(ip), ksl], v_ref[blk(qb), ksl], v_ref[blk(inx), ksl],
                                vc_ref[:, ksl]], axis=0)
        qs, sinks = [], []
        for g in range(group):
            hh = h * group + g
            qh = q_ref[:, hh * HEAD_DIM:(hh + 1) * HEAD_DIM].astype(F32)
            qs.append((_rope(qh, cos_o, sin_o, lane) * scale).astype(BF16))
            sinks.append(jnp.full((BLOCK, 1), sink_ref[hh], F32))
        q4 = jnp.concatenate(qs, axis=0)
        sk = jnp.concatenate(sinks, axis=0)
        s = _dot_nt(q4, kcat) + bias
        m = jnp.maximum(jnp.max(s, axis=-1, keepdims=True), sk)
        p = jnp.exp(s - m)
        denom = jnp.exp(sk - m) + jnp.sum(p, axis=-1, keepdims=True)
        o = _dot(p.astype(BF16), vcat) / denom
        for g in range(group):
            hh = h * group + g
            hs = slice(hh * HEAD_DIM, (hh + 1) * HEAD_DIM)
            gate = _silu(gb_ref[:, hs].astype(F32))
            o_ref[:, hs] = (o[g * BLOCK:(g + 1) * BLOCK, :] * gate).astype(o_ref.dtype)


def _attention(sink, proj, projc, bsz, cols, cos_t, sin_t):
    s_len = proj.shape[0] // bsz
    c_len = projc.shape[0] // bsz
    nb = s_len // BLOCK
    aw = ATT_HEADS * HEAD_DIM
    kw = ATT_KV_HEADS * HEAD_DIM
    qi = np.arange(BLOCK)[:, None]
    kj = np.arange(BLOCK)[None, :]
    variants = []
    for v in range(4):
        prev = np.where((kj >= qi) & bool(v & 1), 0.0, NEG)
        nxt = np.where((kj <= qi) & bool(v & 2), 0.0, NEG)
        variants.append(np.concatenate([prev, np.zeros((BLOCK, BLOCK)), nxt, np.zeros((BLOCK, c_len))], axis=1))
    bias = jnp.asarray(np.stack(variants), F32)
    ncol = 3 * BLOCK + c_len

    def variant(b, i):
        return ((i > 0).astype(jnp.int32) + 2 * (i < nb - 1).astype(jnp.int32), 0, 0)

    return pl.pallas_call(
        _attn_kernel,
        grid=(bsz, nb),
        in_specs=[pl.BlockSpec(memory_space=pltpu.SMEM),
                  pl.BlockSpec((BLOCK, aw), lambda b, i: (b * nb + i, cols["q"])),
                  pl.BlockSpec((BLOCK, aw), lambda b, i: (b * nb + i, cols["gb"])),
                  pl.BlockSpec((s_len, kw), lambda b, i: (b, cols["k"])),
                  pl.BlockSpec((s_len, kw), lambda b, i: (b, cols["v"])),
                  pl.BlockSpec((c_len, kw), lambda b, i: (b, cols["k"])),
                  pl.BlockSpec((c_len, kw), lambda b, i: (b, cols["v"])),
                  pl.BlockSpec((s_len, LANES), lambda b, i: (0, 0)),
                  pl.BlockSpec((s_len, LANES), lambda b, i: (0, 0)),
                  pl.BlockSpec((1, BLOCK, ncol), variant)],
        out_specs=pl.BlockSpec((BLOCK, aw), lambda b, i: (b * nb + i, 0)),
        out_shape=jax.ShapeDtypeStruct((proj.shape[0], aw), BF16),
        compiler_params=_params(("parallel", "arbitrary")),
        name="window_attention",
    )(sink, proj, proj, proj, proj, projc, projc, cos_t, sin_t, bias)


def _ctx_attn_kernel(sink_ref, q_ref, gb_ref, k_ref, v_ref, o_ref):
    group = ATT_HEADS // ATT_KV_HEADS
    scale = HEAD_DIM ** -0.5
    n = q_ref.shape[0]
    for h in range(ATT_KV_HEADS):
        ksl = slice(h * HEAD_DIM, (h + 1) * HEAD_DIM)
        qs, sinks = [], []
        for g in range(group):
            hh = h * group + g
            qs.append((q_ref[:, hh * HEAD_DIM:(hh + 1) * HEAD_DIM].astype(F32) * scale).astype(BF16))
            sinks.append(jnp.full((n, 1), sink_ref[hh], F32))
        q4 = jnp.concatenate(qs, axis=0)
        sk = jnp.concatenate(sinks, axis=0)
        s = _dot_nt(q4, k_ref[:, ksl])
        m = jnp.maximum(jnp.max(s, axis=-1, keepdims=True), sk)
        p = jnp.exp(s - m)
        denom = jnp.exp(sk - m) + jnp.sum(p, axis=-1, keepdims=True)
        o = _dot(p.astype(BF16), v_ref[:, ksl]) / denom
        for g in range(group):
            hh = h * group + g
            hs = slice(hh * HEAD_DIM, (hh + 1) * HEAD_DIM)
            o_ref[:, hs] = (o[g * n:(g + 1) * n, :] * _silu(gb_ref[:, hs].astype(F32))).astype(o_ref.dtype)


def _ctx_attention(sink, projc, bsz, cols):
    c_len = projc.shape[0] // bsz
    aw = ATT_HEADS * HEAD_DIM
    kw = ATT_KV_HEADS * HEAD_DIM
    return pl.pallas_call(
        _ctx_attn_kernel,
        grid=(bsz,),
        in_specs=[pl.BlockSpec(memory_space=pltpu.SMEM),
                  pl.BlockSpec((c_len, aw), lambda b: (b, cols["q"])),
                  pl.BlockSpec((c_len, aw), lambda b: (b, cols["gb"])),
                  pl.BlockSpec((c_len, kw), lambda b: (b, cols["k"])),
                  pl.BlockSpec((c_len, kw), lambda b: (b, cols["v"]))],
        out_specs=pl.BlockSpec((c_len, aw), lambda b: (b, 0)),
        out_shape=jax.ShapeDtypeStruct((projc.shape[0], aw), BF16),
        compiler_params=_params(("parallel",)),
        name="context_attention",
    )(sink, projc, projc, projc, projc)


def _ret_kernel(dl_ref, q_ref, k_ref, v_ref, gd_ref, qc_ref, kc_ref, vc_ref, gdc_ref, o_ref, oc_ref,
                acc_ref, accc_ref, u_ref, uc_ref, st_ref, stc_ref):
    c = RET_CHUNK
    s_len = q_ref.shape[0]
    c_len = qc_ref.shape[0]
    lg = -_softplus(-dl_ref[0])
    lgf, lgb = lg[0:1, :], lg[1:2, :]
    ri = lax.broadcasted_iota(jnp.int32, (c, c), 0)
    ci = lax.broadcasted_iota(jnp.int32, (c, c), 1)
    diff = (ri - ci).astype(F32)
    dmat = jnp.where(ri >= ci, jnp.exp(jnp.maximum(diff, 0.0) * lgf), jnp.exp(jnp.maximum(-diff, 0.0) * lgb))
    idx = ri.astype(F32)
    qdec_f = jnp.exp((idx + 1.0) * lgf)
    kdec_f = jnp.exp((c - 1.0 - idx) * lgf)
    qdec_b = jnp.exp((c - idx) * lgb)
    kdec_b = jnp.exp(idx * lgb)
    cdec_f = jnp.exp(c * lgf)
    cdec_b = jnp.exp(c * lgb)

    def chunk(j):
        return pl.ds(pl.multiple_of(j * c, c), c)

    def intra(qr, kr, vr, acc, u, n):
        def body(j, _):
            q, k, v = qr[chunk(j), :], kr[chunk(j), :], vr[chunk(j), :]
            att = (_dot_nt(q, k) * dmat).astype(BF16)
            acc[chunk(j), :] = _dot(att, v)
            kf = k.astype(F32)
            kcat = jnp.concatenate([(kf * kdec_f).astype(BF16), (kf * kdec_b).astype(BF16)], axis=1)
            u[j] = _dot_tn(kcat, v)
            return 0
        lax.fori_loop(0, n // c, body, 0, unroll=min(4, n // c))

    def states(u, st, n, sf, sb):
        nch = n // c

        def fbody(j, s):
            st[j, 0:c, :] = s.astype(BF16)
            return s * cdec_f + u[j, 0:c, :]
        sf = lax.fori_loop(0, nch, fbody, sf)

        def bbody(jj, s):
            j = nch - 1 - jj
            st[j, c:2 * c, :] = s.astype(BF16)
            return s * cdec_b + u[j, c:2 * c, :]
        sb = lax.fori_loop(0, nch, bbody, sb)
        return sf, sb

    def cross(qr, gr, acc, st, outr, n):
        def body(j, _):
            qf = qr[chunk(j), :].astype(F32)
            qcat = jnp.concatenate([(qf * qdec_f).astype(BF16), (qf * qdec_b).astype(BF16)], axis=1)
            o = (acc[chunk(j), :] + _dot(qcat, st[j])) * (RET_DK ** -0.5)
            o = o * lax.rsqrt(jnp.mean(o * o, axis=-1, keepdims=True) + EPS)
            outr[chunk(j), :] = (o * _silu(gr[chunk(j), :].astype(F32))).astype(outr.dtype)
            return 0
        lax.fori_loop(0, n // c, body, 0, unroll=min(4, n // c))

    intra(qc_ref, kc_ref, vc_ref, accc_ref, uc_ref, c_len)
    intra(q_ref, k_ref, v_ref, acc_ref, u_ref, s_len)
    zero = jnp.zeros((RET_DK, LANES), F32)
    sf, sb = states(uc_ref, stc_ref, c_len, zero, zero)
    states(u_ref, st_ref, s_len, sf, sb)
    cross(qc_ref, gdc_ref, accc_ref, stc_ref, oc_ref, c_len)
    cross(q_ref, gd_ref, acc_ref, st_ref, o_ref, s_len)


def _retention(dl, proj, projc, bsz, cols):
    s_len = proj.shape[0] // bsz
    c_len = projc.shape[0] // bsz
    hb = lambda n, col: pl.BlockSpec((n, LANES), lambda b, h: (b, col + h))
    return pl.pallas_call(
        _ret_kernel,
        grid=(bsz, RET_HEADS),
        in_specs=[pl.BlockSpec((1, 2, LANES), lambda b, h: (h, 0, 0)),
                  hb(s_len, cols["q"]), hb(s_len, cols["k"]), hb(s_len, cols["v"]), hb(s_len, cols["gd"]),
                  hb(c_len, cols["q"]), hb(c_len, cols["k"]), hb(c_len, cols["v"]), hb(c_len, cols["gd"])],
        out_specs=[pl.BlockSpec((s_len, LANES), lambda b, h: (b, h)),
                   pl.BlockSpec((c_len, LANES), lambda b, h: (b, h))],
        out_shape=[jax.ShapeDtypeStruct((proj.shape[0], RET_HEADS * LANES), BF16),
                   jax.ShapeDtypeStruct((projc.shape[0], RET_HEADS * LANES), BF16)],
        scratch_shapes=[pltpu.VMEM((s_len, LANES), F32), pltpu.VMEM((c_len, LANES), F32),
                        pltpu.VMEM((s_len // RET_CHUNK, 2 * RET_CHUNK, LANES), F32),
                        pltpu.VMEM((c_len // RET_CHUNK, 2 * RET_CHUNK, LANES), F32),
                        pltpu.VMEM((s_len // RET_CHUNK, 2 * RET_CHUNK, LANES), BF16),
                        pltpu.VMEM((c_len // RET_CHUNK, 2 * RET_CHUNK, LANES), BF16)],
        compiler_params=_params(("parallel", "parallel")),
        name="retention",
    )(dl, proj, proj, proj, proj, projc, projc, projc, projc)


def _hy_hid_kernel(z_ref, w1_ref, b1_ref, w2_ref, b2_ref, f_ref, o_ref):
    f = f_ref[...]
    h = jnp.sin(f * (_dot(z_ref[...], w1_ref[...], precision=HIGHEST) + b1_ref[...]))
    o_ref[...] = jnp.sin(f * (_dot(h, w2_ref[...], precision=HIGHEST) + b2_ref[...]))


def _hy_hidden(zfull, w1p, b1, w2, b2, freq):
    n, e = zfull.shape
    fd = w2.shape[0]
    tr = min(n, 1024)
    full = lambda shp: pl.BlockSpec(shp, lambda i: (0, 0))
    return pl.pallas_call(
        _hy_hid_kernel,
        grid=(n // tr,),
        in_specs=[pl.BlockSpec((tr, e), lambda i: (i, 0)), full((e, fd)), full((1, fd)), full((fd, fd)),
                  full((1, fd)), full((1, fd))],
        out_specs=pl.BlockSpec((tr, fd), lambda i: (i, 0)),
        out_shape=jax.ShapeDtypeStruct((n, fd), F32),
        compiler_params=_params(("parallel",)),
        name="hyena_filter_mlp",
    )(zfull, w1p, b1, w2, b2, freq)


def _hy_filt_kernel(hid_ref, w3f_ref, w3b_ref, tn_ref, dl_ref, o_ref):
    half = hid_ref.shape[0] // 2
    decay = jnp.exp(-tn_ref[...] * jnp.abs(dl_ref[...]))
    top = _dot(hid_ref[0:half, :], w3f_ref[...], precision=HIGHEST)
    bot = _dot(hid_ref[half:, :], w3b_ref[...], precision=HIGHEST)
    row = lax.broadcasted_iota(jnp.int32, bot.shape, 0)
    bot = jnp.where(row == 0, 0.0, bot)
    o_ref[0, 0:half, :] = top * decay[0:half, :]
    o_ref[0, half:, :] = bot * decay[half:, :]


def _hy_filters(hid, w3, tn_full, deltas):
    n, fd = hid.shape
    wch = deltas.shape[1]
    nsl = wch // LANES
    return pl.pallas_call(
        _hy_filt_kernel,
        grid=(HY_ORDER, nsl),
        in_specs=[pl.BlockSpec((n, fd), lambda o, j: (0, 0)),
                  pl.BlockSpec((fd, LANES), lambda o, j: (0, o * 2 * nsl + j)),
                  pl.BlockSpec((fd, LANES), lambda o, j: (0, o * 2 * nsl + nsl + j)),
                  pl.BlockSpec((n, LANES), lambda o, j: (0, 0)),
                  pl.BlockSpec((1, LANES), lambda o, j: (0, j))],
        out_specs=pl.BlockSpec((1, n, LANES), lambda o, j: (o, 0, j)),
        out_shape=jax.ShapeDtypeStruct((HY_ORDER, n, wch), F32),
        compiler_params=_params(("parallel", "parallel")),
        name="hyena_filter",
    )(hid, w3, w3, tn_full, deltas)


def _hy_kfft_kernel(kern_ref, f1_ref, f2_ref, o_ref, g_ref, *, n1, n2):
    pg = 2 * n1 + HY_PITCH_PAD

    def stage1(i2, _):
        x = kern_ref[pl.ds(i2, n1, stride=n2), :]
        g_ref[pl.ds(pl.multiple_of(i2 * pg, 8), 2 * n1), :] = _dot(f1_ref[i2], x, precision=HIGHEST)
        return 0
    lax.fori_loop(0, n2, stage1, 0, unroll=HY_UNROLL)

    def stage2(k1, _):
        x = jnp.concatenate([g_ref[pl.ds(k1, n2, stride=pg), :],
                             g_ref[pl.ds(n1 + k1, n2, stride=pg), :]], axis=0)
        o_ref[pl.ds(pl.multiple_of(k1 * 2 * n2, 2 * n2), 2 * n2), :] = _dot(f2_ref[...], x, precision=HIGHEST)
        return 0
    lax.fori_loop(0, n1, stage2, 0, unroll=HY_UNROLL // 2)


def _hy_kfft(kern, f1k, f2, n1, n2):
    orders, n, wch = kern.shape
    nsl = wch // LANES
    return pl.pallas_call(
        functools.partial(_hy_kfft_kernel, n1=n1, n2=n2),
        grid=(orders, nsl),
        in_specs=[pl.BlockSpec((None, n, LANES), lambda o, j: (o, 0, j)),
                  pl.BlockSpec(f1k.shape, lambda o, j: (0, 0, 0)),
                  pl.BlockSpec(f2.shape, lambda o, j: (0, 0))],
        out_specs=pl.BlockSpec((None, None, 2 * n, LANES), lambda o, j: (o, j, 0, 0)),
        out_shape=jax.ShapeDtypeStruct((orders, nsl, 2 * n, LANES), F32),
        scratch_shapes=[pltpu.VMEM((n2 * (2 * n1 + HY_PITCH_PAD), LANES), F32)],
        compiler_params=_params(("parallel", "parallel")),
        name="hyena_filter_fft",
    )(kern, f1k, f2)


def _hy_prep_kernel(z_ref, w_ref, b_ref, o_ref, zp_ref, *, n2):
    l_len = z_ref.shape[0]
    h1 = l_len // n2
    zeros8 = jnp.zeros((8, LANES), F32)
    zp_ref[0:8, :] = zeros8
    zp_ref[l_len + 8:l_len + 16, :] = zeros8
    zp_ref[8:l_len + 8, :] = z_ref[...]
    w0, w1, w2, b = w_ref[0:1, :], w_ref[1:2, :], w_ref[2:3, :], b_ref[...]

    def tap(j):
        return zp_ref[pl.ds(8 + j, h1, stride=n2), :]

    def body(i2, carry):
        zm, z0 = carry
        zn = tap(i2 + 1)
        o_ref[pl.ds(pl.multiple_of(i2 * h1, h1), h1), :] = b + w0 * zm + w1 * z0 + w2 * zn
        return z0, zn
    lax.fori_loop(0, n2, body, (tap(-1), tap(0)), unroll=HY_UNROLL)


def _hy_prep(z2, bsz, conv_w, conv_b, n2):
    m, wch = z2.shape
    l_len = m // bsz
    return pl.pallas_call(
        functools.partial(_hy_prep_kernel, n2=n2),
        grid=(bsz, wch // LANES),
        in_specs=[pl.BlockSpec((l_len, LANES), lambda b, j: (b, j)),
                  pl.BlockSpec((3, LANES), lambda b, j: (0, j)),
                  pl.BlockSpec((1, LANES), lambda b, j: (0, j))],
        out_specs=pl.BlockSpec((l_len, LANES), lambda b, j: (b, j)),
        out_shape=jax.ShapeDtypeStruct(z2.shape, F32),
        scratch_shapes=[pltpu.VMEM((l_len + 16, LANES), F32)],
        compiler_params=_params(("parallel", "parallel")),
        name="hyena_short_conv",
    )(z2, conv_w, conv_b)


def _pack2(re, im):
    r = lax.bitcast_convert_type(re, jnp.uint32) + jnp.uint32(0x8000)
    i = lax.bitcast_convert_type(im, jnp.uint32) + jnp.uint32(0x8000)
    return (r & jnp.uint32(0xFFFF0000)) | (i >> 16)


def _unpack2(w):
    re = lax.bitcast_convert_type(w & jnp.uint32(0xFFFF0000), F32)
    im = lax.bitcast_convert_type(w << 16, F32)
    return jnp.concatenate([re, im], axis=0).astype(BF16)


def _hy_conv_kernel(y_ref, x_ref, kh_ref, bias_ref, f1_ref, f2_ref, f2i_ref, f1i_ref, o_ref, g_ref, h_ref,
                    *, n1, n2, natural_out):
    h1 = n1 // 2
    l_len = h1 * n2
    pg, ph = n1 + HY_PITCH_PAD, n2 + HY_PITCH_PAD
    bias = bias_ref[...]

    def rows(half, i2):
        return pl.ds(pl.multiple_of(half * l_len + i2 * h1, h1), h1)

    def stage1(i2, _):
        x = jnp.concatenate([y_ref[rows(0, i2), :], y_ref[rows(1, i2), :]], axis=0).astype(BF16)
        a = _dot(f1_ref[i2], x)
        g_ref[pl.ds(pl.multiple_of(i2 * pg, 8), n1), :] = _pack2(a[0:n1, :], a[n1:, :])
        return 0
    lax.fori_loop(0, n2, stage1, 0, unroll=2 * HY_UNROLL)

    def stage23(k1, _):
        x = _unpack2(g_ref[pl.ds(k1, n2, stride=pg), :])
        yh = _dot(f2_ref[...], x)
        base = pl.multiple_of(k1 * 2 * n2, 2 * n2)
        kr = kh_ref[pl.ds(base, n2), :]
        ki = kh_ref[pl.ds(base + n2, n2), :]
        yr, yi = yh[0:n2, :], yh[n2:, :]
        z = jnp.concatenate([yr * kr - yi * ki, yr * ki + yi * kr], axis=0).astype(BF16)
        c = _dot(f2i_ref[...], z)
        h_ref[pl.ds(pl.multiple_of(k1 * ph, 8), n2), :] = _pack2(c[0:n2, :], c[n2:, :])
        return 0
    lax.fori_loop(0, n1, stage23, 0, unroll=HY_UNROLL)

    def stage4(i2, _):
        x = _unpack2(h_ref[pl.ds(i2, n1, stride=ph), :])
        yc = _dot(f1i_ref[i2], x)
        for half in range(2):
            sl = rows(half, i2)
            val = x_ref[sl, :] * (yc[half * h1:(half + 1) * h1, :] + bias * y_ref[sl, :])
            if natural_out:
                o_ref[pl.ds(half * l_len + i2, h1, stride=n2), :] = val
            else:
                o_ref[sl, :] = val
        return 0
    lax.fori_loop(0, n2, stage4, 0, unroll=HY_UNROLL)


def _hy_conv(ysrc, ycol, xsrc, xcol, khat, order, bias, mats, bsz, n1, n2, natural_out):
    l_len = ysrc.shape[0] // bsz
    nsl = khat.shape[1]
    f1, f2, f2i, f1i = mats
    once = pl.Buffered(1)
    cst2 = lambda a: pl.BlockSpec(a.shape, lambda j, p: (0, 0), pipeline_mode=once)
    cst3 = lambda a: pl.BlockSpec(a.shape, lambda j, p: (0, 0, 0), pipeline_mode=once)
    return pl.pallas_call(
        functools.partial(_hy_conv_kernel, n1=n1, n2=n2, natural_out=natural_out),
        grid=(nsl, bsz // 2),
        in_specs=[pl.BlockSpec((2 * l_len, LANES), lambda j, p: (p, ycol + j)),
                  pl.BlockSpec((2 * l_len, LANES), lambda j, p: (p, xcol + j)),
                  pl.BlockSpec((None, None, khat.shape[2], LANES), lambda j, p: (order, j, 0, 0),
                               pipeline_mode=once),
                  pl.BlockSpec((1, LANES), lambda j, p: (0, j)),
                  cst3(f1), cst2(f2), cst2(f2i), cst3(f1i)],
        out_specs=pl.BlockSpec((2 * l_len, LANES), lambda j, p: (p, j)),
        out_shape=jax.ShapeDtypeStruct((ysrc.shape[0], nsl * LANES), F32),
        scratch_shapes=[pltpu.VMEM((n2 * (n1 + HY_PITCH_PAD), LANES), jnp.uint32),
                        pltpu.VMEM((n1 * (n2 + HY_PITCH_PAD), LANES), jnp.uint32)],
        compiler_params=_params(("parallel", "arbitrary")),
        name="hyena_long_conv",
    )(ysrc, xsrc, khat, bias[order][None], f1, f2, f2i, f1i)


def _dft_tables(l_len):
    n = 2 * l_len
    n2 = 128 if l_len >= 1024 else 32
    n1 = n // n2
    h1 = n1 // 2
    i1 = np.arange(n1)[None, None, :]
    k1 = np.arange(n1)[None, :, None]
    i2 = np.arange(n2)[:, None, None]
    ph = 2 * np.pi * (i1 * k1 / n1 + i2 * k1 / n)
    c, s = np.cos(ph), np.sin(ph)
    ch, sh = c[:, :, :h1], s[:, :, :h1]
    f1 = np.concatenate([np.concatenate([ch, sh], 2), np.concatenate([-sh, ch], 2)], 1)
    f1k = np.concatenate([c, -s], 1)
    ct, st = np.swapaxes(ch, 1, 2), np.swapaxes(sh, 1, 2)
    f1i = np.concatenate([np.concatenate([ct, -st], 2), np.concatenate([st, ct], 2)], 1) / n
    a = np.arange(n2)
    ph2 = 2 * np.pi * np.outer(a, a) / n2
    c2, s2 = np.cos(ph2), np.sin(ph2)
    f2 = np.block([[c2, s2], [-s2, c2]])
    f2i = np.block([[c2, -s2], [s2, c2]])
    bf = lambda m: jnp.asarray(m, dtype=F32).astype(BF16)
    return n1, n2, (bf(f1), bf(f2), bf(f2i), bf(f1i)), jnp.asarray(f1k, F32), jnp.asarray(f2, F32)


def _filter_positions(l_len, width):
    lag = np.concatenate([np.arange(l_len), l_len - np.arange(l_len)]).astype(np.float64)
    t = lag / (l_len - 1)
    bands = (HY_EMB - 1) // 2
    w = 2.0 * np.pi * lag / l_len
    f = np.linspace(1e-4, bands - 1, bands)[None]
    z = np.concatenate([t[:, None], np.cos(f * w[:, None]), -np.sin(f * w[:, None])], axis=-1)
    z = np.pad(z, ((0, 0), (0, HY_EMB_PAD - HY_EMB)))
    tn = np.repeat(t[:, None], LANES, axis=1)
    max_decay = math.log(HY_DECAY_TARGET) / HY_FAST_PCT
    min_decay = math.log(HY_DECAY_TARGET) / HY_SLOW_PCT
    deltas = np.linspace(min_decay, max_decay, width)[None]
    return jnp.asarray(z, F32), jnp.asarray(tn, F32), jnp.asarray(deltas, F32)


def _rope_tables(seq):
    n_rows = seq // GRID_W
    row = np.repeat(np.arange(n_rows), GRID_W).astype(np.float64)
    col = np.tile(np.arange(GRID_W), n_rows).astype(np.float64)
    half = HEAD_DIM // 2
    inv = ROPE_BASE ** (-np.arange(0, half, 2, dtype=np.float64) / half)
    ar, ac = row[:, None] * inv, col[:, None] * inv
    cos = np.concatenate([np.cos(ar), np.cos(ar), np.cos(ac), np.cos(ac)], axis=1)
    sin = np.concatenate([-np.sin(ar), np.sin(ar), -np.sin(ac), np.sin(ac)], axis=1)
    return jnp.asarray(cos, F32), jnp.asarray(sin, F32)


def _hyena_spectra(l_len, width, w1, b1, w2, b2, w3, freq, tables):
    n1, n2, _, f1k, f2f = tables
    zfull, tn_full, deltas = _filter_positions(l_len, width)
    w1p = jnp.pad(w1, ((0, HY_EMB_PAD - HY_EMB), (0, 0)))
    hid = _hy_hidden(zfull, w1p, b1[None], w2, b2[None], freq[None])
    kern = _hy_filters(hid, w3, tn_full, deltas)
    return _hy_kfft(kern, f1k, f2f, n1, n2)


def _hyena_seq(zc, bsz, khat, bias, tables):
    n1, n2, mats, _, _ = tables
    nsl = khat.shape[1]
    y1 = _hy_conv(zc, 0, zc, nsl, khat, 0, bias, mats, bsz, n1, n2, False)
    return _hy_conv(y1, 0, zc, 2 * nsl, khat, 1, bias, mats, bsz, n1, n2, True)


def kernel(x, c, ctx, c_ctx, mod_w, mod_b, norm_pre, norm_post, ev_w_in, ev_w_out, lru_conv_w, lru_conv_b, lru_wa, lru_ba, lru_wx, lru_bx, lru_lambda, attn_sink, od_w_in, od_w_out, hy_conv_w, hy_conv_b, hy_w1, hy_b1, hy_w2, hy_b2, hy_w3, hy_freq, hy_bias, ret_decay_logit):
    bsz, s_len, d = x.shape
    c_len = ctx.shape[1]
    depth = mod_w.shape[0]
    assert bsz % 2 == 0 and bsz <= 16 and s_len % 1024 == 0 and c_len % 256 == 0 and c_len <= s_len

    crows = jnp.concatenate([c, c_ctx[None], jnp.zeros((24 - bsz - 1, d), F32)], axis=0)
    mod = _modulation(crows, mod_w, mod_b)

    cos_t, sin_t = _rope_tables(s_len)
    tab_l = _dft_tables(s_len)
    tab_c = _dft_tables(c_len)

    x2 = x.reshape(bsz * s_len, d)
    ctx2 = ctx.reshape(bsz * c_len, d)
    tm = 1024

    for l in range(depth):
        need_ctx = l < depth - 1
        shift, scale, gate = (mod[l, :bsz, i * d:(i + 1) * d].reshape(bsz, 1, d) for i in range(3))
        shift_c, scale_c, gate_c = (mod[l, bsz:bsz + 1, i * d:(i + 1) * d].reshape(1, 1, d) for i in range(3))
        g_pre = norm_pre[l][None]
        g_post = norm_post[l][None]

        def proj(w, dtype, tn):
            wb = w.astype(BF16)
            p = _in_proj(x2, g_pre, scale, shift, s_len, wb, dtype, tm, tn)
            pc = _in_proj(ctx2, g_pre, scale_c, shift_c, bsz * c_len, wb, dtype, tm, tn)
            return p, pc

        if l % 2 == 0:
            e = l // 2
            w_in = ev_w_in[e]
            xa, xac = proj(w_in[:, 0:1024], F32, 1024)
            w_rest = jnp.concatenate([w_in[:, 1024:2048], w_in[:, 2048:3072], w_in[:, 3584:4608],
                                      w_in[:, 3072:3584]], axis=1)
            pr, prc = proj(w_rest, BF16, 1792)
            cols = {"q": 1, "gb": 2, "k": 12, "v": 13}
            nblk = lru_wa.shape[2]
            wg = jnp.stack([lru_wa[e, 0], lru_wx[e, 0], lru_wa[e, 1], lru_wx[e, 1]], axis=1)
            wg = jnp.transpose(wg, (0, 2, 1, 3)).reshape(nblk, LRU_BLOCK_W, 4 * LRU_BLOCK_W).astype(BF16)
            bg = jnp.stack([lru_ba[e, 0], lru_bx[e, 0], lru_ba[e, 1], lru_bx[e, 1]], axis=0)
            bg = jnp.transpose(bg.reshape(4, nblk, LRU_BLOCK_W), (1, 0, 2)).reshape(nblk, 1, 4 * LRU_BLOCK_W)
            ya, yac = _lru(xa, xac, bsz, lru_conv_w[e], lru_conv_b[e][None], wg, bg, lru_lambda[e])
            yb = _attention(attn_sink[e], pr, prc, bsz, cols, cos_t, sin_t)
            w_out = ev_w_out[e].astype(BF16)
            x2_new = _out_proj(ya, 0, pr, 0, yb, 0, w_out, x2, g_post, gate, s_len, tm)
            if need_ctx:
                ybc = _ctx_attention(attn_sink[e], prc, bsz, cols)
                ctx2 = _out_proj(yac, 0, prc, 0, ybc, 0, w_out, ctx2, g_post, gate_c, bsz * c_len, tm)
            x2 = x2_new
        else:
            o = l // 2
            w_in = od_w_in[o]
            z, zc = proj(w_in[:, 0:3072], F32, 1024)
            pr, prc = proj(w_in[:, 3072:8192], BF16, 2560)
            cols = {"q": 8, "k": 16, "v": 24, "gd": 32}
            width = hy_bias.shape[2]
            fargs = (hy_w1[o], hy_b1[o], hy_w2[o], hy_b2[o], hy_w3[o], hy_freq[o])
            khat = _hyena_spectra(s_len, width, *fargs, tab_l)
            zconv = _hy_prep(z, bsz, hy_conv_w[o], hy_conv_b[o][None], tab_l[1])
            yh = _hyena_seq(zconv, bsz, khat, hy_bias[o], tab_l)
            dl = jnp.broadcast_to(jnp.transpose(ret_decay_logit[o])[:, :, None], (RET_HEADS, 2, LANES))
            yd, ydc = _retention(dl, pr, prc, bsz, cols)
            w_out = od_w_out[o].astype(BF16)
            x2_new = _out_proj(yh, 0, pr, 0, yd, 0, w_out, x2, g_post, gate, s_len, tm)
            if need_ctx:
                khat_c = _hyena_spectra(c_len, width, *fargs, tab_c)
                zcconv = _hy_prep(zc, bsz, hy_conv_w[o], hy_conv_b[o][None], tab_c[1])
                yhc = _hyena_seq(zcconv, bsz, khat_c, hy_bias[o], tab_c)
                ctx2 = _out_proj(yhc, 0, prc, 0, ydc, 0, w_out, ctx2, g_post, gate_c, bsz * c_len, tm)
            x2 = x2_new
    return x2.reshape(bsz, s_len, d)
```

```python
import functools
import math

import numpy as np
import jax
import jax.numpy as jnp
from jax import lax
from jax.experimental import pallas as pl
from jax.experimental.pallas import tpu as pltpu

F32 = jnp.float32
BF16 = jnp.bfloat16
HIGHEST = lax.Precision.HIGHEST

EPS = 1e-6
GRID_W = 64
LANES = 128
LRU_BLOCK_W = 128
LRU_CONV = 4
LRU_C = 8.0
LRU_SEG = 16
LRU_POS = 16
LRU_TINY = 1e-30
LRU_HALO = 16
ATT_HEADS = 8
ATT_KV_HEADS = 2
HEAD_DIM = 128
BLOCK = 128
ROPE_BASE = 10000.0
HY_ORDER = 2
HY_EMB = 33
HY_EMB_PAD = 40
HY_DECAY_TARGET = 1e-2
HY_FAST_PCT = 0.3
HY_SLOW_PCT = 1.5
HY_UNROLL = 4
HY_HALO = 16
HY_PITCH_PAD = 8
RET_HEADS = 8
RET_DK = 128
RET_CHUNK = 128

VMEM_LIMIT = 56 * 1024 * 1024
NEG = -1e30


def _params(sem, vmem=VMEM_LIMIT):
    return pltpu.CompilerParams(dimension_semantics=sem, vmem_limit_bytes=vmem)


def _sigmoid(v):
    return 0.5 * (jnp.tanh(0.5 * v) + 1.0)


def _silu(v):
    return v * _sigmoid(v)


def _softplus(v):
    return jnp.maximum(v, 0.0) + jnp.log(1.0 + jnp.exp(-jnp.abs(v)))


def _dot(a, b, **kw):
    return jnp.dot(a, b, preferred_element_type=F32, **kw)


def _dot_nt(a, b):
    return lax.dot_general(a, b, (((1,), (1,)), ((), ())), preferred_element_type=F32)


def _dot_tn(a, b):
    return lax.dot_general(a, b, (((0,), (0,)), ((), ())), preferred_element_type=F32)


def _mod_kernel(c_ref, w_ref, b_ref, o_ref):
    s = _silu(c_ref[...])
    o_ref[0] = _dot(s, w_ref[0], precision=HIGHEST) + b_ref[0]


def _modulation(crows, mod_w, mod_b):
    depth, d, n3 = mod_w.shape
    r = crows.shape[0]
    tn = 1024
    return pl.pallas_call(
        _mod_kernel,
        grid=(depth, n3 // tn),
        in_specs=[pl.BlockSpec((r, d), lambda l, j: (0, 0)),
                  pl.BlockSpec((1, d, tn), lambda l, j: (l, 0, j)),
                  pl.BlockSpec((1, 1, tn), lambda l, j: (l, 0, j))],
        out_specs=pl.BlockSpec((1, r, tn), lambda l, j: (l, 0, j)),
        out_shape=jax.ShapeDtypeStruct((depth, r, n3), F32),
        compiler_params=_params(("parallel", "parallel")),
        name="modulation",
    )(crows, mod_w, mod_b.reshape(depth, 1, n3))


def _inproj_kernel(x_ref, g_ref, sc_ref, sh_ref, w_ref, o_ref, h_ref):
    @pl.when(pl.program_id(1) == 0)
    def _():
        x = x_ref[...]
        y = x * lax.rsqrt(jnp.mean(x * x, axis=-1, keepdims=True) + EPS) * g_ref[...]
        h_ref[...] = (y * (1.0 + sc_ref[0]) + sh_ref[0]).astype(BF16)

    o_ref[...] = _dot(h_ref[...], w_ref[...]).astype(o_ref.dtype)


def _in_proj(x2, g, scale, shift, rows_per_group, w, out_dtype, tm, tn):
    m, d = x2.shape
    n = w.shape[1]
    tpg = rows_per_group // tm
    return pl.pallas_call(
        _inproj_kernel,
        grid=(m // tm, n // tn),
        in_specs=[pl.BlockSpec((tm, d), lambda i, j: (i, 0)),
                  pl.BlockSpec((1, d), lambda i, j: (0, 0)),
                  pl.BlockSpec((1, 1, d), lambda i, j: (i // tpg, 0, 0)),
                  pl.BlockSpec((1, 1, d), lambda i, j: (i // tpg, 0, 0)),
                  pl.BlockSpec((d, tn), lambda i, j: (0, j))],
        out_specs=pl.BlockSpec((tm, tn), lambda i, j: (i, j)),
        out_shape=jax.ShapeDtypeStruct((m, n), out_dtype),
        scratch_shapes=[pltpu.VMEM((tm, d), BF16)],
        compiler_params=_params(("parallel", "arbitrary")),
        name="in_proj",
    )(x2, g, scale, shift, w)


def _outproj_kernel(a_ref, ga_ref, b_ref, w_ref, x_ref, g_ref, gate_ref, o_ref):
    wa = w_ref.shape[0] // 2
    a = (a_ref[...].astype(F32) * _silu(ga_ref[...].astype(F32))).astype(BF16)
    y = _dot(a, w_ref[0:wa, :]) + _dot(b_ref[...], w_ref[wa:, :])
    yn = y * lax.rsqrt(jnp.mean(y * y, axis=-1, keepdims=True) + EPS) * g_ref[...]
    o_ref[...] = x_ref[...] + gate_ref[0] * yn


def _out_proj(a, a_col, ga, ga_col, b, b_col, w, x2, g, gate, rows_per_group, tm):
    m, d = x2.shape
    wa = w.shape[0] // 2
    tpg = rows_per_group // tm
    return pl.pallas_call(
        _outproj_kernel,
        grid=(m // tm,),
        in_specs=[pl.BlockSpec((tm, wa), lambda i: (i, a_col)),
                  pl.BlockSpec((tm, wa), lambda i: (i, ga_col)),
                  pl.BlockSpec((tm, wa), lambda i: (i, b_col)),
                  pl.BlockSpec(w.shape, lambda i: (0, 0)),
                  pl.BlockSpec((tm, d), lambda i: (i, 0)),
                  pl.BlockSpec((1, d), lambda i: (0, 0)),
                  pl.BlockSpec((1, 1, d), lambda i: (i // tpg, 0, 0))],
        out_specs=pl.BlockSpec((tm, d), lambda i: (i, 0)),
        out_shape=jax.ShapeDtypeStruct((m, d), F32),
        compiler_params=_params(("parallel",)),
        name="out_proj",
    )(a, ga, b, w, x2, g, gate)


def _lru_kernel(xa_ref, xac_ref, cw_ref, cb_ref, wg_ref, bg_ref, lam_ref, y_ref, yc_ref,
                xp_ref, hf_ref, pf_ref, hb_ref, pb_ref):
    nseg = LRU_SEG
    npos = LRU_POS
    c8 = -LRU_C * _softplus(-lam_ref[...])
    left = LRU_CONV // 2
    cw = [cw_ref[k:k + 1, :] for k in range(LRU_CONV)]
    cb = cb_ref[...]
    zeros_h = jnp.zeros((LRU_HALO, LANES), F32)
    zeros_s = jnp.zeros((nseg, LANES), F32)
    ones_s = jnp.ones((nseg, LANES), F32)

    def run(src_ref, out_ref, n, c0f, c0b):
        seg = n // nseg
        nblk = seg // npos
        halo = LRU_HALO
        pitch = seg + 2 * halo + 8
        for j in range(nseg):
            lo, hi = j * seg - halo, (j + 1) * seg + halo
            if lo < 0:
                xp_ref[j * pitch:j * pitch + halo, :] = zeros_h
            if hi > n:
                xp_ref[j * pitch + seg + halo:j * pitch + seg + 2 * halo, :] = zeros_h
            lo_c, hi_c = max(lo, 0), min(hi, n)
            xp_ref[j * pitch + (lo_c - lo):j * pitch + (hi_c - lo), :] = src_ref[lo_c:hi_c, :].astype(F32)

        def gather(q):
            return jnp.concatenate([xp_ref[pl.ds(halo + q + c * 8 * pitch, 8, stride=pitch), :]
                                    for c in range(nseg // 8)], axis=0)

        def blk(i):
            return pl.ds(pl.multiple_of(i * npos * nseg, npos * nseg), npos * nseg)

        def pos(v, p):
            return v[p * nseg:(p + 1) * nseg, :]

        def fwd_body(i, carry):
            h, pc = carry
            p0 = i * npos
            xs = [gather(p0 + q - left) for q in range(npos + LRU_CONV - 1)]
            us = []
            for p in range(npos):
                u = cb + cw[0] * xs[p]
                for k in range(1, LRU_CONV):
                    u = u + cw[k] * xs[p + k]
                us.append(u)
            u = jnp.concatenate(us, axis=0)
            g = _dot(u.astype(BF16), wg_ref[0]) + bg_ref[0]
            coef = []
            for d in range(2):
                r = _sigmoid(g[:, (2 * d) * LANES:(2 * d + 1) * LANES])
                gi = _sigmoid(g[:, (2 * d + 1) * LANES:(2 * d + 2) * LANES])
                a = jnp.exp(c8[d:d + 1, :] * r)
                om = 1.0 - a * a
                coef.append((a, om * lax.rsqrt(jnp.maximum(om, LRU_TINY)) * (gi * u)))
            (af, bf), (ab, bb) = coef
            hb_ref[blk(i), :] = ab
            pb_ref[blk(i), :] = bb
            hs, ps = [], []
            for p in range(npos):
                a_p = pos(af, p)
                h = a_p * h + pos(bf, p)
                pc = a_p * pc
                hs.append(h)
                ps.append(pc)
            hf_ref[blk(i), :] = jnp.concatenate(hs, axis=0)
            pf_ref[blk(i), :] = jnp.concatenate(ps, axis=0)
            return h, pc

        hef, pef = lax.fori_loop(0, nblk, fwd_body, (zeros_s, ones_s))

        def bwd_body(ii, carry):
            h, pc = carry
            i = nblk - 1 - ii
            ab = hb_ref[blk(i), :]
            bb = pb_ref[blk(i), :]
            hs, ps = [None] * npos, [None] * npos
            for p in reversed(range(npos)):
                a_p = pos(ab, p)
                h = a_p * h + pos(bb, p)
                pc = a_p * pc
                hs[p] = h
                ps[p] = pc
            hb_ref[blk(i), :] = jnp.concatenate(hs, axis=0)
            pb_ref[blk(i), :] = jnp.concatenate(ps, axis=0)
            return h, pc

        heb, peb = lax.fori_loop(0, nblk, bwd_body, (zeros_s, ones_s))

        rows_f, c = [], c0f
        for j in range(nseg):
            rows_f.append(c)
            c = hef[j:j + 1, :] + pef[j:j + 1, :] * c
        final_f = c
        rows_b, c = [None] * nseg, c0b
        for j in reversed(range(nseg)):
            rows_b[j] = c
            c = heb[j:j + 1, :] + peb[j:j + 1, :] * c
        final_b = c
        cf = jnp.concatenate(rows_f, axis=0)
        cbk = jnp.concatenate(rows_b, axis=0)

        def out_body(p, _):
            rs = pl.ds(pl.multiple_of(p * nseg, nseg), nseg)
            y = hf_ref[rs, :] + pf_ref[rs, :] * cf + hb_ref[rs, :] + pb_ref[rs, :] * cbk
            for c in range(nseg // 8):
                out_ref[pl.ds(p + c * 8 * seg, 8, stride=seg), :] = y[c * 8:(c + 1) * 8, :]
            return 0
        lax.fori_loop(0, seg, out_body, 0, unroll=8)
        return final_f, final_b

    zero = jnp.zeros((1, LANES), F32)
    ff, fb = run(xac_ref, yc_ref, xac_ref.shape[0], zero, zero)
    run(xa_ref, y_ref, xa_ref.shape[0], ff, fb)


def _lru(proj, projc, xa_col, bsz, conv_w, conv_b, wg, bg, lam):
    w = conv_w.shape[1]
    s_len = proj.shape[0] // bsz
    c_len = projc.shape[0] // bsz
    nblk = w // LANES
    seq = lambda n: pl.BlockSpec((n, LANES), lambda b, j: (b, j))
    src = lambda n: pl.BlockSpec((n, LANES), lambda b, j: (b, xa_col + j))
    return pl.pallas_call(
        _lru_kernel,
        grid=(bsz, nblk),
        in_specs=[src(s_len), src(c_len),
                  pl.BlockSpec((LRU_CONV, LANES), lambda b, j: (0, j)),
                  pl.BlockSpec((1, LANES), lambda b, j: (0, j)),
                  pl.BlockSpec((1, LANES, 4 * LANES), lambda b, j: (j, 0, 0)),
                  pl.BlockSpec((1, 1, 4 * LANES), lambda b, j: (j, 0, 0)),
                  pl.BlockSpec((2, LANES), lambda b, j: (0, j))],
        out_specs=[seq(s_len), seq(c_len)],
        out_shape=[jax.ShapeDtypeStruct((proj.shape[0], w), F32), jax.ShapeDtypeStruct((projc.shape[0], w), F32)],
        scratch_shapes=[pltpu.VMEM((s_len + (2 * LRU_HALO + 8) * LRU_SEG, LANES), F32)]
        + [pltpu.VMEM((s_len, LANES), F32)] * 4,
        compiler_params=_params(("parallel", "parallel")),
        name="rglru",
    )(proj, projc, conv_w, conv_b, wg, bg, lam)


def _rope(v, cos, sin, lane):
    swapped = jnp.where((lane & 63) < 32, pltpu.roll(v, LANES - 32, 1), pltpu.roll(v, 32, 1))
    return v * cos + swapped * sin


def _attn_kernel(sink_ref, q_ref, gb_ref, k_ref, v_ref, kc_ref, vc_ref, cos_ref, sin_ref, bias_ref, o_ref):
    qb = pl.program_id(1)
    nb = pl.num_programs(1)
    group = ATT_HEADS // ATT_KV_HEADS
    scale = HEAD_DIM ** -0.5
    lane = lax.broadcasted_iota(jnp.int32, (BLOCK, LANES), 1)

    def blk(i):
        return pl.ds(pl.multiple_of(i * BLOCK, BLOCK), BLOCK)

    ip = jnp.maximum(qb - 1, 0)
    inx = jnp.minimum(qb + 1, nb - 1)
    cos_o, sin_o = cos_ref[blk(qb), :], sin_ref[blk(qb), :]
    cos_p, sin_p = cos_ref[blk(ip), :], sin_ref[blk(ip), :]
    cos_n, sin_n = cos_ref[blk(inx), :], sin_ref[blk(inx), :]

    bias = jnp.concatenate([bias_ref[0]] * group, axis=0)

    for h in range(ATT_KV_HEADS):
        ksl = slice(h * HEAD_DIM, (h + 1) * HEAD_DIM)
        kp = _rope(k_ref[blk(ip), ksl].astype(F32), cos_p, sin_p, lane)
        ko = _rope(k_ref[blk(qb), ksl].astype(F32), cos_o, sin_o, lane)
        kn = _rope(k_ref[blk(inx), ksl].astype(F32), cos_n, sin_n, lane)
        kcat = jnp.concatenate([kp.astype(BF16), ko.astype(BF16), kn.astype(BF16), kc_ref[:, ksl]], axis=0)
        vcat = jnp.concatenate([v_ref[blk(ip), ksl], v_ref[blk(qb), ksl], v_ref[blk(inx), ksl],
                                vc_ref[:, ksl]], axis=0)
        qs, sinks = [], []
        for g in range(group):
            hh = h * group + g
            qh = q_ref[:, hh * HEAD_DIM:(hh + 1) * HEAD_DIM].astype(F32)
            qs.append((_rope(qh, cos_o, sin_o, lane) * scale).astype(BF16))
            sinks.append(jnp.full((BLOCK, 1), sink_ref[hh], F32))
        q4 = jnp.concatenate(qs, axis=0)
        sk = jnp.concatenate(sinks, axis=0)
        s = _dot_nt(q4, kcat) + bias
        m = jnp.maximum(jnp.max(s, axis=-1, keepdims=True), sk)
        p = jnp.exp(s - m)
        denom = jnp.exp(sk - m) + jnp.sum(p, axis=-1, keepdims=True)
        o = _dot(p.astype(BF16), vcat) / denom
        for g in range(group):
            hh = h * group + g
            hs = slice(hh * HEAD_DIM, (hh + 1) * HEAD_DIM)
            gate = _silu(gb_ref[:, hs].astype(F32))
            o_ref[:, hs] = (o[g * BLOCK:(g + 1) * BLOCK, :] * gate).astype(o_ref.dtype)


def _attention(sink, proj, projc, bsz, cols, cos_t, sin_t):
    s_len = proj.shape[0] // bsz
    c_len = projc.shape[0] // bsz
    nb = s_len // BLOCK
    aw = ATT_HEADS * HEAD_DIM
    kw = ATT_KV_HEADS * HEAD_DIM
    qi = np.arange(BLOCK)[:, None]
    kj = np.arange(BLOCK)[None, :]
    variants = []
    for v in range(4):
        prev = np.where((kj >= qi) & bool(v & 1), 0.0, NEG)
        nxt = np.where((kj <= qi) & bool(v & 2), 0.0, NEG)
        variants.append(np.concatenate([prev, np.zeros((BLOCK, BLOCK)), nxt, np.zeros((BLOCK, c_len))], axis=1))
    bias = jnp.asarray(np.stack(variants), F32)
    ncol = 3 * BLOCK + c_len

    def variant(b, i):
        return ((i > 0).astype(jnp.int32) + 2 * (i < nb - 1).astype(jnp.int32), 0, 0)

    return pl.pallas_call(
        _attn_kernel,
        grid=(bsz, nb),
        in_specs=[pl.BlockSpec(memory_space=pltpu.SMEM),
                  pl.BlockSpec((BLOCK, aw), lambda b, i: (b * nb + i, cols["q"])),
                  pl.BlockSpec((BLOCK, aw), lambda b, i: (b * nb + i, cols["gb"])),
                  pl.BlockSpec((s_len, kw), lambda b, i: (b, cols["k"])),
                  pl.BlockSpec((s_len, kw), lambda b, i: (b, cols["v"])),
                  pl.BlockSpec((c_len, kw), lambda b, i: (b, cols["k"])),
                  pl.BlockSpec((c_len, kw), lambda b, i: (b, cols["v"])),
                  pl.BlockSpec((s_len, LANES), lambda b, i: (0, 0)),
                  pl.BlockSpec((s_len, LANES), lambda b, i: (0, 0)),
                  pl.BlockSpec((1, BLOCK, ncol), variant)],
        out_specs=pl.BlockSpec((BLOCK, aw), lambda b, i: (b * nb + i, 0)),
        out_shape=jax.ShapeDtypeStruct((proj.shape[0], aw), BF16),
        compiler_params=_params(("parallel", "arbitrary")),
        name="window_attention",
    )(sink, proj, proj, proj, proj, projc, projc, cos_t, sin_t, bias)


def _ctx_attn_kernel(sink_ref, q_ref, gb_ref, k_ref, v_ref, o_ref):
    group = ATT_HEADS // ATT_KV_HEADS
    scale = HEAD_DIM ** -0.5
    n = q_ref.shape[0]
    for h in range(ATT_KV_HEADS):
        ksl = slice(h * HEAD_DIM, (h + 1) * HEAD_DIM)
        qs, sinks = [], []
        for g in range(group):
            hh = h * group + g
            qs.append((q_ref[:, hh * HEAD_DIM:(hh + 1) * HEAD_DIM].astype(F32) * scale).astype(BF16))
            sinks.append(jnp.full((n, 1), sink_ref[hh], F32))
        q4 = jnp.concatenate(qs, axis=0)
        sk = jnp.concatenate(sinks, axis=0)
        s = _dot_nt(q4, k_ref[:, ksl])
        m = jnp.maximum(jnp.max(s, axis=-1, keepdims=True), sk)
        p = jnp.exp(s - m)
        denom = jnp.exp(sk - m) + jnp.sum(p, axis=-1, keepdims=True)
        o = _dot(p.astype(BF16), v_ref[:, ksl]) / denom
        for g in range(group):
            hh = h * group + g
            hs = slice(hh * HEAD_DIM, (hh + 1) * HEAD_DIM)
            o_ref[:, hs] = (o[g * n:(g + 1) * n, :] * _silu(gb_ref[:, hs].astype(F32))).astype(o_ref.dtype)


def _ctx_attention(sink, projc, bsz, cols):
    c_len = projc.shape[0] // bsz
    aw = ATT_HEADS * HEAD_DIM
    kw = ATT_KV_HEADS * HEAD_DIM
    return pl.pallas_call(
        _ctx_attn_kernel,
        grid=(bsz,),
        in_specs=[pl.BlockSpec(memory_space=pltpu.SMEM),
                  pl.BlockSpec((c_len, aw), lambda b: (b, cols["q"])),
                  pl.BlockSpec((c_len, aw), lambda b: (b, cols["gb"])),
                  pl.BlockSpec((c_len, kw), lambda b: (b, cols["k"])),
                  pl.BlockSpec((c_len, kw), lambda b: (b, cols["v"]))],
        out_specs=pl.BlockSpec((c_len, aw), lambda b: (b, 0)),
        out_shape=jax.ShapeDtypeStruct((projc.shape[0], aw), BF16),
        compiler_params=_params(("parallel",)),
        name="context_attention",
    )(sink, projc, projc, projc, projc)


def _ret_kernel(dl_ref, q_ref, k_ref, v_ref, gd_ref, qc_ref, kc_ref, vc_ref, gdc_ref, o_ref, oc_ref,
                acc_ref, accc_ref, u_ref, uc_ref, st_ref, stc_ref):
    c = RET_CHUNK
    s_len = q_ref.shape[0]
    c_len = qc_ref.shape[0]
    lg = -_softplus(-dl_ref[0])
    lgf, lgb = lg[0:1, :], lg[1:2, :]
    ri = lax.broadcasted_iota(jnp.int32, (c, c), 0)
    ci = lax.broadcasted_iota(jnp.int32, (c, c), 1)
    diff = (ri - ci).astype(F32)
    dmat = jnp.where(ri >= ci, jnp.exp(jnp.maximum(diff, 0.0) * lgf), jnp.exp(jnp.maximum(-diff, 0.0) * lgb))
    idx = ri.astype(F32)
    qdec_f = jnp.exp((idx + 1.0) * lgf)
    kdec_f = jnp.exp((c - 1.0 - idx) * lgf)
    qdec_b = jnp.exp((c - idx) * lgb)
    kdec_b = jnp.exp(idx * lgb)
    cdec_f = jnp.exp(c * lgf)
    cdec_b = jnp.exp(c * lgb)

    def chunk(j):
        return pl.ds(pl.multiple_of(j * c, c), c)

    def intra(qr, kr, vr, acc, u, n):
        def body(j, _):
            q, k, v = qr[chunk(j), :], kr[chunk(j), :], vr[chunk(j), :]
            att = (_dot_nt(q, k) * dmat).astype(BF16)
            acc[chunk(j), :] = _dot(att, v)
            kf = k.astype(F32)
            kcat = jnp.concatenate([(kf * kdec_f).astype(BF16), (kf * kdec_b).astype(BF16)], axis=1)
            u[j] = _dot_tn(kcat, v)
            return 0
        lax.fori_loop(0, n // c, body, 0, unroll=min(4, n // c))

    def states(u, st, n, sf, sb):
        nch = n // c

        def fbody(j, s):
            st[j, 0:c, :] = s.astype(BF16)
            return s * cdec_f + u[j, 0:c, :]
        sf = lax.fori_loop(0, nch, fbody, sf)

        def bbody(jj, s):
            j = nch - 1 - jj
            st[j, c:2 * c, :] = s.astype(BF16)
            return s * cdec_b + u[j, c:2 * c, :]
        sb = lax.fori_loop(0, nch, bbody, sb)
        return sf, sb

    def cross(qr, gr, acc, st, outr, n):
        def body(j, _):
            qf = qr[chunk(j), :].astype(F32)
            qcat = jnp.concatenate([(qf * qdec_f).astype(BF16), (qf * qdec_b).astype(BF16)], axis=1)
            o = (acc[chunk(j), :] + _dot(qcat, st[j])) * (RET_DK ** -0.5)
            o = o * lax.rsqrt(jnp.mean(o * o, axis=-1, keepdims=True) + EPS)
            outr[chunk(j), :] = (o * _silu(gr[chunk(j), :].astype(F32))).astype(outr.dtype)
            return 0
        lax.fori_loop(0, n // c, body, 0, unroll=min(4, n // c))

    intra(qc_ref, kc_ref, vc_ref, accc_ref, uc_ref, c_len)
    intra(q_ref, k_ref, v_ref, acc_ref, u_ref, s_len)
    zero = jnp.zeros((RET_DK, LANES), F32)
    sf, sb = states(uc_ref, stc_ref, c_len, zero, zero)
    states(u_ref, st_ref, s_len, sf, sb)
    cross(qc_ref, gdc_ref, accc_ref, stc_ref, oc_ref, c_len)
    cross(q_ref, gd_ref, acc_ref, st_ref, o_ref, s_len)


def _retention(dl, proj, projc, bsz, cols):
    s_len = proj.shape[0] // bsz
    c_len = projc.shape[0] // bsz
    hb = lambda n, col: pl.BlockSpec((n, LANES), lambda b, h: (b, col + h))
    return pl.pallas_call(
        _ret_kernel,
        grid=(bsz, RET_HEADS),
        in_specs=[pl.BlockSpec((1, 2, LANES), lambda b, h: (h, 0, 0)),
                  hb(s_len, cols["q"]), hb(s_len, cols["k"]), hb(s_len, cols["v"]), hb(s_len, cols["gd"]),
                  hb(c_len, cols["q"]), hb(c_len, cols["k"]), hb(c_len, cols["v"]), hb(c_len, cols["gd"])],
        out_specs=[pl.BlockSpec((s_len, LANES), lambda b, h: (b, h)),
                   pl.BlockSpec((c_len, LANES), lambda b, h: (b, h))],
        out_shape=[jax.ShapeDtypeStruct((proj.shape[0], RET_HEADS * LANES), BF16),
                   jax.ShapeDtypeStruct((projc.shape[0], RET_HEADS * LANES), BF16)],
        scratch_shapes=[pltpu.VMEM((s_len, LANES), F32), pltpu.VMEM((c_len, LANES), F32),
                        pltpu.VMEM((s_len // RET_CHUNK, 2 * RET_CHUNK, LANES), F32),
                        pltpu.VMEM((c_len // RET_CHUNK, 2 * RET_CHUNK, LANES), F32),
                        pltpu.VMEM((s_len // RET_CHUNK, 2 * RET_CHUNK, LANES), BF16),
                        pltpu.VMEM((c_len // RET_CHUNK, 2 * RET_CHUNK, LANES), BF16)],
        compiler_params=_params(("parallel", "parallel")),
        name="retention",
    )(dl, proj, proj, proj, proj, projc, projc, projc, projc)


def _hy_hid_kernel(z_ref, w1_ref, b1_ref, w2_ref, b2_ref, f_ref, o_ref):
    f = f_ref[...]
    h = jnp.sin(f * (_dot(z_ref[...], w1_ref[...], precision=HIGHEST) + b1_ref[...]))
    o_ref[...] = jnp.sin(f * (_dot(h, w2_ref[...], precision=HIGHEST) + b2_ref[...]))


def _hy_hidden(zfull, w1p, b1, w2, b2, freq):
    n, e = zfull.shape
    fd = w2.shape[0]
    tr = min(n, 1024)
    full = lambda shp: pl.BlockSpec(shp, lambda i: (0, 0))
    return pl.pallas_call(
        _hy_hid_kernel,
        grid=(n // tr,),
        in_specs=[pl.BlockSpec((tr, e), lambda i: (i, 0)), full((e, fd)), full((1, fd)), full((fd, fd)),
                  full((1, fd)), full((1, fd))],
        out_specs=pl.BlockSpec((tr, fd), lambda i: (i, 0)),
        out_shape=jax.ShapeDtypeStruct((n, fd), F32),
        compiler_params=_params(("parallel",)),
        name="hyena_filter_mlp",
    )(zfull, w1p, b1, w2, b2, freq)


def _hy_filt_kernel(hid_ref, w3f_ref, w3b_ref, tn_ref, dl_ref, o_ref):
    half = hid_ref.shape[0] // 2
    decay = jnp.exp(-tn_ref[...] * jnp.abs(dl_ref[...]))
    top = _dot(hid_ref[0:half, :], w3f_ref[...], precision=HIGHEST)
    bot = _dot(hid_ref[half:, :], w3b_ref[...], precision=HIGHEST)
    row = lax.broadcasted_iota(jnp.int32, bot.shape, 0)
    bot = jnp.where(row == 0, 0.0, bot)
    o_ref[0, 0:half, :] = top * decay[0:half, :]
    o_ref[0, half:, :] = bot * decay[half:, :]


def _hy_filters(hid, w3, tn_full, deltas):
    n, fd = hid.shape
    wch = deltas.shape[1]
    nsl = wch // LANES
    return pl.pallas_call(
        _hy_filt_kernel,
        grid=(HY_ORDER, nsl),
        in_specs=[pl.BlockSpec((n, fd), lambda o, j: (0, 0)),
                  pl.BlockSpec((fd, LANES), lambda o, j: (0, o * 2 * nsl + j)),
                  pl.BlockSpec((fd, LANES), lambda o, j: (0, o * 2 * nsl + nsl + j)),
                  pl.BlockSpec((n, LANES), lambda o, j: (0, 0)),
                  pl.BlockSpec((1, LANES), lambda o, j: (0, j))],
        out_specs=pl.BlockSpec((1, n, LANES), lambda o, j: (o, 0, j)),
        out_shape=jax.ShapeDtypeStruct((HY_ORDER, n, wch), F32),
        compiler_params=_params(("parallel", "parallel")),
        name="hyena_filter",
    )(hid, w3, w3, tn_full, deltas)


def _hy_kfft_kernel(kern_ref, f1_ref, f2_ref, o_ref, g_ref, *, n1, n2):
    pg = 2 * n1 + HY_PITCH_PAD

    def stage1(i2, _):
        x = kern_ref[pl.ds(i2, n1, stride=n2), :].astype(BF16)
        g_ref[pl.ds(pl.multiple_of(i2 * pg, 8), 2 * n1), :] = _dot(f1_ref[i2], x)
        return 0
    lax.fori_loop(0, n2, stage1, 0, unroll=HY_UNROLL)

    def stage2(k1, _):
        x = jnp.concatenate([g_ref[pl.ds(k1, n2, stride=pg), :],
                             g_ref[pl.ds(n1 + k1, n2, stride=pg), :]], axis=0).astype(BF16)
        o_ref[pl.ds(pl.multiple_of(k1 * 2 * n2, 2 * n2), 2 * n2), :] = _dot(f2_ref[...], x)
        return 0
    lax.fori_loop(0, n1, stage2, 0, unroll=HY_UNROLL // 2)


def _hy_kfft(kern, f1k, f2, n1, n2):
    orders, n, wch = kern.shape
    nsl = wch // LANES
    return pl.pallas_call(
        functools.partial(_hy_kfft_kernel, n1=n1, n2=n2),
        grid=(orders, nsl),
        in_specs=[pl.BlockSpec((None, n, LANES), lambda o, j: (o, 0, j)),
                  pl.BlockSpec(f1k.shape, lambda o, j: (0, 0, 0)),
                  pl.BlockSpec(f2.shape, lambda o, j: (0, 0))],
        out_specs=pl.BlockSpec((None, None, 2 * n, LANES), lambda o, j: (o, j, 0, 0)),
        out_shape=jax.ShapeDtypeStruct((orders, nsl, 2 * n, LANES), F32),
        scratch_shapes=[pltpu.VMEM((n2 * (2 * n1 + HY_PITCH_PAD), LANES), F32)],
        compiler_params=_params(("parallel", "parallel")),
        name="hyena_filter_fft",
    )(kern, f1k, f2)


def _hy_prep_kernel(z_ref, w_ref, b_ref, o_ref, zp_ref, *, n2):
    l_len = z_ref.shape[0]
    h1 = l_len // n2
    halo = HY_HALO
    pitch = n2 + 2 * halo + 8
    zeros_h = jnp.zeros((halo, LANES), F32)
    for i1 in range(h1):
        lo, hi = i1 * n2 - halo, (i1 + 1) * n2 + halo
        if lo < 0:
            zp_ref[i1 * pitch:i1 * pitch + halo, :] = zeros_h
        if hi > l_len:
            zp_ref[i1 * pitch + n2 + halo:i1 * pitch + n2 + 2 * halo, :] = zeros_h
        lo_c, hi_c = max(lo, 0), min(hi, l_len)
        zp_ref[i1 * pitch + (lo_c - lo):i1 * pitch + (hi_c - lo), :] = z_ref[lo_c:hi_c, :].astype(F32)
    w0, w1, w2, b = w_ref[0:1, :], w_ref[1:2, :], w_ref[2:3, :], b_ref[...]

    def tap(j):
        return zp_ref[pl.ds(halo + j, h1, stride=pitch), :]

    def body(i2, carry):
        zm, z0 = carry
        zn = tap(i2 + 1)
        val = b + w0 * zm + w1 * z0 + w2 * zn
        o_ref[pl.ds(pl.multiple_of(i2 * h1, h1), h1), :] = val.astype(o_ref.dtype)
        return z0, zn
    lax.fori_loop(0, n2, body, (tap(-1), tap(0)), unroll=HY_UNROLL)


def _hy_stage_dtype(l_len, n2):
    return BF16 if (l_len // n2) % 16 == 0 else F32


def _hy_prep(proj, z_col, nz, bsz, conv_w, conv_b, n2):
    l_len = proj.shape[0] // bsz
    h1 = l_len // n2
    return pl.pallas_call(
        functools.partial(_hy_prep_kernel, n2=n2),
        grid=(bsz, nz),
        in_specs=[pl.BlockSpec((l_len, LANES), lambda b, j: (b, z_col + j)),
                  pl.BlockSpec((3, LANES), lambda b, j: (0, j)),
                  pl.BlockSpec((1, LANES), lambda b, j: (0, j))],
        out_specs=pl.BlockSpec((l_len, LANES), lambda b, j: (b, j)),
        out_shape=jax.ShapeDtypeStruct((proj.shape[0], nz * LANES), _hy_stage_dtype(l_len, n2)),
        scratch_shapes=[pltpu.VMEM((h1 * (n2 + 2 * HY_HALO + 8), LANES), F32)],
        compiler_params=_params(("parallel", "parallel")),
        name="hyena_short_conv",
    )(proj, conv_w, conv_b)


def _pack2(re, im):
    r = lax.bitcast_convert_type(re, jnp.uint32) + jnp.uint32(0x8000)
    i = lax.bitcast_convert_type(im, jnp.uint32) + jnp.uint32(0x8000)
    return (r & jnp.uint32(0xFFFF0000)) | (i >> 16)


def _unpack2(w):
    re = lax.bitcast_convert_type(w & jnp.uint32(0xFFFF0000), F32)
    im = lax.bitcast_convert_type(w << 16, F32)
    return jnp.concatenate([re, im], axis=0).astype(BF16)


def _hy_conv_kernel(y_ref, x_ref, kh_ref, bias_ref, f1_ref, f2_ref, f2i_ref, f1i_ref, o_ref, g_ref, h_ref,
                    *, n1, n2, natural_out):
    h1 = n1 // 2
    l_len = h1 * n2
    pg, ph = n1 + HY_PITCH_PAD, n2 + HY_PITCH_PAD
    bias = bias_ref[...]

    def rows(half, i2):
        return pl.ds(pl.multiple_of(half * l_len + i2 * h1, h1), h1)

    def stage1(i2, _):
        x = jnp.concatenate([y_ref[rows(0, i2), :], y_ref[rows(1, i2), :]], axis=0).astype(BF16)
        a = _dot(f1_ref[i2], x)
        g_ref[pl.ds(pl.multiple_of(i2 * pg, 8), n1), :] = _pack2(a[0:n1, :], a[n1:, :])
        return 0
    lax.fori_loop(0, n2, stage1, 0, unroll=2 * HY_UNROLL)

    def stage23(k1, _):
        x = _unpack2(g_ref[pl.ds(k1, n2, stride=pg), :])
        yh = _dot(f2_ref[...], x)
        base = pl.multiple_of(k1 * 2 * n2, 2 * n2)
        kr = kh_ref[pl.ds(base, n2), :]
        ki = kh_ref[pl.ds(base + n2, n2), :]
        yr, yi = yh[0:n2, :], yh[n2:, :]
        z = jnp.concatenate([yr * kr - yi * ki, yr * ki + yi * kr], axis=0).astype(BF16)
        c = _dot(f2i_ref[...], z)
        h_ref[pl.ds(pl.multiple_of(k1 * ph, 8), n2), :] = _pack2(c[0:n2, :], c[n2:, :])
        return 0
    lax.fori_loop(0, n1, stage23, 0, unroll=HY_UNROLL)

    def stage4(i2, _):
        x = _unpack2(h_ref[pl.ds(i2, n1, stride=ph), :])
        yc = _dot(f1i_ref[i2], x)
        for half in range(2):
            sl = rows(half, i2)
            val = x_ref[sl, :].astype(F32) * (yc[half * h1:(half + 1) * h1, :] + bias * y_ref[sl, :].astype(F32))
            if natural_out:
                o_ref[pl.ds(half * l_len + i2, h1, stride=n2), :] = val
            else:
                o_ref[sl, :] = val.astype(o_ref.dtype)
        return 0
    lax.fori_loop(0, n2, stage4, 0, unroll=HY_UNROLL)


def _hy_conv(ysrc, ycol, xsrc, xcol, khat, order, bias, mats, bsz, n1, n2, natural_out):
    l_len = ysrc.shape[0] // bsz
    nsl = khat.shape[1]
    f1, f2, f2i, f1i = mats
    once = pl.Buffered(1)
    cst2 = lambda a: pl.BlockSpec(a.shape, lambda j, p: (0, 0), pipeline_mode=once)
    cst3 = lambda a: pl.BlockSpec(a.shape, lambda j, p: (0, 0, 0), pipeline_mode=once)
    return pl.pallas_call(
        functools.partial(_hy_conv_kernel, n1=n1, n2=n2, natural_out=natural_out),
        grid=(nsl, bsz // 2),
        in_specs=[pl.BlockSpec((2 * l_len, LANES), lambda j, p: (p, ycol + j)),
                  pl.BlockSpec((2 * l_len, LANES), lambda j, p: (p, xcol + j)),
                  pl.BlockSpec((None, None, khat.shape[2], LANES), lambda j, p: (order, j, 0, 0),
                               pipeline_mode=once),
                  pl.BlockSpec((1, LANES), lambda j, p: (0, j)),
                  cst3(f1), cst2(f2), cst2(f2i), cst3(f1i)],
        out_specs=pl.BlockSpec((2 * l_len, LANES), lambda j, p: (p, j)),
        out_shape=jax.ShapeDtypeStruct((ysrc.shape[0], nsl * LANES),
                                       F32 if natural_out else _hy_stage_dtype(l_len, n2)),
        scratch_shapes=[pltpu.VMEM((n2 * (n1 + HY_PITCH_PAD), LANES), jnp.uint32),
                        pltpu.VMEM((n1 * (n2 + HY_PITCH_PAD), LANES), jnp.uint32)],
        compiler_params=_params(("parallel", "arbitrary")),
        name="hyena_long_conv",
    )(ysrc, xsrc, khat, bias[order][None], f1, f2, f2i, f1i)


def _dft_tables(l_len):
    n = 2 * l_len
    n2 = 128 if l_len >= 1024 else 32
    n1 = n // n2
    h1 = n1 // 2
    i1 = np.arange(n1)[None, None, :]
    k1 = np.arange(n1)[None, :, None]
    i2 = np.arange(n2)[:, None, None]
    ph = 2 * np.pi * (i1 * k1 / n1 + i2 * k1 / n)
    c, s = np.cos(ph), np.sin(ph)
    ch, sh = c[:, :, :h1], s[:, :, :h1]
    f1 = np.concatenate([np.concatenate([ch, sh], 2), np.concatenate([-sh, ch], 2)], 1)
    f1k = np.concatenate([c, -s], 1)
    ct, st = np.swapaxes(ch, 1, 2), np.swapaxes(sh, 1, 2)
    f1i = np.concatenate([np.concatenate([ct, -st], 2), np.concatenate([st, ct], 2)], 1) / n
    a = np.arange(n2)
    ph2 = 2 * np.pi * np.outer(a, a) / n2
    c2, s2 = np.cos(ph2), np.sin(ph2)
    f2 = np.block([[c2, s2], [-s2, c2]])
    f2i = np.block([[c2, -s2], [s2, c2]])
    bf = lambda m: jnp.asarray(m, dtype=F32).astype(BF16)
    return n1, n2, (bf(f1), bf(f2), bf(f2i), bf(f1i)), bf(f1k)


def _filter_positions(l_len, width):
    lag = np.concatenate([np.arange(l_len), l_len - np.arange(l_len)]).astype(np.float64)
    t = lag / (l_len - 1)
    bands = (HY_EMB - 1) // 2
    w = 2.0 * np.pi * lag / l_len
    f = np.linspace(1e-4, bands - 1, bands)[None]
    z = np.concatenate([t[:, None], np.cos(f * w[:, None]), -np.sin(f * w[:, None])], axis=-1)
    z = np.pad(z, ((0, 0), (0, HY_EMB_PAD - HY_EMB)))
    tn = np.repeat(t[:, None], LANES, axis=1)
    max_decay = math.log(HY_DECAY_TARGET) / HY_FAST_PCT
    min_decay = math.log(HY_DECAY_TARGET) / HY_SLOW_PCT
    deltas = np.linspace(min_decay, max_decay, width)[None]
    return jnp.asarray(z, F32), jnp.asarray(tn, F32), jnp.asarray(deltas, F32)


def _rope_tables(seq):
    n_rows = seq // GRID_W
    row = np.repeat(np.arange(n_rows), GRID_W).astype(np.float64)
    col = np.tile(np.arange(GRID_W), n_rows).astype(np.float64)
    half = HEAD_DIM // 2
    inv = ROPE_BASE ** (-np.arange(0, half, 2, dtype=np.float64) / half)
    ar, ac = row[:, None] * inv, col[:, None] * inv
    cos = np.concatenate([np.cos(ar), np.cos(ar), np.cos(ac), np.cos(ac)], axis=1)
    sin = np.concatenate([-np.sin(ar), np.sin(ar), -np.sin(ac), np.sin(ac)], axis=1)
    return jnp.asarray(cos, F32), jnp.asarray(sin, F32)


def _hyena_spectra(l_len, width, w1, b1, w2, b2, w3, freq, tables):
    n1, n2, mats, f1k = tables
    zfull, tn_full, deltas = _filter_positions(l_len, width)
    w1p = jnp.pad(w1, ((0, HY_EMB_PAD - HY_EMB), (0, 0)))
    hid = _hy_hidden(zfull, w1p, b1[None], w2, b2[None], freq[None])
    kern = _hy_filters(hid, w3, tn_full, deltas)
    return _hy_kfft(kern, f1k, mats[1], n1, n2)


def _hyena_seq(zc, bsz, khat, bias, tables):
    n1, n2, mats, _ = tables
    nsl = khat.shape[1]
    y1 = _hy_conv(zc, 0, zc, nsl, khat, 0, bias, mats, bsz, n1, n2, False)
    return _hy_conv(y1, 0, zc, 2 * nsl, khat, 1, bias, mats, bsz, n1, n2, True)


def kernel(x, c, ctx, c_ctx, mod_w, mod_b, norm_pre, norm_post, ev_w_in, ev_w_out, lru_conv_w, lru_conv_b, lru_wa, lru_ba, lru_wx, lru_bx, lru_lambda, attn_sink, od_w_in, od_w_out, hy_conv_w, hy_conv_b, hy_w1, hy_b1, hy_w2, hy_b2, hy_w3, hy_freq, hy_bias, ret_decay_logit):
    bsz, s_len, d = x.shape
    c_len = ctx.shape[1]
    depth = mod_w.shape[0]
    assert bsz % 2 == 0 and bsz <= 16 and s_len % 1024 == 0 and c_len % 256 == 0 and c_len <= s_len

    crows = jnp.concatenate([c, c_ctx[None], jnp.zeros((24 - bsz - 1, d), F32)], axis=0)
    mod = _modulation(crows, mod_w, mod_b)

    cos_t, sin_t = _rope_tables(s_len)
    tab_l = _dft_tables(s_len)
    tab_c = _dft_tables(c_len)

    x2 = x.reshape(bsz * s_len, d)
    ctx2 = ctx.reshape(bsz * c_len, d)
    tm = 1024

    for l in range(depth):
        need_ctx = l < depth - 1
        shift, scale, gate = (mod[l, :bsz, i * d:(i + 1) * d].reshape(bsz, 1, d) for i in range(3))
        shift_c, scale_c, gate_c = (mod[l, bsz:bsz + 1, i * d:(i + 1) * d].reshape(1, 1, d) for i in range(3))
        g_pre = norm_pre[l][None]
        g_post = norm_post[l][None]

        def proj(w, tn):
            wb = w.astype(BF16)
            p = _in_proj(x2, g_pre, scale, shift, s_len, wb, BF16, tm, tn)
            pc = _in_proj(ctx2, g_pre, scale_c, shift_c, bsz * c_len, wb, BF16, tm, tn)
            return p, pc

        if l % 2 == 0:
            e = l // 2
            w_in = ev_w_in[e]
            w_ord = jnp.concatenate([w_in[:, 1024:2048], w_in[:, 2048:3072], w_in[:, 3584:4608],
                                     w_in[:, 3072:3584], w_in[:, 0:1024]], axis=1)
            pr, prc = proj(w_ord, 1536)
            cols = {"q": 1, "gb": 2, "k": 12, "v": 13}
            xa_col = 3584 // LANES
            nblk = lru_wa.shape[2]
            wg = jnp.stack([lru_wa[e, 0], lru_wx[e, 0], lru_wa[e, 1], lru_wx[e, 1]], axis=1)
            wg = jnp.transpose(wg, (0, 2, 1, 3)).reshape(nblk, LRU_BLOCK_W, 4 * LRU_BLOCK_W).astype(BF16)
            bg = jnp.stack([lru_ba[e, 0], lru_bx[e, 0], lru_ba[e, 1], lru_bx[e, 1]], axis=0)
            bg = jnp.transpose(bg.reshape(4, nblk, LRU_BLOCK_W), (1, 0, 2)).reshape(nblk, 1, 4 * LRU_BLOCK_W)
            ya, yac = _lru(pr, prc, xa_col, bsz, lru_conv_w[e], lru_conv_b[e][None], wg, bg, lru_lambda[e])
            yb = _attention(attn_sink[e], pr, prc, bsz, cols, cos_t, sin_t)
            w_out = ev_w_out[e].astype(BF16)
            x2_new = _out_proj(ya, 0, pr, 0, yb, 0, w_out, x2, g_post, gate, s_len, tm)
            if need_ctx:
                ybc = _ctx_attention(attn_sink[e], prc, bsz, cols)
                ctx2 = _out_proj(yac, 0, prc, 0, ybc, 0, w_out, ctx2, g_post, gate_c, bsz * c_len, tm)
            x2 = x2_new
        else:
            o = l // 2
            w_in = od_w_in[o]
            w_ord = jnp.concatenate([w_in[:, 3072:8192], w_in[:, 0:3072]], axis=1)
            pr, prc = proj(w_ord, 2048)
            cols = {"q": 8, "k": 16, "v": 24, "gd": 32}
            z_col = 5120 // LANES
            width = hy_bias.shape[2]
            nz = (HY_ORDER + 1) * width // LANES
            fargs = (hy_w1[o], hy_b1[o], hy_w2[o], hy_b2[o], hy_w3[o], hy_freq[o])
            khat = _hyena_spectra(s_len, width, *fargs, tab_l)
            zconv = _hy_prep(pr, z_col, nz, bsz, hy_conv_w[o], hy_conv_b[o][None], tab_l[1])
            yh = _hyena_seq(zconv, bsz, khat, hy_bias[o], tab_l)
            dl = jnp.broadcast_to(jnp.transpose(ret_decay_logit[o])[:, :, None], (RET_HEADS, 2, LANES))
            yd, ydc = _retention(dl, pr, prc, bsz, cols)
            w_out = od_w_out[o].astype(BF16)
            x2_new = _out_proj(yh, 0, pr, 0, yd, 0, w_out, x2, g_post, gate, s_len, tm)
            if need_ctx:
                khat_c = _hyena_spectra(c_len, width, *fargs, tab_c)
                zcconv = _hy_prep(prc, z_col, nz, bsz, hy_conv_w[o], hy_conv_b[o][None], tab_c[1])
                yhc = _hyena_seq(zcconv, bsz, khat_c, hy_bias[o], tab_c)
                ctx2 = _out_proj(yhc, 0, prc, 0, ydc, 0, w_out, ctx2, g_post, gate_c, bsz * c_len, tm)
            x2 = x2_new
    return x2.reshape(bsz, s_len, d)
```

```python
import functools
import math

import numpy as np
import jax
import jax.numpy as jnp
from jax import lax
from jax.experimental import pallas as pl
from jax.experimental.pallas import tpu as pltpu

F32 = jnp.float32
BF16 = jnp.bfloat16
HIGHEST = lax.Precision.HIGHEST

EPS = 1e-6
GRID_W = 64
LANES = 128
LRU_BLOCK_W = 128
LRU_CONV = 4
LRU_C = 8.0
LRU_SEG = 16
LRU_POS = 16
LRU_TINY = 1e-30
LRU_HALO = 16
ATT_HEADS = 8
ATT_KV_HEADS = 2
HEAD_DIM = 128
BLOCK = 128
ROPE_BASE = 10000.0
HY_ORDER = 2
HY_EMB = 33
HY_EMB_PAD = 40
HY_DECAY_TARGET = 1e-2
HY_FAST_PCT = 0.3
HY_SLOW_PCT = 1.5
HY_UNROLL = 4
HY_HALO = 16
HY_PITCH_PAD = 8
RET_HEADS = 8
RET_DK = 128
RET_CHUNK = 128

VMEM_LIMIT = 56 * 1024 * 1024
NEG = -1e30


def _params(sem, vmem=VMEM_LIMIT):
    return pltpu.CompilerParams(dimension_semantics=sem, vmem_limit_bytes=vmem)


def _sigmoid(v):
    return 0.5 * (jnp.tanh(0.5 * v) + 1.0)


def _silu(v):
    return v * _sigmoid(v)


def _softplus(v):
    return jnp.maximum(v, 0.0) + jnp.log(1.0 + jnp.exp(-jnp.abs(v)))


def _dot(a, b, **kw):
    return jnp.dot(a, b, preferred_element_type=F32, **kw)


def _dot_nt(a, b):
    return lax.dot_general(a, b, (((1,), (1,)), ((), ())), preferred_element_type=F32)


def _dot_tn(a, b):
    return lax.dot_general(a, b, (((0,), (0,)), ((), ())), preferred_element_type=F32)


def _mod_kernel(c_ref, w_ref, b_ref, o_ref):
    s = _silu(c_ref[...])
    o_ref[0] = _dot(s, w_ref[0], precision=HIGHEST) + b_ref[0]


def _modulation(crows, mod_w, mod_b):
    depth, d, n3 = mod_w.shape
    r = crows.shape[0]
    tn = 1024
    return pl.pallas_call(
        _mod_kernel,
        grid=(depth, n3 // tn),
        in_specs=[pl.BlockSpec((r, d), lambda l, j: (0, 0)),
                  pl.BlockSpec((1, d, tn), lambda l, j: (l, 0, j)),
                  pl.BlockSpec((1, 1, tn), lambda l, j: (l, 0, j))],
        out_specs=pl.BlockSpec((1, r, tn), lambda l, j: (l, 0, j)),
        out_shape=jax.ShapeDtypeStruct((depth, r, n3), F32),
        compiler_params=_params(("parallel", "parallel")),
        name="modulation",
    )(crows, mod_w, mod_b.reshape(depth, 1, n3))


def _inproj_kernel(x_ref, g_ref, sc_ref, sh_ref, w_ref, o_ref, h_ref):
    @pl.when(pl.program_id(1) == 0)
    def _():
        x = x_ref[...]
        y = x * lax.rsqrt(jnp.mean(x * x, axis=-1, keepdims=True) + EPS) * g_ref[...]
        h_ref[...] = (y * (1.0 + sc_ref[0]) + sh_ref[0]).astype(BF16)

    o_ref[...] = _dot(h_ref[...], w_ref[...]).astype(o_ref.dtype)


def _in_proj(x2, g, scale, shift, rows_per_group, w, out_dtype, tm, tn):
    m, d = x2.shape
    n = w.shape[1]
    tpg = rows_per_group // tm
    return pl.pallas_call(
        _inproj_kernel,
        grid=(m // tm, n // tn),
        in_specs=[pl.BlockSpec((tm, d), lambda i, j: (i, 0)),
                  pl.BlockSpec((1, d), lambda i, j: (0, 0)),
                  pl.BlockSpec((1, 1, d), lambda i, j: (i // tpg, 0, 0)),
                  pl.BlockSpec((1, 1, d), lambda i, j: (i // tpg, 0, 0)),
                  pl.BlockSpec((d, tn), lambda i, j: (0, j))],
        out_specs=pl.BlockSpec((tm, tn), lambda i, j: (i, j)),
        out_shape=jax.ShapeDtypeStruct((m, n), out_dtype),
        scratch_shapes=[pltpu.VMEM((tm, d), BF16)],
        compiler_params=_params(("parallel", "arbitrary")),
        name="in_proj",
    )(x2, g, scale, shift, w)


def _outproj_kernel(a_ref, ga_ref, b_ref, w_ref, x_ref, g_ref, gate_ref, o_ref):
    wa = w_ref.shape[0] // 2
    a = (a_ref[...].astype(F32) * _silu(ga_ref[...].astype(F32))).astype(BF16)
    y = _dot(a, w_ref[0:wa, :]) + _dot(b_ref[...], w_ref[wa:, :])
    yn = y * lax.rsqrt(jnp.mean(y * y, axis=-1, keepdims=True) + EPS) * g_ref[...]
    o_ref[...] = x_ref[...] + gate_ref[0] * yn


def _out_proj(a, a_col, ga, ga_col, b, b_col, w, x2, g, gate, rows_per_group, tm):
    m, d = x2.shape
    wa = w.shape[0] // 2
    tpg = rows_per_group // tm
    return pl.pallas_call(
        _outproj_kernel,
        grid=(m // tm,),
        in_specs=[pl.BlockSpec((tm, wa), lambda i: (i, a_col)),
                  pl.BlockSpec((tm, wa), lambda i: (i, ga_col)),
                  pl.BlockSpec((tm, wa), lambda i: (i, b_col)),
                  pl.BlockSpec(w.shape, lambda i: (0, 0)),
                  pl.BlockSpec((tm, d), lambda i: (i, 0)),
                  pl.BlockSpec((1, d), lambda i: (0, 0)),
                  pl.BlockSpec((1, 1, d), lambda i: (i // tpg, 0, 0))],
        out_specs=pl.BlockSpec((tm, d), lambda i: (i, 0)),
        out_shape=jax.ShapeDtypeStruct((m, d), F32),
        compiler_params=_params(("parallel",)),
        name="out_proj",
    )(a, ga, b, w, x2, g, gate)


def _lru_kernel(xa_ref, xac_ref, cw_ref, cb_ref, wg_ref, bg_ref, lam_ref, y_ref, yc_ref,
                xp_ref, hf_ref, pf_ref, hb_ref, pb_ref):
    nseg = LRU_SEG
    npos = LRU_POS
    c8 = -LRU_C * _softplus(-lam_ref[...])
    left = LRU_CONV // 2
    cw = [cw_ref[k:k + 1, :] for k in range(LRU_CONV)]
    cb = cb_ref[...]
    zeros_h = jnp.zeros((LRU_HALO, LANES), F32)
    zeros_s = jnp.zeros((nseg, LANES), F32)
    ones_s = jnp.ones((nseg, LANES), F32)

    def run(src_ref, out_ref, n, c0f, c0b):
        seg = n // nseg
        nblk = seg // npos
        halo = LRU_HALO
        pitch = seg + 2 * halo + 8
        for j in range(nseg):
            lo, hi = j * seg - halo, (j + 1) * seg + halo
            if lo < 0:
                xp_ref[j * pitch:j * pitch + halo, :] = zeros_h
            if hi > n:
                xp_ref[j * pitch + seg + halo:j * pitch + seg + 2 * halo, :] = zeros_h
            lo_c, hi_c = max(lo, 0), min(hi, n)
            xp_ref[j * pitch + (lo_c - lo):j * pitch + (hi_c - lo), :] = src_ref[lo_c:hi_c, :].astype(F32)

        def gather(q):
            return jnp.concatenate([xp_ref[pl.ds(halo + q + c * 8 * pitch, 8, stride=pitch), :]
                                    for c in range(nseg // 8)], axis=0)

        def blk(i):
            return pl.ds(pl.multiple_of(i * npos * nseg, npos * nseg), npos * nseg)

        def pos(v, p):
            return v[p * nseg:(p + 1) * nseg, :]

        def fwd_body(i, carry):
            h, pc = carry
            p0 = i * npos
            xs = [gather(p0 + q - left) for q in range(npos + LRU_CONV - 1)]
            us = []
            for p in range(npos):
                u = cb + cw[0] * xs[p]
                for k in range(1, LRU_CONV):
                    u = u + cw[k] * xs[p + k]
                us.append(u)
            u = jnp.concatenate(us, axis=0)
            g = _dot(u.astype(BF16), wg_ref[0]) + bg_ref[0]
            coef = []
            for d in range(2):
                r = _sigmoid(g[:, (2 * d) * LANES:(2 * d + 1) * LANES])
                gi = _sigmoid(g[:, (2 * d + 1) * LANES:(2 * d + 2) * LANES])
                a = jnp.exp(c8[d:d + 1, :] * r)
                om = 1.0 - a * a
                coef.append((a, om * lax.rsqrt(jnp.maximum(om, LRU_TINY)) * (gi * u)))
            (af, bf), (ab, bb) = coef
            hb_ref[blk(i), :] = ab
            pb_ref[blk(i), :] = bb
            hs, ps = [], []
            for p in range(npos):
                a_p = pos(af, p)
                h = a_p * h + pos(bf, p)
                pc = a_p * pc
                hs.append(h)
                ps.append(pc)
            hf_ref[blk(i), :] = jnp.concatenate(hs, axis=0)
            pf_ref[blk(i), :] = jnp.concatenate(ps, axis=0)
            return h, pc

        hef, pef = lax.fori_loop(0, nblk, fwd_body, (zeros_s, ones_s))

        def bwd_body(ii, carry):
            h, pc = carry
            i = nblk - 1 - ii
            ab = hb_ref[blk(i), :]
            bb = pb_ref[blk(i), :]
            hs, ps = [None] * npos, [None] * npos
            for p in reversed(range(npos)):
                a_p = pos(ab, p)
                h = a_p * h + pos(bb, p)
                pc = a_p * pc
                hs[p] = h
                ps[p] = pc
            hb_ref[blk(i), :] = jnp.concatenate(hs, axis=0)
            pb_ref[blk(i), :] = jnp.concatenate(ps, axis=0)
            return h, pc

        heb, peb = lax.fori_loop(0, nblk, bwd_body, (zeros_s, ones_s))

        rows_f, c = [], c0f
        for j in range(nseg):
            rows_f.append(c)
            c = hef[j:j + 1, :] + pef[j:j + 1, :] * c
        final_f = c
        rows_b, c = [None] * nseg, c0b
        for j in reversed(range(nseg)):
            rows_b[j] = c
            c = heb[j:j + 1, :] + peb[j:j + 1, :] * c
        final_b = c
        cf = jnp.concatenate(rows_f, axis=0)
        cbk = jnp.concatenate(rows_b, axis=0)

        def out_body(p, _):
            rs = pl.ds(pl.multiple_of(p * nseg, nseg), nseg)
            y = hf_ref[rs, :] + pf_ref[rs, :] * cf + hb_ref[rs, :] + pb_ref[rs, :] * cbk
            for c in range(nseg // 8):
                out_ref[pl.ds(p + c * 8 * seg, 8, stride=seg), :] = y[c * 8:(c + 1) * 8, :]
            return 0
        lax.fori_loop(0, seg, out_body, 0, unroll=8)
        return final_f, final_b

    zero = jnp.zeros((1, LANES), F32)
    ff, fb = run(xac_ref, yc_ref, xac_ref.shape[0], zero, zero)
    run(xa_ref, y_ref, xa_ref.shape[0], ff, fb)


def _lru(proj, projc, xa_col, bsz, conv_w, conv_b, wg, bg, lam):
    w = conv_w.shape[1]
    s_len = proj.shape[0] // bsz
    c_len = projc.shape[0] // bsz
    nblk = w // LANES
    seq = lambda n: pl.BlockSpec((n, LANES), lambda b, j: (b, j))
    src = lambda n: pl.BlockSpec((n, LANES), lambda b, j: (b, xa_col + j))
    return pl.pallas_call(
        _lru_kernel,
        grid=(bsz, nblk),
        in_specs=[src(s_len), src(c_len),
                  pl.BlockSpec((LRU_CONV, LANES), lambda b, j: (0, j)),
                  pl.BlockSpec((1, LANES), lambda b, j: (0, j)),
                  pl.BlockSpec((1, LANES, 4 * LANES), lambda b, j: (j, 0, 0)),
                  pl.BlockSpec((1, 1, 4 * LANES), lambda b, j: (j, 0, 0)),
                  pl.BlockSpec((2, LANES), lambda b, j: (0, j))],
        out_specs=[seq(s_len), seq(c_len)],
        out_shape=[jax.ShapeDtypeStruct((proj.shape[0], w), F32), jax.ShapeDtypeStruct((projc.shape[0], w), F32)],
        scratch_shapes=[pltpu.VMEM((s_len + (2 * LRU_HALO + 8) * LRU_SEG, LANES), F32)]
        + [pltpu.VMEM((s_len, LANES), F32)] * 4,
        compiler_params=_params(("parallel", "parallel")),
        name="rglru",
    )(proj, projc, conv_w, conv_b, wg, bg, lam)


def _rope(v, cos, sin, lane):
    swapped = jnp.where((lane & 63) < 32, pltpu.roll(v, LANES - 32, 1), pltpu.roll(v, 32, 1))
    return v * cos + swapped * sin


def _attn_kernel(sink_ref, q_ref, gb_ref, k_ref, v_ref, kc_ref, vc_ref, cos_ref, sin_ref, bias_ref, o_ref):
    qb = pl.program_id(1)
    nb = pl.num_programs(1)
    group = ATT_HEADS // ATT_KV_HEADS
    scale = HEAD_DIM ** -0.5
    lane = lax.broadcasted_iota(jnp.int32, (BLOCK, LANES), 1)

    def blk(i):
        return pl.ds(pl.multiple_of(i * BLOCK, BLOCK), BLOCK)

    ip = jnp.maximum(qb - 1, 0)
    inx = jnp.minimum(qb + 1, nb - 1)
    cos_o, sin_o = cos_ref[blk(qb), :], sin_ref[blk(qb), :]
    cos_p, sin_p = cos_ref[blk(ip), :], sin_ref[blk(ip), :]
    cos_n, sin_n = cos_ref[blk(inx), :], sin_ref[blk(inx), :]

    bias = jnp.concatenate([bias_ref[0]] * group, axis=0)

    for h in range(ATT_KV_HEADS):
        ksl = slice(h * HEAD_DIM, (h + 1) * HEAD_DIM)
        kp = _rope(k_ref[blk(ip), ksl].astype(F32), cos_p, sin_p, lane)
        ko = _rope(k_ref[blk(qb), ksl].astype(F32), cos_o, sin_o, lane)
        kn = _rope(k_ref[blk(inx), ksl].astype(F32), cos_n, sin_n, lane)
        kcat = jnp.concatenate([kp.astype(BF16), ko.astype(BF16), kn.astype(BF16), kc_ref[:, ksl]], axis=0)
        vcat = jnp.concatenate([v_ref[blk(ip), ksl], v_ref[blk(qb), ksl], v_ref[blk(inx), ksl],
                                vc_ref[:, ksl]], axis=0)
        qs, sinks = [], []
        for g in range(group):
            hh = h * group + g
            qh = q_ref[:, hh * HEAD_DIM:(hh + 1) * HEAD_DIM].astype(F32)
            qs.append((_rope(qh, cos_o, sin_o, lane) * scale).astype(BF16))
            sinks.append(jnp.full((BLOCK, 1), sink_ref[hh], F32))
        q4 = jnp.concatenate(qs, axis=0)
        sk = jnp.concatenate(sinks, axis=0)
        s = _dot_nt(q4, kcat) + bias
        m = jnp.maximum(jnp.max(s, axis=-1, keepdims=True), sk)
        p = jnp.exp(s - m)
        denom = jnp.exp(sk - m) + jnp.sum(p, axis=-1, keepdims=True)
        o = _dot(p.astype(BF16), vcat) / denom
        for g in range(group):
            hh = h * group + g
            hs = slice(hh * HEAD_DIM, (hh + 1) * HEAD_DIM)
            gate = _silu(gb_ref[:, hs].astype(F32))
            o_ref[:, hs] = (o[g * BLOCK:(g + 1) * BLOCK, :] * gate).astype(o_ref.dtype)


def _attention(sink, proj, projc, bsz, cols, cos_t, sin_t):
    s_len = proj.shape[0] // bsz
    c_len = projc.shape[0] // bsz
    nb = s_len // BLOCK
    aw = ATT_HEADS * HEAD_DIM
    kw = ATT_KV_HEADS * HEAD_DIM
    qi = np.arange(BLOCK)[:, None]
    kj = np.arange(BLOCK)[None, :]
    variants = []
    for v in range(4):
        prev = np.where((kj >= qi) & bool(v & 1), 0.0, NEG)
        nxt = np.where((kj <= qi) & bool(v & 2), 0.0, NEG)
        variants.append(np.concatenate([prev, np.zeros((BLOCK, BLOCK)), nxt, np.zeros((BLOCK, c_len))], axis=1))
    bias = jnp.asarray(np.stack(variants), F32)
    ncol = 3 * BLOCK + c_len

    def variant(b, i):
        return ((i > 0).astype(jnp.int32) + 2 * (i < nb - 1).astype(jnp.int32), 0, 0)

    return pl.pallas_call(
        _attn_kernel,
        grid=(bsz, nb),
        in_specs=[pl.BlockSpec(memory_space=pltpu.SMEM),
                  pl.BlockSpec((BLOCK, aw), lambda b, i: (b * nb + i, cols["q"])),
                  pl.BlockSpec((BLOCK, aw), lambda b, i: (b * nb + i, cols["gb"])),
                  pl.BlockSpec((s_len, kw), lambda b, i: (b, cols["k"])),
                  pl.BlockSpec((s_len, kw), lambda b, i: (b, cols["v"])),
                  pl.BlockSpec((c_len, kw), lambda b, i: (b, cols["k"])),
                  pl.BlockSpec((c_len, kw), lambda b, i: (b, cols["v"])),
                  pl.BlockSpec((s_len, LANES), lambda b, i: (0, 0)),
                  pl.BlockSpec((s_len, LANES), lambda b, i: (0, 0)),
                  pl.BlockSpec((1, BLOCK, ncol), variant)],
        out_specs=pl.BlockSpec((BLOCK, aw), lambda b, i: (b * nb + i, 0)),
        out_shape=jax.ShapeDtypeStruct((proj.shape[0], aw), BF16),
        compiler_params=_params(("parallel", "arbitrary")),
        name="window_attention",
    )(sink, proj, proj, proj, proj, projc, projc, cos_t, sin_t, bias)


def _ctx_attn_kernel(sink_ref, q_ref, gb_ref, k_ref, v_ref, o_ref):
    group = ATT_HEADS // ATT_KV_HEADS
    scale = HEAD_DIM ** -0.5
    n = q_ref.shape[0]
    for h in range(ATT_KV_HEADS):
        ksl = slice(h * HEAD_DIM, (h + 1) * HEAD_DIM)
        qs, sinks = [], []
        for g in range(group):
            hh = h * group + g
            qs.append((q_ref[:, hh * HEAD_DIM:(hh + 1) * HEAD_DIM].astype(F32) * scale).astype(BF16))
            sinks.append(jnp.full((n, 1), sink_ref[hh], F32))
        q4 = jnp.concatenate(qs, axis=0)
        sk = jnp.concatenate(sinks, axis=0)
        s = _dot_nt(q4, k_ref[:, ksl])
        m = jnp.maximum(jnp.max(s, axis=-1, keepdims=True), sk)
        p = jnp.exp(s - m)
        denom = jnp.exp(sk - m) + jnp.sum(p, axis=-1, keepdims=True)
        o = _dot(p.astype(BF16), v_ref[:, ksl]) / denom
        for g in range(group):
            hh = h * group + g
            hs = slice(hh * HEAD_DIM, (hh + 1) * HEAD_DIM)
            o_ref[:, hs] = (o[g * n:(g + 1) * n, :] * _silu(gb_ref[:, hs].astype(F32))).astype(o_ref.dtype)


def _ctx_attention(sink, projc, bsz, cols):
    c_len = projc.shape[0] // bsz
    aw = ATT_HEADS * HEAD_DIM
    kw = ATT_KV_HEADS * HEAD_DIM
    return pl.pallas_call(
        _ctx_attn_kernel,
        grid=(bsz,),
        in_specs=[pl.BlockSpec(memory_space=pltpu.SMEM),
                  pl.BlockSpec((c_len, aw), lambda b: (b, cols["q"])),
                  pl.BlockSpec((c_len, aw), lambda b: (b, cols["gb"])),
                  pl.BlockSpec((c_len, kw), lambda b: (b, cols["k"])),
                  pl.BlockSpec((c_len, kw), lambda b: (b, cols["v"]))],
        out_specs=pl.BlockSpec((c_len, aw), lambda b: (b, 0)),
        out_shape=jax.ShapeDtypeStruct((projc.shape[0], aw), BF16),
        compiler_params=_params(("parallel",)),
        name="context_attention",
    )(sink, projc, projc, projc, projc)


def _ret_kernel(dl_ref, q_ref, k_ref, v_ref, gd_ref, qc_ref, kc_ref, vc_ref, gdc_ref, o_ref, oc_ref,
                acc_ref, accc_ref, u_ref, uc_ref, st_ref, stc_ref):
    c = RET_CHUNK
    s_len = q_ref.shape[0]
    c_len = qc_ref.shape[0]
    lg = -_softplus(-dl_ref[0])
    lgf, lgb = lg[0:1, :], lg[1:2, :]
    ri = lax.broadcasted_iota(jnp.int32, (c, c), 0)
    ci = lax.broadcasted_iota(jnp.int32, (c, c), 1)
    diff = (ri - ci).astype(F32)
    dmat = jnp.where(ri >= ci, jnp.exp(jnp.maximum(diff, 0.0) * lgf), jnp.exp(jnp.maximum(-diff, 0.0) * lgb))
    idx = ri.astype(F32)
    qdec_f = jnp.exp((idx + 1.0) * lgf)
    kdec_f = jnp.exp((c - 1.0 - idx) * lgf)
    qdec_b = jnp.exp((c - idx) * lgb)
    kdec_b = jnp.exp(idx * lgb)
    cdec_f = jnp.exp(c * lgf)
    cdec_b = jnp.exp(c * lgb)

    def chunk(j):
        return pl.ds(pl.multiple_of(j * c, c), c)

    def intra(qr, kr, vr, acc, u, n):
        def body(j, _):
            q, k, v = qr[chunk(j), :], kr[chunk(j), :], vr[chunk(j), :]
            att = (_dot_nt(q, k) * dmat).astype(BF16)
            acc[chunk(j), :] = _dot(att, v)
            kf = k.astype(F32)
            kcat = jnp.concatenate([(kf * kdec_f).astype(BF16), (kf * kdec_b).astype(BF16)], axis=1)
            u[j] = _dot_tn(kcat, v)
            return 0
        lax.fori_loop(0, n // c, body, 0, unroll=min(4, n // c))

    def states(u, st, n, sf, sb):
        nch = n // c

        def fbody(j, s):
            st[j, 0:c, :] = s.astype(BF16)
            return s * cdec_f + u[j, 0:c, :]
        sf = lax.fori_loop(0, nch, fbody, sf)

        def bbody(jj, s):
            j = nch - 1 - jj
            st[j, c:2 * c, :] = s.astype(BF16)
            return s * cdec_b + u[j, c:2 * c, :]
        sb = lax.fori_loop(0, nch, bbody, sb)
        return sf, sb

    def cross(qr, gr, acc, st, outr, n):
        def body(j, _):
            qf = qr[chunk(j), :].astype(F32)
            qcat = jnp.concatenate([(qf * qdec_f).astype(BF16), (qf * qdec_b).astype(BF16)], axis=1)
            o = (acc[chunk(j), :] + _dot(qcat, st[j])) * (RET_DK ** -0.5)
            o = o * lax.rsqrt(jnp.mean(o * o, axis=-1, keepdims=True) + EPS)
            outr[chunk(j), :] = (o * _silu(gr[chunk(j), :].astype(F32))).astype(outr.dtype)
            return 0
        lax.fori_loop(0, n // c, body, 0, unroll=min(4, n // c))

    intra(qc_ref, kc_ref, vc_ref, accc_ref, uc_ref, c_len)
    intra(q_ref, k_ref, v_ref, acc_ref, u_ref, s_len)
    zero = jnp.zeros((RET_DK, LANES), F32)
    sf, sb = states(uc_ref, stc_ref, c_len, zero, zero)
    states(u_ref, st_ref, s_len, sf, sb)
    cross(qc_ref, gdc_ref, accc_ref, stc_ref, oc_ref, c_len)
    cross(q_ref, gd_ref, acc_ref, st_ref, o_ref, s_len)


def _retention(dl, proj, projc, bsz, cols):
    s_len = proj.shape[0] // bsz
    c_len = projc.shape[0] // bsz
    hb = lambda n, col: pl.BlockSpec((n, LANES), lambda b, h: (b, col + h))
    return pl.pallas_call(
        _ret_kernel,
        grid=(bsz, RET_HEADS),
        in_specs=[pl.BlockSpec((1, 2, LANES), lambda b, h: (h, 0, 0)),
                  hb(s_len, cols["q"]), hb(s_len, cols["k"]), hb(s_len, cols["v"]), hb(s_len, cols["gd"]),
                  hb(c_len, cols["q"]), hb(c_len, cols["k"]), hb(c_len, cols["v"]), hb(c_len, cols["gd"])],
        out_specs=[pl.BlockSpec((s_len, LANES), lambda b, h: (b, h)),
                   pl.BlockSpec((c_len, LANES), lambda b, h: (b, h))],
        out_shape=[jax.ShapeDtypeStruct((proj.shape[0], RET_HEADS * LANES), BF16),
                   jax.ShapeDtypeStruct((projc.shape[0], RET_HEADS * LANES), BF16)],
        scratch_shapes=[pltpu.VMEM((s_len, LANES), F32), pltpu.VMEM((c_len, LANES), F32),
                        pltpu.VMEM((s_len // RET_CHUNK, 2 * RET_CHUNK, LANES), F32),
                        pltpu.VMEM((c_len // RET_CHUNK, 2 * RET_CHUNK, LANES), F32),
                        pltpu.VMEM((s_len // RET_CHUNK, 2 * RET_CHUNK, LANES), BF16),
                        pltpu.VMEM((c_len // RET_CHUNK, 2 * RET_CHUNK, LANES), BF16)],
        compiler_params=_params(("parallel", "parallel")),
        name="retention",
    )(dl, proj, proj, proj, proj, projc, projc, projc, projc)


def _hy_hid_kernel(z_ref, w1_ref, b1_ref, w2_ref, b2_ref, f_ref, o_ref):
    f = f_ref[...]
    h = jnp.sin(f * (_dot(z_ref[...], w1_ref[...], precision=HIGHEST) + b1_ref[...]))
    o_ref[...] = jnp.sin(f * (_dot(h, w2_ref[...], precision=HIGHEST) + b2_ref[...]))


def _hy_hidden(zfull, w1p, b1, w2, b2, freq):
    n, e = zfull.shape
    fd = w2.shape[0]
    tr = min(n, 1024)
    full = lambda shp: pl.BlockSpec(shp, lambda i: (0, 0))
    return pl.pallas_call(
        _hy_hid_kernel,
        grid=(n // tr,),
        in_specs=[pl.BlockSpec((tr, e), lambda i: (i, 0)), full((e, fd)), full((1, fd)), full((fd, fd)),
                  full((1, fd)), full((1, fd))],
        out_specs=pl.BlockSpec((tr, fd), lambda i: (i, 0)),
        out_shape=jax.ShapeDtypeStruct((n, fd), F32),
        compiler_params=_params(("parallel",)),
        name="hyena_filter_mlp",
    )(zfull, w1p, b1, w2, b2, freq)


def _hy_filt_kernel(hid_ref, w3f_ref, w3b_ref, tn_ref, dl_ref, o_ref):
    half = hid_ref.shape[0] // 2
    decay = jnp.exp(-tn_ref[...] * jnp.abs(dl_ref[...]))
    top = _dot(hid_ref[0:half, :], w3f_ref[...], precision=HIGHEST)
    bot = _dot(hid_ref[half:, :], w3b_ref[...], precision=HIGHEST)
    row = lax.broadcasted_iota(jnp.int32, bot.shape, 0)
    bot = jnp.where(row == 0, 0.0, bot)
    o_ref[0, 0:half, :] = top * decay[0:half, :]
    o_ref[0, half:, :] = bot * decay[half:, :]


def _hy_filters(hid, w3, tn_full, deltas):
    n, fd = hid.shape
    wch = deltas.shape[1]
    nsl = wch // LANES
    return pl.pallas_call(
        _hy_filt_kernel,
        grid=(HY_ORDER, nsl),
        in_specs=[pl.BlockSpec((n, fd), lambda o, j: (0, 0)),
                  pl.BlockSpec((fd, LANES), lambda o, j: (0, o * 2 * nsl + j)),
                  pl.BlockSpec((fd, LANES), lambda o, j: (0, o * 2 * nsl + nsl + j)),
                  pl.BlockSpec((n, LANES), lambda o, j: (0, 0)),
                  pl.BlockSpec((1, LANES), lambda o, j: (0, j))],
        out_specs=pl.BlockSpec((1, n, LANES), lambda o, j: (o, 0, j)),
        out_shape=jax.ShapeDtypeStruct((HY_ORDER, n, wch), F32),
        compiler_params=_params(("parallel", "parallel")),
        name="hyena_filter",
    )(hid, w3, w3, tn_full, deltas)


def _hy_kfft_kernel(kern_ref, f1_ref, f2_ref, o_ref, g_ref, *, n1, n2):
    pg = 2 * n1 + HY_PITCH_PAD

    def stage1(i2, _):
        x = kern_ref[pl.ds(i2, n1, stride=n2), :].astype(BF16)
        g_ref[pl.ds(pl.multiple_of(i2 * pg, 8), 2 * n1), :] = _dot(f1_ref[i2], x)
        return 0
    lax.fori_loop(0, n2, stage1, 0, unroll=HY_UNROLL)

    def stage2(k1, _):
        x = jnp.concatenate([g_ref[pl.ds(k1, n2, stride=pg), :],
                             g_ref[pl.ds(n1 + k1, n2, stride=pg), :]], axis=0).astype(BF16)
        o_ref[pl.ds(pl.multiple_of(k1 * 2 * n2, 2 * n2), 2 * n2), :] = _dot(f2_ref[...], x)
        return 0
    lax.fori_loop(0, n1, stage2, 0, unroll=HY_UNROLL // 2)


def _hy_kfft(kern, f1k, f2, n1, n2):
    orders, n, wch = kern.shape
    nsl = wch // LANES
    return pl.pallas_call(
        functools.partial(_hy_kfft_kernel, n1=n1, n2=n2),
        grid=(orders, nsl),
        in_specs=[pl.BlockSpec((None, n, LANES), lambda o, j: (o, 0, j)),
                  pl.BlockSpec(f1k.shape, lambda o, j: (0, 0, 0)),
                  pl.BlockSpec(f2.shape, lambda o, j: (0, 0))],
        out_specs=pl.BlockSpec((None, None, 2 * n, LANES), lambda o, j: (o, j, 0, 0)),
        out_shape=jax.ShapeDtypeStruct((orders, nsl, 2 * n, LANES), F32),
        scratch_shapes=[pltpu.VMEM((n2 * (2 * n1 + HY_PITCH_PAD), LANES), F32)],
        compiler_params=_params(("parallel", "parallel")),
        name="hyena_filter_fft",
    )(kern, f1k, f2)


def _hy_prep_kernel(z_ref, w_ref, b_ref, o_ref, zp_ref, *, n2):
    l_len = z_ref.shape[0]
    h1 = l_len // n2
    halo = HY_HALO
    pitch = n2 + 2 * halo + 8
    zeros_h = jnp.zeros((halo, LANES), F32)
    for i1 in range(h1):
        lo, hi = i1 * n2 - halo, (i1 + 1) * n2 + halo
        if lo < 0:
            zp_ref[i1 * pitch:i1 * pitch + halo, :] = zeros_h
        if hi > l_len:
            zp_ref[i1 * pitch + n2 + halo:i1 * pitch + n2 + 2 * halo, :] = zeros_h
        lo_c, hi_c = max(lo, 0), min(hi, l_len)
        zp_ref[i1 * pitch + (lo_c - lo):i1 * pitch + (hi_c - lo), :] = z_ref[lo_c:hi_c, :].astype(F32)
    w0, w1, w2, b = w_ref[0:1, :], w_ref[1:2, :], w_ref[2:3, :], b_ref[...]

    def tap(j):
        return zp_ref[pl.ds(halo + j, h1, stride=pitch), :]

    def body(i2, carry):
        zm, z0 = carry
        zn = tap(i2 + 1)
        val = b + w0 * zm + w1 * z0 + w2 * zn
        o_ref[pl.ds(pl.multiple_of(i2 * h1, h1), h1), :] = val.astype(o_ref.dtype)
        return z0, zn
    lax.fori_loop(0, n2, body, (tap(-1), tap(0)), unroll=HY_UNROLL)


def _hy_stage_dtype(l_len, n2):
    return BF16 if (l_len // n2) % 16 == 0 else F32


def _hy_prep(proj, z_col, nz, bsz, conv_w, conv_b, n2):
    l_len = proj.shape[0] // bsz
    h1 = l_len // n2
    return pl.pallas_call(
        functools.partial(_hy_prep_kernel, n2=n2),
        grid=(bsz, nz),
        in_specs=[pl.BlockSpec((l_len, LANES), lambda b, j: (b, z_col + j)),
                  pl.BlockSpec((3, LANES), lambda b, j: (0, j)),
                  pl.BlockSpec((1, LANES), lambda b, j: (0, j))],
        out_specs=pl.BlockSpec((l_len, LANES), lambda b, j: (b, j)),
        out_shape=jax.ShapeDtypeStruct((proj.shape[0], nz * LANES), _hy_stage_dtype(l_len, n2)),
        scratch_shapes=[pltpu.VMEM((h1 * (n2 + 2 * HY_HALO + 8), LANES), F32)],
        compiler_params=_params(("parallel", "parallel")),
        name="hyena_short_conv",
    )(proj, conv_w, conv_b)


def _pack2(re, im):
    r = lax.bitcast_convert_type(re, jnp.uint32) + jnp.uint32(0x8000)
    i = lax.bitcast_convert_type(im, jnp.uint32) + jnp.uint32(0x8000)
    return (r & jnp.uint32(0xFFFF0000)) | (i >> 16)


def _unpack2(w):
    re = lax.bitcast_convert_type(w & jnp.uint32(0xFFFF0000), F32)
    im = lax.bitcast_convert_type(w << 16, F32)
    return jnp.concatenate([re, im], axis=0).astype(BF16)


def _hy_conv_kernel(y_ref, x_ref, kh_ref, bias_ref, f1_ref, f2_ref, f2i_ref, f1i_ref, o_ref, g_ref,
                    *, n1, n2, natural_out):
    h1 = n1 // 2
    l_len = h1 * n2
    pg = n1 + HY_PITCH_PAD
    bias = bias_ref[...]
    unroll = HY_UNROLL

    def rows(seq, i2):
        return pl.ds(pl.multiple_of(seq * l_len + i2 * h1, h1), h1)

    def lanes2(a, b):
        return jnp.concatenate([a, b], axis=1)

    def stage1(i2, _):
        x = lanes2(jnp.concatenate([y_ref[rows(0, i2), :], y_ref[rows(1, i2), :]], axis=0),
                   jnp.concatenate([y_ref[rows(2, i2), :], y_ref[rows(3, i2), :]], axis=0)).astype(BF16)
        a = _dot(f1_ref[i2], x)
        w = _pack2(a[0:n1, :], a[n1:, :])
        blk = pl.ds(pl.multiple_of(i2 * pg, 8), n1)
        g_ref[0, blk, :] = w[:, 0:LANES]
        g_ref[1, blk, :] = w[:, LANES:]
        return 0
    lax.fori_loop(0, n2, stage1, 0, unroll=2 * unroll)

    def stage23(t, _):
        outs = []
        for u in range(unroll):
            k1 = t * unroll + u
            col = pl.ds(k1, n2, stride=pg)
            x = lanes2(_unpack2(g_ref[0, col, :]), _unpack2(g_ref[1, col, :]))
            yh = _dot(f2_ref[...], x)
            base = pl.multiple_of(k1 * 2 * n2, 2 * n2)
            kr = kh_ref[pl.ds(base, n2), :]
            ki = kh_ref[pl.ds(base + n2, n2), :]
            kr, ki = lanes2(kr, kr), lanes2(ki, ki)
            yr, yi = yh[0:n2, :], yh[n2:, :]
            z = jnp.concatenate([yr * kr - yi * ki, yr * ki + yi * kr], axis=0).astype(BF16)
            c = _dot(f2i_ref[...], z)
            outs.append((col, _pack2(c[0:n2, :], c[n2:, :])))
        for col, w in outs:
            g_ref[0, col, :] = w[:, 0:LANES]
            g_ref[1, col, :] = w[:, LANES:]
        return 0
    lax.fori_loop(0, n1 // unroll, stage23, 0)

    def stage4(i2, _):
        blk = pl.ds(pl.multiple_of(i2 * pg, 8), n1)
        x = lanes2(_unpack2(g_ref[0, blk, :]), _unpack2(g_ref[1, blk, :]))
        yc = _dot(f1i_ref[i2], x)
        for seq in range(4):
            pair, half = seq // 2, seq % 2
            sl = rows(seq, i2)
            conv = yc[half * h1:(half + 1) * h1, pair * LANES:(pair + 1) * LANES]
            val = x_ref[sl, :].astype(F32) * (conv + bias * y_ref[sl, :].astype(F32))
            if natural_out:
                o_ref[pl.ds(seq * l_len + i2, h1, stride=n2), :] = val
            else:
                o_ref[sl, :] = val.astype(o_ref.dtype)
        return 0
    lax.fori_loop(0, n2, stage4, 0, unroll=unroll)


def _hy_conv(ysrc, ycol, xsrc, xcol, khat, order, bias, mats, bsz, n1, n2, natural_out):
    l_len = ysrc.shape[0] // bsz
    nsl = khat.shape[1]
    f1, f2, f2i, f1i = mats
    once = pl.Buffered(1)
    cst2 = lambda a: pl.BlockSpec(a.shape, lambda j, p: (0, 0), pipeline_mode=once)
    cst3 = lambda a: pl.BlockSpec(a.shape, lambda j, p: (0, 0, 0), pipeline_mode=once)
    return pl.pallas_call(
        functools.partial(_hy_conv_kernel, n1=n1, n2=n2, natural_out=natural_out),
        grid=(nsl, bsz // 4),
        in_specs=[pl.BlockSpec((4 * l_len, LANES), lambda j, p: (p, ycol + j)),
                  pl.BlockSpec((4 * l_len, LANES), lambda j, p: (p, xcol + j)),
                  pl.BlockSpec((None, None, khat.shape[2], LANES), lambda j, p: (order, j, 0, 0),
                               pipeline_mode=once),
                  pl.BlockSpec((1, LANES), lambda j, p: (0, j)),
                  cst3(f1), cst2(f2), cst2(f2i), cst3(f1i)],
        out_specs=pl.BlockSpec((4 * l_len, LANES), lambda j, p: (p, j)),
        out_shape=jax.ShapeDtypeStruct((ysrc.shape[0], nsl * LANES),
                                       F32 if natural_out else _hy_stage_dtype(l_len, n2)),
        scratch_shapes=[pltpu.VMEM((2, n2 * (n1 + HY_PITCH_PAD), LANES), jnp.uint32)],
        compiler_params=_params(("parallel", "arbitrary")),
        name="hyena_long_conv",
    )(ysrc, xsrc, khat, bias[order][None], f1, f2, f2i, f1i)


def _dft_tables(l_len):
    n = 2 * l_len
    n2 = 128 if l_len >= 1024 else 32
    n1 = n // n2
    h1 = n1 // 2
    i1 = np.arange(n1)[None, None, :]
    k1 = np.arange(n1)[None, :, None]
    i2 = np.arange(n2)[:, None, None]
    ph = 2 * np.pi * (i1 * k1 / n1 + i2 * k1 / n)
    c, s = np.cos(ph), np.sin(ph)
    ch, sh = c[:, :, :h1], s[:, :, :h1]
    f1 = np.concatenate([np.concatenate([ch, sh], 2), np.concatenate([-sh, ch], 2)], 1)
    f1k = np.concatenate([c, -s], 1)
    ct, st = np.swapaxes(ch, 1, 2), np.swapaxes(sh, 1, 2)
    f1i = np.concatenate([np.concatenate([ct, -st], 2), np.concatenate([st, ct], 2)], 1) / n
    a = np.arange(n2)
    ph2 = 2 * np.pi * np.outer(a, a) / n2
    c2, s2 = np.cos(ph2), np.sin(ph2)
    f2 = np.block([[c2, s2], [-s2, c2]])
    f2i = np.block([[c2, -s2], [s2, c2]])
    bf = lambda m: jnp.asarray(m, dtype=F32).astype(BF16)
    return n1, n2, (bf(f1), bf(f2), bf(f2i), bf(f1i)), bf(f1k)


def _filter_positions(l_len, width):
    lag = np.concatenate([np.arange(l_len), l_len - np.arange(l_len)]).astype(np.float64)
    t = lag / (l_len - 1)
    bands = (HY_EMB - 1) // 2
    w = 2.0 * np.pi * lag / l_len
    f = np.linspace(1e-4, bands - 1, bands)[None]
    z = np.concatenate([t[:, None], np.cos(f * w[:, None]), -np.sin(f * w[:, None])], axis=-1)
    z = np.pad(z, ((0, 0), (0, HY_EMB_PAD - HY_EMB)))
    tn = np.repeat(t[:, None], LANES, axis=1)
    max_decay = math.log(HY_DECAY_TARGET) / HY_FAST_PCT
    min_decay = math.log(HY_DECAY_TARGET) / HY_SLOW_PCT
    deltas = np.linspace(min_decay, max_decay, width)[None]
    return jnp.asarray(z, F32), jnp.asarray(tn, F32), jnp.asarray(deltas, F32)


def _rope_tables(seq):
    n_rows = seq // GRID_W
    row = np.repeat(np.arange(n_rows), GRID_W).astype(np.float64)
    col = np.tile(np.arange(GRID_W), n_rows).astype(np.float64)
    half = HEAD_DIM // 2
    inv = ROPE_BASE ** (-np.arange(0, half, 2, dtype=np.float64) / half)
    ar, ac = row[:, None] * inv, col[:, None] * inv
    cos = np.concatenate([np.cos(ar), np.cos(ar), np.cos(ac), np.cos(ac)], axis=1)
    sin = np.concatenate([-np.sin(ar), np.sin(ar), -np.sin(ac), np.sin(ac)], axis=1)
    return jnp.asarray(cos, F32), jnp.asarray(sin, F32)


def _hyena_spectra(l_len, width, w1, b1, w2, b2, w3, freq, tables):
    n1, n2, mats, f1k = tables
    zfull, tn_full, deltas = _filter_positions(l_len, width)
    w1p = jnp.pad(w1, ((0, HY_EMB_PAD - HY_EMB), (0, 0)))
    hid = _hy_hidden(zfull, w1p, b1[None], w2, b2[None], freq[None])
    kern = _hy_filters(hid, w3, tn_full, deltas)
    return _hy_kfft(kern, f1k, mats[1], n1, n2)


def _hyena_seq(zc, bsz, khat, bias, tables):
    n1, n2, mats, _ = tables
    nsl = khat.shape[1]
    y1 = _hy_conv(zc, 0, zc, nsl, khat, 0, bias, mats, bsz, n1, n2, False)
    return _hy_conv(y1, 0, zc, 2 * nsl, khat, 1, bias, mats, bsz, n1, n2, True)


def kernel(x, c, ctx, c_ctx, mod_w, mod_b, norm_pre, norm_post, ev_w_in, ev_w_out, lru_conv_w, lru_conv_b, lru_wa, lru_ba, lru_wx, lru_bx, lru_lambda, attn_sink, od_w_in, od_w_out, hy_conv_w, hy_conv_b, hy_w1, hy_b1, hy_w2, hy_b2, hy_w3, hy_freq, hy_bias, ret_decay_logit):
    bsz, s_len, d = x.shape
    c_len = ctx.shape[1]
    depth = mod_w.shape[0]
    assert bsz % 4 == 0 and bsz <= 16 and s_len % 1024 == 0 and c_len % 256 == 0 and c_len <= s_len

    crows = jnp.concatenate([c, c_ctx[None], jnp.zeros((24 - bsz - 1, d), F32)], axis=0)
    mod = _modulation(crows, mod_w, mod_b)

    cos_t, sin_t = _rope_tables(s_len)
    tab_l = _dft_tables(s_len)
    tab_c = _dft_tables(c_len)

    x2 = x.reshape(bsz * s_len, d)
    ctx2 = ctx.reshape(bsz * c_len, d)
    tm = 1024

    for l in range(depth):
        need_ctx = l < depth - 1
        shift, scale, gate = (mod[l, :bsz, i * d:(i + 1) * d].reshape(bsz, 1, d) for i in range(3))
        shift_c, scale_c, gate_c = (mod[l, bsz:bsz + 1, i * d:(i + 1) * d].reshape(1, 1, d) for i in range(3))
        g_pre = norm_pre[l][None]
        g_post = norm_post[l][None]

        def proj(w, tn):
            wb = w.astype(BF16)
            p = _in_proj(x2, g_pre, scale, shift, s_len, wb, BF16, tm, tn)
            pc = _in_proj(ctx2, g_pre, scale_c, shift_c, bsz * c_len, wb, BF16, tm, tn)
            return p, pc

        if l % 2 == 0:
            e = l // 2
            w_in = ev_w_in[e]
            w_ord = jnp.concatenate([w_in[:, 1024:2048], w_in[:, 2048:3072], w_in[:, 3584:4608],
                                     w_in[:, 3072:3584], w_in[:, 0:1024]], axis=1)
            pr, prc = proj(w_ord, 1536)
            cols = {"q": 1, "gb": 2, "k": 12, "v": 13}
            xa_col = 3584 // LANES
            nblk = lru_wa.shape[2]
            wg = jnp.stack([lru_wa[e, 0], lru_wx[e, 0], lru_wa[e, 1], lru_wx[e, 1]], axis=1)
            wg = jnp.transpose(wg, (0, 2, 1, 3)).reshape(nblk, LRU_BLOCK_W, 4 * LRU_BLOCK_W).astype(BF16)
            bg = jnp.stack([lru_ba[e, 0], lru_bx[e, 0], lru_ba[e, 1], lru_bx[e, 1]], axis=0)
            bg = jnp.transpose(bg.reshape(4, nblk, LRU_BLOCK_W), (1, 0, 2)).reshape(nblk, 1, 4 * LRU_BLOCK_W)
            ya, yac = _lru(pr, prc, xa_col, bsz, lru_conv_w[e], lru_conv_b[e][None], wg, bg, lru_lambda[e])
            yb = _attention(attn_sink[e], pr, prc, bsz, cols, cos_t, sin_t)
            w_out = ev_w_out[e].astype(BF16)
            x2_new = _out_proj(ya, 0, pr, 0, yb, 0, w_out, x2, g_post, gate, s_len, tm)
            if need_ctx:
                ybc = _ctx_attention(attn_sink[e], prc, bsz, cols)
                ctx2 = _out_proj(yac, 0, prc, 0, ybc, 0, w_out, ctx2, g_post, gate_c, bsz * c_len, tm)
            x2 = x2_new
        else:
            o = l // 2
            w_in = od_w_in[o]
            w_ord = jnp.concatenate([w_in[:, 3072:8192], w_in[:, 0:3072]], axis=1)
            pr, prc = proj(w_ord, 2048)
            cols = {"q": 8, "k": 16, "v": 24, "gd": 32}
            z_col = 5120 // LANES
            width = hy_bias.shape[2]
            nz = (HY_ORDER + 1) * width // LANES
            fargs = (hy_w1[o], hy_b1[o], hy_w2[o], hy_b2[o], hy_w3[o], hy_freq[o])
            khat = _hyena_spectra(s_len, width, *fargs, tab_l)
            zconv = _hy_prep(pr, z_col, nz, bsz, hy_conv_w[o], hy_conv_b[o][None], tab_l[1])
            yh = _hyena_seq(zconv, bsz, khat, hy_bias[o], tab_l)
            dl = jnp.broadcast_to(jnp.transpose(ret_decay_logit[o])[:, :, None], (RET_HEADS, 2, LANES))
            yd, ydc = _retention(dl, pr, prc, bsz, cols)
            w_out = od_w_out[o].astype(BF16)
            x2_new = _out_proj(yh, 0, pr, 0, yd, 0, w_out, x2, g_post, gate, s_len, tm)
            if need_ctx:
                khat_c = _hyena_spectra(c_len, width, *fargs, tab_c)
                zcconv = _hy_prep(prc, z_col, nz, bsz, hy_conv_w[o], hy_conv_b[o][None], tab_c[1])
                yhc = _hyena_seq(zcconv, bsz, khat_c, hy_bias[o], tab_c)
                ctx2 = _out_proj(yhc, 0, prc, 0, ydc, 0, w_out, ctx2, g_post, gate_c, bsz * c_len, tm)
            x2 = x2_new
    return x2.reshape(bsz, s_len, d)
```

```python
import functools
import math

import numpy as np
import jax
import jax.numpy as jnp
from jax import lax
from jax.experimental import pallas as pl
from jax.experimental.pallas import tpu as pltpu

F32 = jnp.float32
BF16 = jnp.bfloat16
HIGHEST = lax.Precision.HIGHEST

EPS = 1e-6
GRID_W = 64
LANES = 128
LRU_BLOCK_W = 128
LRU_CONV = 4
LRU_C = 8.0
LRU_SEG = 16
LRU_POS = 16
LRU_TINY = 1e-30
LRU_HALO = 16
ATT_HEADS = 8
ATT_KV_HEADS = 2
HEAD_DIM = 128
BLOCK = 128
ROPE_BASE = 10000.0
HY_ORDER = 2
HY_EMB = 33
HY_EMB_PAD = 40
HY_DECAY_TARGET = 1e-2
HY_FAST_PCT = 0.3
HY_SLOW_PCT = 1.5
HY_UNROLL = 4
HY_HALO = 16
HY_PITCH_PAD = 8
RET_HEADS = 8
RET_DK = 128
RET_CHUNK = 256

VMEM_LIMIT = 56 * 1024 * 1024
NEG = -1e30


def _params(sem, vmem=VMEM_LIMIT, **kw):
    return pltpu.CompilerParams(dimension_semantics=sem, vmem_limit_bytes=vmem, **kw)


def _sigmoid(v):
    return 0.5 * (jnp.tanh(0.5 * v) + 1.0)


def _silu(v):
    return v * _sigmoid(v)


def _softplus(v):
    return jnp.maximum(v, 0.0) + jnp.log(1.0 + jnp.exp(-jnp.abs(v)))


def _dot(a, b, **kw):
    return jnp.dot(a, b, preferred_element_type=F32, **kw)


def _dot_nt(a, b):
    return lax.dot_general(a, b, (((1,), (1,)), ((), ())), preferred_element_type=F32)


def _dot_tn(a, b):
    return lax.dot_general(a, b, (((0,), (0,)), ((), ())), preferred_element_type=F32)


def _mod_kernel(c_ref, w_ref, b_ref, o_ref):
    s = _silu(c_ref[...])
    o_ref[0] = _dot(s, w_ref[0], precision=HIGHEST) + b_ref[0]


def _modulation(crows, mod_w, mod_b):
    depth, d, n3 = mod_w.shape
    r = crows.shape[0]
    tn = 1024
    return pl.pallas_call(
        _mod_kernel,
        grid=(depth, n3 // tn),
        in_specs=[pl.BlockSpec((r, d), lambda l, j: (0, 0)),
                  pl.BlockSpec((1, d, tn), lambda l, j: (l, 0, j)),
                  pl.BlockSpec((1, 1, tn), lambda l, j: (l, 0, j))],
        out_specs=pl.BlockSpec((1, r, tn), lambda l, j: (l, 0, j)),
        out_shape=jax.ShapeDtypeStruct((depth, r, n3), F32),
        compiler_params=_params(("parallel", "parallel")),
        name="modulation",
    )(crows, mod_w, mod_b.reshape(depth, 1, n3))


def _inproj_kernel(x_ref, g_ref, sc_ref, sh_ref, w_ref, o_ref, h_ref):
    @pl.when(pl.program_id(1) == 0)
    def _():
        x = x_ref[...]
        y = x * lax.rsqrt(jnp.mean(x * x, axis=-1, keepdims=True) + EPS) * g_ref[...]
        h_ref[...] = (y * (1.0 + sc_ref[0]) + sh_ref[0]).astype(BF16)

    o_ref[...] = _dot(h_ref[...], w_ref[...]).astype(o_ref.dtype)


def _in_proj(x2, g, scale, shift, rows_per_group, w, out_dtype, tm, tn):
    m, d = x2.shape
    n = w.shape[1]
    tpg = rows_per_group // tm
    return pl.pallas_call(
        _inproj_kernel,
        grid=(m // tm, n // tn),
        in_specs=[pl.BlockSpec((tm, d), lambda i, j: (i, 0)),
                  pl.BlockSpec((1, d), lambda i, j: (0, 0)),
                  pl.BlockSpec((1, 1, d), lambda i, j: (i // tpg, 0, 0)),
                  pl.BlockSpec((1, 1, d), lambda i, j: (i // tpg, 0, 0)),
                  pl.BlockSpec((d, tn), lambda i, j: (0, j))],
        out_specs=pl.BlockSpec((tm, tn), lambda i, j: (i, j)),
        out_shape=jax.ShapeDtypeStruct((m, n), out_dtype),
        scratch_shapes=[pltpu.VMEM((tm, d), BF16)],
        compiler_params=_params(("parallel", "arbitrary")),
        name="in_proj",
    )(x2, g, scale, shift, w)


def _outproj_kernel(a_ref, ga_ref, b_ref, w_ref, x_ref, g_ref, gate_ref, o_ref):
    wa = w_ref.shape[0] // 2
    a = (a_ref[...].astype(F32) * _silu(ga_ref[...].astype(F32))).astype(BF16)
    y = _dot(a, w_ref[0:wa, :]) + _dot(b_ref[...], w_ref[wa:, :])
    yn = y * lax.rsqrt(jnp.mean(y * y, axis=-1, keepdims=True) + EPS) * g_ref[...]
    o_ref[...] = x_ref[...] + gate_ref[0] * yn


def _out_proj(a, a_col, ga, ga_col, b, b_col, w, x2, g, gate, rows_per_group, tm):
    m, d = x2.shape
    wa = w.shape[0] // 2
    tpg = rows_per_group // tm
    return pl.pallas_call(
        _outproj_kernel,
        grid=(m // tm,),
        in_specs=[pl.BlockSpec((tm, wa), lambda i: (i, a_col)),
                  pl.BlockSpec((tm, wa), lambda i: (i, ga_col)),
                  pl.BlockSpec((tm, wa), lambda i: (i, b_col)),
                  pl.BlockSpec(w.shape, lambda i: (0, 0)),
                  pl.BlockSpec((tm, d), lambda i: (i, 0)),
                  pl.BlockSpec((1, d), lambda i: (0, 0)),
                  pl.BlockSpec((1, 1, d), lambda i: (i // tpg, 0, 0))],
        out_specs=pl.BlockSpec((tm, d), lambda i: (i, 0)),
        out_shape=jax.ShapeDtypeStruct((m, d), F32),
        compiler_params=_params(("parallel",)),
        name="out_proj",
    )(a, ga, b, w, x2, g, gate)


def _lru_kernel(xa_ref, xac_ref, cw_ref, cb_ref, wg_ref, bg_ref, lam_ref, y_ref, yc_ref,
                xp_ref, hf_ref, pf_ref, hb_ref, pb_ref):
    nseg = LRU_SEG
    npos = LRU_POS
    c8 = -LRU_C * _softplus(-lam_ref[...])
    left = LRU_CONV // 2
    cw = [cw_ref[k:k + 1, :] for k in range(LRU_CONV)]
    cb = cb_ref[...]
    zeros_h = jnp.zeros((LRU_HALO, LANES), F32)
    zeros_s = jnp.zeros((nseg, LANES), F32)
    ones_s = jnp.ones((nseg, LANES), F32)

    def run(src_ref, out_ref, n, c0f, c0b):
        seg = n // nseg
        nblk = seg // npos
        halo = LRU_HALO
        pitch = seg + 2 * halo + 8
        for j in range(nseg):
            lo, hi = j * seg - halo, (j + 1) * seg + halo
            if lo < 0:
                xp_ref[j * pitch:j * pitch + halo, :] = zeros_h
            if hi > n:
                xp_ref[j * pitch + seg + halo:j * pitch + seg + 2 * halo, :] = zeros_h
            lo_c, hi_c = max(lo, 0), min(hi, n)
            xp_ref[j * pitch + (lo_c - lo):j * pitch + (hi_c - lo), :] = src_ref[lo_c:hi_c, :].astype(F32)

        def gather(q):
            return jnp.concatenate([xp_ref[pl.ds(halo + q + c * 8 * pitch, 8, stride=pitch), :]
                                    for c in range(nseg // 8)], axis=0)

        def blk(i):
            return pl.ds(pl.multiple_of(i * npos * nseg, npos * nseg), npos * nseg)

        def pos(v, p):
            return v[p * nseg:(p + 1) * nseg, :]

        def fwd_body(i, carry):
            h, pc = carry
            p0 = i * npos
            xs = [gather(p0 + q - left) for q in range(npos + LRU_CONV - 1)]
            us = []
            for p in range(npos):
                u = cb + cw[0] * xs[p]
                for k in range(1, LRU_CONV):
                    u = u + cw[k] * xs[p + k]
                us.append(u)
            u = jnp.concatenate(us, axis=0)
            g = _dot(u.astype(BF16), wg_ref[0]) + bg_ref[0]
            coef = []
            for d in range(2):
                r = _sigmoid(g[:, (2 * d) * LANES:(2 * d + 1) * LANES])
                gi = _sigmoid(g[:, (2 * d + 1) * LANES:(2 * d + 2) * LANES])
                a = jnp.exp(c8[d:d + 1, :] * r)
                om = 1.0 - a * a
                coef.append((a, om * lax.rsqrt(jnp.maximum(om, LRU_TINY)) * (gi * u)))
            (af, bf), (ab, bb) = coef
            hb_ref[blk(i), :] = ab
            pb_ref[blk(i), :] = bb
            hs, ps = [], []
            for p in range(npos):
                a_p = pos(af, p)
                h = a_p * h + pos(bf, p)
                pc = a_p * pc
                hs.append(h)
                ps.append(pc)
            hf_ref[blk(i), :] = jnp.concatenate(hs, axis=0)
            pf_ref[blk(i), :] = jnp.concatenate(ps, axis=0)
            return h, pc

        hef, pef = lax.fori_loop(0, nblk, fwd_body, (zeros_s, ones_s))

        def bwd_body(ii, carry):
            h, pc = carry
            i = nblk - 1 - ii
            ab = hb_ref[blk(i), :]
            bb = pb_ref[blk(i), :]
            hs, ps = [None] * npos, [None] * npos
            for p in reversed(range(npos)):
                a_p = pos(ab, p)
                h = a_p * h + pos(bb, p)
                pc = a_p * pc
                hs[p] = h
                ps[p] = pc
            hb_ref[blk(i), :] = jnp.concatenate(hs, axis=0)
            pb_ref[blk(i), :] = jnp.concatenate(ps, axis=0)
            return h, pc

        heb, peb = lax.fori_loop(0, nblk, bwd_body, (zeros_s, ones_s))

        rows_f, c = [], c0f
        for j in range(nseg):
            rows_f.append(c)
            c = hef[j:j + 1, :] + pef[j:j + 1, :] * c
        final_f = c
        rows_b, c = [None] * nseg, c0b
        for j in reversed(range(nseg)):
            rows_b[j] = c
            c = heb[j:j + 1, :] + peb[j:j + 1, :] * c
        final_b = c
        cf = jnp.concatenate(rows_f, axis=0)
        cbk = jnp.concatenate(rows_b, axis=0)

        def out_body(p, _):
            rs = pl.ds(pl.multiple_of(p * nseg, nseg), nseg)
            y = hf_ref[rs, :] + pf_ref[rs, :] * cf + hb_ref[rs, :] + pb_ref[rs, :] * cbk
            for c in range(nseg // 8):
                out_ref[pl.ds(p + c * 8 * seg, 8, stride=seg), :] = y[c * 8:(c + 1) * 8, :]
            return 0
        lax.fori_loop(0, seg, out_body, 0, unroll=8)
        return final_f, final_b

    zero = jnp.zeros((1, LANES), F32)
    ff, fb = run(xac_ref, yc_ref, xac_ref.shape[0], zero, zero)
    run(xa_ref, y_ref, xa_ref.shape[0], ff, fb)


def _lru(proj, projc, xa_col, bsz, conv_w, conv_b, wg, bg, lam):
    w = conv_w.shape[1]
    s_len = proj.shape[0] // bsz
    c_len = projc.shape[0] // bsz
    nblk = w // LANES
    seq = lambda n: pl.BlockSpec((n, LANES), lambda b, j: (b, j))
    src = lambda n: pl.BlockSpec((n, LANES), lambda b, j: (b, xa_col + j))
    return pl.pallas_call(
        _lru_kernel,
        grid=(bsz, nblk),
        in_specs=[src(s_len), src(c_len),
                  pl.BlockSpec((LRU_CONV, LANES), lambda b, j: (0, j)),
                  pl.BlockSpec((1, LANES), lambda b, j: (0, j)),
                  pl.BlockSpec((1, LANES, 4 * LANES), lambda b, j: (j, 0, 0)),
                  pl.BlockSpec((1, 1, 4 * LANES), lambda b, j: (j, 0, 0)),
                  pl.BlockSpec((2, LANES), lambda b, j: (0, j))],
        out_specs=[seq(s_len), seq(c_len)],
        out_shape=[jax.ShapeDtypeStruct((proj.shape[0], w), F32), jax.ShapeDtypeStruct((projc.shape[0], w), F32)],
        scratch_shapes=[pltpu.VMEM((s_len + (2 * LRU_HALO + 8) * LRU_SEG, LANES), F32)]
        + [pltpu.VMEM((s_len, LANES), F32)] * 4,
        compiler_params=_params(("parallel", "parallel")),
        name="rglru",
    )(proj, projc, conv_w, conv_b, wg, bg, lam)


def _rope(v, cos, sin, lane):
    swapped = jnp.where((lane & 63) < 32, pltpu.roll(v, LANES - 32, 1), pltpu.roll(v, 32, 1))
    return v * cos + swapped * sin


def _attn_kernel(sink_ref, q_ref, gb_ref, k_ref, v_ref, kc_ref, vc_ref, cos_ref, sin_ref, bias_ref, o_ref):
    qb = pl.program_id(1)
    nb = pl.num_programs(1)
    group = ATT_HEADS // ATT_KV_HEADS
    scale = HEAD_DIM ** -0.5
    lane = lax.broadcasted_iota(jnp.int32, (BLOCK, LANES), 1)

    def blk(i):
        return pl.ds(pl.multiple_of(i * BLOCK, BLOCK), BLOCK)

    ip = jnp.maximum(qb - 1, 0)
    inx = jnp.minimum(qb + 1, nb - 1)
    cos_o, sin_o = cos_ref[blk(qb), :], sin_ref[blk(qb), :]
    cos_p, sin_p = cos_ref[blk(ip), :], sin_ref[blk(ip), :]
    cos_n, sin_n = cos_ref[blk(inx), :], sin_ref[blk(inx), :]

    bias = jnp.concatenate([bias_ref[0]] * group, axis=0)

    for h in range(ATT_KV_HEADS):
        ksl = slice(h * HEAD_DIM, (h + 1) * HEAD_DIM)
        kp = _rope(k_ref[blk(ip), ksl].astype(F32), cos_p, sin_p, lane)
        ko = _rope(k_ref[blk(qb), ksl].astype(F32), cos_o, sin_o, lane)
        kn = _rope(k_ref[blk(inx), ksl].astype(F32), cos_n, sin_n, lane)
        kcat = jnp.concatenate([kp.astype(BF16), ko.astype(BF16), kn.astype(BF16), kc_ref[:, ksl]], axis=0)
        vcat = jnp.concatenate([v_ref[blk(ip), ksl], v_ref[blk(qb), ksl], v_ref[blk(inx), ksl],
                                vc_ref[:, ksl]], axis=0)
        qs, sinks = [], []
        for g in range(group):
            hh = h * group + g
            qh = q_ref[:, hh * HEAD_DIM:(hh + 1) * HEAD_DIM].astype(F32)
            qs.append((_rope(qh, cos_o, sin_o, lane) * scale).astype(BF16))
            sinks.append(jnp.full((BLOCK, 1), sink_ref[hh], F32))
        q4 = jnp.concatenate(qs, axis=0)
        sk = jnp.concatenate(sinks, axis=0)
        s = _dot_nt(q4, kcat) + bias
        m = jnp.maximum(jnp.max(s, axis=-1, keepdims=True), sk)
        p = jnp.exp(s - m)
        denom = jnp.exp(sk - m) + jnp.sum(p, axis=-1, keepdims=True)
        o = _dot(p.astype(BF16), vcat) / denom
        for g in range(group):
            hh = h * group + g
            hs = slice(hh * HEAD_DIM, (hh + 1) * HEAD_DIM)
            gate = _silu(gb_ref[:, hs].astype(F32))
            o_ref[:, hs] = (o[g * BLOCK:(g + 1) * BLOCK, :] * gate).astype(o_ref.dtype)


def _attention(sink, proj, projc, bsz, cols, cos_t, sin_t):
    s_len = proj.shape[0] // bsz
    c_len = projc.shape[0] // bsz
    nb = s_len // BLOCK
    aw = ATT_HEADS * HEAD_DIM
    kw = ATT_KV_HEADS * HEAD_DIM
    qi = np.arange(BLOCK)[:, None]
    kj = np.arange(BLOCK)[None, :]
    variants = []
    for v in range(4):
        prev = np.where((kj >= qi) & bool(v & 1), 0.0, NEG)
        nxt = np.where((kj <= qi) & bool(v & 2), 0.0, NEG)
        variants.append(np.concatenate([prev, np.zeros((BLOCK, BLOCK)), nxt, np.zeros((BLOCK, c_len))], axis=1))
    bias = jnp.asarray(np.stack(variants), F32)
    ncol = 3 * BLOCK + c_len

    def variant(b, i):
        return ((i > 0).astype(jnp.int32) + 2 * (i < nb - 1).astype(jnp.int32), 0, 0)

    return pl.pallas_call(
        _attn_kernel,
        grid=(bsz, nb),
        in_specs=[pl.BlockSpec(memory_space=pltpu.SMEM),
                  pl.BlockSpec((BLOCK, aw), lambda b, i: (b * nb + i, cols["q"])),
                  pl.BlockSpec((BLOCK, aw), lambda b, i: (b * nb + i, cols["gb"])),
                  pl.BlockSpec((s_len, kw), lambda b, i: (b, cols["k"])),
                  pl.BlockSpec((s_len, kw), lambda b, i: (b, cols["v"])),
                  pl.BlockSpec((c_len, kw), lambda b, i: (b, cols["k"])),
                  pl.BlockSpec((c_len, kw), lambda b, i: (b, cols["v"])),
                  pl.BlockSpec((s_len, LANES), lambda b, i: (0, 0)),
                  pl.BlockSpec((s_len, LANES), lambda b, i: (0, 0)),
                  pl.BlockSpec((1, BLOCK, ncol), variant)],
        out_specs=pl.BlockSpec((BLOCK, aw), lambda b, i: (b * nb + i, 0)),
        out_shape=jax.ShapeDtypeStruct((proj.shape[0], aw), BF16),
        compiler_params=_params(("parallel", "arbitrary")),
        name="window_attention",
    )(sink, proj, proj, proj, proj, projc, projc, cos_t, sin_t, bias)


def _ctx_attn_kernel(sink_ref, q_ref, gb_ref, k_ref, v_ref, o_ref):
    group = ATT_HEADS // ATT_KV_HEADS
    scale = HEAD_DIM ** -0.5
    n = q_ref.shape[0]
    for h in range(ATT_KV_HEADS):
        ksl = slice(h * HEAD_DIM, (h + 1) * HEAD_DIM)
        qs, sinks = [], []
        for g in range(group):
            hh = h * group + g
            qs.append((q_ref[:, hh * HEAD_DIM:(hh + 1) * HEAD_DIM].astype(F32) * scale).astype(BF16))
            sinks.append(jnp.full((n, 1), sink_ref[hh], F32))
        q4 = jnp.concatenate(qs, axis=0)
        sk = jnp.concatenate(sinks, axis=0)
        s = _dot_nt(q4, k_ref[:, ksl])
        m = jnp.maximum(jnp.max(s, axis=-1, keepdims=True), sk)
        p = jnp.exp(s - m)
        denom = jnp.exp(sk - m) + jnp.sum(p, axis=-1, keepdims=True)
        o = _dot(p.astype(BF16), v_ref[:, ksl]) / denom
        for g in range(group):
            hh = h * group + g
            hs = slice(hh * HEAD_DIM, (hh + 1) * HEAD_DIM)
            o_ref[:, hs] = (o[g * n:(g + 1) * n, :] * _silu(gb_ref[:, hs].astype(F32))).astype(o_ref.dtype)


def _ctx_attention(sink, projc, bsz, cols):
    c_len = projc.shape[0] // bsz
    aw = ATT_HEADS * HEAD_DIM
    kw = ATT_KV_HEADS * HEAD_DIM
    return pl.pallas_call(
        _ctx_attn_kernel,
        grid=(bsz,),
        in_specs=[pl.BlockSpec(memory_space=pltpu.SMEM),
                  pl.BlockSpec((c_len, aw), lambda b: (b, cols["q"])),
                  pl.BlockSpec((c_len, aw), lambda b: (b, cols["gb"])),
                  pl.BlockSpec((c_len, kw), lambda b: (b, cols["k"])),
                  pl.BlockSpec((c_len, kw), lambda b: (b, cols["v"]))],
        out_specs=pl.BlockSpec((c_len, aw), lambda b: (b, 0)),
        out_shape=jax.ShapeDtypeStruct((projc.shape[0], aw), BF16),
        compiler_params=_params(("parallel",)),
        name="context_attention",
    )(sink, projc, projc, projc, projc)


def _ret_kernel(dl_ref, q_ref, k_ref, v_ref, gd_ref, qc_ref, kc_ref, vc_ref, gdc_ref, o_ref, oc_ref,
                ot_ref, att_ref, u_ref, st_ref, vt_ref, dm_ref):
    c = RET_CHUNK
    s_len = q_ref.shape[0]
    c_len = qc_ref.shape[0]
    lg = -_softplus(-dl_ref[0])
    lgf, lgb = lg[0:1, :], lg[1:2, :]
    wide = lambda t: jnp.concatenate([t] * (c // LANES), axis=1)
    kj = lax.broadcasted_iota(jnp.int32, (c, c), 0)
    qi = lax.broadcasted_iota(jnp.int32, (c, c), 1)
    diff = (qi - kj).astype(F32)
    dm_ref[...] = jnp.where(qi >= kj, jnp.exp(jnp.maximum(diff, 0.0) * wide(lgf)),
                            jnp.exp(jnp.maximum(-diff, 0.0) * wide(lgb)))
    idx = lax.broadcasted_iota(jnp.int32, (c, LANES), 0).astype(F32)
    qdec_f = jnp.exp((idx + 1.0) * lgf)
    kdec_f = jnp.exp((c - 1.0 - idx) * lgf)
    qdec_b = jnp.exp((c - idx) * lgb)
    kdec_b = jnp.exp(idx * lgb)
    cdec_f = jnp.exp(c * lgf)
    cdec_b = jnp.exp(c * lgb)

    def chunk(j):
        return pl.ds(pl.multiple_of(j * c, c), c)

    def transpose_v(vr, n):
        def body(j, _):
            vt_ref[:, chunk(j)] = vr[chunk(j), :].astype(F32).T.astype(BF16)
            return 0
        lax.fori_loop(0, n // c, body, 0, unroll=min(8, n // c))

    def scores(qr, kr, n):
        def body(j, _):
            att_ref[j] = (_dot_nt(kr[chunk(j), :], qr[chunk(j), :]) * dm_ref[...]).astype(BF16)
            return 0
        lax.fori_loop(0, n // c, body, 0, unroll=min(4, n // c))

    def intra(kr, n):
        def body(j, _):
            vt = vt_ref[:, chunk(j)]
            ot_ref[:, chunk(j)] = _dot(vt, att_ref[j])
            kf = kr[chunk(j), :].astype(F32)
            kcat = jnp.concatenate([(kf * kdec_f).astype(BF16), (kf * kdec_b).astype(BF16)], axis=1)
            u_ref[j] = _dot(vt, kcat)
            return 0
        lax.fori_loop(0, n // c, body, 0, unroll=min(4, n // c))

    def states(n, sf, sb):
        nch = n // c

        def fbody(j, s):
            st_ref[j, :, 0:RET_DK] = s.astype(BF16)
            return s * cdec_f + u_ref[j, :, 0:RET_DK]
        sf = lax.fori_loop(0, nch, fbody, sf)

        def bbody(jj, s):
            j = nch - 1 - jj
            st_ref[j, :, RET_DK:] = s.astype(BF16)
            return s * cdec_b + u_ref[j, :, RET_DK:]
        sb = lax.fori_loop(0, nch, bbody, sb)
        return sf, sb

    def cross(qr, gr, outr, n):
        def body(j, _):
            qf = qr[chunk(j), :].astype(F32)
            qcat = jnp.concatenate([(qf * qdec_f).astype(BF16), (qf * qdec_b).astype(BF16)], axis=1)
            ot = ot_ref[:, chunk(j)] + _dot_nt(st_ref[j], qcat)
            o = ot.T * (RET_DK ** -0.5)
            o = o * lax.rsqrt(jnp.mean(o * o, axis=-1, keepdims=True) + EPS)
            outr[chunk(j), :] = (o * _silu(gr[chunk(j), :].astype(F32))).astype(outr.dtype)
            return 0
        lax.fori_loop(0, n // c, body, 0, unroll=min(4, n // c))

    def run(qr, kr, vr, gr, outr, n, sf, sb):
        transpose_v(vr, n)
        scores(qr, kr, n)
        intra(kr, n)
        finals = states(n, sf, sb)
        cross(qr, gr, outr, n)
        return finals

    zero = jnp.zeros((LANES, RET_DK), F32)
    sf, sb = run(qc_ref, kc_ref, vc_ref, gdc_ref, oc_ref, c_len, zero, zero)
    run(q_ref, k_ref, v_ref, gd_ref, o_ref, s_len, sf, sb)


def _retention(dl, proj, projc, bsz, cols):
    s_len = proj.shape[0] // bsz
    c_len = projc.shape[0] // bsz
    hb = lambda n, col: pl.BlockSpec((n, LANES), lambda b, h: (b, col + h))
    return pl.pallas_call(
        _ret_kernel,
        grid=(bsz, RET_HEADS),
        in_specs=[pl.BlockSpec((1, 2, LANES), lambda b, h: (h, 0, 0)),
                  hb(s_len, cols["q"]), hb(s_len, cols["k"]), hb(s_len, cols["v"]), hb(s_len, cols["gd"]),
                  hb(c_len, cols["q"]), hb(c_len, cols["k"]), hb(c_len, cols["v"]), hb(c_len, cols["gd"])],
        out_specs=[pl.BlockSpec((s_len, LANES), lambda b, h: (b, h)),
                   pl.BlockSpec((c_len, LANES), lambda b, h: (b, h))],
        out_shape=[jax.ShapeDtypeStruct((proj.shape[0], RET_HEADS * LANES), BF16),
                   jax.ShapeDtypeStruct((projc.shape[0], RET_HEADS * LANES), BF16)],
        scratch_shapes=[pltpu.VMEM((LANES, s_len), F32),
                        pltpu.VMEM((s_len // RET_CHUNK, RET_CHUNK, RET_CHUNK), BF16),
                        pltpu.VMEM((s_len // RET_CHUNK, LANES, 2 * RET_DK), F32),
                        pltpu.VMEM((s_len // RET_CHUNK, LANES, 2 * RET_DK), BF16),
                        pltpu.VMEM((LANES, s_len), BF16),
                        pltpu.VMEM((RET_CHUNK, RET_CHUNK), F32)],
        compiler_params=_params(("parallel", "parallel")),
        name="retention",
    )(dl, proj, proj, proj, proj, projc, projc, projc, projc)


def _hy_hid_kernel(z_ref, w1_ref, b1_ref, w2_ref, b2_ref, f_ref, o_ref):
    f = f_ref[...]
    h = jnp.sin(f * (_dot(z_ref[...], w1_ref[...], precision=HIGHEST) + b1_ref[...]))
    o_ref[...] = jnp.sin(f * (_dot(h, w2_ref[...], precision=HIGHEST) + b2_ref[...]))


def _hy_hidden(zfull, w1p, b1, w2, b2, freq):
    n, e = zfull.shape
    fd = w2.shape[0]
    tr = min(n, 1024)
    full = lambda shp: pl.BlockSpec(shp, lambda i: (0, 0))
    return pl.pallas_call(
        _hy_hid_kernel,
        grid=(n // tr,),
        in_specs=[pl.BlockSpec((tr, e), lambda i: (i, 0)), full((e, fd)), full((1, fd)), full((fd, fd)),
                  full((1, fd)), full((1, fd))],
        out_specs=pl.BlockSpec((tr, fd), lambda i: (i, 0)),
        out_shape=jax.ShapeDtypeStruct((n, fd), F32),
        compiler_params=_params(("parallel",)),
        name="hyena_filter_mlp",
    )(zfull, w1p, b1, w2, b2, freq)


def _hy_filt_kernel(hid_ref, w3f_ref, w3b_ref, tn_ref, dl_ref, o_ref):
    half = hid_ref.shape[0] // 2
    decay = jnp.exp(-tn_ref[...] * jnp.abs(dl_ref[...]))
    top = _dot(hid_ref[0:half, :], w3f_ref[...], precision=HIGHEST)
    bot = _dot(hid_ref[half:, :], w3b_ref[...], precision=HIGHEST)
    row = lax.broadcasted_iota(jnp.int32, bot.shape, 0)
    bot = jnp.where(row == 0, 0.0, bot)
    o_ref[0, 0:half, :] = top * decay[0:half, :]
    o_ref[0, half:, :] = bot * decay[half:, :]


def _hy_filters(hid, w3, tn_full, deltas):
    n, fd = hid.shape
    wch = deltas.shape[1]
    nsl = wch // LANES
    return pl.pallas_call(
        _hy_filt_kernel,
        grid=(HY_ORDER, nsl),
        in_specs=[pl.BlockSpec((n, fd), lambda o, j: (0, 0)),
                  pl.BlockSpec((fd, LANES), lambda o, j: (0, o * 2 * nsl + j)),
                  pl.BlockSpec((fd, LANES), lambda o, j: (0, o * 2 * nsl + nsl + j)),
                  pl.BlockSpec((n, LANES), lambda o, j: (0, 0)),
                  pl.BlockSpec((1, LANES), lambda o, j: (0, j))],
        out_specs=pl.BlockSpec((1, n, LANES), lambda o, j: (o, 0, j)),
        out_shape=jax.ShapeDtypeStruct((HY_ORDER, n, wch), F32),
        compiler_params=_params(("parallel", "parallel")),
        name="hyena_filter",
    )(hid, w3, w3, tn_full, deltas)


def _hy_kfft_kernel(kern_ref, f1_ref, f2_ref, o_ref, g_ref, *, n1, n2):
    pg = 2 * n1 + HY_PITCH_PAD

    def stage1(i2, _):
        x = kern_ref[pl.ds(i2, n1, stride=n2), :].astype(BF16)
        g_ref[pl.ds(pl.multiple_of(i2 * pg, 8), 2 * n1), :] = _dot(f1_ref[i2], x)
        return 0
    lax.fori_loop(0, n2, stage1, 0, unroll=HY_UNROLL)

    def stage2(k1, _):
        x = jnp.concatenate([g_ref[pl.ds(k1, n2, stride=pg), :],
                             g_ref[pl.ds(n1 + k1, n2, stride=pg), :]], axis=0).astype(BF16)
        o_ref[pl.ds(pl.multiple_of(k1 * 2 * n2, 2 * n2), 2 * n2), :] = _dot(f2_ref[...], x)
        return 0
    lax.fori_loop(0, n1, stage2, 0, unroll=HY_UNROLL // 2)


def _hy_kfft(kern, f1k, f2, n1, n2):
    orders, n, wch = kern.shape
    nsl = wch // LANES
    return pl.pallas_call(
        functools.partial(_hy_kfft_kernel, n1=n1, n2=n2),
        grid=(orders, nsl),
        in_specs=[pl.BlockSpec((None, n, LANES), lambda o, j: (o, 0, j)),
                  pl.BlockSpec(f1k.shape, lambda o, j: (0, 0, 0)),
                  pl.BlockSpec(f2.shape, lambda o, j: (0, 0))],
        out_specs=pl.BlockSpec((None, None, 2 * n, LANES), lambda o, j: (o, j, 0, 0)),
        out_shape=jax.ShapeDtypeStruct((orders, nsl, 2 * n, LANES), F32),
        scratch_shapes=[pltpu.VMEM((n2 * (2 * n1 + HY_PITCH_PAD), LANES), F32)],
        compiler_params=_params(("parallel", "parallel")),
        name="hyena_filter_fft",
    )(kern, f1k, f2)


def _hy_prep_kernel(z_ref, w_ref, b_ref, o_ref, zp_ref, *, n2):
    l_len = z_ref.shape[0]
    h1 = l_len // n2
    halo = HY_HALO
    pitch = n2 + 2 * halo + 8
    zeros_h = jnp.zeros((halo, LANES), F32)
    for i1 in range(h1):
        lo, hi = i1 * n2 - halo, (i1 + 1) * n2 + halo
        if lo < 0:
            zp_ref[i1 * pitch:i1 * pitch + halo, :] = zeros_h
        if hi > l_len:
            zp_ref[i1 * pitch + n2 + halo:i1 * pitch + n2 + 2 * halo, :] = zeros_h
        lo_c, hi_c = max(lo, 0), min(hi, l_len)
        zp_ref[i1 * pitch + (lo_c - lo):i1 * pitch + (hi_c - lo), :] = z_ref[lo_c:hi_c, :].astype(F32)
    w0, w1, w2, b = w_ref[0:1, :], w_ref[1:2, :], w_ref[2:3, :], b_ref[...]

    def tap(j):
        return zp_ref[pl.ds(halo + j, h1, stride=pitch), :]

    def body(i2, carry):
        zm, z0 = carry
        zn = tap(i2 + 1)
        val = b + w0 * zm + w1 * z0 + w2 * zn
        o_ref[pl.ds(pl.multiple_of(i2 * h1, h1), h1), :] = val.astype(o_ref.dtype)
        return z0, zn
    lax.fori_loop(0, n2, body, (tap(-1), tap(0)), unroll=HY_UNROLL)


def _hy_stage_dtype(l_len, n2):
    return BF16 if (l_len // n2) % 16 == 0 else F32


def _hy_prep(proj, z_col, nz, bsz, conv_w, conv_b, n2):
    l_len = proj.shape[0] // bsz
    h1 = l_len // n2
    return pl.pallas_call(
        functools.partial(_hy_prep_kernel, n2=n2),
        grid=(bsz, nz),
        in_specs=[pl.BlockSpec((l_len, LANES), lambda b, j: (b, z_col + j)),
                  pl.BlockSpec((3, LANES), lambda b, j: (0, j)),
                  pl.BlockSpec((1, LANES), lambda b, j: (0, j))],
        out_specs=pl.BlockSpec((l_len, LANES), lambda b, j: (b, j)),
        out_shape=jax.ShapeDtypeStruct((proj.shape[0], nz * LANES), _hy_stage_dtype(l_len, n2)),
        scratch_shapes=[pltpu.VMEM((h1 * (n2 + 2 * HY_HALO + 8), LANES), F32)],
        compiler_params=_params(("parallel", "parallel")),
        name="hyena_short_conv",
    )(proj, conv_w, conv_b)


def _pack2(re, im):
    r = lax.bitcast_convert_type(re, jnp.uint32) + jnp.uint32(0x8000)
    i = lax.bitcast_convert_type(im, jnp.uint32) + jnp.uint32(0x8000)
    return (r & jnp.uint32(0xFFFF0000)) | (i >> 16)


def _unpack2(w):
    re = lax.bitcast_convert_type(w & jnp.uint32(0xFFFF0000), F32)
    im = lax.bitcast_convert_type(w << 16, F32)
    return jnp.concatenate([re, im], axis=0).astype(BF16)


def _hy_conv_kernel(y_ref, x_ref, kh_ref, bias_ref, f1_ref, f2_ref, f2i_ref, f1i_ref, o_ref, g_ref,
                    *, n1, n2, natural_out):
    h1 = n1 // 2
    l_len = h1 * n2
    pg = n1 + HY_PITCH_PAD
    bias = bias_ref[...]
    unroll = HY_UNROLL

    def rows(seq, i2):
        return pl.ds(pl.multiple_of(seq * l_len + i2 * h1, h1), h1)

    def lanes2(a, b):
        return jnp.concatenate([a, b], axis=1)

    def stage1(i2, _):
        x = lanes2(jnp.concatenate([y_ref[rows(0, i2), :], y_ref[rows(1, i2), :]], axis=0),
                   jnp.concatenate([y_ref[rows(2, i2), :], y_ref[rows(3, i2), :]], axis=0)).astype(BF16)
        a = _dot(f1_ref[i2], x)
        w = _pack2(a[0:n1, :], a[n1:, :])
        blk = pl.ds(pl.multiple_of(i2 * pg, 8), n1)
        g_ref[0, blk, :] = w[:, 0:LANES]
        g_ref[1, blk, :] = w[:, LANES:]
        return 0
    lax.fori_loop(0, n2, stage1, 0, unroll=2 * unroll)

    def stage23(t, _):
        outs = []
        for u in range(unroll):
            k1 = t * unroll + u
            col = pl.ds(k1, n2, stride=pg)
            x = lanes2(_unpack2(g_ref[0, col, :]), _unpack2(g_ref[1, col, :]))
            yh = _dot(f2_ref[...], x)
            base = pl.multiple_of(k1 * 2 * n2, 2 * n2)
            kr = kh_ref[pl.ds(base, n2), :]
            ki = kh_ref[pl.ds(base + n2, n2), :]
            kr, ki = lanes2(kr, kr), lanes2(ki, ki)
            yr, yi = yh[0:n2, :], yh[n2:, :]
            z = jnp.concatenate([yr * kr - yi * ki, yr * ki + yi * kr], axis=0).astype(BF16)
            c = _dot(f2i_ref[...], z)
            outs.append((col, _pack2(c[0:n2, :], c[n2:, :])))
        for col, w in outs:
            g_ref[0, col, :] = w[:, 0:LANES]
            g_ref[1, col, :] = w[:, LANES:]
        return 0
    lax.fori_loop(0, n1 // unroll, stage23, 0)

    def stage4(i2, _):
        blk = pl.ds(pl.multiple_of(i2 * pg, 8), n1)
        x = lanes2(_unpack2(g_ref[0, blk, :]), _unpack2(g_ref[1, blk, :]))
        yc = _dot(f1i_ref[i2], x)
        for seq in range(4):
            pair, half = seq // 2, seq % 2
            sl = rows(seq, i2)
            conv = yc[half * h1:(half + 1) * h1, pair * LANES:(pair + 1) * LANES]
            val = x_ref[sl, :].astype(F32) * (conv + bias * y_ref[sl, :].astype(F32))
            if natural_out:
                o_ref[pl.ds(seq * l_len + i2, h1, stride=n2), :] = val
            else:
                o_ref[sl, :] = val.astype(o_ref.dtype)
        return 0
    lax.fori_loop(0, n2, stage4, 0, unroll=unroll)


def _hy_conv(ysrc, ycol, xsrc, xcol, khat, order, bias, mats, bsz, n1, n2, natural_out):
    l_len = ysrc.shape[0] // bsz
    nsl = khat.shape[1]
    f1, f2, f2i, f1i = mats
    once = pl.Buffered(1)
    cst2 = lambda a: pl.BlockSpec(a.shape, lambda j, p: (0, 0), pipeline_mode=once)
    cst3 = lambda a: pl.BlockSpec(a.shape, lambda j, p: (0, 0, 0), pipeline_mode=once)
    return pl.pallas_call(
        functools.partial(_hy_conv_kernel, n1=n1, n2=n2, natural_out=natural_out),
        grid=(nsl, bsz // 4),
        in_specs=[pl.BlockSpec((4 * l_len, LANES), lambda j, p: (p, ycol + j)),
                  pl.BlockSpec((4 * l_len, LANES), lambda j, p: (p, xcol + j)),
                  pl.BlockSpec((None, None, khat.shape[2], LANES), lambda j, p: (order, j, 0, 0),
                               pipeline_mode=once),
                  pl.BlockSpec((1, LANES), lambda j, p: (0, j)),
                  cst3(f1), cst2(f2), cst2(f2i), cst3(f1i)],
        out_specs=pl.BlockSpec((4 * l_len, LANES), lambda j, p: (p, j)),
        out_shape=jax.ShapeDtypeStruct((ysrc.shape[0], nsl * LANES),
                                       F32 if natural_out else _hy_stage_dtype(l_len, n2)),
        scratch_shapes=[pltpu.VMEM((2, n2 * (n1 + HY_PITCH_PAD), LANES), jnp.uint32)],
        compiler_params=_params(("parallel", "arbitrary")),
        name="hyena_long_conv",
    )(ysrc, xsrc, khat, bias[order][None], f1, f2, f2i, f1i)


def _dft_tables(l_len):
    n = 2 * l_len
    n2 = 128 if l_len >= 1024 else 32
    n1 = n // n2
    h1 = n1 // 2
    i1 = np.arange(n1)[None, None, :]
    k1 = np.arange(n1)[None, :, None]
    i2 = np.arange(n2)[:, None, None]
    ph = 2 * np.pi * (i1 * k1 / n1 + i2 * k1 / n)
    c, s = np.cos(ph), np.sin(ph)
    ch, sh = c[:, :, :h1], s[:, :, :h1]
    f1 = np.concatenate([np.concatenate([ch, sh], 2), np.concatenate([-sh, ch], 2)], 1)
    f1k = np.concatenate([c, -s], 1)
    ct, st = np.swapaxes(ch, 1, 2), np.swapaxes(sh, 1, 2)
    f1i = np.concatenate([np.concatenate([ct, -st], 2), np.concatenate([st, ct], 2)], 1) / n
    a = np.arange(n2)
    ph2 = 2 * np.pi * np.outer(a, a) / n2
    c2, s2 = np.cos(ph2), np.sin(ph2)
    f2 = np.block([[c2, s2], [-s2, c2]])
    f2i = np.block([[c2, -s2], [s2, c2]])
    bf = lambda m: jnp.asarray(m, dtype=F32).astype(BF16)
    return n1, n2, (bf(f1), bf(f2), bf(f2i), bf(f1i)), bf(f1k)


def _filter_positions(l_len, width):
    lag = np.concatenate([np.arange(l_len), l_len - np.arange(l_len)]).astype(np.float64)
    t = lag / (l_len - 1)
    bands = (HY_EMB - 1) // 2
    w = 2.0 * np.pi * lag / l_len
    f = np.linspace(1e-4, bands - 1, bands)[None]
    z = np.concatenate([t[:, None], np.cos(f * w[:, None]), -np.sin(f * w[:, None])], axis=-1)
    z = np.pad(z, ((0, 0), (0, HY_EMB_PAD - HY_EMB)))
    tn = np.repeat(t[:, None], LANES, axis=1)
    max_decay = math.log(HY_DECAY_TARGET) / HY_FAST_PCT
    min_decay = math.log(HY_DECAY_TARGET) / HY_SLOW_PCT
    deltas = np.linspace(min_decay, max_decay, width)[None]
    return jnp.asarray(z, F32), jnp.asarray(tn, F32), jnp.asarray(deltas, F32)


def _rope_tables(seq):
    n_rows = seq // GRID_W
    row = np.repeat(np.arange(n_rows), GRID_W).astype(np.float64)
    col = np.tile(np.arange(GRID_W), n_rows).astype(np.float64)
    half = HEAD_DIM // 2
    inv = ROPE_BASE ** (-np.arange(0, half, 2, dtype=np.float64) / half)
    ar, ac = row[:, None] * inv, col[:, None] * inv
    cos = np.concatenate([np.cos(ar), np.cos(ar), np.cos(ac), np.cos(ac)], axis=1)
    sin = np.concatenate([-np.sin(ar), np.sin(ar), -np.sin(ac), np.sin(ac)], axis=1)
    return jnp.asarray(cos, F32), jnp.asarray(sin, F32)


def _hyena_spectra(l_len, width, w1, b1, w2, b2, w3, freq, tables):
    n1, n2, mats, f1k = tables
    zfull, tn_full, deltas = _filter_positions(l_len, width)
    w1p = jnp.pad(w1, ((0, HY_EMB_PAD - HY_EMB), (0, 0)))
    hid = _hy_hidden(zfull, w1p, b1[None], w2, b2[None], freq[None])
    kern = _hy_filters(hid, w3, tn_full, deltas)
    return _hy_kfft(kern, f1k, mats[1], n1, n2)


def _hyena_seq(zc, bsz, khat, bias, tables):
    n1, n2, mats, _ = tables
    nsl = khat.shape[1]
    y1 = _hy_conv(zc, 0, zc, nsl, khat, 0, bias, mats, bsz, n1, n2, False)
    return _hy_conv(y1, 0, zc, 2 * nsl, khat, 1, bias, mats, bsz, n1, n2, True)


def kernel(x, c, ctx, c_ctx, mod_w, mod_b, norm_pre, norm_post, ev_w_in, ev_w_out, lru_conv_w, lru_conv_b, lru_wa, lru_ba, lru_wx, lru_bx, lru_lambda, attn_sink, od_w_in, od_w_out, hy_conv_w, hy_conv_b, hy_w1, hy_b1, hy_w2, hy_b2, hy_w3, hy_freq, hy_bias, ret_decay_logit):
    bsz, s_len, d = x.shape
    c_len = ctx.shape[1]
    depth = mod_w.shape[0]
    assert bsz % 4 == 0 and bsz <= 16 and s_len % 1024 == 0 and c_len % 256 == 0 and c_len <= s_len

    crows = jnp.concatenate([c, c_ctx[None], jnp.zeros((24 - bsz - 1, d), F32)], axis=0)
    mod = _modulation(crows, mod_w, mod_b)

    cos_t, sin_t = _rope_tables(s_len)
    tab_l = _dft_tables(s_len)
    tab_c = _dft_tables(c_len)

    x2 = x.reshape(bsz * s_len, d)
    ctx2 = ctx.reshape(bsz * c_len, d)
    tm = 1024

    for l in range(depth):
        need_ctx = l < depth - 1
        shift, scale, gate = (mod[l, :bsz, i * d:(i + 1) * d].reshape(bsz, 1, d) for i in range(3))
        shift_c, scale_c, gate_c = (mod[l, bsz:bsz + 1, i * d:(i + 1) * d].reshape(1, 1, d) for i in range(3))
        g_pre = norm_pre[l][None]
        g_post = norm_post[l][None]

        def proj(w, tn):
            wb = w.astype(BF16)
            p = _in_proj(x2, g_pre, scale, shift, s_len, wb, BF16, tm, tn)
            pc = _in_proj(ctx2, g_pre, scale_c, shift_c, bsz * c_len, wb, BF16, tm, tn)
            return p, pc

        if l % 2 == 0:
            e = l // 2
            w_in = ev_w_in[e]
            w_ord = jnp.concatenate([w_in[:, 1024:2048], w_in[:, 2048:3072], w_in[:, 3584:4608],
                                     w_in[:, 3072:3584], w_in[:, 0:1024]], axis=1)
            pr, prc = proj(w_ord, 1536)
            cols = {"q": 1, "gb": 2, "k": 12, "v": 13}
            xa_col = 3584 // LANES
            nblk = lru_wa.shape[2]
            wg = jnp.stack([lru_wa[e, 0], lru_wx[e, 0], lru_wa[e, 1], lru_wx[e, 1]], axis=1)
            wg = jnp.transpose(wg, (0, 2, 1, 3)).reshape(nblk, LRU_BLOCK_W, 4 * LRU_BLOCK_W).astype(BF16)
            bg = jnp.stack([lru_ba[e, 0], lru_bx[e, 0], lru_ba[e, 1], lru_bx[e, 1]], axis=0)
            bg = jnp.transpose(bg.reshape(4, nblk, LRU_BLOCK_W), (1, 0, 2)).reshape(nblk, 1, 4 * LRU_BLOCK_W)
            ya, yac = _lru(pr, prc, xa_col, bsz, lru_conv_w[e], lru_conv_b[e][None], wg, bg, lru_lambda[e])
            yb = _attention(attn_sink[e], pr, prc, bsz, cols, cos_t, sin_t)
            w_out = ev_w_out[e].astype(BF16)
            x2_new = _out_proj(ya, 0, pr, 0, yb, 0, w_out, x2, g_post, gate, s_len, tm)
            if need_ctx:
                ybc = _ctx_attention(attn_sink[e], prc, bsz, cols)
                ctx2 = _out_proj(yac, 0, prc, 0, ybc, 0, w_out, ctx2, g_post, gate_c, bsz * c_len, tm)
            x2 = x2_new
        else:
            o = l // 2
            w_in = od_w_in[o]
            w_ord = jnp.concatenate([w_in[:, 3072:8192], w_in[:, 0:3072]], axis=1)
            pr, prc = proj(w_ord, 2048)
            cols = {"q": 8, "k": 16, "v": 24, "gd": 32}
            z_col = 5120 // LANES
            width = hy_bias.shape[2]
            nz = (HY_ORDER + 1) * width // LANES
            fargs = (hy_w1[o], hy_b1[o], hy_w2[o], hy_b2[o], hy_w3[o], hy_freq[o])
            khat = _hyena_spectra(s_len, width, *fargs, tab_l)
            zconv = _hy_prep(pr, z_col, nz, bsz, hy_conv_w[o], hy_conv_b[o][None], tab_l[1])
            yh = _hyena_seq(zconv, bsz, khat, hy_bias[o], tab_l)
            dl = jnp.broadcast_to(jnp.transpose(ret_decay_logit[o])[:, :, None], (RET_HEADS, 2, LANES))
            yd, ydc = _retention(dl, pr, prc, bsz, cols)
            w_out = od_w_out[o].astype(BF16)
            x2_new = _out_proj(yh, 0, pr, 0, yd, 0, w_out, x2, g_post, gate, s_len, tm)
            if need_ctx:
                khat_c = _hyena_spectra(c_len, width, *fargs, tab_c)
                zcconv = _hy_prep(prc, z_col, nz, bsz, hy_conv_w[o], hy_conv_b[o][None], tab_c[1])
                yhc = _hyena_seq(zcconv, bsz, khat_c, hy_bias[o], tab_c)
                ctx2 = _out_proj(yhc, 0, prc, 0, ydc, 0, w_out, ctx2, g_post, gate_c, bsz * c_len, tm)
            x2 = x2_new
    return x2.reshape(bsz, s_len, d)
```

```python
import functools
import math

import numpy as np
import jax
import jax.numpy as jnp
from jax import lax
from jax.experimental import pallas as pl
from jax.experimental.pallas import tpu as pltpu

F32 = jnp.float32
BF16 = jnp.bfloat16
HIGHEST = lax.Precision.HIGHEST

EPS = 1e-6
GRID_W = 64
LANES = 128
LRU_BLOCK_W = 128
LRU_CONV = 4
LRU_C = 8.0
LRU_SEG = 16
LRU_POS = 16
LRU_TINY = 1e-30
LRU_HALO = 16
ATT_HEADS = 8
ATT_KV_HEADS = 2
HEAD_DIM = 128
BLOCK = 128
ROPE_BASE = 10000.0
HY_ORDER = 2
HY_EMB = 33
HY_EMB_PAD = 40
HY_DECAY_TARGET = 1e-2
HY_FAST_PCT = 0.3
HY_SLOW_PCT = 1.5
HY_UNROLL = 4
HY_HALO = 16
HY_PITCH_PAD = 8
RET_HEADS = 8
RET_DK = 128
RET_CHUNK = 256

VMEM_LIMIT = 56 * 1024 * 1024
NEG = -1e30


def _params(sem, vmem=VMEM_LIMIT, **kw):
    return pltpu.CompilerParams(dimension_semantics=sem, vmem_limit_bytes=vmem, **kw)


def _sigmoid(v):
    return 0.5 * (jnp.tanh(0.5 * v) + 1.0)


def _silu(v):
    return v * _sigmoid(v)


def _softplus(v):
    return jnp.maximum(v, 0.0) + jnp.log(1.0 + jnp.exp(-jnp.abs(v)))


def _dot(a, b, **kw):
    return jnp.dot(a, b, preferred_element_type=F32, **kw)


def _dot_nt(a, b):
    return lax.dot_general(a, b, (((1,), (1,)), ((), ())), preferred_element_type=F32)


def _dot_tn(a, b):
    return lax.dot_general(a, b, (((0,), (0,)), ((), ())), preferred_element_type=F32)


def _mod_kernel(c_ref, w_ref, b_ref, o_ref):
    s = _silu(c_ref[...])
    o_ref[0] = _dot(s, w_ref[0], precision=HIGHEST) + b_ref[0]


def _modulation(crows, mod_w, mod_b):
    depth, d, n3 = mod_w.shape
    r = crows.shape[0]
    tn = 1024
    return pl.pallas_call(
        _mod_kernel,
        grid=(depth, n3 // tn),
        in_specs=[pl.BlockSpec((r, d), lambda l, j: (0, 0)),
                  pl.BlockSpec((1, d, tn), lambda l, j: (l, 0, j)),
                  pl.BlockSpec((1, 1, tn), lambda l, j: (l, 0, j))],
        out_specs=pl.BlockSpec((1, r, tn), lambda l, j: (l, 0, j)),
        out_shape=jax.ShapeDtypeStruct((depth, r, n3), F32),
        compiler_params=_params(("parallel", "parallel")),
        name="modulation",
    )(crows, mod_w, mod_b.reshape(depth, 1, n3))


def _rope(v, cos, sin, lane):
    swapped = jnp.where((lane & 63) < 32, pltpu.roll(v, LANES - 32, 1), pltpu.roll(v, 32, 1))
    return v * cos + swapped * sin


def _inproj_kernel(x_ref, g_ref, sc_ref, sh_ref, w_ref, *rest, rope):
    if rope:
        cos_ref, sin_ref, o_ref, h_ref = rest
    else:
        o_ref, h_ref = rest
    j = pl.program_id(1)

    @pl.when(j == 0)
    def _():
        x = x_ref[...]
        y = x * lax.rsqrt(jnp.mean(x * x, axis=-1, keepdims=True) + EPS) * g_ref[...]
        h_ref[...] = (y * (1.0 + sc_ref[0]) + sh_ref[0]).astype(BF16)

    acc = _dot(h_ref[...], w_ref[...])
    if not rope:
        o_ref[...] = acc.astype(o_ref.dtype)
        return
    plain = j >= 0
    for tile, (nrot, scale) in rope.items():
        plain = jnp.logical_and(plain, j != tile)

        @pl.when(j == tile)
        def _(nrot=nrot, scale=scale):
            cos, sin = cos_ref[...], sin_ref[...]
            lane = lax.broadcasted_iota(jnp.int32, cos.shape, 1)
            for blk in range(acc.shape[1] // LANES):
                part = acc[:, blk * LANES:(blk + 1) * LANES]
                if blk < nrot:
                    part = _rope(part, cos, sin, lane) * scale
                o_ref[:, blk * LANES:(blk + 1) * LANES] = part.astype(o_ref.dtype)

    @pl.when(plain)
    def _():
        o_ref[...] = acc.astype(o_ref.dtype)


def _in_proj(x2, g, scale, shift, rows_per_group, w, out_dtype, tm, tn, rope=None, tables=None):
    m, d = x2.shape
    n = w.shape[1]
    tpg = rows_per_group // tm
    in_specs = [pl.BlockSpec((tm, d), lambda i, j: (i, 0)),
                pl.BlockSpec((1, d), lambda i, j: (0, 0)),
                pl.BlockSpec((1, 1, d), lambda i, j: (i // tpg, 0, 0)),
                pl.BlockSpec((1, 1, d), lambda i, j: (i // tpg, 0, 0)),
                pl.BlockSpec((d, tn), lambda i, j: (0, j))]
    args = [x2, g, scale, shift, w]
    if rope:
        in_specs += [pl.BlockSpec((tm, LANES), lambda i, j: (i % tpg, 0))] * 2
        args += list(tables)
    return pl.pallas_call(
        functools.partial(_inproj_kernel, rope=rope),
        grid=(m // tm, n // tn),
        in_specs=in_specs,
        out_specs=pl.BlockSpec((tm, tn), lambda i, j: (i, j)),
        out_shape=jax.ShapeDtypeStruct((m, n), out_dtype),
        scratch_shapes=[pltpu.VMEM((tm, d), BF16)],
        compiler_params=_params(("parallel", "arbitrary")),
        name="in_proj",
    )(*args)


def _outproj_kernel(a_ref, ga_ref, b_ref, gb_ref, w_ref, x_ref, g_ref, gate_ref, o_ref):
    wa = w_ref.shape[0] // 2
    a = (a_ref[...].astype(F32) * _silu(ga_ref[...].astype(F32))).astype(BF16)
    b = (b_ref[...].astype(F32) * _silu(gb_ref[...].astype(F32))).astype(BF16)
    y = _dot(a, w_ref[0:wa, :]) + _dot(b, w_ref[wa:, :])
    yn = y * lax.rsqrt(jnp.mean(y * y, axis=-1, keepdims=True) + EPS) * g_ref[...]
    o_ref[...] = x_ref[...] + gate_ref[0] * yn


def _out_proj(a, ga, ga_col, b, gb, gb_col, w, x2, g, gate, rows_per_group, tm):
    m, d = x2.shape
    wa = w.shape[0] // 2
    tpg = rows_per_group // tm
    return pl.pallas_call(
        _outproj_kernel,
        grid=(m // tm,),
        in_specs=[pl.BlockSpec((tm, wa), lambda i: (i, 0)),
                  pl.BlockSpec((tm, wa), lambda i: (i, ga_col)),
                  pl.BlockSpec((tm, wa), lambda i: (i, 0)),
                  pl.BlockSpec((tm, wa), lambda i: (i, gb_col)),
                  pl.BlockSpec(w.shape, lambda i: (0, 0)),
                  pl.BlockSpec((tm, d), lambda i: (i, 0)),
                  pl.BlockSpec((1, d), lambda i: (0, 0)),
                  pl.BlockSpec((1, 1, d), lambda i: (i // tpg, 0, 0))],
        out_specs=pl.BlockSpec((tm, d), lambda i: (i, 0)),
        out_shape=jax.ShapeDtypeStruct((m, d), F32),
        compiler_params=_params(("parallel",)),
        name="out_proj",
    )(a, ga, b, gb, w, x2, g, gate)


def _lru_kernel(xa_ref, xac_ref, cw_ref, cb_ref, wg_ref, bg_ref, lam_ref, y_ref, yc_ref,
                xp_ref, hf_ref, pf_ref, hb_ref, pb_ref):
    nseg = LRU_SEG
    npos = LRU_POS
    c8 = -LRU_C * _softplus(-lam_ref[...])
    left = LRU_CONV // 2
    cw = [cw_ref[k:k + 1, :] for k in range(LRU_CONV)]
    cb = cb_ref[...]
    zeros_h = jnp.zeros((LRU_HALO, LANES), F32)
    zeros_s = jnp.zeros((nseg, LANES), F32)
    ones_s = jnp.ones((nseg, LANES), F32)

    def run(src_ref, out_ref, n, c0f, c0b):
        seg = n // nseg
        nblk = seg // npos
        halo = LRU_HALO
        pitch = seg + 2 * halo + 8
        for j in range(nseg):
            lo, hi = j * seg - halo, (j + 1) * seg + halo
            if lo < 0:
                xp_ref[j * pitch:j * pitch + halo, :] = zeros_h
            if hi > n:
                xp_ref[j * pitch + seg + halo:j * pitch + seg + 2 * halo, :] = zeros_h
            lo_c, hi_c = max(lo, 0), min(hi, n)
            xp_ref[j * pitch + (lo_c - lo):j * pitch + (hi_c - lo), :] = src_ref[lo_c:hi_c, :].astype(F32)

        def gather(q):
            return jnp.concatenate([xp_ref[pl.ds(halo + q + c * 8 * pitch, 8, stride=pitch), :]
                                    for c in range(nseg // 8)], axis=0)

        def blk(i):
            return pl.ds(pl.multiple_of(i * npos * nseg, npos * nseg), npos * nseg)

        def pos(v, p):
            return v[p * nseg:(p + 1) * nseg, :]

        def fwd_body(i, carry):
            h, pc = carry
            p0 = i * npos
            xs = [gather(p0 + q - left) for q in range(npos + LRU_CONV - 1)]
            us = []
            for p in range(npos):
                u = cb + cw[0] * xs[p]
                for k in range(1, LRU_CONV):
                    u = u + cw[k] * xs[p + k]
                us.append(u)
            u = jnp.concatenate(us, axis=0)
            g = _dot(u.astype(BF16), wg_ref[0]) + bg_ref[0]
            coef = []
            for d in range(2):
                r = _sigmoid(g[:, (2 * d) * LANES:(2 * d + 1) * LANES])
                gi = _sigmoid(g[:, (2 * d + 1) * LANES:(2 * d + 2) * LANES])
                a = jnp.exp(c8[d:d + 1, :] * r)
                om = 1.0 - a * a
                coef.append((a, om * lax.rsqrt(jnp.maximum(om, LRU_TINY)) * (gi * u)))
            (af, bf), (ab, bb) = coef
            hb_ref[blk(i), :] = ab
            pb_ref[blk(i), :] = bb
            hs, ps = [], []
            for p in range(npos):
                a_p = pos(af, p)
                h = a_p * h + pos(bf, p)
                pc = a_p * pc
                hs.append(h)
                ps.append(pc)
            hf_ref[blk(i), :] = jnp.concatenate(hs, axis=0)
            pf_ref[blk(i), :] = jnp.concatenate(ps, axis=0)
            return h, pc

        hef, pef = lax.fori_loop(0, nblk, fwd_body, (zeros_s, ones_s))

        def bwd_body(ii, carry):
            h, pc = carry
            i = nblk - 1 - ii
            ab = hb_ref[blk(i), :]
            bb = pb_ref[blk(i), :]
            hs, ps = [None] * npos, [None] * npos
            for p in reversed(range(npos)):
                a_p = pos(ab, p)
                h = a_p * h + pos(bb, p)
                pc = a_p * pc
                hs[p] = h
                ps[p] = pc
            hb_ref[blk(i), :] = jnp.concatenate(hs, axis=0)
            pb_ref[blk(i), :] = jnp.concatenate(ps, axis=0)
            return h, pc

        heb, peb = lax.fori_loop(0, nblk, bwd_body, (zeros_s, ones_s))

        rows_f, c = [], c0f
        for j in range(nseg):
            rows_f.append(c)
            c = hef[j:j + 1, :] + pef[j:j + 1, :] * c
        final_f = c
        rows_b, c = [None] * nseg, c0b
        for j in reversed(range(nseg)):
            rows_b[j] = c
            c = heb[j:j + 1, :] + peb[j:j + 1, :] * c
        final_b = c
        cf = jnp.concatenate(rows_f, axis=0)
        cbk = jnp.concatenate(rows_b, axis=0)

        def out_body(p, _):
            rs = pl.ds(pl.multiple_of(p * nseg, nseg), nseg)
            y = hf_ref[rs, :] + pf_ref[rs, :] * cf + hb_ref[rs, :] + pb_ref[rs, :] * cbk
            for c in range(nseg // 8):
                out_ref[pl.ds(p + c * 8 * seg, 8, stride=seg), :] = y[c * 8:(c + 1) * 8, :]
            return 0
        lax.fori_loop(0, seg, out_body, 0, unroll=8)
        return final_f, final_b

    zero = jnp.zeros((1, LANES), F32)
    ff, fb = run(xac_ref, yc_ref, xac_ref.shape[0], zero, zero)
    run(xa_ref, y_ref, xa_ref.shape[0], ff, fb)


def _lru(proj, projc, xa_col, bsz, conv_w, conv_b, wg, bg, lam):
    w = conv_w.shape[1]
    s_len = proj.shape[0] // bsz
    c_len = projc.shape[0] // bsz
    nblk = w // LANES
    seq = lambda n: pl.BlockSpec((n, LANES), lambda b, j: (b, j))
    src = lambda n: pl.BlockSpec((n, LANES), lambda b, j: (b, xa_col + j))
    return pl.pallas_call(
        _lru_kernel,
        grid=(bsz, nblk),
        in_specs=[src(s_len), src(c_len),
                  pl.BlockSpec((LRU_CONV, LANES), lambda b, j: (0, j)),
                  pl.BlockSpec((1, LANES), lambda b, j: (0, j)),
                  pl.BlockSpec((1, LANES, 4 * LANES), lambda b, j: (j, 0, 0)),
                  pl.BlockSpec((1, 1, 4 * LANES), lambda b, j: (j, 0, 0)),
                  pl.BlockSpec((2, LANES), lambda b, j: (0, j))],
        out_specs=[seq(s_len), seq(c_len)],
        out_shape=[jax.ShapeDtypeStruct((proj.shape[0], w), F32), jax.ShapeDtypeStruct((projc.shape[0], w), F32)],
        scratch_shapes=[pltpu.VMEM((s_len + (2 * LRU_HALO + 8) * LRU_SEG, LANES), F32)]
        + [pltpu.VMEM((s_len, LANES), F32)] * 4,
        compiler_params=_params(("parallel", "parallel")),
        name="rglru",
    )(proj, projc, conv_w, conv_b, wg, bg, lam)


def _attn_kernel(sink_ref, q_ref, k_ref, v_ref, kc_ref, vc_ref, bias_ref, o_ref):
    qb = pl.program_id(1)
    nb = pl.num_programs(1)
    group = ATT_HEADS // ATT_KV_HEADS

    def blk(i):
        return pl.ds(pl.multiple_of(i * BLOCK, BLOCK), BLOCK)

    ip = jnp.maximum(qb - 1, 0)
    inx = jnp.minimum(qb + 1, nb - 1)
    bias = jnp.concatenate([bias_ref[0]] * group, axis=0)
    ones = jnp.ones((bias.shape[1], HEAD_DIM), BF16)

    for h in range(ATT_KV_HEADS):
        ksl = slice(h * HEAD_DIM, (h + 1) * HEAD_DIM)
        kcat = jnp.concatenate([k_ref[blk(ip), ksl], k_ref[blk(qb), ksl], k_ref[blk(inx), ksl],
                                kc_ref[:, ksl]], axis=0)
        vcat = jnp.concatenate([v_ref[blk(ip), ksl], v_ref[blk(qb), ksl], v_ref[blk(inx), ksl],
                                vc_ref[:, ksl]], axis=0)
        vext = jnp.concatenate([vcat, ones], axis=1)
        heads = [h * group + g for g in range(group)]
        q4 = jnp.concatenate([q_ref[:, hh * HEAD_DIM:(hh + 1) * HEAD_DIM] for hh in heads], axis=0)
        sk = jnp.concatenate([jnp.full((BLOCK, 1), sink_ref[hh], F32) for hh in heads], axis=0)
        s = _dot_nt(q4, kcat) + bias
        m = jnp.maximum(jnp.max(s, axis=-1, keepdims=True), sk)
        p = jnp.exp((s - m).astype(BF16))
        oe = _dot(p, vext)
        o = oe[:, 0:HEAD_DIM] / (jnp.exp(sk - m) + oe[:, HEAD_DIM:HEAD_DIM + 1])
        for g, hh in enumerate(heads):
            o_ref[:, hh * HEAD_DIM:(hh + 1) * HEAD_DIM] = o[g * BLOCK:(g + 1) * BLOCK, :].astype(o_ref.dtype)


def _attention(sink, proj, projc, bsz, cols):
    s_len = proj.shape[0] // bsz
    c_len = projc.shape[0] // bsz
    nb = s_len // BLOCK
    aw = ATT_HEADS * HEAD_DIM
    kw = ATT_KV_HEADS * HEAD_DIM
    qi = np.arange(BLOCK)[:, None]
    kj = np.arange(BLOCK)[None, :]
    variants = []
    for v in range(4):
        prev = np.where((kj >= qi) & bool(v & 1), 0.0, NEG)
        nxt = np.where((kj <= qi) & bool(v & 2), 0.0, NEG)
        variants.append(np.concatenate([prev, np.zeros((BLOCK, BLOCK)), nxt, np.zeros((BLOCK, c_len))], axis=1))
    bias = jnp.asarray(np.stack(variants), F32)
    ncol = 3 * BLOCK + c_len

    def variant(b, i):
        return ((i > 0).astype(jnp.int32) + 2 * (i < nb - 1).astype(jnp.int32), 0, 0)

    return pl.pallas_call(
        _attn_kernel,
        grid=(bsz, nb),
        in_specs=[pl.BlockSpec(memory_space=pltpu.SMEM),
                  pl.BlockSpec((BLOCK, aw), lambda b, i: (b * nb + i, cols["q"])),
                  pl.BlockSpec((s_len, kw), lambda b, i: (b, cols["k"])),
                  pl.BlockSpec((s_len, kw), lambda b, i: (b, cols["v"])),
                  pl.BlockSpec((c_len, kw), lambda b, i: (b, cols["k"])),
                  pl.BlockSpec((c_len, kw), lambda b, i: (b, cols["v"])),
                  pl.BlockSpec((1, BLOCK, ncol), variant)],
        out_specs=pl.BlockSpec((BLOCK, aw), lambda b, i: (b * nb + i, 0)),
        out_shape=jax.ShapeDtypeStruct((proj.shape[0], aw), BF16),
        compiler_params=_params(("parallel", "arbitrary")),
        name="window_attention",
    )(sink, proj, proj, proj, projc, projc, bias)


def _ctx_attn_kernel(sink_ref, q_ref, k_ref, v_ref, o_ref):
    group = ATT_HEADS // ATT_KV_HEADS
    scale = HEAD_DIM ** -0.5
    n = q_ref.shape[0]
    for h in range(ATT_KV_HEADS):
        ksl = slice(h * HEAD_DIM, (h + 1) * HEAD_DIM)
        qs, sinks = [], []
        for g in range(group):
            hh = h * group + g
            qs.append((q_ref[:, hh * HEAD_DIM:(hh + 1) * HEAD_DIM].astype(F32) * scale).astype(BF16))
            sinks.append(jnp.full((n, 1), sink_ref[hh], F32))
        q4 = jnp.concatenate(qs, axis=0)
        sk = jnp.concatenate(sinks, axis=0)
        s = _dot_nt(q4, k_ref[:, ksl])
        m = jnp.maximum(jnp.max(s, axis=-1, keepdims=True), sk)
        p = jnp.exp(s - m)
        denom = jnp.exp(sk - m) + jnp.sum(p, axis=-1, keepdims=True)
        o = _dot(p.astype(BF16), v_ref[:, ksl]) / denom
        for g in range(group):
            hh = h * group + g
            hs = slice(hh * HEAD_DIM, (hh + 1) * HEAD_DIM)
            o_ref[:, hs] = o[g * n:(g + 1) * n, :].astype(o_ref.dtype)


def _ctx_attention(sink, projc, bsz, cols):
    c_len = projc.shape[0] // bsz
    aw = ATT_HEADS * HEAD_DIM
    kw = ATT_KV_HEADS * HEAD_DIM
    return pl.pallas_call(
        _ctx_attn_kernel,
        grid=(bsz,),
        in_specs=[pl.BlockSpec(memory_space=pltpu.SMEM),
                  pl.BlockSpec((c_len, aw), lambda b: (b, cols["q"])),
                  pl.BlockSpec((c_len, kw), lambda b: (b, cols["k"])),
                  pl.BlockSpec((c_len, kw), lambda b: (b, cols["v"]))],
        out_specs=pl.BlockSpec((c_len, aw), lambda b: (b, 0)),
        out_shape=jax.ShapeDtypeStruct((projc.shape[0], aw), BF16),
        compiler_params=_params(("parallel",)),
        name="context_attention",
    )(sink, projc, projc, projc)


def _ret_kernel(dl_ref, q_ref, k_ref, v_ref, qc_ref, kc_ref, vc_ref, o_ref, oc_ref,
                ot_ref, att_ref, u_ref, st_ref, vt_ref, dm_ref):
    c = RET_CHUNK
    s_len = q_ref.shape[0]
    c_len = qc_ref.shape[0]
    lg = -_softplus(-dl_ref[0])
    lgf, lgb = lg[0:1, :], lg[1:2, :]
    wide = lambda t: jnp.concatenate([t] * (c // LANES), axis=1)
    kj = lax.broadcasted_iota(jnp.int32, (c, c), 0)
    qi = lax.broadcasted_iota(jnp.int32, (c, c), 1)
    diff = (qi - kj).astype(F32)
    dm_ref[...] = jnp.where(qi >= kj, jnp.exp(jnp.maximum(diff, 0.0) * wide(lgf)),
                            jnp.exp(jnp.maximum(-diff, 0.0) * wide(lgb)))
    idx = lax.broadcasted_iota(jnp.int32, (c, LANES), 0).astype(F32)
    qdec_f = jnp.exp((idx + 1.0) * lgf)
    kdec_f = jnp.exp((c - 1.0 - idx) * lgf)
    qdec_b = jnp.exp((c - idx) * lgb)
    kdec_b = jnp.exp(idx * lgb)
    cdec_f = jnp.exp(c * lgf)
    cdec_b = jnp.exp(c * lgb)

    def chunk(j):
        return pl.ds(pl.multiple_of(j * c, c), c)

    def transpose_v(vr, n):
        def body(j, _):
            vt_ref[:, chunk(j)] = vr[chunk(j), :].astype(F32).T.astype(BF16)
            return 0
        lax.fori_loop(0, n // c, body, 0, unroll=min(8, n // c))

    def scores(qr, kr, n):
        def body(j, _):
            att_ref[j] = (_dot_nt(kr[chunk(j), :], qr[chunk(j), :]) * dm_ref[...]).astype(BF16)
            return 0
        lax.fori_loop(0, n // c, body, 0, unroll=min(4, n // c))

    def intra(kr, n):
        def body(j, _):
            vt = vt_ref[:, chunk(j)]
            ot_ref[:, chunk(j)] = _dot(vt, att_ref[j])
            kf = kr[chunk(j), :].astype(F32)
            kcat = jnp.concatenate([(kf * kdec_f).astype(BF16), (kf * kdec_b).astype(BF16)], axis=1)
            u_ref[j] = _dot(vt, kcat)
            return 0
        lax.fori_loop(0, n // c, body, 0, unroll=min(4, n // c))

    def states(n, sf, sb):
        nch = n // c

        def fbody(j, s):
            st_ref[j, :, 0:RET_DK] = s.astype(BF16)
            return s * cdec_f + u_ref[j, :, 0:RET_DK]
        sf = lax.fori_loop(0, nch, fbody, sf)

        def bbody(jj, s):
            j = nch - 1 - jj
            st_ref[j, :, RET_DK:] = s.astype(BF16)
            return s * cdec_b + u_ref[j, :, RET_DK:]
        sb = lax.fori_loop(0, nch, bbody, sb)
        return sf, sb

    def cross(qr, outr, n):
        def body(j, _):
            qf = qr[chunk(j), :].astype(F32)
            qcat = jnp.concatenate([(qf * qdec_f).astype(BF16), (qf * qdec_b).astype(BF16)], axis=1)
            ot = ot_ref[:, chunk(j)] + _dot_nt(st_ref[j], qcat)
            o = ot.T * (RET_DK ** -0.5)
            outr[chunk(j), :] = (o * lax.rsqrt(jnp.mean(o * o, axis=-1, keepdims=True) + EPS)).astype(outr.dtype)
            return 0
        lax.fori_loop(0, n // c, body, 0, unroll=min(4, n // c))

    def run(qr, kr, vr, outr, n, sf, sb):
        transpose_v(vr, n)
        scores(qr, kr, n)
        intra(kr, n)
        finals = states(n, sf, sb)
        cross(qr, outr, n)
        return finals

    zero = jnp.zeros((LANES, RET_DK), F32)
    sf, sb = run(qc_ref, kc_ref, vc_ref, oc_ref, c_len, zero, zero)
    run(q_ref, k_ref, v_ref, o_ref, s_len, sf, sb)


def _retention(dl, proj, projc, bsz, cols):
    s_len = proj.shape[0] // bsz
    c_len = projc.shape[0] // bsz
    hb = lambda n, col: pl.BlockSpec((n, LANES), lambda b, h: (b, col + h))
    return pl.pallas_call(
        _ret_kernel,
        grid=(bsz, RET_HEADS),
        in_specs=[pl.BlockSpec((1, 2, LANES), lambda b, h: (h, 0, 0)),
                  hb(s_len, cols["q"]), hb(s_len, cols["k"]), hb(s_len, cols["v"]),
                  hb(c_len, cols["q"]), hb(c_len, cols["k"]), hb(c_len, cols["v"])],
        out_specs=[pl.BlockSpec((s_len, LANES), lambda b, h: (b, h)),
                   pl.BlockSpec((c_len, LANES), lambda b, h: (b, h))],
        out_shape=[jax.ShapeDtypeStruct((proj.shape[0], RET_HEADS * LANES), BF16),
                   jax.ShapeDtypeStruct((projc.shape[0], RET_HEADS * LANES), BF16)],
        scratch_shapes=[pltpu.VMEM((LANES, s_len), F32),
                        pltpu.VMEM((s_len // RET_CHUNK, RET_CHUNK, RET_CHUNK), BF16),
                        pltpu.VMEM((s_len // RET_CHUNK, LANES, 2 * RET_DK), F32),
                        pltpu.VMEM((s_len // RET_CHUNK, LANES, 2 * RET_DK), BF16),
                        pltpu.VMEM((LANES, s_len), BF16),
                        pltpu.VMEM((RET_CHUNK, RET_CHUNK), F32)],
        compiler_params=_params(("parallel", "parallel")),
        name="retention",
    )(dl, proj, proj, proj, projc, projc, projc)


def _hy_hid_kernel(z_ref, w1_ref, b1_ref, w2_ref, b2_ref, f_ref, o_ref):
    f = f_ref[...]
    h = jnp.sin(f * (_dot(z_ref[...], w1_ref[...], precision=HIGHEST) + b1_ref[...]))
    o_ref[...] = jnp.sin(f * (_dot(h, w2_ref[...], precision=HIGHEST) + b2_ref[...]))


def _hy_hidden(zfull, w1p, b1, w2, b2, freq):
    n, e = zfull.shape
    fd = w2.shape[0]
    tr = min(n, 1024)
    full = lambda shp: pl.BlockSpec(shp, lambda i: (0, 0))
    return pl.pallas_call(
        _hy_hid_kernel,
        grid=(n // tr,),
        in_specs=[pl.BlockSpec((tr, e), lambda i: (i, 0)), full((e, fd)), full((1, fd)), full((fd, fd)),
                  full((1, fd)), full((1, fd))],
        out_specs=pl.BlockSpec((tr, fd), lambda i: (i, 0)),
        out_shape=jax.ShapeDtypeStruct((n, fd), F32),
        compiler_params=_params(("parallel",)),
        name="hyena_filter_mlp",
    )(zfull, w1p, b1, w2, b2, freq)


def _hy_filt_kernel(hid_ref, w3f_ref, w3b_ref, tn_ref, dl_ref, o_ref):
    half = hid_ref.shape[0] // 2
    decay = jnp.exp(-tn_ref[...] * jnp.abs(dl_ref[...]))
    top = _dot(hid_ref[0:half, :], w3f_ref[...], precision=HIGHEST)
    bot = _dot(hid_ref[half:, :], w3b_ref[...], precision=HIGHEST)
    row = lax.broadcasted_iota(jnp.int32, bot.shape, 0)
    bot = jnp.where(row == 0, 0.0, bot)
    o_ref[0, 0:half, :] = top * decay[0:half, :]
    o_ref[0, half:, :] = bot * decay[half:, :]


def _hy_filters(hid, w3, tn_full, deltas):
    n, fd = hid.shape
    wch = deltas.shape[1]
    nsl = wch // LANES
    return pl.pallas_call(
        _hy_filt_kernel,
        grid=(HY_ORDER, nsl),
        in_specs=[pl.BlockSpec((n, fd), lambda o, j: (0, 0)),
                  pl.BlockSpec((fd, LANES), lambda o, j: (0, o * 2 * nsl + j)),
                  pl.BlockSpec((fd, LANES), lambda o, j: (0, o * 2 * nsl + nsl + j)),
                  pl.BlockSpec((n, LANES), lambda o, j: (0, 0)),
                  pl.BlockSpec((1, LANES), lambda o, j: (0, j))],
        out_specs=pl.BlockSpec((1, n, LANES), lambda o, j: (o, 0, j)),
        out_shape=jax.ShapeDtypeStruct((HY_ORDER, n, wch), F32),
        compiler_params=_params(("parallel", "parallel")),
        name="hyena_filter",
    )(hid, w3, w3, tn_full, deltas)


def _hy_kfft_kernel(kern_ref, f1_ref, f2_ref, o_ref, g_ref, *, n1, n2):
    pg = 2 * n1 + HY_PITCH_PAD

    def stage1(i2, _):
        x = kern_ref[pl.ds(i2, n1, stride=n2), :].astype(BF16)
        g_ref[pl.ds(pl.multiple_of(i2 * pg, 8), 2 * n1), :] = _dot(f1_ref[i2], x)
        return 0
    lax.fori_loop(0, n2, stage1, 0, unroll=HY_UNROLL)

    def stage2(k1, _):
        x = jnp.concatenate([g_ref[pl.ds(k1, n2, stride=pg), :],
                             g_ref[pl.ds(n1 + k1, n2, stride=pg), :]], axis=0).astype(BF16)
        o_ref[pl.ds(pl.multiple_of(k1 * 2 * n2, 2 * n2), 2 * n2), :] = _dot(f2_ref[...], x)
        return 0
    lax.fori_loop(0, n1, stage2, 0, unroll=HY_UNROLL // 2)


def _hy_kfft(kern, f1k, f2, n1, n2):
    orders, n, wch = kern.shape
    nsl = wch // LANES
    return pl.pallas_call(
        functools.partial(_hy_kfft_kernel, n1=n1, n2=n2),
        grid=(orders, nsl),
        in_specs=[pl.BlockSpec((None, n, LANES), lambda o, j: (o, 0, j)),
                  pl.BlockSpec(f1k.shape, lambda o, j: (0, 0, 0)),
                  pl.BlockSpec(f2.shape, lambda o, j: (0, 0))],
        out_specs=pl.BlockSpec((None, None, 2 * n, LANES), lambda o, j: (o, j, 0, 0)),
        out_shape=jax.ShapeDtypeStruct((orders, nsl, 2 * n, LANES), F32),
        scratch_shapes=[pltpu.VMEM((n2 * (2 * n1 + HY_PITCH_PAD), LANES), F32)],
        compiler_params=_params(("parallel", "parallel")),
        name="hyena_filter_fft",
    )(kern, f1k, f2)


def _hy_prep_kernel(z_ref, w_ref, b_ref, o_ref, zp_ref, *, n2):
    l_len = z_ref.shape[0]
    h1 = l_len // n2
    halo = HY_HALO
    pitch = n2 + 2 * halo + 8
    zeros_h = jnp.zeros((halo, LANES), F32)
    for i1 in range(h1):
        lo, hi = i1 * n2 - halo, (i1 + 1) * n2 + halo
        if lo < 0:
            zp_ref[i1 * pitch:i1 * pitch + halo, :] = zeros_h
        if hi > l_len:
            zp_ref[i1 * pitch + n2 + halo:i1 * pitch + n2 + 2 * halo, :] = zeros_h
        lo_c, hi_c = max(lo, 0), min(hi, l_len)
        zp_ref[i1 * pitch + (lo_c - lo):i1 * pitch + (hi_c - lo), :] = z_ref[lo_c:hi_c, :].astype(F32)
    w0, w1, w2, b = w_ref[0:1, :], w_ref[1:2, :], w_ref[2:3, :], b_ref[...]

    def tap(j):
        return zp_ref[pl.ds(halo + j, h1, stride=pitch), :]

    def body(i2, carry):
        zm, z0 = carry
        zn = tap(i2 + 1)
        val = b + w0 * zm + w1 * z0 + w2 * zn
        o_ref[pl.ds(pl.multiple_of(i2 * h1, h1), h1), :] = val.astype(o_ref.dtype)
        return z0, zn
    lax.fori_loop(0, n2, body, (tap(-1), tap(0)), unroll=HY_UNROLL)


def _hy_stage_dtype(l_len, n2):
    return BF16 if (l_len // n2) % 16 == 0 else F32


def _hy_prep(proj, z_col, nz, bsz, conv_w, conv_b, n2):
    l_len = proj.shape[0] // bsz
    h1 = l_len // n2
    return pl.pallas_call(
        functools.partial(_hy_prep_kernel, n2=n2),
        grid=(bsz, nz),
        in_specs=[pl.BlockSpec((l_len, LANES), lambda b, j: (b, z_col + j)),
                  pl.BlockSpec((3, LANES), lambda b, j: (0, j)),
                  pl.BlockSpec((1, LANES), lambda b, j: (0, j))],
        out_specs=pl.BlockSpec((l_len, LANES), lambda b, j: (b, j)),
        out_shape=jax.ShapeDtypeStruct((proj.shape[0], nz * LANES), _hy_stage_dtype(l_len, n2)),
        scratch_shapes=[pltpu.VMEM((h1 * (n2 + 2 * HY_HALO + 8), LANES), F32)],
        compiler_params=_params(("parallel", "parallel")),
        name="hyena_short_conv",
    )(proj, conv_w, conv_b)


def _pack2(re, im):
    r = lax.bitcast_convert_type(re, jnp.uint32) + jnp.uint32(0x8000)
    i = lax.bitcast_convert_type(im, jnp.uint32) + jnp.uint32(0x8000)
    return (r & jnp.uint32(0xFFFF0000)) | (i >> 16)


def _unpack2(w):
    re = lax.bitcast_convert_type(w & jnp.uint32(0xFFFF0000), F32)
    im = lax.bitcast_convert_type(w << 16, F32)
    return jnp.concatenate([re, im], axis=0).astype(BF16)


def _hy_conv_kernel(y_ref, x_ref, kh_ref, bias_ref, f1_ref, f2_ref, f2i_ref, f1i_ref, o_ref, g_ref,
                    *, n1, n2, natural_out):
    h1 = n1 // 2
    l_len = h1 * n2
    pg = n1 + HY_PITCH_PAD
    bias = bias_ref[...]
    unroll = HY_UNROLL

    def rows(seq, i2):
        return pl.ds(pl.multiple_of(seq * l_len + i2 * h1, h1), h1)

    def lanes2(a, b):
        return jnp.concatenate([a, b], axis=1)

    def stage1(i2, _):
        x = lanes2(jnp.concatenate([y_ref[rows(0, i2), :], y_ref[rows(1, i2), :]], axis=0),
                   jnp.concatenate([y_ref[rows(2, i2), :], y_ref[rows(3, i2), :]], axis=0)).astype(BF16)
        a = _dot(f1_ref[i2], x)
        w = _pack2(a[0:n1, :], a[n1:, :])
        blk = pl.ds(pl.multiple_of(i2 * pg, 8), n1)
        g_ref[0, blk, :] = w[:, 0:LANES]
        g_ref[1, blk, :] = w[:, LANES:]
        return 0
    lax.fori_loop(0, n2, stage1, 0, unroll=2 * unroll)

    def stage23(t, _):
        outs = []
        for u in range(unroll):
            k1 = t * unroll + u
            col = pl.ds(k1, n2, stride=pg)
            x = lanes2(_unpack2(g_ref[0, col, :]), _unpack2(g_ref[1, col, :]))
            yh = _dot(f2_ref[...], x)
            base = pl.multiple_of(k1 * 2 * n2, 2 * n2)
            kr = kh_ref[pl.ds(base, n2), :]
            ki = kh_ref[pl.ds(base + n2, n2), :]
            kr, ki = lanes2(kr, kr), lanes2(ki, ki)
            yr, yi = yh[0:n2, :], yh[n2:, :]
            z = jnp.concatenate([yr * kr - yi * ki, yr * ki + yi * kr], axis=0).astype(BF16)
            c = _dot(f2i_ref[...], z)
            outs.append((col, _pack2(c[0:n2, :], c[n2:, :])))
        for col, w in outs:
            g_ref[0, col, :] = w[:, 0:LANES]
            g_ref[1, col, :] = w[:, LANES:]
        return 0
    lax.fori_loop(0, n1 // unroll, stage23, 0)

    def stage4(i2, _):
        blk = pl.ds(pl.multiple_of(i2 * pg, 8), n1)
        x = lanes2(_unpack2(g_ref[0, blk, :]), _unpack2(g_ref[1, blk, :]))
        yc = _dot(f1i_ref[i2], x)
        for seq in range(4):
            pair, half = seq // 2, seq % 2
            sl = rows(seq, i2)
            conv = yc[half * h1:(half + 1) * h1, pair * LANES:(pair + 1) * LANES]
            val = x_ref[sl, :].astype(F32) * (conv + bias * y_ref[sl, :].astype(F32))
            if natural_out:
                o_ref[pl.ds(seq * l_len + i2, h1, stride=n2), :] = val
            else:
                o_ref[sl, :] = val.astype(o_ref.dtype)
        return 0
    lax.fori_loop(0, n2, stage4, 0, unroll=unroll)


def _hy_conv(ysrc, ycol, xsrc, xcol, khat, order, bias, mats, bsz, n1, n2, natural_out):
    l_len = ysrc.shape[0] // bsz
    nsl = khat.shape[1]
    f1, f2, f2i, f1i = mats
    once = pl.Buffered(1)
    cst2 = lambda a: pl.BlockSpec(a.shape, lambda j, p: (0, 0), pipeline_mode=once)
    cst3 = lambda a: pl.BlockSpec(a.shape, lambda j, p: (0, 0, 0), pipeline_mode=once)
    return pl.pallas_call(
        functools.partial(_hy_conv_kernel, n1=n1, n2=n2, natural_out=natural_out),
        grid=(nsl, bsz // 4),
        in_specs=[pl.BlockSpec((4 * l_len, LANES), lambda j, p: (p, ycol + j)),
                  pl.BlockSpec((4 * l_len, LANES), lambda j, p: (p, xcol + j)),
                  pl.BlockSpec((None, None, khat.shape[2], LANES), lambda j, p: (order, j, 0, 0),
                               pipeline_mode=once),
                  pl.BlockSpec((1, LANES), lambda j, p: (0, j)),
                  cst3(f1), cst2(f2), cst2(f2i), cst3(f1i)],
        out_specs=pl.BlockSpec((4 * l_len, LANES), lambda j, p: (p, j)),
        out_shape=jax.ShapeDtypeStruct((ysrc.shape[0], nsl * LANES),
                                       F32 if natural_out else _hy_stage_dtype(l_len, n2)),
        scratch_shapes=[pltpu.VMEM((2, n2 * (n1 + HY_PITCH_PAD), LANES), jnp.uint32)],
        compiler_params=_params(("parallel", "arbitrary")),
        name="hyena_long_conv",
    )(ysrc, xsrc, khat, bias[order][None], f1, f2, f2i, f1i)


def _dft_tables(l_len):
    n = 2 * l_len
    n2 = 128 if l_len >= 1024 else 32
    n1 = n // n2
    h1 = n1 // 2
    i1 = np.arange(n1)[None, None, :]
    k1 = np.arange(n1)[None, :, None]
    i2 = np.arange(n2)[:, None, None]
    ph = 2 * np.pi * (i1 * k1 / n1 + i2 * k1 / n)
    c, s = np.cos(ph), np.sin(ph)
    ch, sh = c[:, :, :h1], s[:, :, :h1]
    f1 = np.concatenate([np.concatenate([ch, sh], 2), np.concatenate([-sh, ch], 2)], 1)
    f1k = np.concatenate([c, -s], 1)
    ct, st = np.swapaxes(ch, 1, 2), np.swapaxes(sh, 1, 2)
    f1i = np.concatenate([np.concatenate([ct, -st], 2), np.concatenate([st, ct], 2)], 1) / n
    a = np.arange(n2)
    ph2 = 2 * np.pi * np.outer(a, a) / n2
    c2, s2 = np.cos(ph2), np.sin(ph2)
    f2 = np.block([[c2, s2], [-s2, c2]])
    f2i = np.block([[c2, -s2], [s2, c2]])
    bf = lambda m: jnp.asarray(m, dtype=F32).astype(BF16)
    return n1, n2, (bf(f1), bf(f2), bf(f2i), bf(f1i)), bf(f1k)


def _filter_positions(l_len, width):
    lag = np.concatenate([np.arange(l_len), l_len - np.arange(l_len)]).astype(np.float64)
    t = lag / (l_len - 1)
    bands = (HY_EMB - 1) // 2
    w = 2.0 * np.pi * lag / l_len
    f = np.linspace(1e-4, bands - 1, bands)[None]
    z = np.concatenate([t[:, None], np.cos(f * w[:, None]), -np.sin(f * w[:, None])], axis=-1)
    z = np.pad(z, ((0, 0), (0, HY_EMB_PAD - HY_EMB)))
    tn = np.repeat(t[:, None], LANES, axis=1)
    max_decay = math.log(HY_DECAY_TARGET) / HY_FAST_PCT
    min_decay = math.log(HY_DECAY_TARGET) / HY_SLOW_PCT
    deltas = np.linspace(min_decay, max_decay, width)[None]
    return jnp.asarray(z, F32), jnp.asarray(tn, F32), jnp.asarray(deltas, F32)


def _rope_tables(seq):
    n_rows = seq // GRID_W
    row = np.repeat(np.arange(n_rows), GRID_W).astype(np.float64)
    col = np.tile(np.arange(GRID_W), n_rows).astype(np.float64)
    half = HEAD_DIM // 2
    inv = ROPE_BASE ** (-np.arange(0, half, 2, dtype=np.float64) / half)
    ar, ac = row[:, None] * inv, col[:, None] * inv
    cos = np.concatenate([np.cos(ar), np.cos(ar), np.cos(ac), np.cos(ac)], axis=1)
    sin = np.concatenate([-np.sin(ar), np.sin(ar), -np.sin(ac), np.sin(ac)], axis=1)
    return jnp.asarray(cos, F32), jnp.asarray(sin, F32)


def _hyena_spectra(l_len, width, w1, b1, w2, b2, w3, freq, tables):
    n1, n2, mats, f1k = tables
    zfull, tn_full, deltas = _filter_positions(l_len, width)
    w1p = jnp.pad(w1, ((0, HY_EMB_PAD - HY_EMB), (0, 0)))
    hid = _hy_hidden(zfull, w1p, b1[None], w2, b2[None], freq[None])
    kern = _hy_filters(hid, w3, tn_full, deltas)
    return _hy_kfft(kern, f1k, mats[1], n1, n2)


def _hyena_seq(zc, bsz, khat, bias, tables):
    n1, n2, mats, _ = tables
    nsl = khat.shape[1]
    y1 = _hy_conv(zc, 0, zc, nsl, khat, 0, bias, mats, bsz, n1, n2, False)
    return _hy_conv(y1, 0, zc, 2 * nsl, khat, 1, bias, mats, bsz, n1, n2, True)


def kernel(x, c, ctx, c_ctx, mod_w, mod_b, norm_pre, norm_post, ev_w_in, ev_w_out, lru_conv_w, lru_conv_b, lru_wa, lru_ba, lru_wx, lru_bx, lru_lambda, attn_sink, od_w_in, od_w_out, hy_conv_w, hy_conv_b, hy_w1, hy_b1, hy_w2, hy_b2, hy_w3, hy_freq, hy_bias, ret_decay_logit):
    bsz, s_len, d = x.shape
    c_len = ctx.shape[1]
    depth = mod_w.shape[0]
    assert bsz % 4 == 0 and bsz <= 16 and s_len % 1024 == 0 and c_len % 256 == 0 and c_len <= s_len

    crows = jnp.concatenate([c, c_ctx[None], jnp.zeros((24 - bsz - 1, d), F32)], axis=0)
    mod = _modulation(crows, mod_w, mod_b)

    cos_t, sin_t = _rope_tables(s_len)
    tab_l = _dft_tables(s_len)
    tab_c = _dft_tables(c_len)

    x2 = x.reshape(bsz * s_len, d)
    ctx2 = ctx.reshape(bsz * c_len, d)
    tm = 1024

    for l in range(depth):
        need_ctx = l < depth - 1
        shift, scale, gate = (mod[l, :bsz, i * d:(i + 1) * d].reshape(bsz, 1, d) for i in range(3))
        shift_c, scale_c, gate_c = (mod[l, bsz:bsz + 1, i * d:(i + 1) * d].reshape(1, 1, d) for i in range(3))
        g_pre = norm_pre[l][None]
        g_post = norm_post[l][None]

        def proj(w, tn, rope=None):
            wb = w.astype(BF16)
            p = _in_proj(x2, g_pre, scale, shift, s_len, wb, BF16, tm, tn, rope, (cos_t, sin_t))
            pc = _in_proj(ctx2, g_pre, scale_c, shift_c, bsz * c_len, wb, BF16, tm, tn)
            return p, pc

        if l % 2 == 0:
            e = l // 2
            w_in = ev_w_in[e]
            w_ord = jnp.concatenate([w_in[:, 1024:2048], w_in[:, 2048:3072], w_in[:, 3584:4608],
                                     w_in[:, 3072:3584], w_in[:, 0:1024]], axis=1)
            tn = 512
            rope = {2: (4, HEAD_DIM ** -0.5), 3: (4, HEAD_DIM ** -0.5), 6: (ATT_KV_HEADS, 1.0)}
            pr, prc = proj(w_ord, tn, rope)
            cols = {"q": 1, "gb": 2, "k": 12, "v": 13}
            xa_col = 3584 // LANES
            nblk = lru_wa.shape[2]
            wg = jnp.stack([lru_wa[e, 0], lru_wx[e, 0], lru_wa[e, 1], lru_wx[e, 1]], axis=1)
            wg = jnp.transpose(wg, (0, 2, 1, 3)).reshape(nblk, LRU_BLOCK_W, 4 * LRU_BLOCK_W).astype(BF16)
            bg = jnp.stack([lru_ba[e, 0], lru_bx[e, 0], lru_ba[e, 1], lru_bx[e, 1]], axis=0)
            bg = jnp.transpose(bg.reshape(4, nblk, LRU_BLOCK_W), (1, 0, 2)).reshape(nblk, 1, 4 * LRU_BLOCK_W)
            ya, yac = _lru(pr, prc, xa_col, bsz, lru_conv_w[e], lru_conv_b[e][None], wg, bg, lru_lambda[e])
            yb = _attention(attn_sink[e], pr, prc, bsz, cols)
            w_out = ev_w_out[e].astype(BF16)
            x2_new = _out_proj(ya, pr, 0, yb, pr, cols["gb"], w_out, x2, g_post, gate, s_len, tm)
            if need_ctx:
                ybc = _ctx_attention(attn_sink[e], prc, bsz, cols)
                ctx2 = _out_proj(yac, prc, 0, ybc, prc, cols["gb"], w_out, ctx2, g_post, gate_c, bsz * c_len, tm)
            x2 = x2_new
        else:
            o = l // 2
            w_in = od_w_in[o]
            w_ord = jnp.concatenate([w_in[:, 3072:8192], w_in[:, 0:3072]], axis=1)
            pr, prc = proj(w_ord, 2048)
            cols = {"q": 8, "k": 16, "v": 24}
            z_col = 5120 // LANES
            width = hy_bias.shape[2]
            nz = (HY_ORDER + 1) * width // LANES
            fargs = (hy_w1[o], hy_b1[o], hy_w2[o], hy_b2[o], hy_w3[o], hy_freq[o])
            khat = _hyena_spectra(s_len, width, *fargs, tab_l)
            zconv = _hy_prep(pr, z_col, nz, bsz, hy_conv_w[o], hy_conv_b[o][None], tab_l[1])
            yh = _hyena_seq(zconv, bsz, khat, hy_bias[o], tab_l)
            dl = jnp.broadcast_to(jnp.transpose(ret_decay_logit[o])[:, :, None], (RET_HEADS, 2, LANES))
            yd, ydc = _retention(dl, pr, prc, bsz, cols)
            w_out = od_w_out[o].astype(BF16)
            x2_new = _out_proj(yh, pr, 0, yd, pr, 4, w_out, x2, g_post, gate, s_len, tm)
            if need_ctx:
                khat_c = _hyena_spectra(c_len, width, *fargs, tab_c)
                zcconv = _hy_prep(prc, z_col, nz, bsz, hy_conv_w[o], hy_conv_b[o][None], tab_c[1])
                yhc = _hyena_seq(zcconv, bsz, khat_c, hy_bias[o], tab_c)
                ctx2 = _out_proj(yhc, prc, 0, ydc, prc, 4, w_out, ctx2, g_post, gate_c, bsz * c_len, tm)
            x2 = x2_new
    return x2.reshape(bsz, s_len, d)
```

```python
import functools
import math

import numpy as np
import jax
import jax.numpy as jnp
from jax import lax
from jax.experimental import pallas as pl
from jax.experimental.pallas import tpu as pltpu

F32 = jnp.float32
BF16 = jnp.bfloat16
HIGHEST = lax.Precision.HIGHEST

EPS = 1e-6
GRID_W = 64
LANES = 128
LRU_BLOCK_W = 128
LRU_CONV = 4
LRU_C = 8.0
LRU_SEG = 16
LRU_POS = 16
LRU_TINY = 1e-30
LRU_HALO = 16
ATT_HEADS = 8
ATT_KV_HEADS = 2
HEAD_DIM = 128
BLOCK = 128
ROPE_BASE = 10000.0
HY_ORDER = 2
HY_EMB = 33
HY_EMB_PAD = 40
HY_DECAY_TARGET = 1e-2
HY_FAST_PCT = 0.3
HY_SLOW_PCT = 1.5
HY_UNROLL = 4
HY_HALO = 16
HY_PITCH_PAD = 8
RET_HEADS = 8
RET_DK = 128
RET_CHUNK = 256

VMEM_LIMIT = 56 * 1024 * 1024
NEG = -1e30


def _params(sem, vmem=VMEM_LIMIT, **kw):
    return pltpu.CompilerParams(dimension_semantics=sem, vmem_limit_bytes=vmem, **kw)


def _sigmoid(v):
    return 0.5 * (jnp.tanh(0.5 * v) + 1.0)


def _silu(v):
    return v * _sigmoid(v)


def _softplus(v):
    return jnp.maximum(v, 0.0) + jnp.log(1.0 + jnp.exp(-jnp.abs(v)))


def _dot(a, b, **kw):
    return jnp.dot(a, b, preferred_element_type=F32, **kw)


def _dot_nt(a, b):
    return lax.dot_general(a, b, (((1,), (1,)), ((), ())), preferred_element_type=F32)


def _dot_tn(a, b):
    return lax.dot_general(a, b, (((0,), (0,)), ((), ())), preferred_element_type=F32)


def _mod_kernel(c_ref, w_ref, b_ref, o_ref):
    s = _silu(c_ref[...])
    o_ref[0] = _dot(s, w_ref[0], precision=HIGHEST) + b_ref[0]


def _modulation(crows, mod_w, mod_b):
    depth, d, n3 = mod_w.shape
    r = crows.shape[0]
    tn = 1024
    return pl.pallas_call(
        _mod_kernel,
        grid=(depth, n3 // tn),
        in_specs=[pl.BlockSpec((r, d), lambda l, j: (0, 0)),
                  pl.BlockSpec((1, d, tn), lambda l, j: (l, 0, j)),
                  pl.BlockSpec((1, 1, tn), lambda l, j: (l, 0, j))],
        out_specs=pl.BlockSpec((1, r, tn), lambda l, j: (l, 0, j)),
        out_shape=jax.ShapeDtypeStruct((depth, r, n3), F32),
        compiler_params=_params(("parallel", "parallel")),
        name="modulation",
    )(crows, mod_w, mod_b.reshape(depth, 1, n3))


def _rope(v, cos, sin, lane):
    swapped = jnp.where((lane & 63) < 32, pltpu.roll(v, LANES - 32, 1), pltpu.roll(v, 32, 1))
    return v * cos + swapped * sin


def _inproj_kernel(x_ref, g_ref, sc_ref, sh_ref, w_ref, o_ref, h_ref):
    @pl.when(pl.program_id(1) == 0)
    def _():
        x = x_ref[...]
        y = x * lax.rsqrt(jnp.mean(x * x, axis=-1, keepdims=True) + EPS) * g_ref[...]
        h_ref[...] = (y * (1.0 + sc_ref[0]) + sh_ref[0]).astype(BF16)

    o_ref[...] = _dot(h_ref[...], w_ref[...]).astype(o_ref.dtype)


def _in_proj(x2, g, scale, shift, rows_per_group, w, out_dtype, tm, tn):
    m, d = x2.shape
    n = w.shape[1]
    tpg = rows_per_group // tm
    return pl.pallas_call(
        _inproj_kernel,
        grid=(m // tm, n // tn),
        in_specs=[pl.BlockSpec((tm, d), lambda i, j: (i, 0)),
                  pl.BlockSpec((1, d), lambda i, j: (0, 0)),
                  pl.BlockSpec((1, 1, d), lambda i, j: (i // tpg, 0, 0)),
                  pl.BlockSpec((1, 1, d), lambda i, j: (i // tpg, 0, 0)),
                  pl.BlockSpec((d, tn), lambda i, j: (0, j))],
        out_specs=pl.BlockSpec((tm, tn), lambda i, j: (i, j)),
        out_shape=jax.ShapeDtypeStruct((m, n), out_dtype),
        scratch_shapes=[pltpu.VMEM((tm, d), BF16)],
        compiler_params=_params(("parallel", "arbitrary")),
        name="in_proj",
    )(x2, g, scale, shift, w)


def _outproj_kernel(a_ref, ga_ref, b_ref, gb_ref, w_ref, x_ref, g_ref, gate_ref, o_ref):
    wa = w_ref.shape[0] // 2
    a = (a_ref[...].astype(F32) * _silu(ga_ref[...].astype(F32))).astype(BF16)
    b = (b_ref[...].astype(F32) * _silu(gb_ref[...].astype(F32))).astype(BF16)
    y = _dot(a, w_ref[0:wa, :]) + _dot(b, w_ref[wa:, :])
    yn = y * lax.rsqrt(jnp.mean(y * y, axis=-1, keepdims=True) + EPS) * g_ref[...]
    o_ref[...] = x_ref[...] + gate_ref[0] * yn


def _out_proj(a, ga, ga_col, b, gb, gb_col, w, x2, g, gate, rows_per_group, tm):
    m, d = x2.shape
    wa = w.shape[0] // 2
    tpg = rows_per_group // tm
    return pl.pallas_call(
        _outproj_kernel,
        grid=(m // tm,),
        in_specs=[pl.BlockSpec((tm, wa), lambda i: (i, 0)),
                  pl.BlockSpec((tm, wa), lambda i: (i, ga_col)),
                  pl.BlockSpec((tm, wa), lambda i: (i, 0)),
                  pl.BlockSpec((tm, wa), lambda i: (i, gb_col)),
                  pl.BlockSpec(w.shape, lambda i: (0, 0)),
                  pl.BlockSpec((tm, d), lambda i: (i, 0)),
                  pl.BlockSpec((1, d), lambda i: (0, 0)),
                  pl.BlockSpec((1, 1, d), lambda i: (i // tpg, 0, 0))],
        out_specs=pl.BlockSpec((tm, d), lambda i: (i, 0)),
        out_shape=jax.ShapeDtypeStruct((m, d), F32),
        compiler_params=_params(("parallel",)),
        name="out_proj",
    )(a, ga, b, gb, w, x2, g, gate)


def _lru_kernel(xa_ref, xac_ref, cw_ref, cb_ref, wg_ref, bg_ref, lam_ref, y_ref, yc_ref,
                xp_ref, hf_ref, pf_ref, hb_ref, pb_ref):
    nseg = LRU_SEG
    npos = LRU_POS
    c8h = -0.5 * LRU_C * _softplus(-lam_ref[...])
    left = LRU_CONV // 2
    cw = [cw_ref[k:k + 1, :] for k in range(LRU_CONV)]
    cb = cb_ref[...]
    zeros_h = jnp.zeros((LRU_HALO, LANES), F32)
    zeros_s = jnp.zeros((nseg, LANES), F32)
    ones_s = jnp.ones((nseg, LANES), F32)

    def run(src_ref, out_ref, n, c0f, c0b):
        seg = n // nseg
        nblk = seg // npos
        halo = LRU_HALO
        pitch = seg + 2 * halo + 8
        for j in range(nseg):
            lo, hi = j * seg - halo, (j + 1) * seg + halo
            if lo < 0:
                xp_ref[j * pitch:j * pitch + halo, :] = zeros_h
            if hi > n:
                xp_ref[j * pitch + seg + halo:j * pitch + seg + 2 * halo, :] = zeros_h
            lo_c, hi_c = max(lo, 0), min(hi, n)
            xp_ref[j * pitch + (lo_c - lo):j * pitch + (hi_c - lo), :] = src_ref[lo_c:hi_c, :].astype(F32)

        def gather(q):
            return jnp.concatenate([xp_ref[pl.ds(halo + q + c * 8 * pitch, 8, stride=pitch), :]
                                    for c in range(nseg // 8)], axis=0)

        def blk(i):
            return pl.ds(pl.multiple_of(i * npos * nseg, npos * nseg), npos * nseg)

        def pos(v, p):
            return v[p * nseg:(p + 1) * nseg, :]

        def fwd_body(i, carry):
            h, pc = carry
            p0 = i * npos
            xs = [gather(p0 + q - left) for q in range(npos + LRU_CONV - 1)]
            us = []
            for p in range(npos):
                u = cb + cw[0] * xs[p]
                for k in range(1, LRU_CONV):
                    u = u + cw[k] * xs[p + k]
                us.append(u)
            u = jnp.concatenate(us, axis=0)
            g = _dot(u.astype(BF16), wg_ref[0]) + bg_ref[0]
            hu = 0.5 * u
            coef = []
            for d in range(2):
                tr = jnp.tanh(g[:, (2 * d) * LANES:(2 * d + 1) * LANES])
                ti = jnp.tanh(g[:, (2 * d + 1) * LANES:(2 * d + 2) * LANES])
                a = jnp.exp(c8h[d:d + 1, :] * tr + c8h[d:d + 1, :])
                om = 1.0 - a * a
                coef.append((a, om * lax.rsqrt(jnp.maximum(om, LRU_TINY)) * ((ti + 1.0) * hu)))
            (af, bf), (ab, bb) = coef
            hb_ref[blk(i), :] = ab
            pb_ref[blk(i), :] = bb
            hs, ps = [], []
            for p in range(npos):
                a_p = pos(af, p)
                h = a_p * h + pos(bf, p)
                pc = a_p * pc
                hs.append(h)
                ps.append(pc)
            hf_ref[blk(i), :] = jnp.concatenate(hs, axis=0)
            pf_ref[blk(i), :] = jnp.concatenate(ps, axis=0)
            return h, pc

        hef, pef = lax.fori_loop(0, nblk, fwd_body, (zeros_s, ones_s))

        def bwd_body(ii, carry):
            h, pc = carry
            i = nblk - 1 - ii
            ab = hb_ref[blk(i), :]
            bb = pb_ref[blk(i), :]
            hs, ps = [None] * npos, [None] * npos
            for p in reversed(range(npos)):
                a_p = pos(ab, p)
                h = a_p * h + pos(bb, p)
                pc = a_p * pc
                hs[p] = h
                ps[p] = pc
            hb_ref[blk(i), :] = jnp.concatenate(hs, axis=0)
            pb_ref[blk(i), :] = jnp.concatenate(ps, axis=0)
            return h, pc

        heb, peb = lax.fori_loop(0, nblk, bwd_body, (zeros_s, ones_s))

        rows_f, c = [], c0f
        for j in range(nseg):
            rows_f.append(c)
            c = hef[j:j + 1, :] + pef[j:j + 1, :] * c
        final_f = c
        rows_b, c = [None] * nseg, c0b
        for j in reversed(range(nseg)):
            rows_b[j] = c
            c = heb[j:j + 1, :] + peb[j:j + 1, :] * c
        final_b = c
        cf = jnp.concatenate(rows_f, axis=0)
        cbk = jnp.concatenate(rows_b, axis=0)

        opitch = seg + 8

        def out_body(p, _):
            rs = pl.ds(pl.multiple_of(p * nseg, nseg), nseg)
            y = hf_ref[rs, :] + pf_ref[rs, :] * cf + hb_ref[rs, :] + pb_ref[rs, :] * cbk
            for c in range(nseg // 8):
                xp_ref[pl.ds(p + c * 8 * opitch, 8, stride=opitch), :] = y[c * 8:(c + 1) * 8, :]
            return 0
        lax.fori_loop(0, seg, out_body, 0, unroll=8)
        for j in range(nseg):
            out_ref[j * seg:(j + 1) * seg, :] = xp_ref[j * opitch:j * opitch + seg, :].astype(out_ref.dtype)
        return final_f, final_b

    zero = jnp.zeros((1, LANES), F32)
    ff, fb = run(xac_ref, yc_ref, xac_ref.shape[0], zero, zero)
    run(xa_ref, y_ref, xa_ref.shape[0], ff, fb)


def _lru(proj, projc, xa_col, bsz, conv_w, conv_b, wg, bg, lam):
    w = conv_w.shape[1]
    s_len = proj.shape[0] // bsz
    c_len = projc.shape[0] // bsz
    nblk = w // LANES
    seq = lambda n: pl.BlockSpec((n, LANES), lambda b, j: (b, j))
    src = lambda n: pl.BlockSpec((n, LANES), lambda b, j: (b, xa_col + j))
    return pl.pallas_call(
        _lru_kernel,
        grid=(bsz, nblk),
        in_specs=[src(s_len), src(c_len),
                  pl.BlockSpec((LRU_CONV, LANES), lambda b, j: (0, j)),
                  pl.BlockSpec((1, LANES), lambda b, j: (0, j)),
                  pl.BlockSpec((1, LANES, 4 * LANES), lambda b, j: (j, 0, 0)),
                  pl.BlockSpec((1, 1, 4 * LANES), lambda b, j: (j, 0, 0)),
                  pl.BlockSpec((2, LANES), lambda b, j: (0, j))],
        out_specs=[seq(s_len), seq(c_len)],
        out_shape=[jax.ShapeDtypeStruct((proj.shape[0], w), BF16), jax.ShapeDtypeStruct((projc.shape[0], w), BF16)],
        scratch_shapes=[pltpu.VMEM((s_len + (2 * LRU_HALO + 8) * LRU_SEG, LANES), F32)]
        + [pltpu.VMEM((s_len, LANES), F32)] * 4,
        compiler_params=_params(("parallel", "parallel")),
        name="rglru",
    )(proj, projc, conv_w, conv_b, wg, bg, lam)


def _attn_kernel(sink_ref, q_ref, k_ref, v_ref, kc_ref, vc_ref, cos_ref, sin_ref, bias_ref, o_ref):
    qb = pl.program_id(1)
    nb = pl.num_programs(1)
    group = ATT_HEADS // ATT_KV_HEADS
    scale = HEAD_DIM ** -0.5
    lane = lax.broadcasted_iota(jnp.int32, (BLOCK, LANES), 1)

    def blk(i):
        return pl.ds(pl.multiple_of(i * BLOCK, BLOCK), BLOCK)

    def rot(x, tab):
        return _rope(x.astype(F32), tab[0], tab[1], lane).astype(BF16)

    ip = jnp.maximum(qb - 1, 0)
    inx = jnp.minimum(qb + 1, nb - 1)
    tab_p = (cos_ref[blk(ip), :], sin_ref[blk(ip), :])
    tab_o = (cos_ref[blk(qb), :], sin_ref[blk(qb), :])
    tab_n = (cos_ref[blk(inx), :], sin_ref[blk(inx), :])
    tab_q = (tab_o[0] * scale, tab_o[1] * scale)
    bias = jnp.concatenate([bias_ref[0]] * group, axis=0)
    ones = jnp.ones((bias.shape[1], HEAD_DIM), BF16)

    for h in range(ATT_KV_HEADS):
        ksl = slice(h * HEAD_DIM, (h + 1) * HEAD_DIM)
        kcat = jnp.concatenate([rot(k_ref[blk(ip), ksl], tab_p), rot(k_ref[blk(qb), ksl], tab_o),
                                rot(k_ref[blk(inx), ksl], tab_n), kc_ref[:, ksl]], axis=0)
        vcat = jnp.concatenate([v_ref[blk(ip), ksl], v_ref[blk(qb), ksl], v_ref[blk(inx), ksl],
                                vc_ref[:, ksl]], axis=0)
        vext = jnp.concatenate([vcat, ones], axis=1)
        heads = [h * group + g for g in range(group)]
        q4 = jnp.concatenate([rot(q_ref[:, hh * HEAD_DIM:(hh + 1) * HEAD_DIM], tab_q) for hh in heads], axis=0)
        sk = jnp.concatenate([jnp.full((BLOCK, 1), sink_ref[hh], F32) for hh in heads], axis=0)
        s = _dot_nt(q4, kcat) + bias
        m = jnp.maximum(jnp.max(s, axis=-1, keepdims=True), sk)
        p = jnp.exp((s - m).astype(BF16))
        oe = _dot(p, vext)
        o = oe[:, 0:HEAD_DIM] / (jnp.exp(sk - m) + oe[:, HEAD_DIM:HEAD_DIM + 1])
        for g, hh in enumerate(heads):
            o_ref[:, hh * HEAD_DIM:(hh + 1) * HEAD_DIM] = o[g * BLOCK:(g + 1) * BLOCK, :].astype(o_ref.dtype)


def _attention(sink, proj, projc, bsz, cols, cos_t, sin_t):
    s_len = proj.shape[0] // bsz
    c_len = projc.shape[0] // bsz
    nb = s_len // BLOCK
    aw = ATT_HEADS * HEAD_DIM
    kw = ATT_KV_HEADS * HEAD_DIM
    qi = np.arange(BLOCK)[:, None]
    kj = np.arange(BLOCK)[None, :]
    variants = []
    for v in range(4):
        prev = np.where((kj >= qi) & bool(v & 1), 0.0, NEG)
        nxt = np.where((kj <= qi) & bool(v & 2), 0.0, NEG)
        variants.append(np.concatenate([prev, np.zeros((BLOCK, BLOCK)), nxt, np.zeros((BLOCK, c_len))], axis=1))
    bias = jnp.asarray(np.stack(variants), F32)
    ncol = 3 * BLOCK + c_len

    def variant(b, i):
        return ((i > 0).astype(jnp.int32) + 2 * (i < nb - 1).astype(jnp.int32), 0, 0)

    return pl.pallas_call(
        _attn_kernel,
        grid=(bsz, nb),
        in_specs=[pl.BlockSpec(memory_space=pltpu.SMEM),
                  pl.BlockSpec((BLOCK, aw), lambda b, i: (b * nb + i, cols["q"])),
                  pl.BlockSpec((s_len, kw), lambda b, i: (b, cols["k"])),
                  pl.BlockSpec((s_len, kw), lambda b, i: (b, cols["v"])),
                  pl.BlockSpec((c_len, kw), lambda b, i: (b, cols["k"])),
                  pl.BlockSpec((c_len, kw), lambda b, i: (b, cols["v"])),
                  pl.BlockSpec((s_len, LANES), lambda b, i: (0, 0)),
                  pl.BlockSpec((s_len, LANES), lambda b, i: (0, 0)),
                  pl.BlockSpec((1, BLOCK, ncol), variant)],
        out_specs=pl.BlockSpec((BLOCK, aw), lambda b, i: (b * nb + i, 0)),
        out_shape=jax.ShapeDtypeStruct((proj.shape[0], aw), BF16),
        compiler_params=_params(("parallel", "arbitrary")),
        name="window_attention",
    )(sink, proj, proj, proj, projc, projc, cos_t, sin_t, bias)


def _ctx_attn_kernel(sink_ref, q_ref, k_ref, v_ref, o_ref):
    group = ATT_HEADS // ATT_KV_HEADS
    scale = HEAD_DIM ** -0.5
    n = q_ref.shape[0]
    for h in range(ATT_KV_HEADS):
        ksl = slice(h * HEAD_DIM, (h + 1) * HEAD_DIM)
        qs, sinks = [], []
        for g in range(group):
            hh = h * group + g
            qs.append((q_ref[:, hh * HEAD_DIM:(hh + 1) * HEAD_DIM].astype(F32) * scale).astype(BF16))
            sinks.append(jnp.full((n, 1), sink_ref[hh], F32))
        q4 = jnp.concatenate(qs, axis=0)
        sk = jnp.concatenate(sinks, axis=0)
        s = _dot_nt(q4, k_ref[:, ksl])
        m = jnp.maximum(jnp.max(s, axis=-1, keepdims=True), sk)
        p = jnp.exp(s - m)
        denom = jnp.exp(sk - m) + jnp.sum(p, axis=-1, keepdims=True)
        o = _dot(p.astype(BF16), v_ref[:, ksl]) / denom
        for g in range(group):
            hh = h * group + g
            hs = slice(hh * HEAD_DIM, (hh + 1) * HEAD_DIM)
            o_ref[:, hs] = o[g * n:(g + 1) * n, :].astype(o_ref.dtype)


def _ctx_attention(sink, projc, bsz, cols):
    c_len = projc.shape[0] // bsz
    aw = ATT_HEADS * HEAD_DIM
    kw = ATT_KV_HEADS * HEAD_DIM
    return pl.pallas_call(
        _ctx_attn_kernel,
        grid=(bsz,),
        in_specs=[pl.BlockSpec(memory_space=pltpu.SMEM),
                  pl.BlockSpec((c_len, aw), lambda b: (b, cols["q"])),
                  pl.BlockSpec((c_len, kw), lambda b: (b, cols["k"])),
                  pl.BlockSpec((c_len, kw), lambda b: (b, cols["v"]))],
        out_specs=pl.BlockSpec((c_len, aw), lambda b: (b, 0)),
        out_shape=jax.ShapeDtypeStruct((projc.shape[0], aw), BF16),
        compiler_params=_params(("parallel",)),
        name="context_attention",
    )(sink, projc, projc, projc)


def _ret_kernel(dl_ref, q_ref, k_ref, v_ref, qc_ref, kc_ref, vc_ref, o_ref, oc_ref,
                ot_ref, att_ref, u_ref, st_ref, vt_ref, dm_ref):
    c = RET_CHUNK
    s_len = q_ref.shape[0]
    c_len = qc_ref.shape[0]
    lg = -_softplus(-dl_ref[0])
    lgf, lgb = lg[0:1, :], lg[1:2, :]
    wide = lambda t: jnp.concatenate([t] * (c // LANES), axis=1)
    kj = lax.broadcasted_iota(jnp.int32, (c, c), 0)
    qi = lax.broadcasted_iota(jnp.int32, (c, c), 1)
    diff = (qi - kj).astype(F32)
    dm_ref[...] = jnp.where(qi >= kj, jnp.exp(jnp.maximum(diff, 0.0) * wide(lgf)),
                            jnp.exp(jnp.maximum(-diff, 0.0) * wide(lgb)))
    idx = lax.broadcasted_iota(jnp.int32, (c, LANES), 0).astype(F32)
    qdec_f = jnp.exp((idx + 1.0) * lgf)
    kdec_f = jnp.exp((c - 1.0 - idx) * lgf)
    qdec_b = jnp.exp((c - idx) * lgb)
    kdec_b = jnp.exp(idx * lgb)
    cdec_f = jnp.exp(c * lgf)
    cdec_b = jnp.exp(c * lgb)

    def chunk(j):
        return pl.ds(pl.multiple_of(j * c, c), c)

    def transpose_v(vr, n):
        def body(j, _):
            vt_ref[:, chunk(j)] = vr[chunk(j), :].astype(F32).T.astype(BF16)
            return 0
        lax.fori_loop(0, n // c, body, 0, unroll=min(8, n // c))

    def scores(qr, kr, n):
        def body(j, _):
            att_ref[j] = (_dot_nt(kr[chunk(j), :], qr[chunk(j), :]) * dm_ref[...]).astype(BF16)
            return 0
        lax.fori_loop(0, n // c, body, 0, unroll=min(4, n // c))

    def intra(kr, n):
        def body(j, _):
            vt = vt_ref[:, chunk(j)]
            ot_ref[:, chunk(j)] = _dot(vt, att_ref[j])
            kf = kr[chunk(j), :].astype(F32)
            kcat = jnp.concatenate([(kf * kdec_f).astype(BF16), (kf * kdec_b).astype(BF16)], axis=1)
            u_ref[j] = _dot(vt, kcat)
            return 0
        lax.fori_loop(0, n // c, body, 0, unroll=min(4, n // c))

    def states(n, sf, sb):
        nch = n // c

        def fbody(j, s):
            st_ref[j, :, 0:RET_DK] = s.astype(BF16)
            return s * cdec_f + u_ref[j, :, 0:RET_DK]
        sf = lax.fori_loop(0, nch, fbody, sf)

        def bbody(jj, s):
            j = nch - 1 - jj
            st_ref[j, :, RET_DK:] = s.astype(BF16)
            return s * cdec_b + u_ref[j, :, RET_DK:]
        sb = lax.fori_loop(0, nch, bbody, sb)
        return sf, sb

    def cross(qr, outr, n):
        def body(j, _):
            qf = qr[chunk(j), :].astype(F32)
            qcat = jnp.concatenate([(qf * qdec_f).astype(BF16), (qf * qdec_b).astype(BF16)], axis=1)
            ot = ot_ref[:, chunk(j)] + _dot_nt(st_ref[j], qcat)
            o = ot.T * (RET_DK ** -0.5)
            outr[chunk(j), :] = (o * lax.rsqrt(jnp.mean(o * o, axis=-1, keepdims=True) + EPS)).astype(outr.dtype)
            return 0
        lax.fori_loop(0, n // c, body, 0, unroll=min(4, n // c))

    def run(qr, kr, vr, outr, n, sf, sb):
        transpose_v(vr, n)
        scores(qr, kr, n)
        intra(kr, n)
        finals = states(n, sf, sb)
        cross(qr, outr, n)
        return finals

    zero = jnp.zeros((LANES, RET_DK), F32)
    sf, sb = run(qc_ref, kc_ref, vc_ref, oc_ref, c_len, zero, zero)
    run(q_ref, k_ref, v_ref, o_ref, s_len, sf, sb)


def _retention(dl, proj, projc, bsz, cols):
    s_len = proj.shape[0] // bsz
    c_len = projc.shape[0] // bsz
    hb = lambda n, col: pl.BlockSpec((n, LANES), lambda b, h: (b, col + h))
    return pl.pallas_call(
        _ret_kernel,
        grid=(bsz, RET_HEADS),
        in_specs=[pl.BlockSpec((1, 2, LANES), lambda b, h: (h, 0, 0)),
                  hb(s_len, cols["q"]), hb(s_len, cols["k"]), hb(s_len, cols["v"]),
                  hb(c_len, cols["q"]), hb(c_len, cols["k"]), hb(c_len, cols["v"])],
        out_specs=[pl.BlockSpec((s_len, LANES), lambda b, h: (b, h)),
                   pl.BlockSpec((c_len, LANES), lambda b, h: (b, h))],
        out_shape=[jax.ShapeDtypeStruct((proj.shape[0], RET_HEADS * LANES), BF16),
                   jax.ShapeDtypeStruct((projc.shape[0], RET_HEADS * LANES), BF16)],
        scratch_shapes=[pltpu.VMEM((LANES, s_len), F32),
                        pltpu.VMEM((s_len // RET_CHUNK, RET_CHUNK, RET_CHUNK), BF16),
                        pltpu.VMEM((s_len // RET_CHUNK, LANES, 2 * RET_DK), F32),
                        pltpu.VMEM((s_len // RET_CHUNK, LANES, 2 * RET_DK), BF16),
                        pltpu.VMEM((LANES, s_len), BF16),
                        pltpu.VMEM((RET_CHUNK, RET_CHUNK), F32)],
        compiler_params=_params(("parallel", "parallel")),
        name="retention",
    )(dl, proj, proj, proj, projc, projc, projc)


def _hy_hid_kernel(z_ref, w1_ref, b1_ref, w2_ref, b2_ref, f_ref, o_ref):
    f = f_ref[...]
    h = jnp.sin(f * (_dot(z_ref[...], w1_ref[...], precision=HIGHEST) + b1_ref[...]))
    o_ref[...] = jnp.sin(f * (_dot(h, w2_ref[...], precision=HIGHEST) + b2_ref[...]))


def _hy_hidden(zfull, w1p, b1, w2, b2, freq):
    n, e = zfull.shape
    fd = w2.shape[0]
    tr = min(n, 1024)
    full = lambda shp: pl.BlockSpec(shp, lambda i: (0, 0))
    return pl.pallas_call(
        _hy_hid_kernel,
        grid=(n // tr,),
        in_specs=[pl.BlockSpec((tr, e), lambda i: (i, 0)), full((e, fd)), full((1, fd)), full((fd, fd)),
                  full((1, fd)), full((1, fd))],
        out_specs=pl.BlockSpec((tr, fd), lambda i: (i, 0)),
        out_shape=jax.ShapeDtypeStruct((n, fd), F32),
        compiler_params=_params(("parallel",)),
        name="hyena_filter_mlp",
    )(zfull, w1p, b1, w2, b2, freq)


def _hy_filt_kernel(hid_ref, w3f_ref, w3b_ref, tn_ref, dl_ref, o_ref):
    half = hid_ref.shape[0] // 2
    decay = jnp.exp(-tn_ref[...] * jnp.abs(dl_ref[...]))
    top = _dot(hid_ref[0:half, :], w3f_ref[...], precision=HIGHEST)
    bot = _dot(hid_ref[half:, :], w3b_ref[...], precision=HIGHEST)
    row = lax.broadcasted_iota(jnp.int32, bot.shape, 0)
    bot = jnp.where(row == 0, 0.0, bot)
    o_ref[0, 0:half, :] = top * decay[0:half, :]
    o_ref[0, half:, :] = bot * decay[half:, :]


def _hy_filters(hid, w3, tn_full, deltas):
    n, fd = hid.shape
    wch = deltas.shape[1]
    nsl = wch // LANES
    return pl.pallas_call(
        _hy_filt_kernel,
        grid=(HY_ORDER, nsl),
        in_specs=[pl.BlockSpec((n, fd), lambda o, j: (0, 0)),
                  pl.BlockSpec((fd, LANES), lambda o, j: (0, o * 2 * nsl + j)),
                  pl.BlockSpec((fd, LANES), lambda o, j: (0, o * 2 * nsl + nsl + j)),
                  pl.BlockSpec((n, LANES), lambda o, j: (0, 0)),
                  pl.BlockSpec((1, LANES), lambda o, j: (0, j))],
        out_specs=pl.BlockSpec((1, n, LANES), lambda o, j: (o, 0, j)),
        out_shape=jax.ShapeDtypeStruct((HY_ORDER, n, wch), F32),
        compiler_params=_params(("parallel", "parallel")),
        name="hyena_filter",
    )(hid, w3, w3, tn_full, deltas)


def _hy_kfft_kernel(kern_ref, f1_ref, f2_ref, o_ref, g_ref, *, n1, n2):
    pg = 2 * n1 + HY_PITCH_PAD

    def stage1(i2, _):
        x = kern_ref[pl.ds(i2, n1, stride=n2), :].astype(BF16)
        g_ref[pl.ds(pl.multiple_of(i2 * pg, 8), 2 * n1), :] = _dot(f1_ref[i2], x)
        return 0
    lax.fori_loop(0, n2, stage1, 0, unroll=HY_UNROLL)

    def stage2(k1, _):
        x = jnp.concatenate([g_ref[pl.ds(k1, n2, stride=pg), :],
                             g_ref[pl.ds(n1 + k1, n2, stride=pg), :]], axis=0).astype(BF16)
        o_ref[pl.ds(pl.multiple_of(k1 * 2 * n2, 2 * n2), 2 * n2), :] = _dot(f2_ref[...], x)
        return 0
    lax.fori_loop(0, n1, stage2, 0, unroll=HY_UNROLL // 2)


def _hy_kfft(kern, f1k, f2, n1, n2):
    orders, n, wch = kern.shape
    nsl = wch // LANES
    return pl.pallas_call(
        functools.partial(_hy_kfft_kernel, n1=n1, n2=n2),
        grid=(orders, nsl),
        in_specs=[pl.BlockSpec((None, n, LANES), lambda o, j: (o, 0, j)),
                  pl.BlockSpec(f1k.shape, lambda o, j: (0, 0, 0)),
                  pl.BlockSpec(f2.shape, lambda o, j: (0, 0))],
        out_specs=pl.BlockSpec((None, None, 2 * n, LANES), lambda o, j: (o, j, 0, 0)),
        out_shape=jax.ShapeDtypeStruct((orders, nsl, 2 * n, LANES), F32),
        scratch_shapes=[pltpu.VMEM((n2 * (2 * n1 + HY_PITCH_PAD), LANES), F32)],
        compiler_params=_params(("parallel", "parallel")),
        name="hyena_filter_fft",
    )(kern, f1k, f2)


def _hy_prep_kernel(z_ref, w_ref, b_ref, o_ref, zp_ref, *, n2):
    l_len = z_ref.shape[0]
    h1 = l_len // n2
    halo = HY_HALO
    pitch = n2 + 2 * halo + 8
    zeros_h = jnp.zeros((halo, LANES), F32)
    for i1 in range(h1):
        lo, hi = i1 * n2 - halo, (i1 + 1) * n2 + halo
        if lo < 0:
            zp_ref[i1 * pitch:i1 * pitch + halo, :] = zeros_h
        if hi > l_len:
            zp_ref[i1 * pitch + n2 + halo:i1 * pitch + n2 + 2 * halo, :] = zeros_h
        lo_c, hi_c = max(lo, 0), min(hi, l_len)
        zp_ref[i1 * pitch + (lo_c - lo):i1 * pitch + (hi_c - lo), :] = z_ref[lo_c:hi_c, :].astype(F32)
    w0, w1, w2, b = w_ref[0:1, :], w_ref[1:2, :], w_ref[2:3, :], b_ref[...]

    def tap(j):
        return zp_ref[pl.ds(halo + j, h1, stride=pitch), :]

    def body(i2, carry):
        zm, z0 = carry
        zn = tap(i2 + 1)
        val = b + w0 * zm + w1 * z0 + w2 * zn
        o_ref[pl.ds(pl.multiple_of(i2 * h1, h1), h1), :] = val.astype(o_ref.dtype)
        return z0, zn
    lax.fori_loop(0, n2, body, (tap(-1), tap(0)), unroll=HY_UNROLL)


def _hy_stage_dtype(l_len, n2):
    return BF16 if (l_len // n2) % 16 == 0 else F32


def _hy_prep(proj, z_col, nz, bsz, conv_w, conv_b, n2):
    l_len = proj.shape[0] // bsz
    h1 = l_len // n2
    return pl.pallas_call(
        functools.partial(_hy_prep_kernel, n2=n2),
        grid=(bsz, nz),
        in_specs=[pl.BlockSpec((l_len, LANES), lambda b, j: (b, z_col + j)),
                  pl.BlockSpec((3, LANES), lambda b, j: (0, j)),
                  pl.BlockSpec((1, LANES), lambda b, j: (0, j))],
        out_specs=pl.BlockSpec((l_len, LANES), lambda b, j: (b, j)),
        out_shape=jax.ShapeDtypeStruct((proj.shape[0], nz * LANES), _hy_stage_dtype(l_len, n2)),
        scratch_shapes=[pltpu.VMEM((h1 * (n2 + 2 * HY_HALO + 8), LANES), F32)],
        compiler_params=_params(("parallel", "parallel")),
        name="hyena_short_conv",
    )(proj, conv_w, conv_b)


def _pack2(re, im):
    r = lax.bitcast_convert_type(re, jnp.uint32) + jnp.uint32(0x8000)
    i = lax.bitcast_convert_type(im, jnp.uint32) + jnp.uint32(0x8000)
    return (r & jnp.uint32(0xFFFF0000)) | (i >> 16)


def _unpack2(w):
    re = lax.bitcast_convert_type(w & jnp.uint32(0xFFFF0000), F32)
    im = lax.bitcast_convert_type(w << 16, F32)
    return jnp.concatenate([re, im], axis=0).astype(BF16)


def _hy_conv_kernel(y_ref, x_ref, kh_ref, bias_ref, f1_ref, f2_ref, f2i_ref, f1i_ref, o_ref, g_ref,
                    *, n1, n2, natural_out):
    h1 = n1 // 2
    l_len = h1 * n2
    pg = n1 + HY_PITCH_PAD
    bias = bias_ref[...]
    unroll = HY_UNROLL

    def rows(seq, i2):
        return pl.ds(pl.multiple_of(seq * l_len + i2 * h1, h1), h1)

    def lanes2(a, b):
        return jnp.concatenate([a, b], axis=1)

    def stage1(i2, _):
        x = lanes2(jnp.concatenate([y_ref[rows(0, i2), :], y_ref[rows(1, i2), :]], axis=0),
                   jnp.concatenate([y_ref[rows(2, i2), :], y_ref[rows(3, i2), :]], axis=0)).astype(BF16)
        a = _dot(f1_ref[i2], x)
        w = _pack2(a[0:n1, :], a[n1:, :])
        blk = pl.ds(pl.multiple_of(i2 * pg, 8), n1)
        g_ref[0, blk, :] = w[:, 0:LANES]
        g_ref[1, blk, :] = w[:, LANES:]
        return 0
    lax.fori_loop(0, n2, stage1, 0, unroll=2 * unroll)

    def stage23(t, _):
        outs = []
        for u in range(unroll):
            k1 = t * unroll + u
            col = pl.ds(k1, n2, stride=pg)
            x = lanes2(_unpack2(g_ref[0, col, :]), _unpack2(g_ref[1, col, :]))
            yh = _dot(f2_ref[...], x)
            base = pl.multiple_of(k1 * 2 * n2, 2 * n2)
            kr = kh_ref[pl.ds(base, n2), :]
            ki = kh_ref[pl.ds(base + n2, n2), :]
            kr, ki = lanes2(kr, kr), lanes2(ki, ki)
            yr, yi = yh[0:n2, :], yh[n2:, :]
            z = jnp.concatenate([yr * kr - yi * ki, yr * ki + yi * kr], axis=0).astype(BF16)
            c = _dot(f2i_ref[...], z)
            outs.append((col, _pack2(c[0:n2, :], c[n2:, :])))
        for col, w in outs:
            g_ref[0, col, :] = w[:, 0:LANES]
            g_ref[1, col, :] = w[:, LANES:]
        return 0
    lax.fori_loop(0, n1 // unroll, stage23, 0)

    def stage4(i2, _):
        blk = pl.ds(pl.multiple_of(i2 * pg, 8), n1)
        x = lanes2(_unpack2(g_ref[0, blk, :]), _unpack2(g_ref[1, blk, :]))
        yc = _dot(f1i_ref[i2], x)
        for seq in range(4):
            pair, half = seq // 2, seq % 2
            sl = rows(seq, i2)
            conv = yc[half * h1:(half + 1) * h1, pair * LANES:(pair + 1) * LANES]
            val = x_ref[sl, :].astype(F32) * (conv + bias * y_ref[sl, :].astype(F32))
            if natural_out:
                o_ref[pl.ds(seq * l_len + i2, h1, stride=n2), :] = val
            else:
                o_ref[sl, :] = val.astype(o_ref.dtype)
        return 0
    lax.fori_loop(0, n2, stage4, 0, unroll=unroll)


def _hy_conv(ysrc, ycol, xsrc, xcol, khat, order, bias, mats, bsz, n1, n2, natural_out):
    l_len = ysrc.shape[0] // bsz
    nsl = khat.shape[1]
    f1, f2, f2i, f1i = mats
    once = pl.Buffered(1)
    cst2 = lambda a: pl.BlockSpec(a.shape, lambda j, p: (0, 0), pipeline_mode=once)
    cst3 = lambda a: pl.BlockSpec(a.shape, lambda j, p: (0, 0, 0), pipeline_mode=once)
    return pl.pallas_call(
        functools.partial(_hy_conv_kernel, n1=n1, n2=n2, natural_out=natural_out),
        grid=(nsl, bsz // 4),
        in_specs=[pl.BlockSpec((4 * l_len, LANES), lambda j, p: (p, ycol + j)),
                  pl.BlockSpec((4 * l_len, LANES), lambda j, p: (p, xcol + j)),
                  pl.BlockSpec((None, None, khat.shape[2], LANES), lambda j, p: (order, j, 0, 0),
                               pipeline_mode=once),
                  pl.BlockSpec((1, LANES), lambda j, p: (0, j)),
                  cst3(f1), cst2(f2), cst2(f2i), cst3(f1i)],
        out_specs=pl.BlockSpec((4 * l_len, LANES), lambda j, p: (p, j)),
        out_shape=jax.ShapeDtypeStruct((ysrc.shape[0], nsl * LANES),
                                       F32 if natural_out else _hy_stage_dtype(l_len, n2)),
        scratch_shapes=[pltpu.VMEM((2, n2 * (n1 + HY_PITCH_PAD), LANES), jnp.uint32)],
        compiler_params=_params(("parallel", "arbitrary")),
        name="hyena_long_conv",
    )(ysrc, xsrc, khat, bias[order][None], f1, f2, f2i, f1i)


def _dft_tables(l_len):
    n = 2 * l_len
    n2 = 128 if l_len >= 1024 else 32
    n1 = n // n2
    h1 = n1 // 2
    i1 = np.arange(n1)[None, None, :]
    k1 = np.arange(n1)[None, :, None]
    i2 = np.arange(n2)[:, None, None]
    ph = 2 * np.pi * (i1 * k1 / n1 + i2 * k1 / n)
    c, s = np.cos(ph), np.sin(ph)
    ch, sh = c[:, :, :h1], s[:, :, :h1]
    f1 = np.concatenate([np.concatenate([ch, sh], 2), np.concatenate([-sh, ch], 2)], 1)
    f1k = np.concatenate([c, -s], 1)
    ct, st = np.swapaxes(ch, 1, 2), np.swapaxes(sh, 1, 2)
    f1i = np.concatenate([np.concatenate([ct, -st], 2), np.concatenate([st, ct], 2)], 1) / n
    a = np.arange(n2)
    ph2 = 2 * np.pi * np.outer(a, a) / n2
    c2, s2 = np.cos(ph2), np.sin(ph2)
    f2 = np.block([[c2, s2], [-s2, c2]])
    f2i = np.block([[c2, -s2], [s2, c2]])
    bf = lambda m: jnp.asarray(m, dtype=F32).astype(BF16)
    return n1, n2, (bf(f1), bf(f2), bf(f2i), bf(f1i)), bf(f1k)


def _filter_positions(l_len, width):
    lag = np.concatenate([np.arange(l_len), l_len - np.arange(l_len)]).astype(np.float64)
    t = lag / (l_len - 1)
    bands = (HY_EMB - 1) // 2
    w = 2.0 * np.pi * lag / l_len
    f = np.linspace(1e-4, bands - 1, bands)[None]
    z = np.concatenate([t[:, None], np.cos(f * w[:, None]), -np.sin(f * w[:, None])], axis=-1)
    z = np.pad(z, ((0, 0), (0, HY_EMB_PAD - HY_EMB)))
    tn = np.repeat(t[:, None], LANES, axis=1)
    max_decay = math.log(HY_DECAY_TARGET) / HY_FAST_PCT
    min_decay = math.log(HY_DECAY_TARGET) / HY_SLOW_PCT
    deltas = np.linspace(min_decay, max_decay, width)[None]
    return jnp.asarray(z, F32), jnp.asarray(tn, F32), jnp.asarray(deltas, F32)


def _rope_tables(seq):
    n_rows = seq // GRID_W
    row = np.repeat(np.arange(n_rows), GRID_W).astype(np.float64)
    col = np.tile(np.arange(GRID_W), n_rows).astype(np.float64)
    half = HEAD_DIM // 2
    inv = ROPE_BASE ** (-np.arange(0, half, 2, dtype=np.float64) / half)
    ar, ac = row[:, None] * inv, col[:, None] * inv
    cos = np.concatenate([np.cos(ar), np.cos(ar), np.cos(ac), np.cos(ac)], axis=1)
    sin = np.concatenate([-np.sin(ar), np.sin(ar), -np.sin(ac), np.sin(ac)], axis=1)
    return jnp.asarray(cos, F32), jnp.asarray(sin, F32)


def _hyena_spectra(l_len, width, w1, b1, w2, b2, w3, freq, tables):
    n1, n2, mats, f1k = tables
    zfull, tn_full, deltas = _filter_positions(l_len, width)
    w1p = jnp.pad(w1, ((0, HY_EMB_PAD - HY_EMB), (0, 0)))
    hid = _hy_hidden(zfull, w1p, b1[None], w2, b2[None], freq[None])
    kern = _hy_filters(hid, w3, tn_full, deltas)
    return _hy_kfft(kern, f1k, mats[1], n1, n2)


def _hyena_seq(zc, bsz, khat, bias, tables):
    n1, n2, mats, _ = tables
    nsl = khat.shape[1]
    y1 = _hy_conv(zc, 0, zc, nsl, khat, 0, bias, mats, bsz, n1, n2, False)
    return _hy_conv(y1, 0, zc, 2 * nsl, khat, 1, bias, mats, bsz, n1, n2, True)


def kernel(x, c, ctx, c_ctx, mod_w, mod_b, norm_pre, norm_post, ev_w_in, ev_w_out, lru_conv_w, lru_conv_b, lru_wa, lru_ba, lru_wx, lru_bx, lru_lambda, attn_sink, od_w_in, od_w_out, hy_conv_w, hy_conv_b, hy_w1, hy_b1, hy_w2, hy_b2, hy_w3, hy_freq, hy_bias, ret_decay_logit):
    bsz, s_len, d = x.shape
    c_len = ctx.shape[1]
    depth = mod_w.shape[0]
    assert bsz % 4 == 0 and bsz <= 16 and s_len % 1024 == 0 and c_len % 256 == 0 and c_len <= s_len

    crows = jnp.concatenate([c, c_ctx[None], jnp.zeros((24 - bsz - 1, d), F32)], axis=0)
    mod = _modulation(crows, mod_w, mod_b)

    cos_t, sin_t = _rope_tables(s_len)
    tab_l = _dft_tables(s_len)
    tab_c = _dft_tables(c_len)

    x2 = x.reshape(bsz * s_len, d)
    ctx2 = ctx.reshape(bsz * c_len, d)
    tm = 1024

    for l in range(depth):
        need_ctx = l < depth - 1
        shift, scale, gate = (mod[l, :bsz, i * d:(i + 1) * d].reshape(bsz, 1, d) for i in range(3))
        shift_c, scale_c, gate_c = (mod[l, bsz:bsz + 1, i * d:(i + 1) * d].reshape(1, 1, d) for i in range(3))
        g_pre = norm_pre[l][None]
        g_post = norm_post[l][None]

        def proj(w, tn):
            wb = w.astype(BF16)
            p = _in_proj(x2, g_pre, scale, shift, s_len, wb, BF16, tm, tn)
            pc = _in_proj(ctx2, g_pre, scale_c, shift_c, bsz * c_len, wb, BF16, tm, tn)
            return p, pc

        if l % 2 == 0:
            e = l // 2
            w_in = ev_w_in[e]
            w_ord = jnp.concatenate([w_in[:, 1024:2048], w_in[:, 2048:3072], w_in[:, 3584:4608],
                                     w_in[:, 3072:3584], w_in[:, 0:1024]], axis=1)
            pr, prc = proj(w_ord, 1536)
            cols = {"q": 1, "gb": 2, "k": 12, "v": 13}
            xa_col = 3584 // LANES
            nblk = lru_wa.shape[2]
            wg = jnp.stack([lru_wa[e, 0], lru_wx[e, 0], lru_wa[e, 1], lru_wx[e, 1]], axis=1)
            wg = jnp.transpose(wg, (0, 2, 1, 3)).reshape(nblk, LRU_BLOCK_W, 4 * LRU_BLOCK_W)
            bg = jnp.stack([lru_ba[e, 0], lru_bx[e, 0], lru_ba[e, 1], lru_bx[e, 1]], axis=0)
            bg = jnp.transpose(bg.reshape(4, nblk, LRU_BLOCK_W), (1, 0, 2)).reshape(nblk, 1, 4 * LRU_BLOCK_W)
            wg, bg = (0.5 * wg).astype(BF16), 0.5 * bg
            ya, yac = _lru(pr, prc, xa_col, bsz, lru_conv_w[e], lru_conv_b[e][None], wg, bg, lru_lambda[e])
            yb = _attention(attn_sink[e], pr, prc, bsz, cols, cos_t, sin_t)
            w_out = ev_w_out[e].astype(BF16)
            x2_new = _out_proj(ya, pr, 0, yb, pr, cols["gb"], w_out, x2, g_post, gate, s_len, tm)
            if need_ctx:
                ybc = _ctx_attention(attn_sink[e], prc, bsz, cols)
                ctx2 = _out_proj(yac, prc, 0, ybc, prc, cols["gb"], w_out, ctx2, g_post, gate_c, bsz * c_len, tm)
            x2 = x2_new
        else:
            o = l // 2
            w_in = od_w_in[o]
            w_ord = jnp.concatenate([w_in[:, 3072:8192], w_in[:, 0:3072]], axis=1)
            pr, prc = proj(w_ord, 2048)
            cols = {"q": 8, "k": 16, "v": 24}
            z_col = 5120 // LANES
            width = hy_bias.shape[2]
            nz = (HY_ORDER + 1) * width // LANES
            fargs = (hy_w1[o], hy_b1[o], hy_w2[o], hy_b2[o], hy_w3[o], hy_freq[o])
            khat = _hyena_spectra(s_len, width, *fargs, tab_l)
            zconv = _hy_prep(pr, z_col, nz, bsz, hy_conv_w[o], hy_conv_b[o][None], tab_l[1])
            yh = _hyena_seq(zconv, bsz, khat, hy_bias[o], tab_l)
            dl = jnp.broadcast_to(jnp.transpose(ret_decay_logit[o])[:, :, None], (RET_HEADS, 2, LANES))
            yd, ydc = _retention(dl, pr, prc, bsz, cols)
            w_out = od_w_out[o].astype(BF16)
            x2_new = _out_proj(yh, pr, 0, yd, pr, 4, w_out, x2, g_post, gate, s_len, tm)
            if need_ctx:
                khat_c = _hyena_spectra(c_len, width, *fargs, tab_c)
                zcconv = _hy_prep(prc, z_col, nz, bsz, hy_conv_w[o], hy_conv_b[o][None], tab_c[1])
                yhc = _hyena_seq(zcconv, bsz, khat_c, hy_bias[o], tab_c)
                ctx2 = _out_proj(yhc, prc, 0, ydc, prc, 4, w_out, ctx2, g_post, gate_c, bsz * c_len, tm)
            x2 = x2_new
    return x2.reshape(bsz, s_len, d)
```

```python
import functools
import math

import numpy as np
import jax
import jax.numpy as jnp
from jax import lax
from jax.experimental import pallas as pl
from jax.experimental.pallas import tpu as pltpu

F32 = jnp.float32
BF16 = jnp.bfloat16
HIGHEST = lax.Precision.HIGHEST

EPS = 1e-6
GRID_W = 64
LANES = 128
LRU_BLOCK_W = 128
LRU_CONV = 4
LRU_C = 8.0
LRU_SEG = 16
LRU_POS = 16
LRU_TINY = 1e-30
LRU_HALO = 16
ATT_HEADS = 8
ATT_KV_HEADS = 2
HEAD_DIM = 128
BLOCK = 128
ATT_QSUB = 2
ROPE_BASE = 10000.0
HY_ORDER = 2
HY_EMB = 33
HY_EMB_PAD = 40
HY_DECAY_TARGET = 1e-2
HY_FAST_PCT = 0.3
HY_SLOW_PCT = 1.5
HY_UNROLL = 4
HY_PREP_ROWS = 8192
HY_HALO = 16
HY_PITCH_PAD = 8
RET_HEADS = 8
RET_DK = 128
RET_CHUNK = 256

VMEM_LIMIT = 56 * 1024 * 1024
NEG = -1e30


def _params(sem, vmem=VMEM_LIMIT, **kw):
    return pltpu.CompilerParams(dimension_semantics=sem, vmem_limit_bytes=vmem, **kw)


def _sigmoid(v):
    return 0.5 * (jnp.tanh(0.5 * v) + 1.0)


def _silu(v):
    return v * _sigmoid(v)


def _softplus(v):
    return jnp.maximum(v, 0.0) + jnp.log(1.0 + jnp.exp(-jnp.abs(v)))


def _dot(a, b, **kw):
    return jnp.dot(a, b, preferred_element_type=F32, **kw)


def _dot_nt(a, b):
    return lax.dot_general(a, b, (((1,), (1,)), ((), ())), preferred_element_type=F32)


def _dot_tn(a, b):
    return lax.dot_general(a, b, (((0,), (0,)), ((), ())), preferred_element_type=F32)


def _mod_kernel(c_ref, w_ref, b_ref, o_ref):
    s = _silu(c_ref[...])
    o_ref[0] = _dot(s, w_ref[0], precision=HIGHEST) + b_ref[0]


def _modulation(crows, mod_w, mod_b):
    depth, d, n3 = mod_w.shape
    r = crows.shape[0]
    tn = 1024
    return pl.pallas_call(
        _mod_kernel,
        grid=(depth, n3 // tn),
        in_specs=[pl.BlockSpec((r, d), lambda l, j: (0, 0)),
                  pl.BlockSpec((1, d, tn), lambda l, j: (l, 0, j)),
                  pl.BlockSpec((1, 1, tn), lambda l, j: (l, 0, j))],
        out_specs=pl.BlockSpec((1, r, tn), lambda l, j: (l, 0, j)),
        out_shape=jax.ShapeDtypeStruct((depth, r, n3), F32),
        compiler_params=_params(("parallel", "parallel")),
        name="modulation",
    )(crows, mod_w, mod_b.reshape(depth, 1, n3))


def _rope(v, cos, sin, lane):
    swapped = jnp.where((lane & 63) < 32, pltpu.roll(v, LANES - 32, 1), pltpu.roll(v, 32, 1))
    return v * cos + swapped * sin


def _inproj_kernel(x_ref, g_ref, sc_ref, sh_ref, w_ref, o_ref, h_ref):
    @pl.when(pl.program_id(1) == 0)
    def _():
        x = x_ref[...]
        y = x * lax.rsqrt(jnp.mean(x * x, axis=-1, keepdims=True) + EPS) * g_ref[...]
        h_ref[...] = (y * (1.0 + sc_ref[0]) + sh_ref[0]).astype(BF16)

    o_ref[...] = _dot(h_ref[...], w_ref[...]).astype(o_ref.dtype)


def _in_proj(x2, g, scale, shift, rows_per_group, w, out_dtype, tm, tn):
    m, d = x2.shape
    n = w.shape[1]
    tpg = rows_per_group // tm
    return pl.pallas_call(
        _inproj_kernel,
        grid=(m // tm, n // tn),
        in_specs=[pl.BlockSpec((tm, d), lambda i, j: (i, 0)),
                  pl.BlockSpec((1, d), lambda i, j: (0, 0)),
                  pl.BlockSpec((1, 1, d), lambda i, j: (i // tpg, 0, 0)),
                  pl.BlockSpec((1, 1, d), lambda i, j: (i // tpg, 0, 0)),
                  pl.BlockSpec((d, tn), lambda i, j: (0, j))],
        out_specs=pl.BlockSpec((tm, tn), lambda i, j: (i, j)),
        out_shape=jax.ShapeDtypeStruct((m, n), out_dtype),
        scratch_shapes=[pltpu.VMEM((tm, d), BF16)],
        compiler_params=_params(("parallel", "arbitrary")),
        name="in_proj",
    )(x2, g, scale, shift, w)


def _outproj_kernel(a_ref, ga_ref, b_ref, gb_ref, w_ref, x_ref, g_ref, gate_ref, o_ref):
    wa = w_ref.shape[0] // 2
    a = (a_ref[...].astype(F32) * _silu(ga_ref[...].astype(F32))).astype(BF16)
    b = (b_ref[...].astype(F32) * _silu(gb_ref[...].astype(F32))).astype(BF16)
    y = _dot(a, w_ref[0:wa, :]) + _dot(b, w_ref[wa:, :])
    yn = y * lax.rsqrt(jnp.mean(y * y, axis=-1, keepdims=True) + EPS) * g_ref[...]
    o_ref[...] = x_ref[...] + gate_ref[0] * yn


def _out_proj(a, ga, ga_col, b, gb, gb_col, w, x2, g, gate, rows_per_group, tm):
    m, d = x2.shape
    wa = w.shape[0] // 2
    tpg = rows_per_group // tm
    return pl.pallas_call(
        _outproj_kernel,
        grid=(m // tm,),
        in_specs=[pl.BlockSpec((tm, wa), lambda i: (i, 0)),
                  pl.BlockSpec((tm, wa), lambda i: (i, ga_col)),
                  pl.BlockSpec((tm, wa), lambda i: (i, 0)),
                  pl.BlockSpec((tm, wa), lambda i: (i, gb_col)),
                  pl.BlockSpec(w.shape, lambda i: (0, 0)),
                  pl.BlockSpec((tm, d), lambda i: (i, 0)),
                  pl.BlockSpec((1, d), lambda i: (0, 0)),
                  pl.BlockSpec((1, 1, d), lambda i: (i // tpg, 0, 0))],
        out_specs=pl.BlockSpec((tm, d), lambda i: (i, 0)),
        out_shape=jax.ShapeDtypeStruct((m, d), F32),
        compiler_params=_params(("parallel",)),
        name="out_proj",
    )(a, ga, b, gb, w, x2, g, gate)


def _lru_kernel(xa_ref, xac_ref, cw_ref, cb_ref, wg_ref, bg_ref, lam_ref, y_ref, yc_ref,
                xp_ref, hf_ref, pf_ref, hb_ref, pb_ref):
    nseg = LRU_SEG
    npos = LRU_POS
    c8h = -0.5 * LRU_C * _softplus(-lam_ref[...])
    left = LRU_CONV // 2
    cw = [cw_ref[k:k + 1, :] for k in range(LRU_CONV)]
    cb = cb_ref[...]
    zeros_h = jnp.zeros((LRU_HALO, LANES), F32)
    zeros_s = jnp.zeros((nseg, LANES), F32)
    ones_s = jnp.ones((nseg, LANES), F32)

    def run(src_ref, out_ref, n, c0f, c0b):
        seg = n // nseg
        nblk = seg // npos
        halo = LRU_HALO
        pitch = seg + 2 * halo + 8
        for j in range(nseg):
            lo, hi = j * seg - halo, (j + 1) * seg + halo
            if lo < 0:
                xp_ref[j * pitch:j * pitch + halo, :] = zeros_h
            if hi > n:
                xp_ref[j * pitch + seg + halo:j * pitch + seg + 2 * halo, :] = zeros_h
            lo_c, hi_c = max(lo, 0), min(hi, n)
            xp_ref[j * pitch + (lo_c - lo):j * pitch + (hi_c - lo), :] = src_ref[lo_c:hi_c, :].astype(F32)

        def gather(q):
            return jnp.concatenate([xp_ref[pl.ds(halo + q + c * 8 * pitch, 8, stride=pitch), :]
                                    for c in range(nseg // 8)], axis=0)

        def blk(i):
            return pl.ds(pl.multiple_of(i * npos * nseg, npos * nseg), npos * nseg)

        def pos(v, p):
            return v[p * nseg:(p + 1) * nseg, :]

        def fwd_body(i, carry):
            h, pc = carry
            p0 = i * npos
            xs = [gather(p0 + q - left) for q in range(npos + LRU_CONV - 1)]
            us = []
            for p in range(npos):
                u = cb + cw[0] * xs[p]
                for k in range(1, LRU_CONV):
                    u = u + cw[k] * xs[p + k]
                us.append(u)
            u = jnp.concatenate(us, axis=0)
            g = _dot(u.astype(BF16), wg_ref[0]) + bg_ref[0]
            hu = 0.5 * u
            coef = []
            for d in range(2):
                tr = jnp.tanh(g[:, (2 * d) * LANES:(2 * d + 1) * LANES])
                ti = jnp.tanh(g[:, (2 * d + 1) * LANES:(2 * d + 2) * LANES])
                a = jnp.exp(c8h[d:d + 1, :] * tr + c8h[d:d + 1, :])
                om = 1.0 - a * a
                coef.append((a, om * lax.rsqrt(jnp.maximum(om, LRU_TINY)) * ((ti + 1.0) * hu)))
            (af, bf), (ab, bb) = coef
            hb_ref[blk(i), :] = ab
            pb_ref[blk(i), :] = bb
            hs, ps = [], []
            for p in range(npos):
                a_p = pos(af, p)
                h = a_p * h + pos(bf, p)
                pc = a_p * pc
                hs.append(h)
                ps.append(pc)
            hf_ref[blk(i), :] = jnp.concatenate(hs, axis=0)
            pf_ref[blk(i), :] = jnp.concatenate(ps, axis=0)
            return h, pc

        hef, pef = lax.fori_loop(0, nblk, fwd_body, (zeros_s, ones_s))

        def bwd_body(ii, carry):
            h, pc = carry
            i = nblk - 1 - ii
            ab = hb_ref[blk(i), :]
            bb = pb_ref[blk(i), :]
            hs, ps = [None] * npos, [None] * npos
            for p in reversed(range(npos)):
                a_p = pos(ab, p)
                h = a_p * h + pos(bb, p)
                pc = a_p * pc
                hs[p] = h
                ps[p] = pc
            hb_ref[blk(i), :] = jnp.concatenate(hs, axis=0)
            pb_ref[blk(i), :] = jnp.concatenate(ps, axis=0)
            return h, pc

        heb, peb = lax.fori_loop(0, nblk, bwd_body, (zeros_s, ones_s))

        rows_f, c = [], c0f
        for j in range(nseg):
            rows_f.append(c)
            c = hef[j:j + 1, :] + pef[j:j + 1, :] * c
        final_f = c
        rows_b, c = [None] * nseg, c0b
        for j in reversed(range(nseg)):
            rows_b[j] = c
            c = heb[j:j + 1, :] + peb[j:j + 1, :] * c
        final_b = c
        cf = jnp.concatenate(rows_f, axis=0)
        cbk = jnp.concatenate(rows_b, axis=0)

        opitch = seg + 8

        def out_body(p, _):
            rs = pl.ds(pl.multiple_of(p * nseg, nseg), nseg)
            y = hf_ref[rs, :] + pf_ref[rs, :] * cf + hb_ref[rs, :] + pb_ref[rs, :] * cbk
            for c in range(nseg // 8):
                xp_ref[pl.ds(p + c * 8 * opitch, 8, stride=opitch), :] = y[c * 8:(c + 1) * 8, :]
            return 0
        lax.fori_loop(0, seg, out_body, 0, unroll=8)
        for j in range(nseg):
            out_ref[j * seg:(j + 1) * seg, :] = xp_ref[j * opitch:j * opitch + seg, :].astype(out_ref.dtype)
        return final_f, final_b

    zero = jnp.zeros((1, LANES), F32)
    ff, fb = run(xac_ref, yc_ref, xac_ref.shape[0], zero, zero)
    run(xa_ref, y_ref, xa_ref.shape[0], ff, fb)


def _lru(proj, projc, xa_col, bsz, conv_w, conv_b, wg, bg, lam):
    w = conv_w.shape[1]
    s_len = proj.shape[0] // bsz
    c_len = projc.shape[0] // bsz
    nblk = w // LANES
    seq = lambda n: pl.BlockSpec((n, LANES), lambda b, j: (b, j))
    src = lambda n: pl.BlockSpec((n, LANES), lambda b, j: (b, xa_col + j))
    return pl.pallas_call(
        _lru_kernel,
        grid=(bsz, nblk),
        in_specs=[src(s_len), src(c_len),
                  pl.BlockSpec((LRU_CONV, LANES), lambda b, j: (0, j)),
                  pl.BlockSpec((1, LANES), lambda b, j: (0, j)),
                  pl.BlockSpec((1, LANES, 4 * LANES), lambda b, j: (j, 0, 0)),
                  pl.BlockSpec((1, 1, 4 * LANES), lambda b, j: (j, 0, 0)),
                  pl.BlockSpec((2, LANES), lambda b, j: (0, j))],
        out_specs=[seq(s_len), seq(c_len)],
        out_shape=[jax.ShapeDtypeStruct((proj.shape[0], w), BF16), jax.ShapeDtypeStruct((projc.shape[0], w), BF16)],
        scratch_shapes=[pltpu.VMEM((s_len + (2 * LRU_HALO + 8) * LRU_SEG, LANES), F32)]
        + [pltpu.VMEM((s_len, LANES), F32)] * 4,
        compiler_params=_params(("parallel", "parallel")),
        name="rglru",
    )(proj, projc, conv_w, conv_b, wg, bg, lam)


def _attn_kernel(sink_ref, q_ref, k_ref, v_ref, kc_ref, vc_ref, cos_ref, sin_ref, bias_ref, o_ref):
    nb = pl.num_programs(1) * ATT_QSUB
    group = ATT_HEADS // ATT_KV_HEADS
    scale = HEAD_DIM ** -0.5
    lane = lax.broadcasted_iota(jnp.int32, (BLOCK, LANES), 1)
    ones = jnp.ones((bias_ref.shape[2], HEAD_DIM), BF16)

    def blk(i):
        return pl.ds(pl.multiple_of(i * BLOCK, BLOCK), BLOCK)

    def rot(x, tab):
        return _rope(x.astype(F32), tab[0], tab[1], lane).astype(BF16)

    for sub in range(ATT_QSUB):
        qb = pl.program_id(1) * ATT_QSUB + sub
        qrows = slice(sub * BLOCK, (sub + 1) * BLOCK)
        ip = jnp.maximum(qb - 1, 0)
        inx = jnp.minimum(qb + 1, nb - 1)
        tab_p = (cos_ref[blk(ip), :], sin_ref[blk(ip), :])
        tab_o = (cos_ref[blk(qb), :], sin_ref[blk(qb), :])
        tab_n = (cos_ref[blk(inx), :], sin_ref[blk(inx), :])
        tab_q = (tab_o[0] * scale, tab_o[1] * scale)
        variant = jnp.where(qb > 0, 1, 0) + jnp.where(qb < nb - 1, 2, 0)
        bias = jnp.concatenate([bias_ref[variant]] * group, axis=0)

        for h in range(ATT_KV_HEADS):
            ksl = slice(h * HEAD_DIM, (h + 1) * HEAD_DIM)
            kcat = jnp.concatenate([rot(k_ref[blk(ip), ksl], tab_p), rot(k_ref[blk(qb), ksl], tab_o),
                                    rot(k_ref[blk(inx), ksl], tab_n), kc_ref[:, ksl]], axis=0)
            vcat = jnp.concatenate([v_ref[blk(ip), ksl], v_ref[blk(qb), ksl], v_ref[blk(inx), ksl],
                                    vc_ref[:, ksl]], axis=0)
            vext = jnp.concatenate([vcat, ones], axis=1)
            heads = [h * group + g for g in range(group)]
            q4 = jnp.concatenate([rot(q_ref[qrows, hh * HEAD_DIM:(hh + 1) * HEAD_DIM], tab_q) for hh in heads],
                                 axis=0)
            sk = jnp.concatenate([jnp.full((BLOCK, 1), sink_ref[hh], F32) for hh in heads], axis=0)
            s = _dot_nt(q4, kcat) + bias
            m = jnp.maximum(jnp.max(s, axis=-1, keepdims=True), sk)
            p = jnp.exp((s - m).astype(BF16))
            oe = _dot(p, vext)
            o = oe[:, 0:HEAD_DIM] / (jnp.exp(sk - m) + oe[:, HEAD_DIM:HEAD_DIM + 1])
            for g, hh in enumerate(heads):
                o_ref[qrows, hh * HEAD_DIM:(hh + 1) * HEAD_DIM] = o[g * BLOCK:(g + 1) * BLOCK, :].astype(o_ref.dtype)


def _attention(sink, proj, projc, bsz, cols, cos_t, sin_t):
    s_len = proj.shape[0] // bsz
    c_len = projc.shape[0] // bsz
    nb = s_len // BLOCK
    aw = ATT_HEADS * HEAD_DIM
    kw = ATT_KV_HEADS * HEAD_DIM
    qi = np.arange(BLOCK)[:, None]
    kj = np.arange(BLOCK)[None, :]
    variants = []
    for v in range(4):
        prev = np.where((kj >= qi) & bool(v & 1), 0.0, NEG)
        nxt = np.where((kj <= qi) & bool(v & 2), 0.0, NEG)
        variants.append(np.concatenate([prev, np.zeros((BLOCK, BLOCK)), nxt, np.zeros((BLOCK, c_len))], axis=1))
    bias = jnp.asarray(np.stack(variants), F32)
    ncol = 3 * BLOCK + c_len
    nstep = nb // ATT_QSUB
    tq = ATT_QSUB * BLOCK

    return pl.pallas_call(
        _attn_kernel,
        grid=(bsz, nstep),
        in_specs=[pl.BlockSpec(memory_space=pltpu.SMEM),
                  pl.BlockSpec((tq, aw), lambda b, i: (b * nstep + i, cols["q"])),
                  pl.BlockSpec((s_len, kw), lambda b, i: (b, cols["k"])),
                  pl.BlockSpec((s_len, kw), lambda b, i: (b, cols["v"])),
                  pl.BlockSpec((c_len, kw), lambda b, i: (b, cols["k"])),
                  pl.BlockSpec((c_len, kw), lambda b, i: (b, cols["v"])),
                  pl.BlockSpec((s_len, LANES), lambda b, i: (0, 0)),
                  pl.BlockSpec((s_len, LANES), lambda b, i: (0, 0)),
                  pl.BlockSpec((4, BLOCK, ncol), lambda b, i: (0, 0, 0))],
        out_specs=pl.BlockSpec((tq, aw), lambda b, i: (b * nstep + i, 0)),
        out_shape=jax.ShapeDtypeStruct((proj.shape[0], aw), BF16),
        compiler_params=_params(("parallel", "arbitrary")),
        name="window_attention",
    )(sink, proj, proj, proj, projc, projc, cos_t, sin_t, bias)


def _ctx_attn_kernel(sink_ref, q_ref, k_ref, v_ref, o_ref):
    group = ATT_HEADS // ATT_KV_HEADS
    scale = HEAD_DIM ** -0.5
    n = q_ref.shape[0]
    for h in range(ATT_KV_HEADS):
        ksl = slice(h * HEAD_DIM, (h + 1) * HEAD_DIM)
        qs, sinks = [], []
        for g in range(group):
            hh = h * group + g
            qs.append((q_ref[:, hh * HEAD_DIM:(hh + 1) * HEAD_DIM].astype(F32) * scale).astype(BF16))
            sinks.append(jnp.full((n, 1), sink_ref[hh], F32))
        q4 = jnp.concatenate(qs, axis=0)
        sk = jnp.concatenate(sinks, axis=0)
        s = _dot_nt(q4, k_ref[:, ksl])
        m = jnp.maximum(jnp.max(s, axis=-1, keepdims=True), sk)
        p = jnp.exp(s - m)
        denom = jnp.exp(sk - m) + jnp.sum(p, axis=-1, keepdims=True)
        o = _dot(p.astype(BF16), v_ref[:, ksl]) / denom
        for g in range(group):
            hh = h * group + g
            hs = slice(hh * HEAD_DIM, (hh + 1) * HEAD_DIM)
            o_ref[:, hs] = o[g * n:(g + 1) * n, :].astype(o_ref.dtype)


def _ctx_attention(sink, projc, bsz, cols):
    c_len = projc.shape[0] // bsz
    aw = ATT_HEADS * HEAD_DIM
    kw = ATT_KV_HEADS * HEAD_DIM
    return pl.pallas_call(
        _ctx_attn_kernel,
        grid=(bsz,),
        in_specs=[pl.BlockSpec(memory_space=pltpu.SMEM),
                  pl.BlockSpec((c_len, aw), lambda b: (b, cols["q"])),
                  pl.BlockSpec((c_len, kw), lambda b: (b, cols["k"])),
                  pl.BlockSpec((c_len, kw), lambda b: (b, cols["v"]))],
        out_specs=pl.BlockSpec((c_len, aw), lambda b: (b, 0)),
        out_shape=jax.ShapeDtypeStruct((projc.shape[0], aw), BF16),
        compiler_params=_params(("parallel",)),
        name="context_attention",
    )(sink, projc, projc, projc)


def _ret_kernel(dl_ref, q_ref, k_ref, v_ref, qc_ref, kc_ref, vc_ref, o_ref, oc_ref,
                ot_ref, att_ref, u_ref, st_ref, vt_ref, dm_ref):
    c = RET_CHUNK
    s_len = q_ref.shape[0]
    c_len = qc_ref.shape[0]
    lg = -_softplus(-dl_ref[0])
    lgf, lgb = lg[0:1, :], lg[1:2, :]
    wide = lambda t: jnp.concatenate([t] * (c // LANES), axis=1)
    kj = lax.broadcasted_iota(jnp.int32, (c, c), 0)
    qi = lax.broadcasted_iota(jnp.int32, (c, c), 1)
    diff = (qi - kj).astype(F32)
    dm_ref[...] = jnp.where(qi >= kj, jnp.exp(jnp.maximum(diff, 0.0) * wide(lgf)),
                            jnp.exp(jnp.maximum(-diff, 0.0) * wide(lgb)))
    idx = lax.broadcasted_iota(jnp.int32, (c, LANES), 0).astype(F32)
    qdec_f = jnp.exp((idx + 1.0) * lgf)
    kdec_f = jnp.exp((c - 1.0 - idx) * lgf)
    qdec_b = jnp.exp((c - idx) * lgb)
    kdec_b = jnp.exp(idx * lgb)
    cdec_f = jnp.exp(c * lgf)
    cdec_b = jnp.exp(c * lgb)

    def chunk(j):
        return pl.ds(pl.multiple_of(j * c, c), c)

    def transpose_v(vr, n):
        def body(j, _):
            vt_ref[:, chunk(j)] = vr[chunk(j), :].astype(F32).T.astype(BF16)
            return 0
        lax.fori_loop(0, n // c, body, 0, unroll=min(8, n // c))

    def scores(qr, kr, n):
        def body(j, _):
            att_ref[j] = (_dot_nt(kr[chunk(j), :], qr[chunk(j), :]) * dm_ref[...]).astype(BF16)
            return 0
        lax.fori_loop(0, n // c, body, 0, unroll=min(4, n // c))

    def intra(kr, n):
        def body(j, _):
            vt = vt_ref[:, chunk(j)]
            ot_ref[:, chunk(j)] = _dot(vt, att_ref[j])
            kf = kr[chunk(j), :].astype(F32)
            kcat = jnp.concatenate([(kf * kdec_f).astype(BF16), (kf * kdec_b).astype(BF16)], axis=1)
            u_ref[j] = _dot(vt, kcat)
            return 0
        lax.fori_loop(0, n // c, body, 0, unroll=min(4, n // c))

    def states(n, sf, sb):
        nch = n // c

        def fbody(j, s):
            st_ref[j, :, 0:RET_DK] = s.astype(BF16)
            return s * cdec_f + u_ref[j, :, 0:RET_DK]
        sf = lax.fori_loop(0, nch, fbody, sf)

        def bbody(jj, s):
            j = nch - 1 - jj
            st_ref[j, :, RET_DK:] = s.astype(BF16)
            return s * cdec_b + u_ref[j, :, RET_DK:]
        sb = lax.fori_loop(0, nch, bbody, sb)
        return sf, sb

    def cross(qr, outr, n):
        def body(j, _):
            qf = qr[chunk(j), :].astype(F32)
            qcat = jnp.concatenate([(qf * qdec_f).astype(BF16), (qf * qdec_b).astype(BF16)], axis=1)
            ot = ot_ref[:, chunk(j)] + _dot_nt(st_ref[j], qcat)
            o = ot.T * (RET_DK ** -0.5)
            outr[chunk(j), :] = (o * lax.rsqrt(jnp.mean(o * o, axis=-1, keepdims=True) + EPS)).astype(outr.dtype)
            return 0
        lax.fori_loop(0, n // c, body, 0, unroll=min(4, n // c))

    def run(qr, kr, vr, outr, n, sf, sb):
        transpose_v(vr, n)
        scores(qr, kr, n)
        intra(kr, n)
        finals = states(n, sf, sb)
        cross(qr, outr, n)
        return finals

    zero = jnp.zeros((LANES, RET_DK), F32)
    sf, sb = run(qc_ref, kc_ref, vc_ref, oc_ref, c_len, zero, zero)
    run(q_ref, k_ref, v_ref, o_ref, s_len, sf, sb)


def _retention(dl, proj, projc, bsz, cols):
    s_len = proj.shape[0] // bsz
    c_len = projc.shape[0] // bsz
    hb = lambda n, col: pl.BlockSpec((n, LANES), lambda b, h: (b, col + h))
    return pl.pallas_call(
        _ret_kernel,
        grid=(bsz, RET_HEADS),
        in_specs=[pl.BlockSpec((1, 2, LANES), lambda b, h: (h, 0, 0)),
                  hb(s_len, cols["q"]), hb(s_len, cols["k"]), hb(s_len, cols["v"]),
                  hb(c_len, cols["q"]), hb(c_len, cols["k"]), hb(c_len, cols["v"])],
        out_specs=[pl.BlockSpec((s_len, LANES), lambda b, h: (b, h)),
                   pl.BlockSpec((c_len, LANES), lambda b, h: (b, h))],
        out_shape=[jax.ShapeDtypeStruct((proj.shape[0], RET_HEADS * LANES), BF16),
                   jax.ShapeDtypeStruct((projc.shape[0], RET_HEADS * LANES), BF16)],
        scratch_shapes=[pltpu.VMEM((LANES, s_len), F32),
                        pltpu.VMEM((s_len // RET_CHUNK, RET_CHUNK, RET_CHUNK), BF16),
                        pltpu.VMEM((s_len // RET_CHUNK, LANES, 2 * RET_DK), F32),
                        pltpu.VMEM((s_len // RET_CHUNK, LANES, 2 * RET_DK), BF16),
                        pltpu.VMEM((LANES, s_len), BF16),
                        pltpu.VMEM((RET_CHUNK, RET_CHUNK), F32)],
        compiler_params=_params(("parallel", "parallel")),
        name="retention",
    )(dl, proj, proj, proj, projc, projc, projc)


def _hy_hid_kernel(z_ref, w1_ref, b1_ref, w2_ref, b2_ref, f_ref, o_ref):
    f = f_ref[...]
    h = jnp.sin(f * (_dot(z_ref[...], w1_ref[...], precision=HIGHEST) + b1_ref[...]))
    o_ref[...] = jnp.sin(f * (_dot(h, w2_ref[...], precision=HIGHEST) + b2_ref[...]))


def _hy_hidden(zfull, w1p, b1, w2, b2, freq):
    n, e = zfull.shape
    fd = w2.shape[0]
    tr = min(n, 1024)
    full = lambda shp: pl.BlockSpec(shp, lambda i: (0, 0))
    return pl.pallas_call(
        _hy_hid_kernel,
        grid=(n // tr,),
        in_specs=[pl.BlockSpec((tr, e), lambda i: (i, 0)), full((e, fd)), full((1, fd)), full((fd, fd)),
                  full((1, fd)), full((1, fd))],
        out_specs=pl.BlockSpec((tr, fd), lambda i: (i, 0)),
        out_shape=jax.ShapeDtypeStruct((n, fd), F32),
        compiler_params=_params(("parallel",)),
        name="hyena_filter_mlp",
    )(zfull, w1p, b1, w2, b2, freq)


def _hy_filt_kernel(hid_ref, w3f_ref, w3b_ref, tn_ref, dl_ref, o_ref):
    half = hid_ref.shape[0] // 2
    decay = jnp.exp(-tn_ref[...] * jnp.abs(dl_ref[...]))
    top = _dot(hid_ref[0:half, :], w3f_ref[...], precision=HIGHEST)
    bot = _dot(hid_ref[half:, :], w3b_ref[...], precision=HIGHEST)
    row = lax.broadcasted_iota(jnp.int32, bot.shape, 0)
    bot = jnp.where(row == 0, 0.0, bot)
    o_ref[0, 0:half, :] = top * decay[0:half, :]
    o_ref[0, half:, :] = bot * decay[half:, :]


def _hy_filters(hid, w3, tn_full, deltas):
    n, fd = hid.shape
    wch = deltas.shape[1]
    nsl = wch // LANES
    return pl.pallas_call(
        _hy_filt_kernel,
        grid=(HY_ORDER, nsl),
        in_specs=[pl.BlockSpec((n, fd), lambda o, j: (0, 0)),
                  pl.BlockSpec((fd, LANES), lambda o, j: (0, o * 2 * nsl + j)),
                  pl.BlockSpec((fd, LANES), lambda o, j: (0, o * 2 * nsl + nsl + j)),
                  pl.BlockSpec((n, LANES), lambda o, j: (0, 0)),
                  pl.BlockSpec((1, LANES), lambda o, j: (0, j))],
        out_specs=pl.BlockSpec((1, n, LANES), lambda o, j: (o, 0, j)),
        out_shape=jax.ShapeDtypeStruct((HY_ORDER, n, wch), F32),
        compiler_params=_params(("parallel", "parallel")),
        name="hyena_filter",
    )(hid, w3, w3, tn_full, deltas)


def _hy_kfft_kernel(kern_ref, f1_ref, f2_ref, o_ref, g_ref, *, n1, n2):
    pg = 2 * n1 + HY_PITCH_PAD

    def stage1(i2, _):
        x = kern_ref[pl.ds(i2, n1, stride=n2), :].astype(BF16)
        g_ref[pl.ds(pl.multiple_of(i2 * pg, 8), 2 * n1), :] = _dot(f1_ref[i2], x)
        return 0
    lax.fori_loop(0, n2, stage1, 0, unroll=HY_UNROLL)

    def stage2(k1, _):
        x = jnp.concatenate([g_ref[pl.ds(k1, n2, stride=pg), :],
                             g_ref[pl.ds(n1 + k1, n2, stride=pg), :]], axis=0).astype(BF16)
        o_ref[pl.ds(pl.multiple_of(k1 * 2 * n2, 2 * n2), 2 * n2), :] = _dot(f2_ref[...], x)
        return 0
    lax.fori_loop(0, n1, stage2, 0, unroll=HY_UNROLL // 2)


def _hy_kfft(kern, f1k, f2, n1, n2):
    orders, n, wch = kern.shape
    nsl = wch // LANES
    return pl.pallas_call(
        functools.partial(_hy_kfft_kernel, n1=n1, n2=n2),
        grid=(orders, nsl),
        in_specs=[pl.BlockSpec((None, n, LANES), lambda o, j: (o, 0, j)),
                  pl.BlockSpec(f1k.shape, lambda o, j: (0, 0, 0)),
                  pl.BlockSpec(f2.shape, lambda o, j: (0, 0))],
        out_specs=pl.BlockSpec((None, None, 2 * n, LANES), lambda o, j: (o, j, 0, 0)),
        out_shape=jax.ShapeDtypeStruct((orders, nsl, 2 * n, LANES), F32),
        scratch_shapes=[pltpu.VMEM((n2 * (2 * n1 + HY_PITCH_PAD), LANES), F32)],
        compiler_params=_params(("parallel", "parallel")),
        name="hyena_filter_fft",
    )(kern, f1k, f2)


def _hy_prep_kernel(z_ref, w_ref, b_ref, o_ref, zp_ref, *, n2, nseq):
    l_len = z_ref.shape[0] // nseq
    h1 = l_len // n2
    halo = HY_HALO
    pitch = n2 + 2 * halo + 8
    zeros_h = jnp.zeros((halo, LANES), F32)
    for s in range(nseq):
        for i1 in range(h1):
            base = (s * h1 + i1) * pitch
            lo, hi = i1 * n2 - halo, (i1 + 1) * n2 + halo
            if lo < 0:
                zp_ref[base:base + halo, :] = zeros_h
            if hi > l_len:
                zp_ref[base + n2 + halo:base + n2 + 2 * halo, :] = zeros_h
            lo_c, hi_c = max(lo, 0), min(hi, l_len)
            zp_ref[base + (lo_c - lo):base + (hi_c - lo), :] = (
                z_ref[s * l_len + lo_c:s * l_len + hi_c, :].astype(F32))
    w0, w1, w2, b = w_ref[0:1, :], w_ref[1:2, :], w_ref[2:3, :], b_ref[...]

    def tap(j):
        return zp_ref[pl.ds(halo + j, nseq * h1, stride=pitch), :]

    def body(i2, carry):
        zm, z0 = carry
        zn = tap(i2 + 1)
        val = (b + w0 * zm + w1 * z0 + w2 * zn).astype(o_ref.dtype)
        for s in range(nseq):
            o_ref[pl.ds(pl.multiple_of(s * l_len + i2 * h1, h1), h1), :] = val[s * h1:(s + 1) * h1, :]
        return z0, zn
    lax.fori_loop(0, n2, body, (tap(-1), tap(0)), unroll=HY_UNROLL if nseq <= 2 else 1)


def _hy_stage_dtype(l_len, n2):
    return BF16 if (l_len // n2) % 16 == 0 else F32


def _hy_prep(proj, z_col, nz, bsz, conv_w, conv_b, n2):
    l_len = proj.shape[0] // bsz
    h1 = l_len // n2
    nseq = max(1, min(bsz, HY_PREP_ROWS // l_len))
    while bsz % nseq:
        nseq -= 1
    return pl.pallas_call(
        functools.partial(_hy_prep_kernel, n2=n2, nseq=nseq),
        grid=(bsz // nseq, nz),
        in_specs=[pl.BlockSpec((nseq * l_len, LANES), lambda b, j: (b, z_col + j)),
                  pl.BlockSpec((3, LANES), lambda b, j: (0, j)),
                  pl.BlockSpec((1, LANES), lambda b, j: (0, j))],
        out_specs=pl.BlockSpec((nseq * l_len, LANES), lambda b, j: (b, j)),
        out_shape=jax.ShapeDtypeStruct((proj.shape[0], nz * LANES), _hy_stage_dtype(l_len, n2)),
        scratch_shapes=[pltpu.VMEM((nseq * h1 * (n2 + 2 * HY_HALO + 8), LANES), F32)],
        compiler_params=_params(("parallel", "parallel")),
        name="hyena_short_conv",
    )(proj, conv_w, conv_b)


def _pack2(re, im):
    r = lax.bitcast_convert_type(re, jnp.uint32) + jnp.uint32(0x8000)
    i = lax.bitcast_convert_type(im, jnp.uint32) + jnp.uint32(0x8000)
    return (r & jnp.uint32(0xFFFF0000)) | (i >> 16)


def _unpack2(w):
    re = lax.bitcast_convert_type(w & jnp.uint32(0xFFFF0000), F32)
    im = lax.bitcast_convert_type(w << 16, F32)
    return jnp.concatenate([re, im], axis=0).astype(BF16)


def _hy_conv_kernel(y_ref, x_ref, kh_ref, bias_ref, f1_ref, f2_ref, f2i_ref, f1i_ref, o_ref, g_ref,
                    *, n1, n2, natural_out):
    h1 = n1 // 2
    l_len = h1 * n2
    pg = n1 + HY_PITCH_PAD
    bias = bias_ref[...]
    unroll = HY_UNROLL

    def rows(seq, i2):
        return pl.ds(pl.multiple_of(seq * l_len + i2 * h1, h1), h1)

    def lanes2(a, b):
        return jnp.concatenate([a, b], axis=1)

    def stage1(i2, _):
        x = lanes2(jnp.concatenate([y_ref[rows(0, i2), :], y_ref[rows(1, i2), :]], axis=0),
                   jnp.concatenate([y_ref[rows(2, i2), :], y_ref[rows(3, i2), :]], axis=0)).astype(BF16)
        a = _dot(f1_ref[i2], x)
        w = _pack2(a[0:n1, :], a[n1:, :])
        blk = pl.ds(pl.multiple_of(i2 * pg, 8), n1)
        g_ref[0, blk, :] = w[:, 0:LANES]
        g_ref[1, blk, :] = w[:, LANES:]
        return 0
    lax.fori_loop(0, n2, stage1, 0, unroll=2 * unroll)

    def stage23(t, _):
        outs = []
        for u in range(unroll):
            k1 = t * unroll + u
            col = pl.ds(k1, n2, stride=pg)
            x = lanes2(_unpack2(g_ref[0, col, :]), _unpack2(g_ref[1, col, :]))
            yh = _dot(f2_ref[...], x)
            base = pl.multiple_of(k1 * 2 * n2, 2 * n2)
            kr = kh_ref[pl.ds(base, n2), :]
            ki = kh_ref[pl.ds(base + n2, n2), :]
            kr, ki = lanes2(kr, kr), lanes2(ki, ki)
            yr, yi = yh[0:n2, :], yh[n2:, :]
            z = jnp.concatenate([yr * kr - yi * ki, yr * ki + yi * kr], axis=0).astype(BF16)
            c = _dot(f2i_ref[...], z)
            outs.append((col, _pack2(c[0:n2, :], c[n2:, :])))
        for col, w in outs:
            g_ref[0, col, :] = w[:, 0:LANES]
            g_ref[1, col, :] = w[:, LANES:]
        return 0
    lax.fori_loop(0, n1 // unroll, stage23, 0)

    def stage4(i2, _):
        blk = pl.ds(pl.multiple_of(i2 * pg, 8), n1)
        x = lanes2(_unpack2(g_ref[0, blk, :]), _unpack2(g_ref[1, blk, :]))
        yc = _dot(f1i_ref[i2], x)
        for seq in range(4):
            pair, half = seq // 2, seq % 2
            sl = rows(seq, i2)
            conv = yc[half * h1:(half + 1) * h1, pair * LANES:(pair + 1) * LANES]
            val = x_ref[sl, :].astype(F32) * (conv + bias * y_ref[sl, :].astype(F32))
            if natural_out:
                o_ref[pl.ds(seq * l_len + i2, h1, stride=n2), :] = val
            else:
                o_ref[sl, :] = val.astype(o_ref.dtype)
        return 0
    lax.fori_loop(0, n2, stage4, 0, unroll=unroll)


def _hy_conv(ysrc, ycol, xsrc, xcol, khat, order, bias, mats, bsz, n1, n2, natural_out):
    l_len = ysrc.shape[0] // bsz
    nsl = khat.shape[1]
    f1, f2, f2i, f1i = mats
    once = pl.Buffered(1)
    cst2 = lambda a: pl.BlockSpec(a.shape, lambda j, p: (0, 0), pipeline_mode=once)
    cst3 = lambda a: pl.BlockSpec(a.shape, lambda j, p: (0, 0, 0), pipeline_mode=once)
    return pl.pallas_call(
        functools.partial(_hy_conv_kernel, n1=n1, n2=n2, natural_out=natural_out),
        grid=(nsl, bsz // 4),
        in_specs=[pl.BlockSpec((4 * l_len, LANES), lambda j, p: (p, ycol + j)),
                  pl.BlockSpec((4 * l_len, LANES), lambda j, p: (p, xcol + j)),
                  pl.BlockSpec((None, None, khat.shape[2], LANES), lambda j, p: (order, j, 0, 0),
                               pipeline_mode=once),
                  pl.BlockSpec((1, LANES), lambda j, p: (0, j)),
                  cst3(f1), cst2(f2), cst2(f2i), cst3(f1i)],
        out_specs=pl.BlockSpec((4 * l_len, LANES), lambda j, p: (p, j)),
        out_shape=jax.ShapeDtypeStruct((ysrc.shape[0], nsl * LANES),
                                       F32 if natural_out else _hy_stage_dtype(l_len, n2)),
        scratch_shapes=[pltpu.VMEM((2, n2 * (n1 + HY_PITCH_PAD), LANES), jnp.uint32)],
        compiler_params=_params(("parallel", "arbitrary")),
        name="hyena_long_conv",
    )(ysrc, xsrc, khat, bias[order][None], f1, f2, f2i, f1i)


def _dft_tables(l_len):
    n = 2 * l_len
    n2 = 128 if l_len >= 1024 else 32
    n1 = n // n2
    h1 = n1 // 2
    i1 = np.arange(n1)[None, None, :]
    k1 = np.arange(n1)[None, :, None]
    i2 = np.arange(n2)[:, None, None]
    ph = 2 * np.pi * (i1 * k1 / n1 + i2 * k1 / n)
    c, s = np.cos(ph), np.sin(ph)
    ch, sh = c[:, :, :h1], s[:, :, :h1]
    f1 = np.concatenate([np.concatenate([ch, sh], 2), np.concatenate([-sh, ch], 2)], 1)
    f1k = np.concatenate([c, -s], 1)
    ct, st = np.swapaxes(ch, 1, 2), np.swapaxes(sh, 1, 2)
    f1i = np.concatenate([np.concatenate([ct, -st], 2), np.concatenate([st, ct], 2)], 1) / n
    a = np.arange(n2)
    ph2 = 2 * np.pi * np.outer(a, a) / n2
    c2, s2 = np.cos(ph2), np.sin(ph2)
    f2 = np.block([[c2, s2], [-s2, c2]])
    f2i = np.block([[c2, -s2], [s2, c2]])
    bf = lambda m: jnp.asarray(m, dtype=F32).astype(BF16)
    return n1, n2, (bf(f1), bf(f2), bf(f2i), bf(f1i)), bf(f1k)


def _filter_positions(l_len, width):
    lag = np.concatenate([np.arange(l_len), l_len - np.arange(l_len)]).astype(np.float64)
    t = lag / (l_len - 1)
    bands = (HY_EMB - 1) // 2
    w = 2.0 * np.pi * lag / l_len
    f = np.linspace(1e-4, bands - 1, bands)[None]
    z = np.concatenate([t[:, None], np.cos(f * w[:, None]), -np.sin(f * w[:, None])], axis=-1)
    z = np.pad(z, ((0, 0), (0, HY_EMB_PAD - HY_EMB)))
    tn = np.repeat(t[:, None], LANES, axis=1)
    max_decay = math.log(HY_DECAY_TARGET) / HY_FAST_PCT
    min_decay = math.log(HY_DECAY_TARGET) / HY_SLOW_PCT
    deltas = np.linspace(min_decay, max_decay, width)[None]
    return jnp.asarray(z, F32), jnp.asarray(tn, F32), jnp.asarray(deltas, F32)


def _rope_tables(seq):
    n_rows = seq // GRID_W
    row = np.repeat(np.arange(n_rows), GRID_W).astype(np.float64)
    col = np.tile(np.arange(GRID_W), n_rows).astype(np.float64)
    half = HEAD_DIM // 2
    inv = ROPE_BASE ** (-np.arange(0, half, 2, dtype=np.float64) / half)
    ar, ac = row[:, None] * inv, col[:, None] * inv
    cos = np.concatenate([np.cos(ar), np.cos(ar), np.cos(ac), np.cos(ac)], axis=1)
    sin = np.concatenate([-np.sin(ar), np.sin(ar), -np.sin(ac), np.sin(ac)], axis=1)
    return jnp.asarray(cos, F32), jnp.asarray(sin, F32)


def _hyena_spectra(l_len, width, w1, b1, w2, b2, w3, freq, tables):
    n1, n2, mats, f1k = tables
    zfull, tn_full, deltas = _filter_positions(l_len, width)
    w1p = jnp.pad(w1, ((0, HY_EMB_PAD - HY_EMB), (0, 0)))
    hid = _hy_hidden(zfull, w1p, b1[None], w2, b2[None], freq[None])
    kern = _hy_filters(hid, w3, tn_full, deltas)
    return _hy_kfft(kern, f1k, mats[1], n1, n2)


def _hyena_seq(zc, bsz, khat, bias, tables):
    n1, n2, mats, _ = tables
    nsl = khat.shape[1]
    y1 = _hy_conv(zc, 0, zc, nsl, khat, 0, bias, mats, bsz, n1, n2, False)
    return _hy_conv(y1, 0, zc, 2 * nsl, khat, 1, bias, mats, bsz, n1, n2, True)


def kernel(x, c, ctx, c_ctx, mod_w, mod_b, norm_pre, norm_post, ev_w_in, ev_w_out, lru_conv_w, lru_conv_b, lru_wa, lru_ba, lru_wx, lru_bx, lru_lambda, attn_sink, od_w_in, od_w_out, hy_conv_w, hy_conv_b, hy_w1, hy_b1, hy_w2, hy_b2, hy_w3, hy_freq, hy_bias, ret_decay_logit):
    bsz, s_len, d = x.shape
    c_len = ctx.shape[1]
    depth = mod_w.shape[0]
    assert bsz % 4 == 0 and bsz <= 16 and s_len % 1024 == 0 and c_len % 256 == 0 and c_len <= s_len

    crows = jnp.concatenate([c, c_ctx[None], jnp.zeros((24 - bsz - 1, d), F32)], axis=0)
    mod = _modulation(crows, mod_w, mod_b)

    cos_t, sin_t = _rope_tables(s_len)
    tab_l = _dft_tables(s_len)
    tab_c = _dft_tables(c_len)

    x2 = x.reshape(bsz * s_len, d)
    ctx2 = ctx.reshape(bsz * c_len, d)
    tm = 1024

    for l in range(depth):
        need_ctx = l < depth - 1
        shift, scale, gate = (mod[l, :bsz, i * d:(i + 1) * d].reshape(bsz, 1, d) for i in range(3))
        shift_c, scale_c, gate_c = (mod[l, bsz:bsz + 1, i * d:(i + 1) * d].reshape(1, 1, d) for i in range(3))
        g_pre = norm_pre[l][None]
        g_post = norm_post[l][None]

        def proj(w, tn):
            wb = w.astype(BF16)
            p = _in_proj(x2, g_pre, scale, shift, s_len, wb, BF16, tm, tn)
            pc = _in_proj(ctx2, g_pre, scale_c, shift_c, bsz * c_len, wb, BF16, tm, tn)
            return p, pc

        if l % 2 == 0:
            e = l // 2
            w_in = ev_w_in[e]
            w_ord = jnp.concatenate([w_in[:, 1024:2048], w_in[:, 2048:3072], w_in[:, 3584:4608],
                                     w_in[:, 3072:3584], w_in[:, 0:1024]], axis=1)
            pr, prc = proj(w_ord, 1536)
            cols = {"q": 1, "gb": 2, "k": 12, "v": 13}
            xa_col = 3584 // LANES
            nblk = lru_wa.shape[2]
            wg = jnp.stack([lru_wa[e, 0], lru_wx[e, 0], lru_wa[e, 1], lru_wx[e, 1]], axis=1)
            wg = jnp.transpose(wg, (0, 2, 1, 3)).reshape(nblk, LRU_BLOCK_W, 4 * LRU_BLOCK_W)
            bg = jnp.stack([lru_ba[e, 0], lru_bx[e, 0], lru_ba[e, 1], lru_bx[e, 1]], axis=0)
            bg = jnp.transpose(bg.reshape(4, nblk, LRU_BLOCK_W), (1, 0, 2)).reshape(nblk, 1, 4 * LRU_BLOCK_W)
            wg, bg = (0.5 * wg).astype(BF16), 0.5 * bg
            ya, yac = _lru(pr, prc, xa_col, bsz, lru_conv_w[e], lru_conv_b[e][None], wg, bg, lru_lambda[e])
            yb = _attention(attn_sink[e], pr, prc, bsz, cols, cos_t, sin_t)
            w_out = ev_w_out[e].astype(BF16)
            x2_new = _out_proj(ya, pr, 0, yb, pr, cols["gb"], w_out, x2, g_post, gate, s_len, tm)
            if need_ctx:
                ybc = _ctx_attention(attn_sink[e], prc, bsz, cols)
                ctx2 = _out_proj(yac, prc, 0, ybc, prc, cols["gb"], w_out, ctx2, g_post, gate_c, bsz * c_len, tm)
            x2 = x2_new
        else:
            o = l // 2
            w_in = od_w_in[o]
            w_ord = jnp.concatenate([w_in[:, 3072:8192], w_in[:, 0:3072]], axis=1)
            pr, prc = proj(w_ord, 2048)
            cols = {"q": 8, "k": 16, "v": 24}
            z_col = 5120 // LANES
            width = hy_bias.shape[2]
            nz = (HY_ORDER + 1) * width // LANES
            fargs = (hy_w1[o], hy_b1[o], hy_w2[o], hy_b2[o], hy_w3[o], hy_freq[o])
            khat = _hyena_spectra(s_len, width, *fargs, tab_l)
            zconv = _hy_prep(pr, z_col, nz, bsz, hy_conv_w[o], hy_conv_b[o][None], tab_l[1])
            yh = _hyena_seq(zconv, bsz, khat, hy_bias[o], tab_l)
            dl = jnp.broadcast_to(jnp.transpose(ret_decay_logit[o])[:, :, None], (RET_HEADS, 2, LANES))
            yd, ydc = _retention(dl, pr, prc, bsz, cols)
            w_out = od_w_out[o].astype(BF16)
            x2_new = _out_proj(yh, pr, 0, yd, pr, 4, w_out, x2, g_post, gate, s_len, tm)
            if need_ctx:
                khat_c = _hyena_spectra(c_len, width, *fargs, tab_c)
                zcconv = _hy_prep(prc, z_col, nz, bsz, hy_conv_w[o], hy_conv_b[o][None], tab_c[1])
                yhc = _hyena_seq(zcconv, bsz, khat_c, hy_bias[o], tab_c)
                ctx2 = _out_proj(yhc, prc, 0, ydc, prc, 4, w_out, ctx2, g_post, gate_c, bsz * c_len, tm)
            x2 = x2_new
    return x2.reshape(bsz, s_len, d)
```

```python
import functools
import math

import numpy as np
import jax
import jax.numpy as jnp
from jax import lax
from jax.experimental import pallas as pl
from jax.experimental.pallas import tpu as pltpu

F32 = jnp.float32
BF16 = jnp.bfloat16
HIGHEST = lax.Precision.HIGHEST

EPS = 1e-6
GRID_W = 64
LANES = 128
LRU_BLOCK_W = 128
LRU_CONV = 4
LRU_C = 8.0
LRU_SEG = 16
LRU_POS = 16
LRU_TINY = 1e-30
LRU_HALO = 16
ATT_HEADS = 8
ATT_KV_HEADS = 2
HEAD_DIM = 128
BLOCK = 128
ATT_QSUB = 4
ROPE_BASE = 10000.0
HY_ORDER = 2
HY_EMB = 33
HY_EMB_PAD = 40
HY_DECAY_TARGET = 1e-2
HY_FAST_PCT = 0.3
HY_SLOW_PCT = 1.5
HY_UNROLL = 4
HY_PREP_ROWS = 16384
HY_HALO = 16
HY_PITCH_PAD = 8
RET_HEADS = 8
RET_DK = 128
RET_CHUNK = 256

V7X_VMEM_BYTES = 64 * 1024 * 1024
VMEM_LIMIT = V7X_VMEM_BYTES - 8 * 1024 * 1024
ROW_TILE = 1024
EVEN_COL_TILE = 1536
ODD_COL_TILE = 2048
NEG = -1e30


def _params(sem, vmem=VMEM_LIMIT, **kw):
    return pltpu.CompilerParams(dimension_semantics=sem, vmem_limit_bytes=vmem, **kw)


def _sigmoid(v):
    return 0.5 * (jnp.tanh(0.5 * v) + 1.0)


def _silu(v):
    return v * _sigmoid(v)


def _softplus(v):
    return jnp.maximum(v, 0.0) + jnp.log(1.0 + jnp.exp(-jnp.abs(v)))


def _dot(a, b, **kw):
    return jnp.dot(a, b, preferred_element_type=F32, **kw)


def _dot_nt(a, b):
    return lax.dot_general(a, b, (((1,), (1,)), ((), ())), preferred_element_type=F32)


def _mod_kernel(c_ref, w_ref, b_ref, o_ref):
    s = _silu(c_ref[...])
    o_ref[0] = _dot(s, w_ref[0], precision=HIGHEST) + b_ref[0]


def _modulation(crows, mod_w, mod_b):
    depth, d, n3 = mod_w.shape
    r = crows.shape[0]
    tn = 1024
    return pl.pallas_call(
        _mod_kernel,
        grid=(depth, n3 // tn),
        in_specs=[pl.BlockSpec((r, d), lambda l, j: (0, 0)),
                  pl.BlockSpec((1, d, tn), lambda l, j: (l, 0, j)),
                  pl.BlockSpec((1, 1, tn), lambda l, j: (l, 0, j))],
        out_specs=pl.BlockSpec((1, r, tn), lambda l, j: (l, 0, j)),
        out_shape=jax.ShapeDtypeStruct((depth, r, n3), F32),
        compiler_params=_params(("parallel", "parallel")),
        name="modulation",
    )(crows, mod_w, mod_b.reshape(depth, 1, n3))


def _rope(v, cos, sin, lane):
    swapped = jnp.where((lane & 63) < 32, pltpu.roll(v, LANES - 32, 1), pltpu.roll(v, 32, 1))
    return v * cos + swapped * sin


def _inproj_kernel(x_ref, g_ref, sc_ref, sh_ref, w_ref, o_ref, h_ref):
    @pl.when(pl.program_id(1) == 0)
    def _():
        x = x_ref[...]
        y = x * lax.rsqrt(jnp.mean(x * x, axis=-1, keepdims=True) + EPS) * g_ref[...]
        h_ref[...] = (y * (1.0 + sc_ref[0]) + sh_ref[0]).astype(BF16)

    o_ref[...] = _dot(h_ref[...], w_ref[...]).astype(o_ref.dtype)


def _in_proj(x2, g, scale, shift, rows_per_group, w, out_dtype, tm, tn):
    m, d = x2.shape
    n = w.shape[1]
    tpg = rows_per_group // tm
    return pl.pallas_call(
        _inproj_kernel,
        grid=(m // tm, n // tn),
        in_specs=[pl.BlockSpec((tm, d), lambda i, j: (i, 0)),
                  pl.BlockSpec((1, d), lambda i, j: (0, 0)),
                  pl.BlockSpec((1, 1, d), lambda i, j: (i // tpg, 0, 0)),
                  pl.BlockSpec((1, 1, d), lambda i, j: (i // tpg, 0, 0)),
                  pl.BlockSpec((d, tn), lambda i, j: (0, j))],
        out_specs=pl.BlockSpec((tm, tn), lambda i, j: (i, j)),
        out_shape=jax.ShapeDtypeStruct((m, n), out_dtype),
        scratch_shapes=[pltpu.VMEM((tm, d), BF16)],
        compiler_params=_params(("parallel", "arbitrary")),
        name="in_proj",
    )(x2, g, scale, shift, w)


def _outproj_kernel(a_ref, ga_ref, b_ref, gb_ref, w_ref, x_ref, g_ref, gate_ref, o_ref):
    wa = w_ref.shape[0] // 2
    a = (a_ref[...].astype(F32) * _silu(ga_ref[...].astype(F32))).astype(BF16)
    b = (b_ref[...].astype(F32) * _silu(gb_ref[...].astype(F32))).astype(BF16)
    y = _dot(a, w_ref[0:wa, :]) + _dot(b, w_ref[wa:, :])
    yn = y * lax.rsqrt(jnp.mean(y * y, axis=-1, keepdims=True) + EPS) * g_ref[...]
    o_ref[...] = x_ref[...] + gate_ref[0] * yn


def _out_proj(a, ga, ga_col, b, gb, gb_col, w, x2, g, gate, rows_per_group, tm):
    m, d = x2.shape
    wa = w.shape[0] // 2
    tpg = rows_per_group // tm
    return pl.pallas_call(
        _outproj_kernel,
        grid=(m // tm,),
        in_specs=[pl.BlockSpec((tm, wa), lambda i: (i, 0)),
                  pl.BlockSpec((tm, wa), lambda i: (i, ga_col)),
                  pl.BlockSpec((tm, wa), lambda i: (i, 0)),
                  pl.BlockSpec((tm, wa), lambda i: (i, gb_col)),
                  pl.BlockSpec(w.shape, lambda i: (0, 0)),
                  pl.BlockSpec((tm, d), lambda i: (i, 0)),
                  pl.BlockSpec((1, d), lambda i: (0, 0)),
                  pl.BlockSpec((1, 1, d), lambda i: (i // tpg, 0, 0))],
        out_specs=pl.BlockSpec((tm, d), lambda i: (i, 0)),
        out_shape=jax.ShapeDtypeStruct((m, d), F32),
        compiler_params=_params(("parallel",)),
        name="out_proj",
    )(a, ga, b, gb, w, x2, g, gate)


def _lru_kernel(xa_ref, xac_ref, cw_ref, cb_ref, wg_ref, bg_ref, lam_ref, y_ref, yc_ref,
                xp_ref, hf_ref, pf_ref, hb_ref, pb_ref):
    nseg = LRU_SEG
    npos = LRU_POS
    c8h = -0.5 * LRU_C * _softplus(-lam_ref[...])
    left = LRU_CONV // 2
    cw = [cw_ref[k:k + 1, :] for k in range(LRU_CONV)]
    cb = cb_ref[...]
    zeros_h = jnp.zeros((LRU_HALO, LANES), F32)
    zeros_s = jnp.zeros((nseg, LANES), F32)
    ones_s = jnp.ones((nseg, LANES), F32)

    def run(src_ref, out_ref, n, c0f, c0b):
        seg = n // nseg
        nblk = seg // npos
        halo = LRU_HALO
        pitch = seg + 2 * halo + 8
        for j in range(nseg):
            lo, hi = j * seg - halo, (j + 1) * seg + halo
            if lo < 0:
                xp_ref[j * pitch:j * pitch + halo, :] = zeros_h
            if hi > n:
                xp_ref[j * pitch + seg + halo:j * pitch + seg + 2 * halo, :] = zeros_h
            lo_c, hi_c = max(lo, 0), min(hi, n)
            xp_ref[j * pitch + (lo_c - lo):j * pitch + (hi_c - lo), :] = src_ref[lo_c:hi_c, :].astype(F32)

        def gather(q):
            return jnp.concatenate([xp_ref[pl.ds(halo + q + c * 8 * pitch, 8, stride=pitch), :]
                                    for c in range(nseg // 8)], axis=0)

        def blk(i):
            return pl.ds(pl.multiple_of(i * npos * nseg, npos * nseg), npos * nseg)

        def pos(v, p):
            return v[p * nseg:(p + 1) * nseg, :]

        def fwd_body(i, carry):
            h, pc = carry
            p0 = i * npos
            xs = [gather(p0 + q - left) for q in range(npos + LRU_CONV - 1)]
            us = []
            for p in range(npos):
                u = cb + cw[0] * xs[p]
                for k in range(1, LRU_CONV):
                    u = u + cw[k] * xs[p + k]
                us.append(u)
            u = jnp.concatenate(us, axis=0)
            g = _dot(u.astype(BF16), wg_ref[0]) + bg_ref[0]
            hu = 0.5 * u
            coef = []
            for d in range(2):
                tr = jnp.tanh(g[:, (2 * d) * LANES:(2 * d + 1) * LANES])
                ti = jnp.tanh(g[:, (2 * d + 1) * LANES:(2 * d + 2) * LANES])
                a = jnp.exp(c8h[d:d + 1, :] * tr + c8h[d:d + 1, :])
                om = 1.0 - a * a
                coef.append((a, om * lax.rsqrt(jnp.maximum(om, LRU_TINY)) * ((ti + 1.0) * hu)))
            (af, bf), (ab, bb) = coef
            hb_ref[blk(i), :] = ab
            pb_ref[blk(i), :] = bb
            hs, ps = [], []
            for p in range(npos):
                a_p = pos(af, p)
                h = a_p * h + pos(bf, p)
                pc = a_p * pc
                hs.append(h)
                ps.append(pc)
            hf_ref[blk(i), :] = jnp.concatenate(hs, axis=0)
            pf_ref[blk(i), :] = jnp.concatenate(ps, axis=0)
            return h, pc

        hef, pef = lax.fori_loop(0, nblk, fwd_body, (zeros_s, ones_s))

        def bwd_body(ii, carry):
            h, pc = carry
            i = nblk - 1 - ii
            ab = hb_ref[blk(i), :]
            bb = pb_ref[blk(i), :]
            hs, ps = [None] * npos, [None] * npos
            for p in reversed(range(npos)):
                a_p = pos(ab, p)
                h = a_p * h + pos(bb, p)
                pc = a_p * pc
                hs[p] = h
                ps[p] = pc
            hb_ref[blk(i), :] = jnp.concatenate(hs, axis=0)
            pb_ref[blk(i), :] = jnp.concatenate(ps, axis=0)
            return h, pc

        heb, peb = lax.fori_loop(0, nblk, bwd_body, (zeros_s, ones_s))

        rows_f, c = [], c0f
        for j in range(nseg):
            rows_f.append(c)
            c = hef[j:j + 1, :] + pef[j:j + 1, :] * c
        final_f = c
        rows_b, c = [None] * nseg, c0b
        for j in reversed(range(nseg)):
            rows_b[j] = c
            c = heb[j:j + 1, :] + peb[j:j + 1, :] * c
        final_b = c
        cf = jnp.concatenate(rows_f, axis=0)
        cbk = jnp.concatenate(rows_b, axis=0)

        opitch = seg + 8

        def out_body(p, _):
            rs = pl.ds(pl.multiple_of(p * nseg, nseg), nseg)
            y = hf_ref[rs, :] + pf_ref[rs, :] * cf + hb_ref[rs, :] + pb_ref[rs, :] * cbk
            for c in range(nseg // 8):
                xp_ref[pl.ds(p + c * 8 * opitch, 8, stride=opitch), :] = y[c * 8:(c + 1) * 8, :]
            return 0
        lax.fori_loop(0, seg, out_body, 0, unroll=8)
        for j in range(nseg):
            out_ref[j * seg:(j + 1) * seg, :] = xp_ref[j * opitch:j * opitch + seg, :].astype(out_ref.dtype)
        return final_f, final_b

    zero = jnp.zeros((1, LANES), F32)
    ff, fb = run(xac_ref, yc_ref, xac_ref.shape[0], zero, zero)
    run(xa_ref, y_ref, xa_ref.shape[0], ff, fb)


def _lru(proj, projc, xa_col, bsz, conv_w, conv_b, wg, bg, lam):
    w = conv_w.shape[1]
    s_len = proj.shape[0] // bsz
    c_len = projc.shape[0] // bsz
    nblk = w // LANES
    seq = lambda n: pl.BlockSpec((n, LANES), lambda b, j: (b, j))
    src = lambda n: pl.BlockSpec((n, LANES), lambda b, j: (b, xa_col + j))
    return pl.pallas_call(
        _lru_kernel,
        grid=(bsz, nblk),
        in_specs=[src(s_len), src(c_len),
                  pl.BlockSpec((LRU_CONV, LANES), lambda b, j: (0, j)),
                  pl.BlockSpec((1, LANES), lambda b, j: (0, j)),
                  pl.BlockSpec((1, LANES, 4 * LANES), lambda b, j: (j, 0, 0)),
                  pl.BlockSpec((1, 1, 4 * LANES), lambda b, j: (j, 0, 0)),
                  pl.BlockSpec((2, LANES), lambda b, j: (0, j))],
        out_specs=[seq(s_len), seq(c_len)],
        out_shape=[jax.ShapeDtypeStruct((proj.shape[0], w), BF16), jax.ShapeDtypeStruct((projc.shape[0], w), BF16)],
        scratch_shapes=[pltpu.VMEM((s_len + (2 * LRU_HALO + 8) * LRU_SEG, LANES), F32)]
        + [pltpu.VMEM((s_len, LANES), F32)] * 4,
        compiler_params=_params(("parallel", "parallel")),
        name="rglru",
    )(proj, projc, conv_w, conv_b, wg, bg, lam)


def _attn_kernel(sink_ref, q_ref, k_ref, v_ref, kc_ref, vc_ref, cos_ref, sin_ref, bias_ref, o_ref):
    nb = pl.num_programs(1) * ATT_QSUB
    group = ATT_HEADS // ATT_KV_HEADS
    scale = HEAD_DIM ** -0.5
    lane = lax.broadcasted_iota(jnp.int32, (BLOCK, LANES), 1)
    ones = jnp.ones((bias_ref.shape[2], HEAD_DIM), BF16)

    def blk(i):
        return pl.ds(pl.multiple_of(i * BLOCK, BLOCK), BLOCK)

    def rot(x, tab):
        return _rope(x.astype(F32), tab[0], tab[1], lane).astype(BF16)

    for sub in range(ATT_QSUB):
        qb = pl.program_id(1) * ATT_QSUB + sub
        qrows = slice(sub * BLOCK, (sub + 1) * BLOCK)
        ip = jnp.maximum(qb - 1, 0)
        inx = jnp.minimum(qb + 1, nb - 1)
        tab_p = (cos_ref[blk(ip), :], sin_ref[blk(ip), :])
        tab_o = (cos_ref[blk(qb), :], sin_ref[blk(qb), :])
        tab_n = (cos_ref[blk(inx), :], sin_ref[blk(inx), :])
        tab_q = (tab_o[0] * scale, tab_o[1] * scale)
        variant = jnp.where(qb > 0, 1, 0) + jnp.where(qb < nb - 1, 2, 0)
        bias = jnp.concatenate([bias_ref[variant]] * group, axis=0)

        for h in range(ATT_KV_HEADS):
            ksl = slice(h * HEAD_DIM, (h + 1) * HEAD_DIM)
            kcat = jnp.concatenate([rot(k_ref[blk(ip), ksl], tab_p), rot(k_ref[blk(qb), ksl], tab_o),
                                    rot(k_ref[blk(inx), ksl], tab_n), kc_ref[:, ksl]], axis=0)
            vcat = jnp.concatenate([v_ref[blk(ip), ksl], v_ref[blk(qb), ksl], v_ref[blk(inx), ksl],
                                    vc_ref[:, ksl]], axis=0)
            vext = jnp.concatenate([vcat, ones], axis=1)
            heads = [h * group + g for g in range(group)]
            q4 = jnp.concatenate([rot(q_ref[qrows, hh * HEAD_DIM:(hh + 1) * HEAD_DIM], tab_q) for hh in heads],
                                 axis=0)
            sk = jnp.concatenate([jnp.full((BLOCK, 1), sink_ref[hh], F32) for hh in heads], axis=0)
            s = _dot_nt(q4, kcat) + bias
            m = jnp.maximum(jnp.max(s, axis=-1, keepdims=True), sk)
            p = jnp.exp((s - m).astype(BF16))
            oe = _dot(p, vext)
            o = oe[:, 0:HEAD_DIM] / (jnp.exp(sk - m) + oe[:, HEAD_DIM:HEAD_DIM + 1])
            for g, hh in enumerate(heads):
                o_ref[qrows, hh * HEAD_DIM:(hh + 1) * HEAD_DIM] = o[g * BLOCK:(g + 1) * BLOCK, :].astype(o_ref.dtype)


def _attention(sink, proj, projc, bsz, cols, cos_t, sin_t):
    s_len = proj.shape[0] // bsz
    c_len = projc.shape[0] // bsz
    nb = s_len // BLOCK
    aw = ATT_HEADS * HEAD_DIM
    kw = ATT_KV_HEADS * HEAD_DIM
    qi = np.arange(BLOCK)[:, None]
    kj = np.arange(BLOCK)[None, :]
    variants = []
    for v in range(4):
        prev = np.where((kj >= qi) & bool(v & 1), 0.0, NEG)
        nxt = np.where((kj <= qi) & bool(v & 2), 0.0, NEG)
        variants.append(np.concatenate([prev, np.zeros((BLOCK, BLOCK)), nxt, np.zeros((BLOCK, c_len))], axis=1))
    bias = jnp.asarray(np.stack(variants), F32)
    ncol = 3 * BLOCK + c_len
    nstep = nb // ATT_QSUB
    tq = ATT_QSUB * BLOCK

    return pl.pallas_call(
        _attn_kernel,
        grid=(bsz, nstep),
        in_specs=[pl.BlockSpec(memory_space=pltpu.SMEM),
                  pl.BlockSpec((tq, aw), lambda b, i: (b * nstep + i, cols["q"])),
                  pl.BlockSpec((s_len, kw), lambda b, i: (b, cols["k"])),
                  pl.BlockSpec((s_len, kw), lambda b, i: (b, cols["v"])),
                  pl.BlockSpec((c_len, kw), lambda b, i: (b, cols["k"])),
                  pl.BlockSpec((c_len, kw), lambda b, i: (b, cols["v"])),
                  pl.BlockSpec((s_len, LANES), lambda b, i: (0, 0)),
                  pl.BlockSpec((s_len, LANES), lambda b, i: (0, 0)),
                  pl.BlockSpec((4, BLOCK, ncol), lambda b, i: (0, 0, 0))],
        out_specs=pl.BlockSpec((tq, aw), lambda b, i: (b * nstep + i, 0)),
        out_shape=jax.ShapeDtypeStruct((proj.shape[0], aw), BF16),
        compiler_params=_params(("parallel", "arbitrary")),
        name="window_attention",
    )(sink, proj, proj, proj, projc, projc, cos_t, sin_t, bias)


def _ctx_attn_kernel(sink_ref, q_ref, k_ref, v_ref, o_ref):
    group = ATT_HEADS // ATT_KV_HEADS
    scale = HEAD_DIM ** -0.5
    n = q_ref.shape[0]
    for h in range(ATT_KV_HEADS):
        ksl = slice(h * HEAD_DIM, (h + 1) * HEAD_DIM)
        qs, sinks = [], []
        for g in range(group):
            hh = h * group + g
            qs.append((q_ref[:, hh * HEAD_DIM:(hh + 1) * HEAD_DIM].astype(F32) * scale).astype(BF16))
            sinks.append(jnp.full((n, 1), sink_ref[hh], F32))
        q4 = jnp.concatenate(qs, axis=0)
        sk = jnp.concatenate(sinks, axis=0)
        s = _dot_nt(q4, k_ref[:, ksl])
        m = jnp.maximum(jnp.max(s, axis=-1, keepdims=True), sk)
        p = jnp.exp(s - m)
        denom = jnp.exp(sk - m) + jnp.sum(p, axis=-1, keepdims=True)
        o = _dot(p.astype(BF16), v_ref[:, ksl]) / denom
        for g in range(group):
            hh = h * group + g
            hs = slice(hh * HEAD_DIM, (hh + 1) * HEAD_DIM)
            o_ref[:, hs] = o[g * n:(g + 1) * n, :].astype(o_ref.dtype)


def _ctx_attention(sink, projc, bsz, cols):
    c_len = projc.shape[0] // bsz
    aw = ATT_HEADS * HEAD_DIM
    kw = ATT_KV_HEADS * HEAD_DIM
    return pl.pallas_call(
        _ctx_attn_kernel,
        grid=(bsz,),
        in_specs=[pl.BlockSpec(memory_space=pltpu.SMEM),
                  pl.BlockSpec((c_len, aw), lambda b: (b, cols["q"])),
                  pl.BlockSpec((c_len, kw), lambda b: (b, cols["k"])),
                  pl.BlockSpec((c_len, kw), lambda b: (b, cols["v"]))],
        out_specs=pl.BlockSpec((c_len, aw), lambda b: (b, 0)),
        out_shape=jax.ShapeDtypeStruct((projc.shape[0], aw), BF16),
        compiler_params=_params(("parallel",)),
        name="context_attention",
    )(sink, projc, projc, projc)


def _ret_kernel(dl_ref, q_ref, k_ref, v_ref, qc_ref, kc_ref, vc_ref, o_ref, oc_ref,
                ot_ref, att_ref, u_ref, st_ref, vt_ref, dm_ref):
    c = RET_CHUNK
    s_len = q_ref.shape[0]
    c_len = qc_ref.shape[0]
    lg = -_softplus(-dl_ref[0])
    lgf, lgb = lg[0:1, :], lg[1:2, :]
    wide = lambda t: jnp.concatenate([t] * (c // LANES), axis=1)
    kj = lax.broadcasted_iota(jnp.int32, (c, c), 0)
    qi = lax.broadcasted_iota(jnp.int32, (c, c), 1)
    diff = (qi - kj).astype(F32)
    dm_ref[...] = jnp.where(qi >= kj, jnp.exp(jnp.maximum(diff, 0.0) * wide(lgf)),
                            jnp.exp(jnp.maximum(-diff, 0.0) * wide(lgb)))
    idx = lax.broadcasted_iota(jnp.int32, (c, LANES), 0).astype(F32)
    qdec_f = jnp.exp((idx + 1.0) * lgf)
    kdec_f = jnp.exp((c - 1.0 - idx) * lgf)
    qdec_b = jnp.exp((c - idx) * lgb)
    kdec_b = jnp.exp(idx * lgb)
    cdec_f = jnp.exp(c * lgf)
    cdec_b = jnp.exp(c * lgb)

    def chunk(j):
        return pl.ds(pl.multiple_of(j * c, c), c)

    def transpose_v(vr, n):
        def body(j, _):
            vt_ref[:, chunk(j)] = vr[chunk(j), :].astype(F32).T.astype(BF16)
            return 0
        lax.fori_loop(0, n // c, body, 0, unroll=min(8, n // c))

    def scores(qr, kr, n):
        def body(j, _):
            att_ref[j] = (_dot_nt(kr[chunk(j), :], qr[chunk(j), :]) * dm_ref[...]).astype(BF16)
            return 0
        lax.fori_loop(0, n // c, body, 0, unroll=min(4, n // c))

    def intra(kr, n):
        def body(j, _):
            vt = vt_ref[:, chunk(j)]
            ot_ref[:, chunk(j)] = _dot(vt, att_ref[j])
            kf = kr[chunk(j), :].astype(F32)
            kcat = jnp.concatenate([(kf * kdec_f).astype(BF16), (kf * kdec_b).astype(BF16)], axis=1)
            u_ref[j] = _dot(vt, kcat)
            return 0
        lax.fori_loop(0, n // c, body, 0, unroll=min(4, n // c))

    def states(n, sf, sb):
        nch = n // c

        def fbody(j, s):
            st_ref[j, :, 0:RET_DK] = s.astype(BF16)
            return s * cdec_f + u_ref[j, :, 0:RET_DK]
        sf = lax.fori_loop(0, nch, fbody, sf)

        def bbody(jj, s):
            j = nch - 1 - jj
            st_ref[j, :, RET_DK:] = s.astype(BF16)
            return s * cdec_b + u_ref[j, :, RET_DK:]
        sb = lax.fori_loop(0, nch, bbody, sb)
        return sf, sb

    def cross(qr, outr, n):
        def body(j, _):
            qf = qr[chunk(j), :].astype(F32)
            qcat = jnp.concatenate([(qf * qdec_f).astype(BF16), (qf * qdec_b).astype(BF16)], axis=1)
            ot = ot_ref[:, chunk(j)] + _dot_nt(st_ref[j], qcat)
            o = ot.T * (RET_DK ** -0.5)
            outr[chunk(j), :] = (o * lax.rsqrt(jnp.mean(o * o, axis=-1, keepdims=True) + EPS)).astype(outr.dtype)
            return 0
        lax.fori_loop(0, n // c, body, 0, unroll=min(4, n // c))

    def run(qr, kr, vr, outr, n, sf, sb):
        transpose_v(vr, n)
        scores(qr, kr, n)
        intra(kr, n)
        finals = states(n, sf, sb)
        cross(qr, outr, n)
        return finals

    zero = jnp.zeros((LANES, RET_DK), F32)
    sf, sb = run(qc_ref, kc_ref, vc_ref, oc_ref, c_len, zero, zero)
    run(q_ref, k_ref, v_ref, o_ref, s_len, sf, sb)


def _retention(dl, proj, projc, bsz, cols):
    s_len = proj.shape[0] // bsz
    c_len = projc.shape[0] // bsz
    hb = lambda n, col: pl.BlockSpec((n, LANES), lambda b, h: (b, col + h))
    return pl.pallas_call(
        _ret_kernel,
        grid=(bsz, RET_HEADS),
        in_specs=[pl.BlockSpec((1, 2, LANES), lambda b, h: (h, 0, 0)),
                  hb(s_len, cols["q"]), hb(s_len, cols["k"]), hb(s_len, cols["v"]),
                  hb(c_len, cols["q"]), hb(c_len, cols["k"]), hb(c_len, cols["v"])],
        out_specs=[pl.BlockSpec((s_len, LANES), lambda b, h: (b, h)),
                   pl.BlockSpec((c_len, LANES), lambda b, h: (b, h))],
        out_shape=[jax.ShapeDtypeStruct((proj.shape[0], RET_HEADS * LANES), BF16),
                   jax.ShapeDtypeStruct((projc.shape[0], RET_HEADS * LANES), BF16)],
        scratch_shapes=[pltpu.VMEM((LANES, s_len), F32),
                        pltpu.VMEM((s_len // RET_CHUNK, RET_CHUNK, RET_CHUNK), BF16),
                        pltpu.VMEM((s_len // RET_CHUNK, LANES, 2 * RET_DK), F32),
                        pltpu.VMEM((s_len // RET_CHUNK, LANES, 2 * RET_DK), BF16),
                        pltpu.VMEM((LANES, s_len), BF16),
                        pltpu.VMEM((RET_CHUNK, RET_CHUNK), F32)],
        compiler_params=_params(("parallel", "parallel")),
        name="retention",
    )(dl, proj, proj, proj, projc, projc, projc)


def _hy_hid_kernel(z_ref, w1_ref, b1_ref, w2_ref, b2_ref, f_ref, o_ref):
    f = f_ref[...]
    h = jnp.sin(f * (_dot(z_ref[...], w1_ref[...], precision=HIGHEST) + b1_ref[...]))
    o_ref[...] = jnp.sin(f * (_dot(h, w2_ref[...], precision=HIGHEST) + b2_ref[...]))


def _hy_hidden(zfull, w1p, b1, w2, b2, freq):
    n, e = zfull.shape
    fd = w2.shape[0]
    tr = min(n, 1024)
    full = lambda shp: pl.BlockSpec(shp, lambda i: (0, 0))
    return pl.pallas_call(
        _hy_hid_kernel,
        grid=(n // tr,),
        in_specs=[pl.BlockSpec((tr, e), lambda i: (i, 0)), full((e, fd)), full((1, fd)), full((fd, fd)),
                  full((1, fd)), full((1, fd))],
        out_specs=pl.BlockSpec((tr, fd), lambda i: (i, 0)),
        out_shape=jax.ShapeDtypeStruct((n, fd), F32),
        compiler_params=_params(("parallel",)),
        name="hyena_filter_mlp",
    )(zfull, w1p, b1, w2, b2, freq)


def _hy_filt_kernel(hid_ref, w3f_ref, w3b_ref, tn_ref, dl_ref, o_ref):
    half = hid_ref.shape[0] // 2
    decay = jnp.exp(-tn_ref[...] * jnp.abs(dl_ref[...]))
    top = _dot(hid_ref[0:half, :], w3f_ref[...], precision=HIGHEST)
    bot = _dot(hid_ref[half:, :], w3b_ref[...], precision=HIGHEST)
    row = lax.broadcasted_iota(jnp.int32, bot.shape, 0)
    bot = jnp.where(row == 0, 0.0, bot)
    o_ref[0, 0:half, :] = top * decay[0:half, :]
    o_ref[0, half:, :] = bot * decay[half:, :]


def _hy_filters(hid, w3, tn_full, deltas):
    n, fd = hid.shape
    wch = deltas.shape[1]
    nsl = wch // LANES
    return pl.pallas_call(
        _hy_filt_kernel,
        grid=(HY_ORDER, nsl),
        in_specs=[pl.BlockSpec((n, fd), lambda o, j: (0, 0)),
                  pl.BlockSpec((fd, LANES), lambda o, j: (0, o * 2 * nsl + j)),
                  pl.BlockSpec((fd, LANES), lambda o, j: (0, o * 2 * nsl + nsl + j)),
                  pl.BlockSpec((n, LANES), lambda o, j: (0, 0)),
                  pl.BlockSpec((1, LANES), lambda o, j: (0, j))],
        out_specs=pl.BlockSpec((1, n, LANES), lambda o, j: (o, 0, j)),
        out_shape=jax.ShapeDtypeStruct((HY_ORDER, n, wch), F32),
        compiler_params=_params(("parallel", "parallel")),
        name="hyena_filter",
    )(hid, w3, w3, tn_full, deltas)


def _hy_kfft_kernel(kern_ref, f1_ref, f2_ref, o_ref, g_ref, *, n1, n2):
    pg = 2 * n1 + HY_PITCH_PAD

    def stage1(i2, _):
        x = kern_ref[pl.ds(i2, n1, stride=n2), :].astype(BF16)
        g_ref[pl.ds(pl.multiple_of(i2 * pg, 8), 2 * n1), :] = _dot(f1_ref[i2], x)
        return 0
    lax.fori_loop(0, n2, stage1, 0, unroll=HY_UNROLL)

    def stage2(k1, _):
        x = jnp.concatenate([g_ref[pl.ds(k1, n2, stride=pg), :],
                             g_ref[pl.ds(n1 + k1, n2, stride=pg), :]], axis=0).astype(BF16)
        o_ref[pl.ds(pl.multiple_of(k1 * 2 * n2, 2 * n2), 2 * n2), :] = _dot(f2_ref[...], x)
        return 0
    lax.fori_loop(0, n1, stage2, 0, unroll=HY_UNROLL // 2)


def _hy_kfft(kern, f1k, f2, n1, n2):
    orders, n, wch = kern.shape
    nsl = wch // LANES
    return pl.pallas_call(
        functools.partial(_hy_kfft_kernel, n1=n1, n2=n2),
        grid=(orders, nsl),
        in_specs=[pl.BlockSpec((None, n, LANES), lambda o, j: (o, 0, j)),
                  pl.BlockSpec(f1k.shape, lambda o, j: (0, 0, 0)),
                  pl.BlockSpec(f2.shape, lambda o, j: (0, 0))],
        out_specs=pl.BlockSpec((None, None, 2 * n, LANES), lambda o, j: (o, j, 0, 0)),
        out_shape=jax.ShapeDtypeStruct((orders, nsl, 2 * n, LANES), F32),
        scratch_shapes=[pltpu.VMEM((n2 * (2 * n1 + HY_PITCH_PAD), LANES), F32)],
        compiler_params=_params(("parallel", "parallel")),
        name="hyena_filter_fft",
    )(kern, f1k, f2)


def _hy_prep_kernel(z_ref, w_ref, b_ref, o_ref, zp_ref, *, n2, nseq):
    l_len = z_ref.shape[0] // nseq
    h1 = l_len // n2
    halo = HY_HALO
    pitch = n2 + 2 * halo + 8
    zeros_h = jnp.zeros((halo, LANES), F32)
    for s in range(nseq):
        for i1 in range(h1):
            base = (s * h1 + i1) * pitch
            lo, hi = i1 * n2 - halo, (i1 + 1) * n2 + halo
            if lo < 0:
                zp_ref[base:base + halo, :] = zeros_h
            if hi > l_len:
                zp_ref[base + n2 + halo:base + n2 + 2 * halo, :] = zeros_h
            lo_c, hi_c = max(lo, 0), min(hi, l_len)
            zp_ref[base + (lo_c - lo):base + (hi_c - lo), :] = (
                z_ref[s * l_len + lo_c:s * l_len + hi_c, :].astype(F32))
    w0, w1, w2, b = w_ref[0:1, :], w_ref[1:2, :], w_ref[2:3, :], b_ref[...]

    def tap(j):
        return zp_ref[pl.ds(halo + j, nseq * h1, stride=pitch), :]

    def body(i2, carry):
        zm, z0 = carry
        zn = tap(i2 + 1)
        val = (b + w0 * zm + w1 * z0 + w2 * zn).astype(o_ref.dtype)
        for s in range(nseq):
            o_ref[pl.ds(pl.multiple_of(s * l_len + i2 * h1, h1), h1), :] = val[s * h1:(s + 1) * h1, :]
        return z0, zn
    lax.fori_loop(0, n2, body, (tap(-1), tap(0)), unroll=HY_UNROLL if nseq <= 2 else 1)


def _hy_stage_dtype(l_len, n2):
    return BF16 if (l_len // n2) % 16 == 0 else F32


def _hy_prep(proj, z_col, nz, bsz, conv_w, conv_b, n2):
    l_len = proj.shape[0] // bsz
    h1 = l_len // n2
    nseq = max(1, min(bsz, HY_PREP_ROWS // l_len))
    while bsz % nseq:
        nseq -= 1
    return pl.pallas_call(
        functools.partial(_hy_prep_kernel, n2=n2, nseq=nseq),
        grid=(bsz // nseq, nz),
        in_specs=[pl.BlockSpec((nseq * l_len, LANES), lambda b, j: (b, z_col + j)),
                  pl.BlockSpec((3, LANES), lambda b, j: (0, j)),
                  pl.BlockSpec((1, LANES), lambda b, j: (0, j))],
        out_specs=pl.BlockSpec((nseq * l_len, LANES), lambda b, j: (b, j)),
        out_shape=jax.ShapeDtypeStruct((proj.shape[0], nz * LANES), _hy_stage_dtype(l_len, n2)),
        scratch_shapes=[pltpu.VMEM((nseq * h1 * (n2 + 2 * HY_HALO + 8), LANES), F32)],
        compiler_params=_params(("parallel", "parallel")),
        name="hyena_short_conv",
    )(proj, conv_w, conv_b)


def _pack2(re, im):
    r = lax.bitcast_convert_type(re, jnp.uint32) + jnp.uint32(0x8000)
    i = lax.bitcast_convert_type(im, jnp.uint32) + jnp.uint32(0x8000)
    return (r & jnp.uint32(0xFFFF0000)) | (i >> 16)


def _unpack2(w):
    re = lax.bitcast_convert_type(w & jnp.uint32(0xFFFF0000), F32)
    im = lax.bitcast_convert_type(w << 16, F32)
    return jnp.concatenate([re, im], axis=0).astype(BF16)


def _hy_conv_kernel(y_ref, x_ref, kh_ref, bias_ref, f1_ref, f2_ref, f2i_ref, f1i_ref, o_ref, g_ref,
                    *, n1, n2, natural_out):
    h1 = n1 // 2
    l_len = h1 * n2
    pg = n1 + HY_PITCH_PAD
    bias = bias_ref[...]
    unroll = HY_UNROLL

    def rows(seq, i2):
        return pl.ds(pl.multiple_of(seq * l_len + i2 * h1, h1), h1)

    def lanes2(a, b):
        return jnp.concatenate([a, b], axis=1)

    def stage1(i2, _):
        x = lanes2(jnp.concatenate([y_ref[rows(0, i2), :], y_ref[rows(1, i2), :]], axis=0),
                   jnp.concatenate([y_ref[rows(2, i2), :], y_ref[rows(3, i2), :]], axis=0)).astype(BF16)
        a = _dot(f1_ref[i2], x)
        w = _pack2(a[0:n1, :], a[n1:, :])
        blk = pl.ds(pl.multiple_of(i2 * pg, 8), n1)
        g_ref[0, blk, :] = w[:, 0:LANES]
        g_ref[1, blk, :] = w[:, LANES:]
        return 0
    lax.fori_loop(0, n2, stage1, 0, unroll=2 * unroll)

    def stage23(t, _):
        outs = []
        for u in range(unroll):
            k1 = t * unroll + u
            col = pl.ds(k1, n2, stride=pg)
            x = lanes2(_unpack2(g_ref[0, col, :]), _unpack2(g_ref[1, col, :]))
            yh = _dot(f2_ref[...], x)
            base = pl.multiple_of(k1 * 2 * n2, 2 * n2)
            kr = kh_ref[pl.ds(base, n2), :]
            ki = kh_ref[pl.ds(base + n2, n2), :]
            kr, ki = lanes2(kr, kr), lanes2(ki, ki)
            yr, yi = yh[0:n2, :], yh[n2:, :]
            z = jnp.concatenate([yr * kr - yi * ki, yr * ki + yi * kr], axis=0).astype(BF16)
            c = _dot(f2i_ref[...], z)
            outs.append((col, _pack2(c[0:n2, :], c[n2:, :])))
        for col, w in outs:
            g_ref[0, col, :] = w[:, 0:LANES]
            g_ref[1, col, :] = w[:, LANES:]
        return 0
    lax.fori_loop(0, n1 // unroll, stage23, 0)

    def stage4(i2, _):
        blk = pl.ds(pl.multiple_of(i2 * pg, 8), n1)
        x = lanes2(_unpack2(g_ref[0, blk, :]), _unpack2(g_ref[1, blk, :]))
        yc = _dot(f1i_ref[i2], x)
        for seq in range(4):
            pair, half = seq // 2, seq % 2
            sl = rows(seq, i2)
            conv = yc[half * h1:(half + 1) * h1, pair * LANES:(pair + 1) * LANES]
            val = x_ref[sl, :].astype(F32) * (conv + bias * y_ref[sl, :].astype(F32))
            if natural_out:
                o_ref[pl.ds(seq * l_len + i2, h1, stride=n2), :] = val
            else:
                o_ref[sl, :] = val.astype(o_ref.dtype)
        return 0
    lax.fori_loop(0, n2, stage4, 0, unroll=unroll)


def _hy_conv(ysrc, ycol, xsrc, xcol, khat, order, bias, mats, bsz, n1, n2, natural_out):
    l_len = ysrc.shape[0] // bsz
    nsl = khat.shape[1]
    f1, f2, f2i, f1i = mats
    once = pl.Buffered(1)
    cst2 = lambda a: pl.BlockSpec(a.shape, lambda j, p: (0, 0), pipeline_mode=once)
    cst3 = lambda a: pl.BlockSpec(a.shape, lambda j, p: (0, 0, 0), pipeline_mode=once)
    return pl.pallas_call(
        functools.partial(_hy_conv_kernel, n1=n1, n2=n2, natural_out=natural_out),
        grid=(nsl, bsz // 4),
        in_specs=[pl.BlockSpec((4 * l_len, LANES), lambda j, p: (p, ycol + j)),
                  pl.BlockSpec((4 * l_len, LANES), lambda j, p: (p, xcol + j)),
                  pl.BlockSpec((None, None, khat.shape[2], LANES), lambda j, p: (order, j, 0, 0),
                               pipeline_mode=once),
                  pl.BlockSpec((1, LANES), lambda j, p: (0, j)),
                  cst3(f1), cst2(f2), cst2(f2i), cst3(f1i)],
        out_specs=pl.BlockSpec((4 * l_len, LANES), lambda j, p: (p, j)),
        out_shape=jax.ShapeDtypeStruct((ysrc.shape[0], nsl * LANES),
                                       F32 if natural_out else _hy_stage_dtype(l_len, n2)),
        scratch_shapes=[pltpu.VMEM((2, n2 * (n1 + HY_PITCH_PAD), LANES), jnp.uint32)],
        compiler_params=_params(("parallel", "arbitrary")),
        name="hyena_long_conv",
    )(ysrc, xsrc, khat, bias[order][None], f1, f2, f2i, f1i)


def _dft_tables(l_len):
    n = 2 * l_len
    n2 = 128 if l_len >= 1024 else 32
    n1 = n // n2
    h1 = n1 // 2
    i1 = np.arange(n1)[None, None, :]
    k1 = np.arange(n1)[None, :, None]
    i2 = np.arange(n2)[:, None, None]
    ph = 2 * np.pi * (i1 * k1 / n1 + i2 * k1 / n)
    c, s = np.cos(ph), np.sin(ph)
    ch, sh = c[:, :, :h1], s[:, :, :h1]
    f1 = np.concatenate([np.concatenate([ch, sh], 2), np.concatenate([-sh, ch], 2)], 1)
    f1k = np.concatenate([c, -s], 1)
    ct, st = np.swapaxes(ch, 1, 2), np.swapaxes(sh, 1, 2)
    f1i = np.concatenate([np.concatenate([ct, -st], 2), np.concatenate([st, ct], 2)], 1) / n
    a = np.arange(n2)
    ph2 = 2 * np.pi * np.outer(a, a) / n2
    c2, s2 = np.cos(ph2), np.sin(ph2)
    f2 = np.block([[c2, s2], [-s2, c2]])
    f2i = np.block([[c2, -s2], [s2, c2]])
    bf = lambda m: jnp.asarray(m, dtype=F32).astype(BF16)
    return n1, n2, (bf(f1), bf(f2), bf(f2i), bf(f1i)), bf(f1k)


def _filter_positions(l_len, width):
    lag = np.concatenate([np.arange(l_len), l_len - np.arange(l_len)]).astype(np.float64)
    t = lag / (l_len - 1)
    bands = (HY_EMB - 1) // 2
    w = 2.0 * np.pi * lag / l_len
    f = np.linspace(1e-4, bands - 1, bands)[None]
    z = np.concatenate([t[:, None], np.cos(f * w[:, None]), -np.sin(f * w[:, None])], axis=-1)
    z = np.pad(z, ((0, 0), (0, HY_EMB_PAD - HY_EMB)))
    tn = np.repeat(t[:, None], LANES, axis=1)
    max_decay = math.log(HY_DECAY_TARGET) / HY_FAST_PCT
    min_decay = math.log(HY_DECAY_TARGET) / HY_SLOW_PCT
    deltas = np.linspace(min_decay, max_decay, width)[None]
    return jnp.asarray(z, F32), jnp.asarray(tn, F32), jnp.asarray(deltas, F32)


def _rope_tables(seq):
    n_rows = seq // GRID_W
    row = np.repeat(np.arange(n_rows), GRID_W).astype(np.float64)
    col = np.tile(np.arange(GRID_W), n_rows).astype(np.float64)
    half = HEAD_DIM // 2
    inv = ROPE_BASE ** (-np.arange(0, half, 2, dtype=np.float64) / half)
    ar, ac = row[:, None] * inv, col[:, None] * inv
    cos = np.concatenate([np.cos(ar), np.cos(ar), np.cos(ac), np.cos(ac)], axis=1)
    sin = np.concatenate([-np.sin(ar), np.sin(ar), -np.sin(ac), np.sin(ac)], axis=1)
    return jnp.asarray(cos, F32), jnp.asarray(sin, F32)


def _hyena_spectra(l_len, width, w1, b1, w2, b2, w3, freq, tables):
    n1, n2, mats, f1k = tables
    zfull, tn_full, deltas = _filter_positions(l_len, width)
    w1p = jnp.pad(w1, ((0, HY_EMB_PAD - HY_EMB), (0, 0)))
    hid = _hy_hidden(zfull, w1p, b1[None], w2, b2[None], freq[None])
    kern = _hy_filters(hid, w3, tn_full, deltas)
    return _hy_kfft(kern, f1k, mats[1], n1, n2)


def _hyena_seq(zc, bsz, khat, bias, tables):
    n1, n2, mats, _ = tables
    nsl = khat.shape[1]
    y1 = _hy_conv(zc, 0, zc, nsl, khat, 0, bias, mats, bsz, n1, n2, False)
    return _hy_conv(y1, 0, zc, 2 * nsl, khat, 1, bias, mats, bsz, n1, n2, True)


def kernel(x, c, ctx, c_ctx, mod_w, mod_b, norm_pre, norm_post, ev_w_in, ev_w_out, lru_conv_w, lru_conv_b, lru_wa, lru_ba, lru_wx, lru_bx, lru_lambda, attn_sink, od_w_in, od_w_out, hy_conv_w, hy_conv_b, hy_w1, hy_b1, hy_w2, hy_b2, hy_w3, hy_freq, hy_bias, ret_decay_logit):
    bsz, s_len, d = x.shape
    c_len = ctx.shape[1]
    depth = mod_w.shape[0]
    assert bsz % 4 == 0 and bsz <= 16 and s_len % 1024 == 0 and c_len % 256 == 0 and c_len <= s_len

    crows = jnp.concatenate([c, c_ctx[None], jnp.zeros((24 - bsz - 1, d), F32)], axis=0)
    mod = _modulation(crows, mod_w, mod_b)

    cos_t, sin_t = _rope_tables(s_len)
    tab_l = _dft_tables(s_len)
    tab_c = _dft_tables(c_len)

    x2 = x.reshape(bsz * s_len, d)
    ctx2 = ctx.reshape(bsz * c_len, d)
    tm = ROW_TILE

    for l in range(depth):
        need_ctx = l < depth - 1
        shift, scale, gate = (mod[l, :bsz, i * d:(i + 1) * d].reshape(bsz, 1, d) for i in range(3))
        shift_c, scale_c, gate_c = (mod[l, bsz:bsz + 1, i * d:(i + 1) * d].reshape(1, 1, d) for i in range(3))
        g_pre = norm_pre[l][None]
        g_post = norm_post[l][None]

        def proj(w, tn):
            wb = w.astype(BF16)
            p = _in_proj(x2, g_pre, scale, shift, s_len, wb, BF16, tm, tn)
            pc = _in_proj(ctx2, g_pre, scale_c, shift_c, bsz * c_len, wb, BF16, tm, tn)
            return p, pc

        if l % 2 == 0:
            e = l // 2
            w_in = ev_w_in[e]
            w_ord = jnp.concatenate([w_in[:, 1024:2048], w_in[:, 2048:3072], w_in[:, 3584:4608],
                                     w_in[:, 3072:3584], w_in[:, 0:1024]], axis=1)
            pr, prc = proj(w_ord, EVEN_COL_TILE)
            cols = {"q": 1, "gb": 2, "k": 12, "v": 13}
            xa_col = 3584 // LANES
            nblk = lru_wa.shape[2]
            wg = jnp.stack([lru_wa[e, 0], lru_wx[e, 0], lru_wa[e, 1], lru_wx[e, 1]], axis=1)
            wg = jnp.transpose(wg, (0, 2, 1, 3)).reshape(nblk, LRU_BLOCK_W, 4 * LRU_BLOCK_W)
            bg = jnp.stack([lru_ba[e, 0], lru_bx[e, 0], lru_ba[e, 1], lru_bx[e, 1]], axis=0)
            bg = jnp.transpose(bg.reshape(4, nblk, LRU_BLOCK_W), (1, 0, 2)).reshape(nblk, 1, 4 * LRU_BLOCK_W)
            wg, bg = (0.5 * wg).astype(BF16), 0.5 * bg
            ya, yac = _lru(pr, prc, xa_col, bsz, lru_conv_w[e], lru_conv_b[e][None], wg, bg, lru_lambda[e])
            yb = _attention(attn_sink[e], pr, prc, bsz, cols, cos_t, sin_t)
            w_out = ev_w_out[e].astype(BF16)
            x2_new = _out_proj(ya, pr, 0, yb, pr, cols["gb"], w_out, x2, g_post, gate, s_len, tm)
            if need_ctx:
                ybc = _ctx_attention(attn_sink[e], prc, bsz, cols)
                ctx2 = _out_proj(yac, prc, 0, ybc, prc, cols["gb"], w_out, ctx2, g_post, gate_c, bsz * c_len, tm)
            x2 = x2_new
        else:
            o = l // 2
            w_in = od_w_in[o]
            w_ord = jnp.concatenate([w_in[:, 3072:8192], w_in[:, 0:3072]], axis=1)
            pr, prc = proj(w_ord, ODD_COL_TILE)
            cols = {"q": 8, "k": 16, "v": 24}
            z_col = 5120 // LANES
            width = hy_bias.shape[2]
            nz = (HY_ORDER + 1) * width // LANES
            fargs = (hy_w1[o], hy_b1[o], hy_w2[o], hy_b2[o], hy_w3[o], hy_freq[o])
            khat = _hyena_spectra(s_len, width, *fargs, tab_l)
            zconv = _hy_prep(pr, z_col, nz, bsz, hy_conv_w[o], hy_conv_b[o][None], tab_l[1])
            yh = _hyena_seq(zconv, bsz, khat, hy_bias[o], tab_l)
            dl = jnp.broadcast_to(jnp.transpose(ret_decay_logit[o])[:, :, None], (RET_HEADS, 2, LANES))
            yd, ydc = _retention(dl, pr, prc, bsz, cols)
            w_out = od_w_out[o].astype(BF16)
            x2_new = _out_proj(yh, pr, 0, yd, pr, 4, w_out, x2, g_post, gate, s_len, tm)
            if need_ctx:
                khat_c = _hyena_spectra(c_len, width, *fargs, tab_c)
                zcconv = _hy_prep(prc, z_col, nz, bsz, hy_conv_w[o], hy_conv_b[o][None], tab_c[1])
                yhc = _hyena_seq(zcconv, bsz, khat_c, hy_bias[o], tab_c)
                ctx2 = _out_proj(yhc, prc, 0, ydc, prc, 4, w_out, ctx2, g_post, gate_c, bsz * c_len, tm)
            x2 = x2_new
    return x2.reshape(bsz, s_len, d)
```

```python
import functools
import math

import numpy as np
import jax
import jax.numpy as jnp
from jax import lax
from jax.experimental import pallas as pl
from jax.experimental.pallas import tpu as pltpu

F32 = jnp.float32
BF16 = jnp.bfloat16
HIGHEST = lax.Precision.HIGHEST

EPS = 1e-6
GRID_W = 64
LANES = 128
LRU_BLOCK_W = 128
LRU_CONV = 4
LRU_C = 8.0
LRU_SEG = 16
LRU_POS = 32
LRU_TINY = 1e-30
LRU_HALO = 16
ATT_HEADS = 8
ATT_KV_HEADS = 2
HEAD_DIM = 128
BLOCK = 128
ATT_QSUB = 4
ROPE_BASE = 10000.0
HY_ORDER = 2
HY_EMB = 33
HY_EMB_PAD = 40
HY_DECAY_TARGET = 1e-2
HY_FAST_PCT = 0.3
HY_SLOW_PCT = 1.5
HY_UNROLL = 16
HY_PREP_ROWS = 16384
HY_HALO = 16
HY_PITCH_PAD = 8
RET_HEADS = 8
RET_DK = 128
RET_CHUNK = 256
RET_UNROLL = 8

V7X_VMEM_BYTES = 64 * 1024 * 1024
VMEM_LIMIT = V7X_VMEM_BYTES - 8 * 1024 * 1024
ROW_TILE = 1024
EVEN_COL_TILE = 1536
ODD_COL_TILE = 2048
NEG = -1e30


def _params(sem, vmem=VMEM_LIMIT, **kw):
    return pltpu.CompilerParams(dimension_semantics=sem, vmem_limit_bytes=vmem, **kw)


def _sigmoid(v):
    return 0.5 * (jnp.tanh(0.5 * v) + 1.0)


def _silu(v):
    return v * _sigmoid(v)


def _softplus(v):
    return jnp.maximum(v, 0.0) + jnp.log(1.0 + jnp.exp(-jnp.abs(v)))


def _dot(a, b, **kw):
    return jnp.dot(a, b, preferred_element_type=F32, **kw)


def _dot_nt(a, b):
    return lax.dot_general(a, b, (((1,), (1,)), ((), ())), preferred_element_type=F32)


def _mod_kernel(c_ref, w_ref, b_ref, o_ref):
    s = _silu(c_ref[...])
    o_ref[0] = _dot(s, w_ref[0], precision=HIGHEST) + b_ref[0]


def _modulation(crows, mod_w, mod_b):
    depth, d, n3 = mod_w.shape
    r = crows.shape[0]
    tn = 1024
    return pl.pallas_call(
        _mod_kernel,
        grid=(depth, n3 // tn),
        in_specs=[pl.BlockSpec((r, d), lambda l, j: (0, 0)),
                  pl.BlockSpec((1, d, tn), lambda l, j: (l, 0, j)),
                  pl.BlockSpec((1, 1, tn), lambda l, j: (l, 0, j))],
        out_specs=pl.BlockSpec((1, r, tn), lambda l, j: (l, 0, j)),
        out_shape=jax.ShapeDtypeStruct((depth, r, n3), F32),
        compiler_params=_params(("parallel", "parallel")),
        name="modulation",
    )(crows, mod_w, mod_b.reshape(depth, 1, n3))


def _rope(v, cos, sin, lane):
    swapped = jnp.where((lane & 63) < 32, pltpu.roll(v, LANES - 32, 1), pltpu.roll(v, 32, 1))
    return v * cos + swapped * sin


def _inproj_kernel(x_ref, g_ref, sc_ref, sh_ref, w_ref, o_ref, h_ref):
    @pl.when(pl.program_id(1) == 0)
    def _():
        x = x_ref[...]
        y = x * lax.rsqrt(jnp.mean(x * x, axis=-1, keepdims=True) + EPS) * g_ref[...]
        h_ref[...] = (y * (1.0 + sc_ref[0]) + sh_ref[0]).astype(BF16)

    o_ref[...] = _dot(h_ref[...], w_ref[...]).astype(o_ref.dtype)


def _in_proj(x2, g, scale, shift, rows_per_group, w, out_dtype, tm, tn):
    m, d = x2.shape
    n = w.shape[1]
    tpg = rows_per_group // tm
    return pl.pallas_call(
        _inproj_kernel,
        grid=(m // tm, n // tn),
        in_specs=[pl.BlockSpec((tm, d), lambda i, j: (i, 0)),
                  pl.BlockSpec((1, d), lambda i, j: (0, 0)),
                  pl.BlockSpec((1, 1, d), lambda i, j: (i // tpg, 0, 0)),
                  pl.BlockSpec((1, 1, d), lambda i, j: (i // tpg, 0, 0)),
                  pl.BlockSpec((d, tn), lambda i, j: (0, j))],
        out_specs=pl.BlockSpec((tm, tn), lambda i, j: (i, j)),
        out_shape=jax.ShapeDtypeStruct((m, n), out_dtype),
        scratch_shapes=[pltpu.VMEM((tm, d), BF16)],
        compiler_params=_params(("parallel", "arbitrary")),
        name="in_proj",
    )(x2, g, scale, shift, w)


def _outproj_kernel(a_ref, ga_ref, b_ref, gb_ref, w_ref, x_ref, g_ref, gate_ref, o_ref):
    wa = w_ref.shape[0] // 2
    a = (a_ref[...].astype(F32) * _silu(ga_ref[...].astype(F32))).astype(BF16)
    b = (b_ref[...].astype(F32) * _silu(gb_ref[...].astype(F32))).astype(BF16)
    y = _dot(a, w_ref[0:wa, :]) + _dot(b, w_ref[wa:, :])
    yn = y * lax.rsqrt(jnp.mean(y * y, axis=-1, keepdims=True) + EPS) * g_ref[...]
    o_ref[...] = x_ref[...] + gate_ref[0] * yn


def _out_proj(a, ga, ga_col, b, gb, gb_col, w, x2, g, gate, rows_per_group, tm):
    m, d = x2.shape
    wa = w.shape[0] // 2
    tpg = rows_per_group // tm
    return pl.pallas_call(
        _outproj_kernel,
        grid=(m // tm,),
        in_specs=[pl.BlockSpec((tm, wa), lambda i: (i, 0)),
                  pl.BlockSpec((tm, wa), lambda i: (i, ga_col)),
                  pl.BlockSpec((tm, wa), lambda i: (i, 0)),
                  pl.BlockSpec((tm, wa), lambda i: (i, gb_col)),
                  pl.BlockSpec(w.shape, lambda i: (0, 0)),
                  pl.BlockSpec((tm, d), lambda i: (i, 0)),
                  pl.BlockSpec((1, d), lambda i: (0, 0)),
                  pl.BlockSpec((1, 1, d), lambda i: (i // tpg, 0, 0))],
        out_specs=pl.BlockSpec((tm, d), lambda i: (i, 0)),
        out_shape=jax.ShapeDtypeStruct((m, d), F32),
        compiler_params=_params(("parallel",)),
        name="out_proj",
    )(a, ga, b, gb, w, x2, g, gate)


def _lru_kernel(xa_ref, xac_ref, cw_ref, cb_ref, wg_ref, bg_ref, lam_ref, y_ref, yc_ref,
                xp_ref, hf_ref, pf_ref, hb_ref, pb_ref):
    nseg = LRU_SEG
    c8h =-0.5 * LRU_C * _softplus(-lam_ref[...])
    left = LRU_CONV // 2
    cw = [cw_ref[k:k + 1, :] for k in range(LRU_CONV)]
    cb = cb_ref[...]
    zeros_h = jnp.zeros((LRU_HALO, LANES), F32)
    zeros_s = jnp.zeros((nseg, LANES), F32)
    ones_s = jnp.ones((nseg, LANES), F32)

    def run(src_ref, out_ref, n, c0f, c0b):
        seg = n // nseg
        npos = min(LRU_POS, seg)
        nblk = seg // npos
        halo = LRU_HALO
        pitch = seg + 2 * halo + 8
        for j in range(nseg):
            lo, hi = j * seg - halo, (j + 1) * seg + halo
            if lo < 0:
                xp_ref[j * pitch:j * pitch + halo, :] = zeros_h
            if hi > n:
                xp_ref[j * pitch + seg + halo:j * pitch + seg + 2 * halo, :] = zeros_h
            lo_c, hi_c = max(lo, 0), min(hi, n)
            xp_ref[j * pitch + (lo_c - lo):j * pitch + (hi_c - lo), :] = src_ref[lo_c:hi_c, :].astype(F32)

        def gather(q):
            return jnp.concatenate([xp_ref[pl.ds(halo + q + c * 8 * pitch, 8, stride=pitch), :]
                                    for c in range(nseg // 8)], axis=0)

        def blk(i):
            return pl.ds(pl.multiple_of(i * npos * nseg, npos * nseg), npos * nseg)

        def pos(v, p):
            return v[p * nseg:(p + 1) * nseg, :]

        def fwd_body(i, carry):
            h, pc = carry
            p0 = i * npos
            xs = [gather(p0 + q - left) for q in range(npos + LRU_CONV - 1)]
            us = []
            for p in range(npos):
                u = cb + cw[0] * xs[p]
                for k in range(1, LRU_CONV):
                    u = u + cw[k] * xs[p + k]
                us.append(u)
            u = jnp.concatenate(us, axis=0)
            g = _dot(u.astype(BF16), wg_ref[0]) + bg_ref[0]
            hu = 0.5 * u
            coef = []
            for d in range(2):
                tr = jnp.tanh(g[:, (2 * d) * LANES:(2 * d + 1) * LANES])
                ti = jnp.tanh(g[:, (2 * d + 1) * LANES:(2 * d + 2) * LANES])
                a = jnp.exp(c8h[d:d + 1, :] * tr + c8h[d:d + 1, :])
                om = 1.0 - a * a
                coef.append((a, om * lax.rsqrt(jnp.maximum(om, LRU_TINY)) * ((ti + 1.0) * hu)))
            (af, bf), (ab, bb) = coef
            hb_ref[blk(i), :] = ab
            pb_ref[blk(i), :] = bb
            hs, ps = [], []
            for p in range(npos):
                a_p = pos(af, p)
                h = a_p * h + pos(bf, p)
                pc = a_p * pc
                hs.append(h)
                ps.append(pc)
            hf_ref[blk(i), :] = jnp.concatenate(hs, axis=0)
            pf_ref[blk(i), :] = jnp.concatenate(ps, axis=0)
            return h, pc

        hef, pef = lax.fori_loop(0, nblk, fwd_body, (zeros_s, ones_s))

        def bwd_body(ii, carry):
            h, pc = carry
            i = nblk - 1 - ii
            ab = hb_ref[blk(i), :]
            bb = pb_ref[blk(i), :]
            hs, ps = [None] * npos, [None] * npos
            for p in reversed(range(npos)):
                a_p = pos(ab, p)
                h = a_p * h + pos(bb, p)
                pc = a_p * pc
                hs[p] = h
                ps[p] = pc
            hb_ref[blk(i), :] = jnp.concatenate(hs, axis=0)
            pb_ref[blk(i), :] = jnp.concatenate(ps, axis=0)
            return h, pc

        heb, peb = lax.fori_loop(0, nblk, bwd_body, (zeros_s, ones_s))

        rows_f, c = [], c0f
        for j in range(nseg):
            rows_f.append(c)
            c = hef[j:j + 1, :] + pef[j:j + 1, :] * c
        final_f = c
        rows_b, c = [None] * nseg, c0b
        for j in reversed(range(nseg)):
            rows_b[j] = c
            c = heb[j:j + 1, :] + peb[j:j + 1, :] * c
        final_b = c
        cf = jnp.concatenate(rows_f, axis=0)
        cbk = jnp.concatenate(rows_b, axis=0)

        opitch = seg + 8

        def out_body(p, _):
            rs = pl.ds(pl.multiple_of(p * nseg, nseg), nseg)
            y = hf_ref[rs, :] + pf_ref[rs, :] * cf + hb_ref[rs, :] + pb_ref[rs, :] * cbk
            for c in range(nseg // 8):
                xp_ref[pl.ds(p + c * 8 * opitch, 8, stride=opitch), :] = y[c * 8:(c + 1) * 8, :]
            return 0
        lax.fori_loop(0, seg, out_body, 0, unroll=8)
        for j in range(nseg):
            out_ref[j * seg:(j + 1) * seg, :] = xp_ref[j * opitch:j * opitch + seg, :].astype(out_ref.dtype)
        return final_f, final_b

    zero = jnp.zeros((1, LANES), F32)
    ff, fb = run(xac_ref, yc_ref, xac_ref.shape[0], zero, zero)
    run(xa_ref, y_ref, xa_ref.shape[0], ff, fb)


def _lru(proj, projc, xa_col, bsz, conv_w, conv_b, wg, bg, lam):
    w = conv_w.shape[1]
    s_len = proj.shape[0] // bsz
    c_len = projc.shape[0] // bsz
    nblk = w // LANES
    seq = lambda n: pl.BlockSpec((n, LANES), lambda b, j: (b, j))
    src = lambda n: pl.BlockSpec((n, LANES), lambda b, j: (b, xa_col + j))
    return pl.pallas_call(
        _lru_kernel,
        grid=(bsz, nblk),
        in_specs=[src(s_len), src(c_len),
                  pl.BlockSpec((LRU_CONV, LANES), lambda b, j: (0, j)),
                  pl.BlockSpec((1, LANES), lambda b, j: (0, j)),
                  pl.BlockSpec((1, LANES, 4 * LANES), lambda b, j: (j, 0, 0)),
                  pl.BlockSpec((1, 1, 4 * LANES), lambda b, j: (j, 0, 0)),
                  pl.BlockSpec((2, LANES), lambda b, j: (0, j))],
        out_specs=[seq(s_len), seq(c_len)],
        out_shape=[jax.ShapeDtypeStruct((proj.shape[0], w), BF16), jax.ShapeDtypeStruct((projc.shape[0], w), BF16)],
        scratch_shapes=[pltpu.VMEM((s_len + (2 * LRU_HALO + 8) * LRU_SEG, LANES), F32)]
        + [pltpu.VMEM((s_len, LANES), F32)] * 4,
        compiler_params=_params(("parallel", "parallel")),
        name="rglru",
    )(proj, projc, conv_w, conv_b, wg, bg, lam)


def _attn_kernel(sink_ref, q_ref, k_ref, v_ref, kc_ref, vc_ref, cos_ref, sin_ref, bias_ref, o_ref):
    nb = pl.num_programs(1) * ATT_QSUB
    group = ATT_HEADS // ATT_KV_HEADS
    scale = HEAD_DIM ** -0.5
    lane = lax.broadcasted_iota(jnp.int32, (BLOCK, LANES), 1)
    ones = jnp.ones((bias_ref.shape[2], HEAD_DIM), BF16)

    def blk(i):
        return pl.ds(pl.multiple_of(i * BLOCK, BLOCK), BLOCK)

    def rot(x, tab):
        return _rope(x.astype(F32), tab[0], tab[1], lane).astype(BF16)

    for sub in range(ATT_QSUB):
        qb = pl.program_id(1) * ATT_QSUB + sub
        qrows = slice(sub * BLOCK, (sub + 1) * BLOCK)
        ip = jnp.maximum(qb - 1, 0)
        inx = jnp.minimum(qb + 1, nb - 1)
        tab_p = (cos_ref[blk(ip), :], sin_ref[blk(ip), :])
        tab_o = (cos_ref[blk(qb), :], sin_ref[blk(qb), :])
        tab_n = (cos_ref[blk(inx), :], sin_ref[blk(inx), :])
        tab_q = (tab_o[0] * scale, tab_o[1] * scale)
        variant = jnp.where(qb > 0, 1, 0) + jnp.where(qb < nb - 1, 2, 0)
        bias = jnp.concatenate([bias_ref[variant]] * group, axis=0)

        for h in range(ATT_KV_HEADS):
            ksl = slice(h * HEAD_DIM, (h + 1) * HEAD_DIM)
            kcat = jnp.concatenate([rot(k_ref[blk(ip), ksl], tab_p), rot(k_ref[blk(qb), ksl], tab_o),
                                    rot(k_ref[blk(inx), ksl], tab_n), kc_ref[:, ksl]], axis=0)
            vcat = jnp.concatenate([v_ref[blk(ip), ksl], v_ref[blk(qb), ksl], v_ref[blk(inx), ksl],
                                    vc_ref[:, ksl]], axis=0)
            vext = jnp.concatenate([vcat, ones], axis=1)
            heads = [h * group + g for g in range(group)]
            q4 = jnp.concatenate([rot(q_ref[qrows, hh * HEAD_DIM:(hh + 1) * HEAD_DIM], tab_q) for hh in heads],
                                 axis=0)
            sk = jnp.concatenate([jnp.full((BLOCK, 1), sink_ref[hh], F32) for hh in heads], axis=0)
            s = _dot_nt(q4, kcat) + bias
            m = jnp.maximum(jnp.max(s, axis=-1, keepdims=True), sk)
            p = jnp.exp((s - m).astype(BF16))
            oe = _dot(p, vext)
            o = oe[:, 0:HEAD_DIM] / (jnp.exp(sk - m) + oe[:, HEAD_DIM:HEAD_DIM + 1])
            for g, hh in enumerate(heads):
                o_ref[qrows, hh * HEAD_DIM:(hh + 1) * HEAD_DIM] = o[g * BLOCK:(g + 1) * BLOCK, :].astype(o_ref.dtype)


def _attention(sink, proj, projc, bsz, cols, cos_t, sin_t):
    s_len = proj.shape[0] // bsz
    c_len = projc.shape[0] // bsz
    nb = s_len // BLOCK
    aw = ATT_HEADS * HEAD_DIM
    kw = ATT_KV_HEADS * HEAD_DIM
    qi = np.arange(BLOCK)[:, None]
    kj = np.arange(BLOCK)[None, :]
    variants = []
    for v in range(4):
        prev = np.where((kj >= qi) & bool(v & 1), 0.0, NEG)
        nxt = np.where((kj <= qi) & bool(v & 2), 0.0, NEG)
        variants.append(np.concatenate([prev, np.zeros((BLOCK, BLOCK)), nxt, np.zeros((BLOCK, c_len))], axis=1))
    bias = jnp.asarray(np.stack(variants), F32)
    ncol = 3 * BLOCK + c_len
    nstep = nb // ATT_QSUB
    tq = ATT_QSUB * BLOCK

    return pl.pallas_call(
        _attn_kernel,
        grid=(bsz, nstep),
        in_specs=[pl.BlockSpec(memory_space=pltpu.SMEM),
                  pl.BlockSpec((tq, aw), lambda b, i: (b * nstep + i, cols["q"])),
                  pl.BlockSpec((s_len, kw), lambda b, i: (b, cols["k"])),
                  pl.BlockSpec((s_len, kw), lambda b, i: (b, cols["v"])),
                  pl.BlockSpec((c_len, kw), lambda b, i: (b, cols["k"])),
                  pl.BlockSpec((c_len, kw), lambda b, i: (b, cols["v"])),
                  pl.BlockSpec((s_len, LANES), lambda b, i: (0, 0)),
                  pl.BlockSpec((s_len, LANES), lambda b, i: (0, 0)),
                  pl.BlockSpec((4, BLOCK, ncol), lambda b, i: (0, 0, 0))],
        out_specs=pl.BlockSpec((tq, aw), lambda b, i: (b * nstep + i, 0)),
        out_shape=jax.ShapeDtypeStruct((proj.shape[0], aw), BF16),
        compiler_params=_params(("parallel", "arbitrary")),
        name="window_attention",
    )(sink, proj, proj, proj, projc, projc, cos_t, sin_t, bias)


def _ctx_attn_kernel(sink_ref, q_ref, k_ref, v_ref, o_ref):
    group = ATT_HEADS // ATT_KV_HEADS
    scale = HEAD_DIM ** -0.5
    n = q_ref.shape[0]
    for h in range(ATT_KV_HEADS):
        ksl = slice(h * HEAD_DIM, (h + 1) * HEAD_DIM)
        qs, sinks = [], []
        for g in range(group):
            hh = h * group + g
            qs.append((q_ref[:, hh * HEAD_DIM:(hh + 1) * HEAD_DIM].astype(F32) * scale).astype(BF16))
            sinks.append(jnp.full((n, 1), sink_ref[hh], F32))
        q4 = jnp.concatenate(qs, axis=0)
        sk = jnp.concatenate(sinks, axis=0)
        s = _dot_nt(q4, k_ref[:, ksl])
        m = jnp.maximum(jnp.max(s, axis=-1, keepdims=True), sk)
        p = jnp.exp(s - m)
        denom = jnp.exp(sk - m) + jnp.sum(p, axis=-1, keepdims=True)
        o = _dot(p.astype(BF16), v_ref[:, ksl]) / denom
        for g in range(group):
            hh = h * group + g
            hs = slice(hh * HEAD_DIM, (hh + 1) * HEAD_DIM)
            o_ref[:, hs] = o[g * n:(g + 1) * n, :].astype(o_ref.dtype)


def _ctx_attention(sink, projc, bsz, cols):
    c_len = projc.shape[0] // bsz
    aw = ATT_HEADS * HEAD_DIM
    kw = ATT_KV_HEADS * HEAD_DIM
    return pl.pallas_call(
        _ctx_attn_kernel,
        grid=(bsz,),
        in_specs=[pl.BlockSpec(memory_space=pltpu.SMEM),
                  pl.BlockSpec((c_len, aw), lambda b: (b, cols["q"])),
                  pl.BlockSpec((c_len, kw), lambda b: (b, cols["k"])),
                  pl.BlockSpec((c_len, kw), lambda b: (b, cols["v"]))],
        out_specs=pl.BlockSpec((c_len, aw), lambda b: (b, 0)),
        out_shape=jax.ShapeDtypeStruct((projc.shape[0], aw), BF16),
        compiler_params=_params(("parallel",)),
        name="context_attention",
    )(sink, projc, projc, projc)


def _ret_kernel(dl_ref, q_ref, k_ref, v_ref, qc_ref, kc_ref, vc_ref, o_ref, oc_ref,
                ot_ref, att_ref, u_ref, st_ref, vt_ref, dm_ref):
    c = RET_CHUNK
    s_len = q_ref.shape[0]
    c_len = qc_ref.shape[0]
    lg = -_softplus(-dl_ref[0])
    lgf, lgb = lg[0:1, :], lg[1:2, :]
    wide = lambda t: jnp.concatenate([t] * (c // LANES), axis=1)
    kj = lax.broadcasted_iota(jnp.int32, (c, c), 0)
    qi = lax.broadcasted_iota(jnp.int32, (c, c), 1)
    diff = (qi - kj).astype(F32)
    dm_ref[...] = jnp.where(qi >= kj, jnp.exp(jnp.maximum(diff, 0.0) * wide(lgf)),
                            jnp.exp(jnp.maximum(-diff, 0.0) * wide(lgb)))
    idx = lax.broadcasted_iota(jnp.int32, (c, LANES), 0).astype(F32)
    qdec_f = jnp.exp((idx + 1.0) * lgf)
    kdec_f = jnp.exp((c - 1.0 - idx) * lgf)
    qdec_b = jnp.exp((c - idx) * lgb)
    kdec_b = jnp.exp(idx * lgb)
    cdec_f = jnp.exp(c * lgf)
    cdec_b = jnp.exp(c * lgb)

    def chunk(j):
        return pl.ds(pl.multiple_of(j * c, c), c)

    def transpose_v(vr, n):
        def body(j, _):
            vt_ref[:, chunk(j)] = vr[chunk(j), :].astype(F32).T.astype(BF16)
            return 0
        lax.fori_loop(0, n // c, body, 0, unroll=min(8, n // c))

    def scores(qr, kr, n):
        def body(j, _):
            att_ref[j] = (_dot_nt(kr[chunk(j), :], qr[chunk(j), :]) * dm_ref[...]).astype(BF16)
            return 0
        lax.fori_loop(0, n // c, body, 0, unroll=min(RET_UNROLL, n // c))

    def intra(kr, n):
        def body(j, _):
            vt = vt_ref[:, chunk(j)]
            ot_ref[:, chunk(j)] = _dot(vt, att_ref[j])
            kf = kr[chunk(j), :].astype(F32)
            kcat = jnp.concatenate([(kf * kdec_f).astype(BF16), (kf * kdec_b).astype(BF16)], axis=1)
            u_ref[j] = _dot(vt, kcat)
            return 0
        lax.fori_loop(0, n // c, body, 0, unroll=min(RET_UNROLL, n // c))

    def states(n, sf, sb):
        nch = n // c

        def fbody(j, s):
            st_ref[j, :, 0:RET_DK] = s.astype(BF16)
            return s * cdec_f + u_ref[j, :, 0:RET_DK]
        sf = lax.fori_loop(0, nch, fbody, sf)

        def bbody(jj, s):
            j = nch - 1 - jj
            st_ref[j, :, RET_DK:] = s.astype(BF16)
            return s * cdec_b + u_ref[j, :, RET_DK:]
        sb = lax.fori_loop(0, nch, bbody, sb)
        return sf, sb

    def cross(qr, outr, n):
        def body(j, _):
            qf = qr[chunk(j), :].astype(F32)
            qcat = jnp.concatenate([(qf * qdec_f).astype(BF16), (qf * qdec_b).astype(BF16)], axis=1)
            ot = ot_ref[:, chunk(j)] + _dot_nt(st_ref[j], qcat)
            o = ot.T * (RET_DK ** -0.5)
            outr[chunk(j), :] = (o * lax.rsqrt(jnp.mean(o * o, axis=-1, keepdims=True) + EPS)).astype(outr.dtype)
            return 0
        lax.fori_loop(0, n // c, body, 0, unroll=min(RET_UNROLL, n // c))

    def run(qr, kr, vr, outr, n, sf, sb):
        transpose_v(vr, n)
        scores(qr, kr, n)
        intra(kr, n)
        finals = states(n, sf, sb)
        cross(qr, outr, n)
        return finals

    zero = jnp.zeros((LANES, RET_DK), F32)
    sf, sb = run(qc_ref, kc_ref, vc_ref, oc_ref, c_len, zero, zero)
    run(q_ref, k_ref, v_ref, o_ref, s_len, sf, sb)


def _retention(dl, proj, projc, bsz, cols):
    s_len = proj.shape[0] // bsz
    c_len = projc.shape[0] // bsz
    hb = lambda n, col: pl.BlockSpec((n, LANES), lambda b, h: (b, col + h))
    return pl.pallas_call(
        _ret_kernel,
        grid=(bsz, RET_HEADS),
        in_specs=[pl.BlockSpec((1, 2, LANES), lambda b, h: (h, 0, 0)),
                  hb(s_len, cols["q"]), hb(s_len, cols["k"]), hb(s_len, cols["v"]),
                  hb(c_len, cols["q"]), hb(c_len, cols["k"]), hb(c_len, cols["v"])],
        out_specs=[pl.BlockSpec((s_len, LANES), lambda b, h: (b, h)),
                   pl.BlockSpec((c_len, LANES), lambda b, h: (b, h))],
        out_shape=[jax.ShapeDtypeStruct((proj.shape[0], RET_HEADS * LANES), BF16),
                   jax.ShapeDtypeStruct((projc.shape[0], RET_HEADS * LANES), BF16)],
        scratch_shapes=[pltpu.VMEM((LANES, s_len), F32),
                        pltpu.VMEM((s_len // RET_CHUNK, RET_CHUNK, RET_CHUNK), BF16),
                        pltpu.VMEM((s_len // RET_CHUNK, LANES, 2 * RET_DK), F32),
                        pltpu.VMEM((s_len // RET_CHUNK, LANES, 2 * RET_DK), BF16),
                        pltpu.VMEM((LANES, s_len), BF16),
                        pltpu.VMEM((RET_CHUNK, RET_CHUNK), F32)],
        compiler_params=_params(("parallel", "parallel")),
        name="retention",
    )(dl, proj, proj, proj, projc, projc, projc)


def _hy_hid_kernel(z_ref, w1_ref, b1_ref, w2_ref, b2_ref, f_ref, o_ref):
    f = f_ref[...]
    h = jnp.sin(f * (_dot(z_ref[...], w1_ref[...], precision=HIGHEST) + b1_ref[...]))
    o_ref[...] = jnp.sin(f * (_dot(h, w2_ref[...], precision=HIGHEST) + b2_ref[...]))


def _hy_hidden(zfull, w1p, b1, w2, b2, freq):
    n, e = zfull.shape
    fd = w2.shape[0]
    tr = min(n, 1024)
    full = lambda shp: pl.BlockSpec(shp, lambda i: (0, 0))
    return pl.pallas_call(
        _hy_hid_kernel,
        grid=(n // tr,),
        in_specs=[pl.BlockSpec((tr, e), lambda i: (i, 0)), full((e, fd)), full((1, fd)), full((fd, fd)),
                  full((1, fd)), full((1, fd))],
        out_specs=pl.BlockSpec((tr, fd), lambda i: (i, 0)),
        out_shape=jax.ShapeDtypeStruct((n, fd), F32),
        compiler_params=_params(("parallel",)),
        name="hyena_filter_mlp",
    )(zfull, w1p, b1, w2, b2, freq)


def _hy_filt_kernel(hid_ref, w3f_ref, w3b_ref, tn_ref, dl_ref, o_ref):
    half = hid_ref.shape[0] // 2
    decay = jnp.exp(-tn_ref[...] * jnp.abs(dl_ref[...]))
    top = _dot(hid_ref[0:half, :], w3f_ref[...], precision=HIGHEST)
    bot = _dot(hid_ref[half:, :], w3b_ref[...], precision=HIGHEST)
    row = lax.broadcasted_iota(jnp.int32, bot.shape, 0)
    bot = jnp.where(row == 0, 0.0, bot)
    o_ref[0, 0:half, :] = top * decay[0:half, :]
    o_ref[0, half:, :] = bot * decay[half:, :]


def _hy_filters(hid, w3, tn_full, deltas):
    n, fd = hid.shape
    wch = deltas.shape[1]
    nsl = wch // LANES
    return pl.pallas_call(
        _hy_filt_kernel,
        grid=(HY_ORDER, nsl),
        in_specs=[pl.BlockSpec((n, fd), lambda o, j: (0, 0)),
                  pl.BlockSpec((fd, LANES), lambda o, j: (0, o * 2 * nsl + j)),
                  pl.BlockSpec((fd, LANES), lambda o, j: (0, o * 2 * nsl + nsl + j)),
                  pl.BlockSpec((n, LANES), lambda o, j: (0, 0)),
                  pl.BlockSpec((1, LANES), lambda o, j: (0, j))],
        out_specs=pl.BlockSpec((1, n, LANES), lambda o, j: (o, 0, j)),
        out_shape=jax.ShapeDtypeStruct((HY_ORDER, n, wch), F32),
        compiler_params=_params(("parallel", "parallel")),
        name="hyena_filter",
    )(hid, w3, w3, tn_full, deltas)


def _hy_kfft_kernel(kern_ref, f1_ref, f2_ref, o_ref, g_ref, *, n1, n2):
    pg = 2 * n1 + HY_PITCH_PAD

    def stage1(i2, _):
        x = kern_ref[pl.ds(i2, n1, stride=n2), :].astype(BF16)
        g_ref[pl.ds(pl.multiple_of(i2 * pg, 8), 2 * n1), :] = _dot(f1_ref[i2], x)
        return 0
    lax.fori_loop(0, n2, stage1, 0, unroll=HY_UNROLL)

    def stage2(k1, _):
        x = jnp.concatenate([g_ref[pl.ds(k1, n2, stride=pg), :],
                             g_ref[pl.ds(n1 + k1, n2, stride=pg), :]], axis=0).astype(BF16)
        o_ref[pl.ds(pl.multiple_of(k1 * 2 * n2, 2 * n2), 2 * n2), :] = _dot(f2_ref[...], x)
        return 0
    lax.fori_loop(0, n1, stage2, 0, unroll=HY_UNROLL // 2)


def _hy_kfft(kern, f1k, f2, n1, n2):
    orders, n, wch = kern.shape
    nsl = wch // LANES
    return pl.pallas_call(
        functools.partial(_hy_kfft_kernel, n1=n1, n2=n2),
        grid=(orders, nsl),
        in_specs=[pl.BlockSpec((None, n, LANES), lambda o, j: (o, 0, j)),
                  pl.BlockSpec(f1k.shape, lambda o, j: (0, 0, 0)),
                  pl.BlockSpec(f2.shape, lambda o, j: (0, 0))],
        out_specs=pl.BlockSpec((None, None, 2 * n, LANES), lambda o, j: (o, j, 0, 0)),
        out_shape=jax.ShapeDtypeStruct((orders, nsl, 2 * n, LANES), F32),
        scratch_shapes=[pltpu.VMEM((n2 * (2 * n1 + HY_PITCH_PAD), LANES), F32)],
        compiler_params=_params(("parallel", "parallel")),
        name="hyena_filter_fft",
    )(kern, f1k, f2)


def _hy_prep_kernel(z_ref, w_ref, b_ref, o_ref, zp_ref, *, n2, nseq):
    l_len = z_ref.shape[0] // nseq
    h1 = l_len // n2
    halo = HY_HALO
    pitch = n2 + 2 * halo + 8
    zeros_h = jnp.zeros((halo, LANES), F32)
    for s in range(nseq):
        for i1 in range(h1):
            base = (s * h1 + i1) * pitch
            lo, hi = i1 * n2 - halo, (i1 + 1) * n2 + halo
            if lo < 0:
                zp_ref[base:base + halo, :] = zeros_h
            if hi > l_len:
                zp_ref[base + n2 + halo:base + n2 + 2 * halo, :] = zeros_h
            lo_c, hi_c = max(lo, 0), min(hi, l_len)
            zp_ref[base + (lo_c - lo):base + (hi_c - lo), :] = (
                z_ref[s * l_len + lo_c:s * l_len + hi_c, :].astype(F32))
    w0, w1, w2, b = w_ref[0:1, :], w_ref[1:2, :], w_ref[2:3, :], b_ref[...]

    def tap(j):
        return zp_ref[pl.ds(halo + j, nseq * h1, stride=pitch), :]

    def body(i2, carry):
        zm, z0 = carry
        zn = tap(i2 + 1)
        val = (b + w0 * zm + w1 * z0 + w2 * zn).astype(o_ref.dtype)
        for s in range(nseq):
            o_ref[pl.ds(pl.multiple_of(s * l_len + i2 * h1, h1), h1), :] = val[s * h1:(s + 1) * h1, :]
        return z0, zn
    lax.fori_loop(0, n2, body, (tap(-1), tap(0)), unroll=HY_UNROLL if nseq <= 2 else 1)


def _hy_stage_dtype(l_len, n2):
    return BF16 if (l_len // n2) % 16 == 0 else F32


def _hy_prep(proj, z_col, nz, bsz, conv_w, conv_b, n2):
    l_len = proj.shape[0] // bsz
    h1 = l_len // n2
    nseq = max(1, min(bsz, HY_PREP_ROWS // l_len))
    while bsz % nseq:
        nseq -= 1
    return pl.pallas_call(
        functools.partial(_hy_prep_kernel, n2=n2, nseq=nseq),
        grid=(bsz // nseq, nz),
        in_specs=[pl.BlockSpec((nseq * l_len, LANES), lambda b, j: (b, z_col + j)),
                  pl.BlockSpec((3, LANES), lambda b, j: (0, j)),
                  pl.BlockSpec((1, LANES), lambda b, j: (0, j))],
        out_specs=pl.BlockSpec((nseq * l_len, LANES), lambda b, j: (b, j)),
        out_shape=jax.ShapeDtypeStruct((proj.shape[0], nz * LANES), _hy_stage_dtype(l_len, n2)),
        scratch_shapes=[pltpu.VMEM((nseq * h1 * (n2 + 2 * HY_HALO + 8), LANES), F32)],
        compiler_params=_params(("parallel", "parallel")),
        name="hyena_short_conv",
    )(proj, conv_w, conv_b)


def _pack2(re, im):
    r = lax.bitcast_convert_type(re, jnp.uint32) + jnp.uint32(0x8000)
    i = lax.bitcast_convert_type(im, jnp.uint32) + jnp.uint32(0x8000)
    return (r & jnp.uint32(0xFFFF0000)) | (i >> 16)


def _unpack2(w):
    re = lax.bitcast_convert_type(w & jnp.uint32(0xFFFF0000), F32)
    im = lax.bitcast_convert_type(w << 16, F32)
    return jnp.concatenate([re, im], axis=0).astype(BF16)


def _hy_conv_kernel(y_ref, x_ref, kh_ref, bias_ref, f1_ref, f2_ref, f2i_ref, f1i_ref, o_ref, g_ref,
                    *, n1, n2, natural_out):
    h1 = n1 // 2
    l_len = h1 * n2
    pg = n1 + HY_PITCH_PAD
    bias = bias_ref[...]
    unroll = min(HY_UNROLL, n1)

    def rows(seq, i2):
        return pl.ds(pl.multiple_of(seq * l_len + i2 * h1, h1), h1)

    def lanes2(a, b):
        return jnp.concatenate([a, b], axis=1)

    def stage1(i2, _):
        x = lanes2(jnp.concatenate([y_ref[rows(0, i2), :], y_ref[rows(1, i2), :]], axis=0),
                   jnp.concatenate([y_ref[rows(2, i2), :], y_ref[rows(3, i2), :]], axis=0)).astype(BF16)
        a = _dot(f1_ref[i2], x)
        w = _pack2(a[0:n1, :], a[n1:, :])
        blk = pl.ds(pl.multiple_of(i2 * pg, 8), n1)
        g_ref[0, blk, :] = w[:, 0:LANES]
        g_ref[1, blk, :] = w[:, LANES:]
        return 0
    lax.fori_loop(0, n2, stage1, 0, unroll=2 * unroll)

    def stage23(t, _):
        outs = []
        for u in range(unroll):
            k1 = t * unroll + u
            col = pl.ds(k1, n2, stride=pg)
            x = lanes2(_unpack2(g_ref[0, col, :]), _unpack2(g_ref[1, col, :]))
            yh = _dot(f2_ref[...], x)
            base = pl.multiple_of(k1 * 2 * n2, 2 * n2)
            kr = kh_ref[pl.ds(base, n2), :]
            ki = kh_ref[pl.ds(base + n2, n2), :]
            kr, ki = lanes2(kr, kr), lanes2(ki, ki)
            yr, yi = yh[0:n2, :], yh[n2:, :]
            z = jnp.concatenate([yr * kr - yi * ki, yr * ki + yi * kr], axis=0).astype(BF16)
            c = _dot(f2i_ref[...], z)
            outs.append((col, _pack2(c[0:n2, :], c[n2:, :])))
        for col, w in outs:
            g_ref[0, col, :] = w[:, 0:LANES]
            g_ref[1, col, :] = w[:, LANES:]
        return 0
    lax.fori_loop(0, n1 // unroll, stage23, 0)

    def stage4(i2, _):
        blk = pl.ds(pl.multiple_of(i2 * pg, 8), n1)
        x = lanes2(_unpack2(g_ref[0, blk, :]), _unpack2(g_ref[1, blk, :]))
        yc = _dot(f1i_ref[i2], x)
        for seq in range(4):
            pair, half = seq // 2, seq % 2
            sl = rows(seq, i2)
            conv = yc[half * h1:(half + 1) * h1, pair * LANES:(pair + 1) * LANES]
            val = x_ref[sl, :].astype(F32) * (conv + bias * y_ref[sl, :].astype(F32))
            if natural_out:
                o_ref[pl.ds(seq * l_len + i2, h1, stride=n2), :] = val
            else:
                o_ref[sl, :] = val.astype(o_ref.dtype)
        return 0
    lax.fori_loop(0, n2, stage4, 0, unroll=unroll)


def _hy_conv(ysrc, ycol, xsrc, xcol, khat, order, bias, mats, bsz, n1, n2, natural_out):
    l_len = ysrc.shape[0] // bsz
    nsl = khat.shape[1]
    f1, f2, f2i, f1i = mats
    once = pl.Buffered(1)
    cst2 = lambda a: pl.BlockSpec(a.shape, lambda j, p: (0, 0), pipeline_mode=once)
    cst3 = lambda a: pl.BlockSpec(a.shape, lambda j, p: (0, 0, 0), pipeline_mode=once)
    return pl.pallas_call(
        functools.partial(_hy_conv_kernel, n1=n1, n2=n2, natural_out=natural_out),
        grid=(nsl, bsz // 4),
        in_specs=[pl.BlockSpec((4 * l_len, LANES), lambda j, p: (p, ycol + j)),
                  pl.BlockSpec((4 * l_len, LANES), lambda j, p: (p, xcol + j)),
                  pl.BlockSpec((None, None, khat.shape[2], LANES), lambda j, p: (order, j, 0, 0),
                               pipeline_mode=once),
                  pl.BlockSpec((1, LANES), lambda j, p: (0, j)),
                  cst3(f1), cst2(f2), cst2(f2i), cst3(f1i)],
        out_specs=pl.BlockSpec((4 * l_len, LANES), lambda j, p: (p, j)),
        out_shape=jax.ShapeDtypeStruct((ysrc.shape[0], nsl * LANES),
                                       F32 if natural_out else _hy_stage_dtype(l_len, n2)),
        scratch_shapes=[pltpu.VMEM((2, n2 * (n1 + HY_PITCH_PAD), LANES), jnp.uint32)],
        compiler_params=_params(("parallel", "arbitrary")),
        name="hyena_long_conv",
    )(ysrc, xsrc, khat, bias[order][None], f1, f2, f2i, f1i)


def _dft_tables(l_len):
    n = 2 * l_len
    n2 = 128 if l_len >= 1024 else 32
    n1 = n // n2
    h1 = n1 // 2
    i1 = np.arange(n1)[None, None, :]
    k1 = np.arange(n1)[None, :, None]
    i2 = np.arange(n2)[:, None, None]
    ph = 2 * np.pi * (i1 * k1 / n1 + i2 * k1 / n)
    c, s = np.cos(ph), np.sin(ph)
    ch, sh = c[:, :, :h1], s[:, :, :h1]
    f1 = np.concatenate([np.concatenate([ch, sh], 2), np.concatenate([-sh, ch], 2)], 1)
    f1k = np.concatenate([c, -s], 1)
    ct, st = np.swapaxes(ch, 1, 2), np.swapaxes(sh, 1, 2)
    f1i = np.concatenate([np.concatenate([ct, -st], 2), np.concatenate([st, ct], 2)], 1) / n
    a = np.arange(n2)
    ph2 = 2 * np.pi * np.outer(a, a) / n2
    c2, s2 = np.cos(ph2), np.sin(ph2)
    f2 = np.block([[c2, s2], [-s2, c2]])
    f2i = np.block([[c2, -s2], [s2, c2]])
    bf = lambda m: jnp.asarray(m, dtype=F32).astype(BF16)
    return n1, n2, (bf(f1), bf(f2), bf(f2i), bf(f1i)), bf(f1k)


def _filter_positions(l_len, width):
    lag = np.concatenate([np.arange(l_len), l_len - np.arange(l_len)]).astype(np.float64)
    t = lag / (l_len - 1)
    bands = (HY_EMB - 1) // 2
    w = 2.0 * np.pi * lag / l_len
    f = np.linspace(1e-4, bands - 1, bands)[None]
    z = np.concatenate([t[:, None], np.cos(f * w[:, None]), -np.sin(f * w[:, None])], axis=-1)
    z = np.pad(z, ((0, 0), (0, HY_EMB_PAD - HY_EMB)))
    tn = np.repeat(t[:, None], LANES, axis=1)
    max_decay = math.log(HY_DECAY_TARGET) / HY_FAST_PCT
    min_decay = math.log(HY_DECAY_TARGET) / HY_SLOW_PCT
    deltas = np.linspace(min_decay, max_decay, width)[None]
    return jnp.asarray(z, F32), jnp.asarray(tn, F32), jnp.asarray(deltas, F32)


def _rope_tables(seq):
    n_rows = seq // GRID_W
    row = np.repeat(np.arange(n_rows), GRID_W).astype(np.float64)
    col = np.tile(np.arange(GRID_W), n_rows).astype(np.float64)
    half = HEAD_DIM // 2
    inv = ROPE_BASE ** (-np.arange(0, half, 2, dtype=np.float64) / half)
    ar, ac = row[:, None] * inv, col[:, None] * inv
    cos = np.concatenate([np.cos(ar), np.cos(ar), np.cos(ac), np.cos(ac)], axis=1)
    sin = np.concatenate([-np.sin(ar), np.sin(ar), -np.sin(ac), np.sin(ac)], axis=1)
    return jnp.asarray(cos, F32), jnp.asarray(sin, F32)


def _hyena_spectra(l_len, width, w1, b1, w2, b2, w3, freq, tables):
    n1, n2, mats, f1k = tables
    zfull, tn_full, deltas = _filter_positions(l_len, width)
    w1p = jnp.pad(w1, ((0, HY_EMB_PAD - HY_EMB), (0, 0)))
    hid = _hy_hidden(zfull, w1p, b1[None], w2, b2[None], freq[None])
    kern = _hy_filters(hid, w3, tn_full, deltas)
    return _hy_kfft(kern, f1k, mats[1], n1, n2)


def _hyena_seq(zc, bsz, khat, bias, tables):
    n1, n2, mats, _ = tables
    nsl = khat.shape[1]
    y1 = _hy_conv(zc, 0, zc, nsl, khat, 0, bias, mats, bsz, n1, n2, False)
    return _hy_conv(y1, 0, zc, 2 * nsl, khat, 1, bias, mats, bsz, n1, n2, True)


def kernel(x, c, ctx, c_ctx, mod_w, mod_b, norm_pre, norm_post, ev_w_in, ev_w_out, lru_conv_w, lru_conv_b, lru_wa, lru_ba, lru_wx, lru_bx, lru_lambda, attn_sink, od_w_in, od_w_out, hy_conv_w, hy_conv_b, hy_w1, hy_b1, hy_w2, hy_b2, hy_w3, hy_freq, hy_bias, ret_decay_logit):
    bsz, s_len, d = x.shape
    c_len = ctx.shape[1]
    depth = mod_w.shape[0]
    assert bsz % 4 == 0 and bsz <= 16 and s_len % 1024 == 0 and c_len % 256 == 0 and c_len <= s_len

    crows = jnp.concatenate([c, c_ctx[None], jnp.zeros((24 - bsz - 1, d), F32)], axis=0)
    mod = _modulation(crows, mod_w, mod_b)

    cos_t, sin_t = _rope_tables(s_len)
    tab_l = _dft_tables(s_len)
    tab_c = _dft_tables(c_len)

    x2 = x.reshape(bsz * s_len, d)
    ctx2 = ctx.reshape(bsz * c_len, d)
    tm = ROW_TILE

    for l in range(depth):
        need_ctx = l < depth - 1
        shift, scale, gate = (mod[l, :bsz, i * d:(i + 1) * d].reshape(bsz, 1, d) for i in range(3))
        shift_c, scale_c, gate_c = (mod[l, bsz:bsz + 1, i * d:(i + 1) * d].reshape(1, 1, d) for i in range(3))
        g_pre = norm_pre[l][None]
        g_post = norm_post[l][None]

        def proj(w, tn):
            wb = w.astype(BF16)
            p = _in_proj(x2, g_pre, scale, shift, s_len, wb, BF16, tm, tn)
            pc = _in_proj(ctx2, g_pre, scale_c, shift_c, bsz * c_len, wb, BF16, tm, tn)
            return p, pc

        if l % 2 == 0:
            e = l // 2
            w_in = ev_w_in[e]
            w_ord = jnp.concatenate([w_in[:, 1024:2048], w_in[:, 2048:3072], w_in[:, 3584:4608],
                                     w_in[:, 3072:3584], w_in[:, 0:1024]], axis=1)
            pr, prc = proj(w_ord, EVEN_COL_TILE)
            cols = {"q": 1, "gb": 2, "k": 12, "v": 13}
            xa_col = 3584 // LANES
            nblk = lru_wa.shape[2]
            wg = jnp.stack([lru_wa[e, 0], lru_wx[e, 0], lru_wa[e, 1], lru_wx[e, 1]], axis=1)
            wg = jnp.transpose(wg, (0, 2, 1, 3)).reshape(nblk, LRU_BLOCK_W, 4 * LRU_BLOCK_W)
            bg = jnp.stack([lru_ba[e, 0], lru_bx[e, 0], lru_ba[e, 1], lru_bx[e, 1]], axis=0)
            bg = jnp.transpose(bg.reshape(4, nblk, LRU_BLOCK_W), (1, 0, 2)).reshape(nblk, 1, 4 * LRU_BLOCK_W)
            wg, bg = (0.5 * wg).astype(BF16), 0.5 * bg
            ya, yac = _lru(pr, prc, xa_col, bsz, lru_conv_w[e], lru_conv_b[e][None], wg, bg, lru_lambda[e])
            yb = _attention(attn_sink[e], pr, prc, bsz, cols, cos_t, sin_t)
            w_out = ev_w_out[e].astype(BF16)
            x2_new = _out_proj(ya, pr, 0, yb, pr, cols["gb"], w_out, x2, g_post, gate, s_len, tm)
            if need_ctx:
                ybc = _ctx_attention(attn_sink[e], prc, bsz, cols)
                ctx2 = _out_proj(yac, prc, 0, ybc, prc, cols["gb"], w_out, ctx2, g_post, gate_c, bsz * c_len, tm)
            x2 = x2_new
        else:
            o = l // 2
            w_in = od_w_in[o]
            w_ord = jnp.concatenate([w_in[:, 3072:8192], w_in[:, 0:3072]], axis=1)
            pr, prc = proj(w_ord, ODD_COL_TILE)
            cols = {"q": 8, "k": 16, "v": 24}
            z_col = 5120 // LANES
            width = hy_bias.shape[2]
            nz = (HY_ORDER + 1) * width // LANES
            fargs = (hy_w1[o], hy_b1[o], hy_w2[o], hy_b2[o], hy_w3[o], hy_freq[o])
            khat = _hyena_spectra(s_len, width, *fargs, tab_l)
            zconv = _hy_prep(pr, z_col, nz, bsz, hy_conv_w[o], hy_conv_b[o][None], tab_l[1])
            yh = _hyena_seq(zconv, bsz, khat, hy_bias[o], tab_l)
            dl = jnp.broadcast_to(jnp.transpose(ret_decay_logit[o])[:, :, None], (RET_HEADS, 2, LANES))
            yd, ydc = _retention(dl, pr, prc, bsz, cols)
            w_out = od_w_out[o].astype(BF16)
            x2_new = _out_proj(yh, pr, 0, yd, pr, 4, w_out, x2, g_post, gate, s_len, tm)
            if need_ctx:
                khat_c = _hyena_spectra(c_len, width, *fargs, tab_c)
                zcconv = _hy_prep(prc, z_col, nz, bsz, hy_conv_w[o], hy_conv_b[o][None], tab_c[1])
                yhc = _hyena_seq(zcconv, bsz, khat_c, hy_bias[o], tab_c)
                ctx2 = _out_proj(yhc, prc, 0, ydc, prc, 4, w_out, ctx2, g_post, gate_c, bsz * c_len, tm)
            x2 = x2_new
    return x2.reshape(bsz, s_len, d)
```

```python
import functools
import math

import numpy as np
import jax
import jax.numpy as jnp
from jax import lax
from jax.experimental import pallas as pl
from jax.experimental.pallas import tpu as pltpu

F32 = jnp.float32
BF16 = jnp.bfloat16
HIGHEST = lax.Precision.HIGHEST

EPS = 1e-6
GRID_W = 64
LANES = 128
LRU_BLOCK_W = 128
LRU_CONV = 4
LRU_C = 8.0
LRU_SEG = 16
LRU_POS = 64
LRU_TINY = 1e-30
LRU_HALO = 16
ATT_HEADS = 8
ATT_KV_HEADS = 2
HEAD_DIM = 128
BLOCK = 128
ATT_QSUB = 4
ATT_HSUB = 4
ROPE_BASE = 10000.0
HY_ORDER = 2
HY_EMB = 33
HY_EMB_PAD = 40
HY_DECAY_TARGET = 1e-2
HY_FAST_PCT = 0.3
HY_SLOW_PCT = 1.5
HY_UNROLL = 32
HY_PREP_ROWS = 16384
HY_HALO = 16
HY_PITCH_PAD = 8
RET_HEADS = 8
RET_DK = 128
RET_CHUNK = 256
RET_UNROLL = 8

V7X_VMEM_BYTES = 64 * 1024 * 1024
VMEM_LIMIT = V7X_VMEM_BYTES - 8 * 1024 * 1024
ROW_TILE = 1024
IN_ROW_TILE = 2048
EVEN_COL_TILE = 1536
ODD_COL_TILE = 2048
NEG = -1e30


def _params(sem, vmem=VMEM_LIMIT, **kw):
    return pltpu.CompilerParams(dimension_semantics=sem, vmem_limit_bytes=vmem, **kw)


def _sigmoid(v):
    return 0.5 * (jnp.tanh(0.5 * v) + 1.0)


def _silu(v):
    return v * _sigmoid(v)


def _softplus(v):
    return jnp.maximum(v, 0.0) + jnp.log(1.0 + jnp.exp(-jnp.abs(v)))


def _dot(a, b, **kw):
    return jnp.dot(a, b, preferred_element_type=F32, **kw)


def _dot_nt(a, b):
    return lax.dot_general(a, b, (((1,), (1,)), ((), ())), preferred_element_type=F32)


def _mod_kernel(c_ref, w_ref, b_ref, o_ref):
    s = _silu(c_ref[...])
    o_ref[0] = _dot(s, w_ref[0], precision=HIGHEST) + b_ref[0]


def _modulation(crows, mod_w, mod_b):
    depth, d, n3 = mod_w.shape
    r = crows.shape[0]
    tn = 1024
    return pl.pallas_call(
        _mod_kernel,
        grid=(depth, n3 // tn),
        in_specs=[pl.BlockSpec((r, d), lambda l, j: (0, 0)),
                  pl.BlockSpec((1, d, tn), lambda l, j: (l, 0, j)),
                  pl.BlockSpec((1, 1, tn), lambda l, j: (l, 0, j))],
        out_specs=pl.BlockSpec((1, r, tn), lambda l, j: (l, 0, j)),
        out_shape=jax.ShapeDtypeStruct((depth, r, n3), F32),
        compiler_params=_params(("parallel", "parallel")),
        name="modulation",
    )(crows, mod_w, mod_b.reshape(depth, 1, n3))


def _rope(v, cos, sin, lane):
    swapped = jnp.where((lane & 63) < 32, pltpu.roll(v, LANES - 32, 1), pltpu.roll(v, 32, 1))
    return v * cos + swapped * sin


def _inproj_kernel(x_ref, g_ref, sc_ref, sh_ref, w_ref, o_ref, h_ref):
    @pl.when(pl.program_id(1) == 0)
    def _():
        x = x_ref[...]
        y = x * lax.rsqrt(jnp.mean(x * x, axis=-1, keepdims=True) + EPS) * g_ref[...]
        h_ref[...] = (y * (1.0 + sc_ref[0]) + sh_ref[0]).astype(BF16)

    o_ref[...] = _dot(h_ref[...], w_ref[...]).astype(o_ref.dtype)


def _in_proj(x2, g, scale, shift, rows_per_group, w, out_dtype, tm, tn):
    m, d = x2.shape
    n = w.shape[1]
    tpg = rows_per_group // tm
    return pl.pallas_call(
        _inproj_kernel,
        grid=(m // tm, n // tn),
        in_specs=[pl.BlockSpec((tm, d), lambda i, j: (i, 0)),
                  pl.BlockSpec((1, d), lambda i, j: (0, 0)),
                  pl.BlockSpec((1, 1, d), lambda i, j: (i // tpg, 0, 0)),
                  pl.BlockSpec((1, 1, d), lambda i, j: (i // tpg, 0, 0)),
                  pl.BlockSpec((d, tn), lambda i, j: (0, j))],
        out_specs=pl.BlockSpec((tm, tn), lambda i, j: (i, j)),
        out_shape=jax.ShapeDtypeStruct((m, n), out_dtype),
        scratch_shapes=[pltpu.VMEM((tm, d), BF16)],
        compiler_params=_params(("parallel", "arbitrary")),
        name="in_proj",
    )(x2, g, scale, shift, w)


def _outproj_kernel(a_ref, ga_ref, b_ref, gb_ref, w_ref, x_ref, g_ref, gate_ref, o_ref):
    wa = w_ref.shape[0] // 2
    a = (a_ref[...].astype(F32) * _silu(ga_ref[...].astype(F32))).astype(BF16)
    b = (b_ref[...].astype(F32) * _silu(gb_ref[...].astype(F32))).astype(BF16)
    y = _dot(a, w_ref[0:wa, :]) + _dot(b, w_ref[wa:, :])
    yn = y * lax.rsqrt(jnp.mean(y * y, axis=-1, keepdims=True) + EPS) * g_ref[...]
    o_ref[...] = x_ref[...] + gate_ref[0] * yn


def _out_proj(a, ga, ga_col, b, gb, gb_col, w, x2, g, gate, rows_per_group, tm):
    m, d = x2.shape
    wa = w.shape[0] // 2
    tpg = rows_per_group // tm
    return pl.pallas_call(
        _outproj_kernel,
        grid=(m // tm,),
        in_specs=[pl.BlockSpec((tm, wa), lambda i: (i, 0)),
                  pl.BlockSpec((tm, wa), lambda i: (i, ga_col)),
                  pl.BlockSpec((tm, wa), lambda i: (i, 0)),
                  pl.BlockSpec((tm, wa), lambda i: (i, gb_col)),
                  pl.BlockSpec(w.shape, lambda i: (0, 0)),
                  pl.BlockSpec((tm, d), lambda i: (i, 0)),
                  pl.BlockSpec((1, d), lambda i: (0, 0)),
                  pl.BlockSpec((1, 1, d), lambda i: (i // tpg, 0, 0))],
        out_specs=pl.BlockSpec((tm, d), lambda i: (i, 0)),
        out_shape=jax.ShapeDtypeStruct((m, d), F32),
        compiler_params=_params(("parallel",)),
        name="out_proj",
    )(a, ga, b, gb, w, x2, g, gate)


def _lru_kernel(xa_ref, xac_ref, cw_ref, cb_ref, wg_ref, bg_ref, lam_ref, y_ref, yc_ref,
                xp_ref, hf_ref, pf_ref, hb_ref, pb_ref):
    nseg = LRU_SEG
    c8h =-0.5 * LRU_C * _softplus(-lam_ref[...])
    left = LRU_CONV // 2
    cw = [cw_ref[k:k + 1, :] for k in range(LRU_CONV)]
    cb = cb_ref[...]
    zeros_h = jnp.zeros((LRU_HALO, LANES), F32)
    zeros_s = jnp.zeros((nseg, LANES), F32)
    ones_s = jnp.ones((nseg, LANES), F32)

    def run(src_ref, out_ref, n, c0f, c0b):
        seg = n // nseg
        npos = min(LRU_POS, seg)
        nblk = seg // npos
        halo = LRU_HALO
        pitch = seg + 2 * halo + 8
        for j in range(nseg):
            lo, hi = j * seg - halo, (j + 1) * seg + halo
            if lo < 0:
                xp_ref[j * pitch:j * pitch + halo, :] = zeros_h
            if hi > n:
                xp_ref[j * pitch + seg + halo:j * pitch + seg + 2 * halo, :] = zeros_h
            lo_c, hi_c = max(lo, 0), min(hi, n)
            xp_ref[j * pitch + (lo_c - lo):j * pitch + (hi_c - lo), :] = src_ref[lo_c:hi_c, :].astype(F32)

        def gather(q):
            return jnp.concatenate([xp_ref[pl.ds(halo + q + c * 8 * pitch, 8, stride=pitch), :]
                                    for c in range(nseg // 8)], axis=0)

        def blk(i):
            return pl.ds(pl.multiple_of(i * npos * nseg, npos * nseg), npos * nseg)

        def pos(v, p):
            return v[p * nseg:(p + 1) * nseg, :]

        def fwd_body(i, carry):
            h, pc = carry
            p0 = i * npos
            xs = [gather(p0 + q - left) for q in range(npos + LRU_CONV - 1)]
            us = []
            for p in range(npos):
                u = cb + cw[0] * xs[p]
                for k in range(1, LRU_CONV):
                    u = u + cw[k] * xs[p + k]
                us.append(u)
            u = jnp.concatenate(us, axis=0)
            g = _dot(u.astype(BF16), wg_ref[0]) + bg_ref[0]
            hu = 0.5 * u
            coef = []
            for d in range(2):
                tr = jnp.tanh(g[:, (2 * d) * LANES:(2 * d + 1) * LANES])
                ti = jnp.tanh(g[:, (2 * d + 1) * LANES:(2 * d + 2) * LANES])
                a = jnp.exp(c8h[d:d + 1, :] * tr + c8h[d:d + 1, :])
                om = 1.0 - a * a
                coef.append((a, om * lax.rsqrt(jnp.maximum(om, LRU_TINY)) * ((ti + 1.0) * hu)))
            (af, bf), (ab, bb) = coef
            hb_ref[blk(i), :] = ab
            pb_ref[blk(i), :] = bb
            hs, ps = [], []
            for p in range(npos):
                a_p = pos(af, p)
                h = a_p * h + pos(bf, p)
                pc = a_p * pc
                hs.append(h)
                ps.append(pc)
            hf_ref[blk(i), :] = jnp.concatenate(hs, axis=0)
            pf_ref[blk(i), :] = jnp.concatenate(ps, axis=0)
            return h, pc

        hef, pef = lax.fori_loop(0, nblk, fwd_body, (zeros_s, ones_s))

        def bwd_body(ii, carry):
            h, pc = carry
            i = nblk - 1 - ii
            ab = hb_ref[blk(i), :]
            bb = pb_ref[blk(i), :]
            hs, ps = [None] * npos, [None] * npos
            for p in reversed(range(npos)):
                a_p = pos(ab, p)
                h = a_p * h + pos(bb, p)
                pc = a_p * pc
                hs[p] = h
                ps[p] = pc
            hb_ref[blk(i), :] = jnp.concatenate(hs, axis=0)
            pb_ref[blk(i), :] = jnp.concatenate(ps, axis=0)
            return h, pc

        heb, peb = lax.fori_loop(0, nblk, bwd_body, (zeros_s, ones_s))

        rows_f, c = [], c0f
        for j in range(nseg):
            rows_f.append(c)
            c = hef[j:j + 1, :] + pef[j:j + 1, :] * c
        final_f = c
        rows_b, c = [None] * nseg, c0b
        for j in reversed(range(nseg)):
            rows_b[j] = c
            c = heb[j:j + 1, :] + peb[j:j + 1, :] * c
        final_b = c
        cf = jnp.concatenate(rows_f, axis=0)
        cbk = jnp.concatenate(rows_b, axis=0)

        opitch = seg + 8

        def out_body(p, _):
            rs = pl.ds(pl.multiple_of(p * nseg, nseg), nseg)
            y = hf_ref[rs, :] + pf_ref[rs, :] * cf + hb_ref[rs, :] + pb_ref[rs, :] * cbk
            for c in range(nseg // 8):
                xp_ref[pl.ds(p + c * 8 * opitch, 8, stride=opitch), :] = y[c * 8:(c + 1) * 8, :]
            return 0
        lax.fori_loop(0, seg, out_body, 0, unroll=8)
        for j in range(nseg):
            out_ref[j * seg:(j + 1) * seg, :] = xp_ref[j * opitch:j * opitch + seg, :].astype(out_ref.dtype)
        return final_f, final_b

    zero = jnp.zeros((1, LANES), F32)
    ff, fb = run(xac_ref, yc_ref, xac_ref.shape[0], zero, zero)
    run(xa_ref, y_ref, xa_ref.shape[0], ff, fb)


def _lru(proj, projc, xa_col, bsz, conv_w, conv_b, wg, bg, lam):
    w = conv_w.shape[1]
    s_len = proj.shape[0] // bsz
    c_len = projc.shape[0] // bsz
    nblk = w // LANES
    seq = lambda n: pl.BlockSpec((n, LANES), lambda b, j: (b, j))
    src = lambda n: pl.BlockSpec((n, LANES), lambda b, j: (b, xa_col + j))
    return pl.pallas_call(
        _lru_kernel,
        grid=(bsz, nblk),
        in_specs=[src(s_len), src(c_len),
                  pl.BlockSpec((LRU_CONV, LANES), lambda b, j: (0, j)),
                  pl.BlockSpec((1, LANES), lambda b, j: (0, j)),
                  pl.BlockSpec((1, LANES, 4 * LANES), lambda b, j: (j, 0, 0)),
                  pl.BlockSpec((1, 1, 4 * LANES), lambda b, j: (j, 0, 0)),
                  pl.BlockSpec((2, LANES), lambda b, j: (0, j))],
        out_specs=[seq(s_len), seq(c_len)],
        out_shape=[jax.ShapeDtypeStruct((proj.shape[0], w), BF16), jax.ShapeDtypeStruct((projc.shape[0], w), BF16)],
        scratch_shapes=[pltpu.VMEM((s_len + (2 * LRU_HALO + 8) * LRU_SEG, LANES), F32)]
        + [pltpu.VMEM((s_len, LANES), F32)] * 4,
        compiler_params=_params(("parallel", "parallel")),
        name="rglru",
    )(proj, projc, conv_w, conv_b, wg, bg, lam)


def _attn_kernel(sink_ref, q_ref, k_ref, v_ref, kc_ref, vc_ref, cos_ref, sin_ref, bias_ref, o_ref):
    nb = pl.num_programs(1) * ATT_QSUB
    group = ATT_HEADS // ATT_KV_HEADS
    scale = HEAD_DIM ** -0.5
    lane = lax.broadcasted_iota(jnp.int32, (BLOCK, LANES), 1)
    ones = jnp.ones((bias_ref.shape[2], HEAD_DIM), BF16)

    def blk(i):
        return pl.ds(pl.multiple_of(i * BLOCK, BLOCK), BLOCK)

    def rot(x, tab):
        return _rope(x.astype(F32), tab[0], tab[1], lane).astype(BF16)

    for sub in range(ATT_QSUB):
        qb = pl.program_id(1) * ATT_QSUB + sub
        qrows = slice(sub * BLOCK, (sub + 1) * BLOCK)
        ip = jnp.maximum(qb - 1, 0)
        inx = jnp.minimum(qb + 1, nb - 1)
        tab_p = (cos_ref[blk(ip), :], sin_ref[blk(ip), :])
        tab_o = (cos_ref[blk(qb), :], sin_ref[blk(qb), :])
        tab_n = (cos_ref[blk(inx), :], sin_ref[blk(inx), :])
        tab_q = (tab_o[0] * scale, tab_o[1] * scale)
        variant = jnp.where(qb > 0, 1, 0) + jnp.where(qb < nb - 1, 2, 0)
        bias = jnp.concatenate([bias_ref[variant]] * ATT_HSUB, axis=0)

        for h in range(ATT_KV_HEADS):
            ksl = slice(h * HEAD_DIM, (h + 1) * HEAD_DIM)
            kcat = jnp.concatenate([rot(k_ref[blk(ip), ksl], tab_p), rot(k_ref[blk(qb), ksl], tab_o),
                                    rot(k_ref[blk(inx), ksl], tab_n), kc_ref[:, ksl]], axis=0)
            vcat = jnp.concatenate([v_ref[blk(ip), ksl], v_ref[blk(qb), ksl], v_ref[blk(inx), ksl],
                                    vc_ref[:, ksl]], axis=0)
            vext = jnp.concatenate([vcat, ones], axis=1)
            for g0 in range(0, group, ATT_HSUB):
                heads = [h * group + g0 + g for g in range(ATT_HSUB)]
                qs = jnp.concatenate([rot(q_ref[qrows, hh * HEAD_DIM:(hh + 1) * HEAD_DIM], tab_q)
                                      for hh in heads], axis=0)
                sk = jnp.concatenate([jnp.full((BLOCK, 1), sink_ref[hh], F32) for hh in heads], axis=0)
                s = _dot_nt(qs, kcat) + bias
                m = jnp.maximum(jnp.max(s, axis=-1, keepdims=True), sk)
                p = jnp.exp((s - m).astype(BF16))
                oe = _dot(p, vext)
                o = oe[:, 0:HEAD_DIM] / (jnp.exp(sk - m) + oe[:, HEAD_DIM:HEAD_DIM + 1])
                for g, hh in enumerate(heads):
                    o_ref[qrows, hh * HEAD_DIM:(hh + 1) * HEAD_DIM] = (
                        o[g * BLOCK:(g + 1) * BLOCK, :].astype(o_ref.dtype))


def _attention(sink, proj, projc, bsz, cols, cos_t, sin_t):
    s_len = proj.shape[0] // bsz
    c_len = projc.shape[0] // bsz
    nb = s_len // BLOCK
    aw = ATT_HEADS * HEAD_DIM
    kw = ATT_KV_HEADS * HEAD_DIM
    qi = np.arange(BLOCK)[:, None]
    kj = np.arange(BLOCK)[None, :]
    variants = []
    for v in range(4):
        prev = np.where((kj >= qi) & bool(v & 1), 0.0, NEG)
        nxt = np.where((kj <= qi) & bool(v & 2), 0.0, NEG)
        variants.append(np.concatenate([prev, np.zeros((BLOCK, BLOCK)), nxt, np.zeros((BLOCK, c_len))], axis=1))
    bias = jnp.asarray(np.stack(variants), F32)
    ncol = 3 * BLOCK + c_len
    nstep = nb // ATT_QSUB
    tq = ATT_QSUB * BLOCK

    return pl.pallas_call(
        _attn_kernel,
        grid=(bsz, nstep),
        in_specs=[pl.BlockSpec(memory_space=pltpu.SMEM),
                  pl.BlockSpec((tq, aw), lambda b, i: (b * nstep + i, cols["q"])),
                  pl.BlockSpec((s_len, kw), lambda b, i: (b, cols["k"])),
                  pl.BlockSpec((s_len, kw), lambda b, i: (b, cols["v"])),
                  pl.BlockSpec((c_len, kw), lambda b, i: (b, cols["k"])),
                  pl.BlockSpec((c_len, kw), lambda b, i: (b, cols["v"])),
                  pl.BlockSpec((s_len, LANES), lambda b, i: (0, 0)),
                  pl.BlockSpec((s_len, LANES), lambda b, i: (0, 0)),
                  pl.BlockSpec((4, BLOCK, ncol), lambda b, i: (0, 0, 0))],
        out_specs=pl.BlockSpec((tq, aw), lambda b, i: (b * nstep + i, 0)),
        out_shape=jax.ShapeDtypeStruct((proj.shape[0], aw), BF16),
        compiler_params=_params(("parallel", "arbitrary")),
        name="window_attention",
    )(sink, proj, proj, proj, projc, projc, cos_t, sin_t, bias)


def _ctx_attn_kernel(sink_ref, q_ref, k_ref, v_ref, o_ref):
    group = ATT_HEADS // ATT_KV_HEADS
    scale = HEAD_DIM ** -0.5
    n = q_ref.shape[0]
    for h in range(ATT_KV_HEADS):
        ksl = slice(h * HEAD_DIM, (h + 1) * HEAD_DIM)
        qs, sinks = [], []
        for g in range(group):
            hh = h * group + g
            qs.append((q_ref[:, hh * HEAD_DIM:(hh + 1) * HEAD_DIM].astype(F32) * scale).astype(BF16))
            sinks.append(jnp.full((n, 1), sink_ref[hh], F32))
        q4 = jnp.concatenate(qs, axis=0)
        sk = jnp.concatenate(sinks, axis=0)
        s = _dot_nt(q4, k_ref[:, ksl])
        m = jnp.maximum(jnp.max(s, axis=-1, keepdims=True), sk)
        p = jnp.exp(s - m)
        denom = jnp.exp(sk - m) + jnp.sum(p, axis=-1, keepdims=True)
        o = _dot(p.astype(BF16), v_ref[:, ksl]) / denom
        for g in range(group):
            hh = h * group + g
            hs = slice(hh * HEAD_DIM, (hh + 1) * HEAD_DIM)
            o_ref[:, hs] = o[g * n:(g + 1) * n, :].astype(o_ref.dtype)


def _ctx_attention(sink, projc, bsz, cols):
    c_len = projc.shape[0] // bsz
    aw = ATT_HEADS * HEAD_DIM
    kw = ATT_KV_HEADS * HEAD_DIM
    return pl.pallas_call(
        _ctx_attn_kernel,
        grid=(bsz,),
        in_specs=[pl.BlockSpec(memory_space=pltpu.SMEM),
                  pl.BlockSpec((c_len, aw), lambda b: (b, cols["q"])),
                  pl.BlockSpec((c_len, kw), lambda b: (b, cols["k"])),
                  pl.BlockSpec((c_len, kw), lambda b: (b, cols["v"]))],
        out_specs=pl.BlockSpec((c_len, aw), lambda b: (b, 0)),
        out_shape=jax.ShapeDtypeStruct((projc.shape[0], aw), BF16),
        compiler_params=_params(("parallel",)),
        name="context_attention",
    )(sink, projc, projc, projc)


def _ret_kernel(dl_ref, q_ref, k_ref, v_ref, qc_ref, kc_ref, vc_ref, o_ref, oc_ref,
                ot_ref, att_ref, u_ref, st_ref, vt_ref, dm_ref):
    c = RET_CHUNK
    s_len = q_ref.shape[0]
    c_len = qc_ref.shape[0]
    lg = -_softplus(-dl_ref[0])
    lgf, lgb = lg[0:1, :], lg[1:2, :]
    wide = lambda t: jnp.concatenate([t] * (c // LANES), axis=1)
    kj = lax.broadcasted_iota(jnp.int32, (c, c), 0)
    qi = lax.broadcasted_iota(jnp.int32, (c, c), 1)
    diff = (qi - kj).astype(F32)
    dm_ref[...] = jnp.where(qi >= kj, jnp.exp(jnp.maximum(diff, 0.0) * wide(lgf)),
                            jnp.exp(jnp.maximum(-diff, 0.0) * wide(lgb)))
    idx = lax.broadcasted_iota(jnp.int32, (c, LANES), 0).astype(F32)
    qdec_f = jnp.exp((idx + 1.0) * lgf)
    kdec_f = jnp.exp((c - 1.0 - idx) * lgf)
    qdec_b = jnp.exp((c - idx) * lgb)
    kdec_b = jnp.exp(idx * lgb)
    cdec_f = jnp.exp(c * lgf)
    cdec_b = jnp.exp(c * lgb)

    def chunk(j):
        return pl.ds(pl.multiple_of(j * c, c), c)

    def transpose_v(vr, n):
        def body(j, _):
            vt_ref[:, chunk(j)] = vr[chunk(j), :].astype(F32).T.astype(BF16)
            return 0
        lax.fori_loop(0, n // c, body, 0, unroll=min(8, n // c))

    def scores(qr, kr, n):
        def body(j, _):
            att_ref[j] = (_dot_nt(kr[chunk(j), :], qr[chunk(j), :]) * dm_ref[...]).astype(BF16)
            return 0
        lax.fori_loop(0, n // c, body, 0, unroll=min(RET_UNROLL, n // c))

    def intra(kr, n):
        def body(j, _):
            vt = vt_ref[:, chunk(j)]
            ot_ref[:, chunk(j)] = _dot(vt, att_ref[j])
            kf = kr[chunk(j), :].astype(F32)
            kcat = jnp.concatenate([(kf * kdec_f).astype(BF16), (kf * kdec_b).astype(BF16)], axis=1)
            u_ref[j] = _dot(vt, kcat)
            return 0
        lax.fori_loop(0, n // c, body, 0, unroll=min(RET_UNROLL, n // c))

    def states(n, sf, sb):
        nch = n // c

        def fbody(j, s):
            st_ref[j, :, 0:RET_DK] = s.astype(BF16)
            return s * cdec_f + u_ref[j, :, 0:RET_DK]
        sf = lax.fori_loop(0, nch, fbody, sf)

        def bbody(jj, s):
            j = nch - 1 - jj
            st_ref[j, :, RET_DK:] = s.astype(BF16)
            return s * cdec_b + u_ref[j, :, RET_DK:]
        sb = lax.fori_loop(0, nch, bbody, sb)
        return sf, sb

    def cross(qr, outr, n):
        def body(j, _):
            qf = qr[chunk(j), :].astype(F32)
            qcat = jnp.concatenate([(qf * qdec_f).astype(BF16), (qf * qdec_b).astype(BF16)], axis=1)
            ot = ot_ref[:, chunk(j)] + _dot_nt(st_ref[j], qcat)
            o = ot.T * (RET_DK ** -0.5)
            outr[chunk(j), :] = (o * lax.rsqrt(jnp.mean(o * o, axis=-1, keepdims=True) + EPS)).astype(outr.dtype)
            return 0
        lax.fori_loop(0, n // c, body, 0, unroll=min(RET_UNROLL, n // c))

    def run(qr, kr, vr, outr, n, sf, sb):
        transpose_v(vr, n)
        scores(qr, kr, n)
        intra(kr, n)
        finals = states(n, sf, sb)
        cross(qr, outr, n)
        return finals

    zero = jnp.zeros((LANES, RET_DK), F32)
    sf, sb = run(qc_ref, kc_ref, vc_ref, oc_ref, c_len, zero, zero)
    run(q_ref, k_ref, v_ref, o_ref, s_len, sf, sb)


def _retention(dl, proj, projc, bsz, cols):
    s_len = proj.shape[0] // bsz
    c_len = projc.shape[0] // bsz
    hb = lambda n, col: pl.BlockSpec((n, LANES), lambda b, h: (b, col + h))
    return pl.pallas_call(
        _ret_kernel,
        grid=(bsz, RET_HEADS),
        in_specs=[pl.BlockSpec((1, 2, LANES), lambda b, h: (h, 0, 0)),
                  hb(s_len, cols["q"]), hb(s_len, cols["k"]), hb(s_len, cols["v"]),
                  hb(c_len, cols["q"]), hb(c_len, cols["k"]), hb(c_len, cols["v"])],
        out_specs=[pl.BlockSpec((s_len, LANES), lambda b, h: (b, h)),
                   pl.BlockSpec((c_len, LANES), lambda b, h: (b, h))],
        out_shape=[jax.ShapeDtypeStruct((proj.shape[0], RET_HEADS * LANES), BF16),
                   jax.ShapeDtypeStruct((projc.shape[0], RET_HEADS * LANES), BF16)],
        scratch_shapes=[pltpu.VMEM((LANES, s_len), F32),
                        pltpu.VMEM((s_len // RET_CHUNK, RET_CHUNK, RET_CHUNK), BF16),
                        pltpu.VMEM((s_len // RET_CHUNK, LANES, 2 * RET_DK), F32),
                        pltpu.VMEM((s_len // RET_CHUNK, LANES, 2 * RET_DK), BF16),
                        pltpu.VMEM((LANES, s_len), BF16),
                        pltpu.VMEM((RET_CHUNK, RET_CHUNK), F32)],
        compiler_params=_params(("parallel", "parallel")),
        name="retention",
    )(dl, proj, proj, proj, projc, projc, projc)


def _hy_hid_kernel(z_ref, w1_ref, b1_ref, w2_ref, b2_ref, f_ref, o_ref):
    f = f_ref[...]
    h = jnp.sin(f * (_dot(z_ref[...], w1_ref[...], precision=HIGHEST) + b1_ref[...]))
    o_ref[...] = jnp.sin(f * (_dot(h, w2_ref[...], precision=HIGHEST) + b2_ref[...]))


def _hy_hidden(zfull, w1p, b1, w2, b2, freq):
    n, e = zfull.shape
    fd = w2.shape[0]
    tr = min(n, 1024)
    full = lambda shp: pl.BlockSpec(shp, lambda i: (0, 0))
    return pl.pallas_call(
        _hy_hid_kernel,
        grid=(n // tr,),
        in_specs=[pl.BlockSpec((tr, e), lambda i: (i, 0)), full((e, fd)), full((1, fd)), full((fd, fd)),
                  full((1, fd)), full((1, fd))],
        out_specs=pl.BlockSpec((tr, fd), lambda i: (i, 0)),
        out_shape=jax.ShapeDtypeStruct((n, fd), F32),
        compiler_params=_params(("parallel",)),
        name="hyena_filter_mlp",
    )(zfull, w1p, b1, w2, b2, freq)


def _hy_filt_kernel(hid_ref, w3f_ref, w3b_ref, tn_ref, dl_ref, o_ref):
    half = hid_ref.shape[0] // 2
    decay = jnp.exp(-tn_ref[...] * jnp.abs(dl_ref[...]))
    top = _dot(hid_ref[0:half, :], w3f_ref[...], precision=HIGHEST)
    bot = _dot(hid_ref[half:, :], w3b_ref[...], precision=HIGHEST)
    row = lax.broadcasted_iota(jnp.int32, bot.shape, 0)
    bot = jnp.where(row == 0, 0.0, bot)
    o_ref[0, 0:half, :] = top * decay[0:half, :]
    o_ref[0, half:, :] = bot * decay[half:, :]


def _hy_filters(hid, w3, tn_full, deltas):
    n, fd = hid.shape
    wch = deltas.shape[1]
    nsl = wch // LANES
    return pl.pallas_call(
        _hy_filt_kernel,
        grid=(HY_ORDER, nsl),
        in_specs=[pl.BlockSpec((n, fd), lambda o, j: (0, 0)),
                  pl.BlockSpec((fd, LANES), lambda o, j: (0, o * 2 * nsl + j)),
                  pl.BlockSpec((fd, LANES), lambda o, j: (0, o * 2 * nsl + nsl + j)),
                  pl.BlockSpec((n, LANES), lambda o, j: (0, 0)),
                  pl.BlockSpec((1, LANES), lambda o, j: (0, j))],
        out_specs=pl.BlockSpec((1, n, LANES), lambda o, j: (o, 0, j)),
        out_shape=jax.ShapeDtypeStruct((HY_ORDER, n, wch), F32),
        compiler_params=_params(("parallel", "parallel")),
        name="hyena_filter",
    )(hid, w3, w3, tn_full, deltas)


def _hy_kfft_kernel(kern_ref, f1_ref, f2_ref, o_ref, g_ref, *, n1, n2):
    pg = 2 * n1 + HY_PITCH_PAD

    def stage1(i2, _):
        x = kern_ref[pl.ds(i2, n1, stride=n2), :].astype(BF16)
        g_ref[pl.ds(pl.multiple_of(i2 * pg, 8), 2 * n1), :] = _dot(f1_ref[i2], x)
        return 0
    lax.fori_loop(0, n2, stage1, 0, unroll=HY_UNROLL)

    def stage2(k1, _):
        x = jnp.concatenate([g_ref[pl.ds(k1, n2, stride=pg), :],
                             g_ref[pl.ds(n1 + k1, n2, stride=pg), :]], axis=0).astype(BF16)
        o_ref[pl.ds(pl.multiple_of(k1 * 2 * n2, 2 * n2), 2 * n2), :] = _dot(f2_ref[...], x)
        return 0
    lax.fori_loop(0, n1, stage2, 0, unroll=HY_UNROLL // 2)


def _hy_kfft(kern, f1k, f2, n1, n2):
    orders, n, wch = kern.shape
    nsl = wch // LANES
    return pl.pallas_call(
        functools.partial(_hy_kfft_kernel, n1=n1, n2=n2),
        grid=(orders, nsl),
        in_specs=[pl.BlockSpec((None, n, LANES), lambda o, j: (o, 0, j)),
                  pl.BlockSpec(f1k.shape, lambda o, j: (0, 0, 0)),
                  pl.BlockSpec(f2.shape, lambda o, j: (0, 0))],
        out_specs=pl.BlockSpec((None, None, 2 * n, LANES), lambda o, j: (o, j, 0, 0)),
        out_shape=jax.ShapeDtypeStruct((orders, nsl, 2 * n, LANES), F32),
        scratch_shapes=[pltpu.VMEM((n2 * (2 * n1 + HY_PITCH_PAD), LANES), F32)],
        compiler_params=_params(("parallel", "parallel")),
        name="hyena_filter_fft",
    )(kern, f1k, f2)


def _hy_prep_kernel(z_ref, w_ref, b_ref, o_ref, zp_ref, *, n2, nseq):
    l_len = z_ref.shape[0] // nseq
    h1 = l_len // n2
    halo = HY_HALO
    pitch = n2 + 2 * halo + 8
    zeros_h = jnp.zeros((halo, LANES), F32)
    for s in range(nseq):
        for i1 in range(h1):
            base = (s * h1 + i1) * pitch
            lo, hi = i1 * n2 - halo, (i1 + 1) * n2 + halo
            if lo < 0:
                zp_ref[base:base + halo, :] = zeros_h
            if hi > l_len:
                zp_ref[base + n2 + halo:base + n2 + 2 * halo, :] = zeros_h
            lo_c, hi_c = max(lo, 0), min(hi, l_len)
            zp_ref[base + (lo_c - lo):base + (hi_c - lo), :] = (
                z_ref[s * l_len + lo_c:s * l_len + hi_c, :].astype(F32))
    w0, w1, w2, b = w_ref[0:1, :], w_ref[1:2, :], w_ref[2:3, :], b_ref[...]

    def tap(j):
        return zp_ref[pl.ds(halo + j, nseq * h1, stride=pitch), :]

    def body(i2, carry):
        zm, z0 = carry
        zn = tap(i2 + 1)
        val = (b + w0 * zm + w1 * z0 + w2 * zn).astype(o_ref.dtype)
        for s in range(nseq):
            o_ref[pl.ds(pl.multiple_of(s * l_len + i2 * h1, h1), h1), :] = val[s * h1:(s + 1) * h1, :]
        return z0, zn
    lax.fori_loop(0, n2, body, (tap(-1), tap(0)), unroll=HY_UNROLL if nseq <= 2 else 1)


def _hy_stage_dtype(l_len, n2):
    return BF16 if (l_len // n2) % 16 == 0 else F32


def _hy_prep(proj, z_col, nz, bsz, conv_w, conv_b, n2):
    l_len = proj.shape[0] // bsz
    h1 = l_len // n2
    nseq = max(1, min(bsz, HY_PREP_ROWS // l_len))
    while bsz % nseq:
        nseq -= 1
    return pl.pallas_call(
        functools.partial(_hy_prep_kernel, n2=n2, nseq=nseq),
        grid=(bsz // nseq, nz),
        in_specs=[pl.BlockSpec((nseq * l_len, LANES), lambda b, j: (b, z_col + j)),
                  pl.BlockSpec((3, LANES), lambda b, j: (0, j)),
                  pl.BlockSpec((1, LANES), lambda b, j: (0, j))],
        out_specs=pl.BlockSpec((nseq * l_len, LANES), lambda b, j: (b, j)),
        out_shape=jax.ShapeDtypeStruct((proj.shape[0], nz * LANES), _hy_stage_dtype(l_len, n2)),
        scratch_shapes=[pltpu.VMEM((nseq * h1 * (n2 + 2 * HY_HALO + 8), LANES), F32)],
        compiler_params=_params(("parallel", "parallel")),
        name="hyena_short_conv",
    )(proj, conv_w, conv_b)


def _pack2(re, im):
    r = lax.bitcast_convert_type(re, jnp.uint32) + jnp.uint32(0x8000)
    i = lax.bitcast_convert_type(im, jnp.uint32) + jnp.uint32(0x8000)
    return (r & jnp.uint32(0xFFFF0000)) | (i >> 16)


def _unpack2(w):
    re = lax.bitcast_convert_type(w & jnp.uint32(0xFFFF0000), F32)
    im = lax.bitcast_convert_type(w << 16, F32)
    return jnp.concatenate([re, im], axis=0).astype(BF16)


def _hy_conv_kernel(y_ref, x_ref, kh_ref, bias_ref, f1_ref, f2_ref, f2i_ref, f1i_ref, o_ref, g_ref,
                    *, n1, n2, natural_out):
    h1 = n1 // 2
    l_len = h1 * n2
    pg = n1 + HY_PITCH_PAD
    bias = bias_ref[...]
    unroll = min(HY_UNROLL, n1)

    def rows(seq, i2):
        return pl.ds(pl.multiple_of(seq * l_len + i2 * h1, h1), h1)

    def lanes2(a, b):
        return jnp.concatenate([a, b], axis=1)

    def stage1(i2, _):
        x = lanes2(jnp.concatenate([y_ref[rows(0, i2), :], y_ref[rows(1, i2), :]], axis=0),
                   jnp.concatenate([y_ref[rows(2, i2), :], y_ref[rows(3, i2), :]], axis=0)).astype(BF16)
        a = _dot(f1_ref[i2], x)
        w = _pack2(a[0:n1, :], a[n1:, :])
        blk = pl.ds(pl.multiple_of(i2 * pg, 8), n1)
        g_ref[0, blk, :] = w[:, 0:LANES]
        g_ref[1, blk, :] = w[:, LANES:]
        return 0
    lax.fori_loop(0, n2, stage1, 0, unroll=2 * unroll)

    def stage23(t, _):
        outs = []
        for u in range(unroll):
            k1 = t * unroll + u
            col = pl.ds(k1, n2, stride=pg)
            x = lanes2(_unpack2(g_ref[0, col, :]), _unpack2(g_ref[1, col, :]))
            yh = _dot(f2_ref[...], x)
            base = pl.multiple_of(k1 * 2 * n2, 2 * n2)
            kr = kh_ref[pl.ds(base, n2), :]
            ki = kh_ref[pl.ds(base + n2, n2), :]
            kr, ki = lanes2(kr, kr), lanes2(ki, ki)
            yr, yi = yh[0:n2, :], yh[n2:, :]
            z = jnp.concatenate([yr * kr - yi * ki, yr * ki + yi * kr], axis=0).astype(BF16)
            c = _dot(f2i_ref[...], z)
            outs.append((col, _pack2(c[0:n2, :], c[n2:, :])))
        for col, w in outs:
            g_ref[0, col, :] = w[:, 0:LANES]
            g_ref[1, col, :] = w[:, LANES:]
        return 0
    lax.fori_loop(0, n1 // unroll, stage23, 0)

    def stage4(i2, _):
        blk = pl.ds(pl.multiple_of(i2 * pg, 8), n1)
        x = lanes2(_unpack2(g_ref[0, blk, :]), _unpack2(g_ref[1, blk, :]))
        yc = _dot(f1i_ref[i2], x)
        for seq in range(4):
            pair, half = seq // 2, seq % 2
            sl = rows(seq, i2)
            conv = yc[half * h1:(half + 1) * h1, pair * LANES:(pair + 1) * LANES]
            val = x_ref[sl, :].astype(F32) * (conv + bias * y_ref[sl, :].astype(F32))
            if natural_out:
                o_ref[pl.ds(seq * l_len + i2, h1, stride=n2), :] = val
            else:
                o_ref[sl, :] = val.astype(o_ref.dtype)
        return 0
    lax.fori_loop(0, n2, stage4, 0, unroll=unroll)


def _hy_conv(ysrc, ycol, xsrc, xcol, khat, order, bias, mats, bsz, n1, n2, natural_out):
    l_len = ysrc.shape[0] // bsz
    nsl = khat.shape[1]
    f1, f2, f2i, f1i = mats
    once = pl.Buffered(1)
    cst2 = lambda a: pl.BlockSpec(a.shape, lambda j, p: (0, 0), pipeline_mode=once)
    cst3 = lambda a: pl.BlockSpec(a.shape, lambda j, p: (0, 0, 0), pipeline_mode=once)
    return pl.pallas_call(
        functools.partial(_hy_conv_kernel, n1=n1, n2=n2, natural_out=natural_out),
        grid=(nsl, bsz // 4),
        in_specs=[pl.BlockSpec((4 * l_len, LANES), lambda j, p: (p, ycol + j)),
                  pl.BlockSpec((4 * l_len, LANES), lambda j, p: (p, xcol + j)),
                  pl.BlockSpec((None, None, khat.shape[2], LANES), lambda j, p: (order, j, 0, 0),
                               pipeline_mode=once),
                  pl.BlockSpec((1, LANES), lambda j, p: (0, j)),
                  cst3(f1), cst2(f2), cst2(f2i), cst3(f1i)],
        out_specs=pl.BlockSpec((4 * l_len, LANES), lambda j, p: (p, j)),
        out_shape=jax.ShapeDtypeStruct((ysrc.shape[0], nsl * LANES),
                                       F32 if natural_out else _hy_stage_dtype(l_len, n2)),
        scratch_shapes=[pltpu.VMEM((2, n2 * (n1 + HY_PITCH_PAD), LANES), jnp.uint32)],
        compiler_params=_params(("parallel", "arbitrary")),
        name="hyena_long_conv",
    )(ysrc, xsrc, khat, bias[order][None], f1, f2, f2i, f1i)


def _dft_tables(l_len):
    n = 2 * l_len
    n2 = 128 if l_len >= 1024 else 32
    n1 = n // n2
    h1 = n1 // 2
    i1 = np.arange(n1)[None, None, :]
    k1 = np.arange(n1)[None, :, None]
    i2 = np.arange(n2)[:, None, None]
    ph = 2 * np.pi * (i1 * k1 / n1 + i2 * k1 / n)
    c, s = np.cos(ph), np.sin(ph)
    ch, sh = c[:, :, :h1], s[:, :, :h1]
    f1 = np.concatenate([np.concatenate([ch, sh], 2), np.concatenate([-sh, ch], 2)], 1)
    f1k = np.concatenate([c, -s], 1)
    ct, st = np.swapaxes(ch, 1, 2), np.swapaxes(sh, 1, 2)
    f1i = np.concatenate([np.concatenate([ct, -st], 2), np.concatenate([st, ct], 2)], 1) / n
    a = np.arange(n2)
    ph2 = 2 * np.pi * np.outer(a, a) / n2
    c2, s2 = np.cos(ph2), np.sin(ph2)
    f2 = np.block([[c2, s2], [-s2, c2]])
    f2i = np.block([[c2, -s2], [s2, c2]])
    bf = lambda m: jnp.asarray(m, dtype=F32).astype(BF16)
    return n1, n2, (bf(f1), bf(f2), bf(f2i), bf(f1i)), bf(f1k)


def _filter_positions(l_len, width):
    lag = np.concatenate([np.arange(l_len), l_len - np.arange(l_len)]).astype(np.float64)
    t = lag / (l_len - 1)
    bands = (HY_EMB - 1) // 2
    w = 2.0 * np.pi * lag / l_len
    f = np.linspace(1e-4, bands - 1, bands)[None]
    z = np.concatenate([t[:, None], np.cos(f * w[:, None]), -np.sin(f * w[:, None])], axis=-1)
    z = np.pad(z, ((0, 0), (0, HY_EMB_PAD - HY_EMB)))
    tn = np.repeat(t[:, None], LANES, axis=1)
    max_decay = math.log(HY_DECAY_TARGET) / HY_FAST_PCT
    min_decay = math.log(HY_DECAY_TARGET) / HY_SLOW_PCT
    deltas = np.linspace(min_decay, max_decay, width)[None]
    return jnp.asarray(z, F32), jnp.asarray(tn, F32), jnp.asarray(deltas, F32)


def _rope_tables(seq):
    n_rows = seq // GRID_W
    row = np.repeat(np.arange(n_rows), GRID_W).astype(np.float64)
    col = np.tile(np.arange(GRID_W), n_rows).astype(np.float64)
    half = HEAD_DIM // 2
    inv = ROPE_BASE ** (-np.arange(0, half, 2, dtype=np.float64) / half)
    ar, ac = row[:, None] * inv, col[:, None] * inv
    cos = np.concatenate([np.cos(ar), np.cos(ar), np.cos(ac), np.cos(ac)], axis=1)
    sin = np.concatenate([-np.sin(ar), np.sin(ar), -np.sin(ac), np.sin(ac)], axis=1)
    return jnp.asarray(cos, F32), jnp.asarray(sin, F32)


def _hyena_spectra(l_len, width, w1, b1, w2, b2, w3, freq, tables):
    n1, n2, mats, f1k = tables
    zfull, tn_full, deltas = _filter_positions(l_len, width)
    w1p = jnp.pad(w1, ((0, HY_EMB_PAD - HY_EMB), (0, 0)))
    hid = _hy_hidden(zfull, w1p, b1[None], w2, b2[None], freq[None])
    kern = _hy_filters(hid, w3, tn_full, deltas)
    return _hy_kfft(kern, f1k, mats[1], n1, n2)


def _hyena_seq(zc, bsz, khat, bias, tables):
    n1, n2, mats, _ = tables
    nsl = khat.shape[1]
    y1 = _hy_conv(zc, 0, zc, nsl, khat, 0, bias, mats, bsz, n1, n2, False)
    return _hy_conv(y1, 0, zc, 2 * nsl, khat, 1, bias, mats, bsz, n1, n2, True)


def kernel(x, c, ctx, c_ctx, mod_w, mod_b, norm_pre, norm_post, ev_w_in, ev_w_out, lru_conv_w, lru_conv_b, lru_wa, lru_ba, lru_wx, lru_bx, lru_lambda, attn_sink, od_w_in, od_w_out, hy_conv_w, hy_conv_b, hy_w1, hy_b1, hy_w2, hy_b2, hy_w3, hy_freq, hy_bias, ret_decay_logit):
    bsz, s_len, d = x.shape
    c_len = ctx.shape[1]
    depth = mod_w.shape[0]
    assert bsz % 4 == 0 and bsz <= 16 and s_len % 1024 == 0 and c_len % 256 == 0 and c_len <= s_len

    crows = jnp.concatenate([c, c_ctx[None], jnp.zeros((24 - bsz - 1, d), F32)], axis=0)
    mod = _modulation(crows, mod_w, mod_b)

    cos_t, sin_t = _rope_tables(s_len)
    tab_l = _dft_tables(s_len)
    tab_c = _dft_tables(c_len)

    x2 = x.reshape(bsz * s_len, d)
    ctx2 = ctx.reshape(bsz * c_len, d)
    tm = ROW_TILE

    for l in range(depth):
        need_ctx = l < depth - 1
        shift, scale, gate = (mod[l, :bsz, i * d:(i + 1) * d].reshape(bsz, 1, d) for i in range(3))
        shift_c, scale_c, gate_c = (mod[l, bsz:bsz + 1, i * d:(i + 1) * d].reshape(1, 1, d) for i in range(3))
        g_pre = norm_pre[l][None]
        g_post = norm_post[l][None]

        def proj(w, tn):
            wb = w.astype(BF16)
            p = _in_proj(x2, g_pre, scale, shift, s_len, wb, BF16, IN_ROW_TILE, tn)
            pc = _in_proj(ctx2, g_pre, scale_c, shift_c, bsz * c_len, wb, BF16, IN_ROW_TILE, tn)
            return p, pc

        if l % 2 == 0:
            e = l // 2
            w_in = ev_w_in[e]
            w_ord = jnp.concatenate([w_in[:, 1024:2048], w_in[:, 2048:3072], w_in[:, 3584:4608],
                                     w_in[:, 3072:3584], w_in[:, 0:1024]], axis=1)
            pr, prc = proj(w_ord, EVEN_COL_TILE)
            cols = {"q": 1, "gb": 2, "k": 12, "v": 13}
            xa_col = 3584 // LANES
            nblk = lru_wa.shape[2]
            wg = jnp.stack([lru_wa[e, 0], lru_wx[e, 0], lru_wa[e, 1], lru_wx[e, 1]], axis=1)
            wg = jnp.transpose(wg, (0, 2, 1, 3)).reshape(nblk, LRU_BLOCK_W, 4 * LRU_BLOCK_W)
            bg = jnp.stack([lru_ba[e, 0], lru_bx[e, 0], lru_ba[e, 1], lru_bx[e, 1]], axis=0)
            bg = jnp.transpose(bg.reshape(4, nblk, LRU_BLOCK_W), (1, 0, 2)).reshape(nblk, 1, 4 * LRU_BLOCK_W)
            wg, bg = (0.5 * wg).astype(BF16), 0.5 * bg
            ya, yac = _lru(pr, prc, xa_col, bsz, lru_conv_w[e], lru_conv_b[e][None], wg, bg, lru_lambda[e])
            yb = _attention(attn_sink[e], pr, prc, bsz, cols, cos_t, sin_t)
            w_out = ev_w_out[e].astype(BF16)
            x2_new = _out_proj(ya, pr, 0, yb, pr, cols["gb"], w_out, x2, g_post, gate, s_len, tm)
            if need_ctx:
                ybc = _ctx_attention(attn_sink[e], prc, bsz, cols)
                ctx2 = _out_proj(yac, prc, 0, ybc, prc, cols["gb"], w_out, ctx2, g_post, gate_c, bsz * c_len, tm)
            x2 = x2_new
        else:
            o = l // 2
            w_in = od_w_in[o]
            w_ord = jnp.concatenate([w_in[:, 3072:8192], w_in[:, 0:3072]], axis=1)
            pr, prc = proj(w_ord, ODD_COL_TILE)
            cols = {"q": 8, "k": 16, "v": 24}
            z_col = 5120 // LANES
            width = hy_bias.shape[2]
            nz = (HY_ORDER + 1) * width // LANES
            fargs = (hy_w1[o], hy_b1[o], hy_w2[o], hy_b2[o], hy_w3[o], hy_freq[o])
            khat = _hyena_spectra(s_len, width, *fargs, tab_l)
            zconv = _hy_prep(pr, z_col, nz, bsz, hy_conv_w[o], hy_conv_b[o][None], tab_l[1])
            yh = _hyena_seq(zconv, bsz, khat, hy_bias[o], tab_l)
            dl = jnp.broadcast_to(jnp.transpose(ret_decay_logit[o])[:, :, None], (RET_HEADS, 2, LANES))
            yd, ydc = _retention(dl, pr, prc, bsz, cols)
            w_out = od_w_out[o].astype(BF16)
            x2_new = _out_proj(yh, pr, 0, yd, pr, 4, w_out, x2, g_post, gate, s_len, tm)
            if need_ctx:
                khat_c = _hyena_spectra(c_len, width, *fargs, tab_c)
                zcconv = _hy_prep(prc, z_col, nz, bsz, hy_conv_w[o], hy_conv_b[o][None], tab_c[1])
                yhc = _hyena_seq(zcconv, bsz, khat_c, hy_bias[o], tab_c)
                ctx2 = _out_proj(yhc, prc, 0, ydc, prc, 4, w_out, ctx2, g_post, gate_c, bsz * c_len, tm)
            x2 = x2_new
    return x2.reshape(bsz, s_len, d)
```

```python
import functools
import math

import numpy as np
import jax
import jax.numpy as jnp
from jax import lax
from jax.experimental import pallas as pl
from jax.experimental.pallas import tpu as pltpu

F32 = jnp.float32
BF16 = jnp.bfloat16
HIGHEST = lax.Precision.HIGHEST

EPS = 1e-6
GRID_W = 64
LANES = 128
LRU_BLOCK_W = 128
LRU_CONV = 4
LRU_C = 8.0
LRU_SEG = 16
LRU_POS = 64
LRU_TINY = 1e-30
LRU_HALO = 16
ATT_HEADS = 8
ATT_KV_HEADS = 2
HEAD_DIM = 128
BLOCK = 128
ATT_QSUB = 4
ATT_HSUB = 4
ROPE_BASE = 10000.0
HY_ORDER = 2
HY_EMB = 33
HY_EMB_PAD = 40
HY_DECAY_TARGET = 1e-2
HY_FAST_PCT = 0.3
HY_SLOW_PCT = 1.5
HY_UNROLL = 32
HY_PREP_ROWS = 16384
HY_HALO = 16
HY_PITCH_PAD = 8
RET_HEADS = 8
RET_DK = 128
RET_CHUNK = 256
RET_UNROLL = 8

V7X_VMEM_BYTES = 64 * 1024 * 1024
VMEM_LIMIT = V7X_VMEM_BYTES - 8 * 1024 * 1024
ROW_TILE = 1024
IN_ROW_TILE = 2048
OUT_SPLIT = 4
EVEN_COL_TILE = 1536
ODD_COL_TILE = 2048
NEG = -1e30


def _params(sem, vmem=VMEM_LIMIT, **kw):
    return pltpu.CompilerParams(dimension_semantics=sem, vmem_limit_bytes=vmem, **kw)


def _sigmoid(v):
    return 0.5 * (jnp.tanh(0.5 * v) + 1.0)


def _silu(v):
    return v * _sigmoid(v)


def _softplus(v):
    return jnp.maximum(v, 0.0) + jnp.log(1.0 + jnp.exp(-jnp.abs(v)))


def _dot(a, b, **kw):
    return jnp.dot(a, b, preferred_element_type=F32, **kw)


def _dot_nt(a, b):
    return lax.dot_general(a, b, (((1,), (1,)), ((), ())), preferred_element_type=F32)


def _mod_kernel(c_ref, w_ref, b_ref, o_ref):
    s = _silu(c_ref[...])
    o_ref[0] = _dot(s, w_ref[0], precision=HIGHEST) + b_ref[0]


def _modulation(crows, mod_w, mod_b):
    depth, d, n3 = mod_w.shape
    r = crows.shape[0]
    tn = 1024
    return pl.pallas_call(
        _mod_kernel,
        grid=(depth, n3 // tn),
        in_specs=[pl.BlockSpec((r, d), lambda l, j: (0, 0)),
                  pl.BlockSpec((1, d, tn), lambda l, j: (l, 0, j)),
                  pl.BlockSpec((1, 1, tn), lambda l, j: (l, 0, j))],
        out_specs=pl.BlockSpec((1, r, tn), lambda l, j: (l, 0, j)),
        out_shape=jax.ShapeDtypeStruct((depth, r, n3), F32),
        compiler_params=_params(("parallel", "parallel")),
        name="modulation",
    )(crows, mod_w, mod_b.reshape(depth, 1, n3))


def _rope(v, cos, sin, lane):
    swapped = jnp.where((lane & 63) < 32, pltpu.roll(v, LANES - 32, 1), pltpu.roll(v, 32, 1))
    return v * cos + swapped * sin


def _inproj_kernel(x_ref, g_ref, sc_ref, sh_ref, w_ref, o_ref, h_ref):
    @pl.when(pl.program_id(1) == 0)
    def _():
        x = x_ref[...]
        y = x * lax.rsqrt(jnp.mean(x * x, axis=-1, keepdims=True) + EPS) * g_ref[...]
        h_ref[...] = (y * (1.0 + sc_ref[0]) + sh_ref[0]).astype(BF16)

    o_ref[...] = _dot(h_ref[...], w_ref[...]).astype(o_ref.dtype)


def _in_proj(x2, g, scale, shift, rows_per_group, w, out_dtype, tm, tn):
    m, d = x2.shape
    n = w.shape[1]
    tpg = rows_per_group // tm
    return pl.pallas_call(
        _inproj_kernel,
        grid=(m // tm, n // tn),
        in_specs=[pl.BlockSpec((tm, d), lambda i, j: (i, 0)),
                  pl.BlockSpec((1, d), lambda i, j: (0, 0)),
                  pl.BlockSpec((1, 1, d), lambda i, j: (i // tpg, 0, 0)),
                  pl.BlockSpec((1, 1, d), lambda i, j: (i // tpg, 0, 0)),
                  pl.BlockSpec((d, tn), lambda i, j: (0, j))],
        out_specs=pl.BlockSpec((tm, tn), lambda i, j: (i, j)),
        out_shape=jax.ShapeDtypeStruct((m, n), out_dtype),
        scratch_shapes=[pltpu.VMEM((tm, d), BF16)],
        compiler_params=_params(("parallel", "arbitrary")),
        name="in_proj",
    )(x2, g, scale, shift, w)


def _outproj_kernel(a_ref, ga_ref, b_ref, gb_ref, w_ref, x_ref, g_ref, gate_ref, o_ref):
    wa = w_ref.shape[0] // 2
    rows = a_ref.shape[0] // OUT_SPLIT
    for part in range(OUT_SPLIT):
        sl = slice(part * rows, (part + 1) * rows)
        a = a_ref[sl, :].astype(BF16) * _silu(ga_ref[sl, :])
        b = b_ref[sl, :].astype(BF16) * _silu(gb_ref[sl, :])
        y = _dot(a, w_ref[0:wa, :]) + _dot(b, w_ref[wa:, :])
        yn = y * lax.rsqrt(jnp.mean(y * y, axis=-1, keepdims=True) + EPS) * g_ref[...]
        o_ref[sl, :] = x_ref[sl, :] + gate_ref[0] * yn


def _out_proj(a, ga, ga_col, b, gb, gb_col, w, x2, g, gate, rows_per_group, tm):
    m, d = x2.shape
    wa = w.shape[0] // 2
    tpg = rows_per_group // tm
    return pl.pallas_call(
        _outproj_kernel,
        grid=(m // tm,),
        in_specs=[pl.BlockSpec((tm, wa), lambda i: (i, 0)),
                  pl.BlockSpec((tm, wa), lambda i: (i, ga_col)),
                  pl.BlockSpec((tm, wa), lambda i: (i, 0)),
                  pl.BlockSpec((tm, wa), lambda i: (i, gb_col)),
                  pl.BlockSpec(w.shape, lambda i: (0, 0)),
                  pl.BlockSpec((tm, d), lambda i: (i, 0)),
                  pl.BlockSpec((1, d), lambda i: (0, 0)),
                  pl.BlockSpec((1, 1, d), lambda i: (i // tpg, 0, 0))],
        out_specs=pl.BlockSpec((tm, d), lambda i: (i, 0)),
        out_shape=jax.ShapeDtypeStruct((m, d), F32),
        compiler_params=_params(("parallel",)),
        name="out_proj",
    )(a, ga, b, gb, w, x2, g, gate)


def _lru_kernel(xa_ref, xac_ref, cw_ref, cb_ref, wg_ref, bg_ref, lam_ref, y_ref, yc_ref,
                xp_ref, hf_ref, pf_ref, hb_ref, pb_ref):
    nseg = LRU_SEG
    c8h =-0.5 * LRU_C * _softplus(-lam_ref[...])
    left = LRU_CONV // 2
    cw = [cw_ref[k:k + 1, :] for k in range(LRU_CONV)]
    cb = cb_ref[...]
    zeros_h = jnp.zeros((LRU_HALO, LANES), F32)
    zeros_s = jnp.zeros((nseg, LANES), F32)
    ones_s = jnp.ones((nseg, LANES), F32)

    def run(src_ref, out_ref, n, c0f, c0b):
        seg = n // nseg
        npos = min(LRU_POS, seg)
        nblk = seg // npos
        halo = LRU_HALO
        pitch = seg + 2 * halo + 8
        for j in range(nseg):
            lo, hi = j * seg - halo, (j + 1) * seg + halo
            if lo < 0:
                xp_ref[j * pitch:j * pitch + halo, :] = zeros_h
            if hi > n:
                xp_ref[j * pitch + seg + halo:j * pitch + seg + 2 * halo, :] = zeros_h
            lo_c, hi_c = max(lo, 0), min(hi, n)
            xp_ref[j * pitch + (lo_c - lo):j * pitch + (hi_c - lo), :] = src_ref[lo_c:hi_c, :].astype(F32)

        def gather(q):
            return jnp.concatenate([xp_ref[pl.ds(halo + q + c * 8 * pitch, 8, stride=pitch), :]
                                    for c in range(nseg // 8)], axis=0)

        def blk(i):
            return pl.ds(pl.multiple_of(i * npos * nseg, npos * nseg), npos * nseg)

        def pos(v, p):
            return v[p * nseg:(p + 1) * nseg, :]

        def fwd_body(i, carry):
            h, pc = carry
            p0 = i * npos
            xs = [gather(p0 + q - left) for q in range(npos + LRU_CONV - 1)]
            us = []
            for p in range(npos):
                u = cb + cw[0] * xs[p]
                for k in range(1, LRU_CONV):
                    u = u + cw[k] * xs[p + k]
                us.append(u)
            u = jnp.concatenate(us, axis=0)
            g = _dot(u.astype(BF16), wg_ref[0]) + bg_ref[0]
            hu = 0.5 * u
            coef = []
            for d in range(2):
                tr = jnp.tanh(g[:, (2 * d) * LANES:(2 * d + 1) * LANES])
                ti = jnp.tanh(g[:, (2 * d + 1) * LANES:(2 * d + 2) * LANES])
                a = jnp.exp(c8h[d:d + 1, :] * tr + c8h[d:d + 1, :])
                om = 1.0 - a * a
                coef.append((a, om * lax.rsqrt(jnp.maximum(om, LRU_TINY)) * ((ti + 1.0) * hu)))
            (af, bf), (ab, bb) = coef
            hb_ref[blk(i), :] = ab
            pb_ref[blk(i), :] = bb
            hs, ps = [], []
            for p in range(npos):
                a_p = pos(af, p)
                h = a_p * h + pos(bf, p)
                pc = a_p * pc
                hs.append(h)
                ps.append(pc)
            hf_ref[blk(i), :] = jnp.concatenate(hs, axis=0)
            pf_ref[blk(i), :] = jnp.concatenate(ps, axis=0)
            return h, pc

        hef, pef = lax.fori_loop(0, nblk, fwd_body, (zeros_s, ones_s))

        def bwd_body(ii, carry):
            h, pc = carry
            i = nblk - 1 - ii
            ab = hb_ref[blk(i), :]
            bb = pb_ref[blk(i), :]
            hs, ps = [None] * npos, [None] * npos
            for p in reversed(range(npos)):
                a_p = pos(ab, p)
                h = a_p * h + pos(bb, p)
                pc = a_p * pc
                hs[p] = h
                ps[p] = pc
            hb_ref[blk(i), :] = jnp.concatenate(hs, axis=0)
            pb_ref[blk(i), :] = jnp.concatenate(ps, axis=0)
            return h, pc

        heb, peb = lax.fori_loop(0, nblk, bwd_body, (zeros_s, ones_s))

        rows_f, c = [], c0f
        for j in range(nseg):
            rows_f.append(c)
            c = hef[j:j + 1, :] + pef[j:j + 1, :] * c
        final_f = c
        rows_b, c = [None] * nseg, c0b
        for j in reversed(range(nseg)):
            rows_b[j] = c
            c = heb[j:j + 1, :] + peb[j:j + 1, :] * c
        final_b = c
        cf = jnp.concatenate(rows_f, axis=0)
        cbk = jnp.concatenate(rows_b, axis=0)

        opitch = seg + 8

        def out_body(p, _):
            rs = pl.ds(pl.multiple_of(p * nseg, nseg), nseg)
            y = hf_ref[rs, :] + pf_ref[rs, :] * cf + hb_ref[rs, :] + pb_ref[rs, :] * cbk
            for c in range(nseg // 8):
                xp_ref[pl.ds(p + c * 8 * opitch, 8, stride=opitch), :] = y[c * 8:(c + 1) * 8, :]
            return 0
        lax.fori_loop(0, seg, out_body, 0, unroll=8)
        for j in range(nseg):
            out_ref[j * seg:(j + 1) * seg, :] = xp_ref[j * opitch:j * opitch + seg, :].astype(out_ref.dtype)
        return final_f, final_b

    zero = jnp.zeros((1, LANES), F32)
    ff, fb = run(xac_ref, yc_ref, xac_ref.shape[0], zero, zero)
    run(xa_ref, y_ref, xa_ref.shape[0], ff, fb)


def _lru(proj, projc, xa_col, bsz, conv_w, conv_b, wg, bg, lam):
    w = conv_w.shape[1]
    s_len = proj.shape[0] // bsz
    c_len = projc.shape[0] // bsz
    nblk = w // LANES
    seq = lambda n: pl.BlockSpec((n, LANES), lambda b, j: (b, j))
    src = lambda n: pl.BlockSpec((n, LANES), lambda b, j: (b, xa_col + j))
    return pl.pallas_call(
        _lru_kernel,
        grid=(bsz, nblk),
        in_specs=[src(s_len), src(c_len),
                  pl.BlockSpec((LRU_CONV, LANES), lambda b, j: (0, j)),
                  pl.BlockSpec((1, LANES), lambda b, j: (0, j)),
                  pl.BlockSpec((1, LANES, 4 * LANES), lambda b, j: (j, 0, 0)),
                  pl.BlockSpec((1, 1, 4 * LANES), lambda b, j: (j, 0, 0)),
                  pl.BlockSpec((2, LANES), lambda b, j: (0, j))],
        out_specs=[seq(s_len), seq(c_len)],
        out_shape=[jax.ShapeDtypeStruct((proj.shape[0], w), BF16), jax.ShapeDtypeStruct((projc.shape[0], w), BF16)],
        scratch_shapes=[pltpu.VMEM((s_len + (2 * LRU_HALO + 8) * LRU_SEG, LANES), F32)]
        + [pltpu.VMEM((s_len, LANES), F32)] * 4,
        compiler_params=_params(("parallel", "parallel")),
        name="rglru",
    )(proj, projc, conv_w, conv_b, wg, bg, lam)


def _attn_kernel(sink_ref, q_ref, k_ref, v_ref, kc_ref, vc_ref, cos_ref, sin_ref, bias_ref, o_ref):
    nb = pl.num_programs(1) * ATT_QSUB
    group = ATT_HEADS // ATT_KV_HEADS
    scale = HEAD_DIM ** -0.5
    lane = lax.broadcasted_iota(jnp.int32, (BLOCK, LANES), 1)
    ones = jnp.ones((bias_ref.shape[2], HEAD_DIM), BF16)

    def blk(i):
        return pl.ds(pl.multiple_of(i * BLOCK, BLOCK), BLOCK)

    def rot(x, tab):
        return _rope(x.astype(F32), tab[0], tab[1], lane).astype(BF16)

    for sub in range(ATT_QSUB):
        qb = pl.program_id(1) * ATT_QSUB + sub
        qrows = slice(sub * BLOCK, (sub + 1) * BLOCK)
        ip = jnp.maximum(qb - 1, 0)
        inx = jnp.minimum(qb + 1, nb - 1)
        tab_p = (cos_ref[blk(ip), :], sin_ref[blk(ip), :])
        tab_o = (cos_ref[blk(qb), :], sin_ref[blk(qb), :])
        tab_n = (cos_ref[blk(inx), :], sin_ref[blk(inx), :])
        tab_q = (tab_o[0] * scale, tab_o[1] * scale)
        variant = jnp.where(qb > 0, 1, 0) + jnp.where(qb < nb - 1, 2, 0)
        bias = jnp.concatenate([bias_ref[variant]] * ATT_HSUB, axis=0)

        for h in range(ATT_KV_HEADS):
            ksl = slice(h * HEAD_DIM, (h + 1) * HEAD_DIM)
            kcat = jnp.concatenate([rot(k_ref[blk(ip), ksl], tab_p), rot(k_ref[blk(qb), ksl], tab_o),
                                    rot(k_ref[blk(inx), ksl], tab_n), kc_ref[:, ksl]], axis=0)
            vcat = jnp.concatenate([v_ref[blk(ip), ksl], v_ref[blk(qb), ksl], v_ref[blk(inx), ksl],
                                    vc_ref[:, ksl]], axis=0)
            vext = jnp.concatenate([vcat, ones], axis=1)
            for g0 in range(0, group, ATT_HSUB):
                heads = [h * group + g0 + g for g in range(ATT_HSUB)]
                qs = jnp.concatenate([rot(q_ref[qrows, hh * HEAD_DIM:(hh + 1) * HEAD_DIM], tab_q)
                                      for hh in heads], axis=0)
                sk = jnp.concatenate([jnp.full((BLOCK, 1), sink_ref[hh], F32) for hh in heads], axis=0)
                s = _dot_nt(qs, kcat) + bias
                m = jnp.maximum(jnp.max(s, axis=-1, keepdims=True), sk)
                p = jnp.exp((s - m).astype(BF16))
                oe = _dot(p, vext)
                o = oe[:, 0:HEAD_DIM] / (jnp.exp(sk - m) + oe[:, HEAD_DIM:HEAD_DIM + 1])
                for g, hh in enumerate(heads):
                    o_ref[qrows, hh * HEAD_DIM:(hh + 1) * HEAD_DIM] = (
                        o[g * BLOCK:(g + 1) * BLOCK, :].astype(o_ref.dtype))


def _attention(sink, proj, projc, bsz, cols, cos_t, sin_t):
    s_len = proj.shape[0] // bsz
    c_len = projc.shape[0] // bsz
    nb = s_len // BLOCK
    aw = ATT_HEADS * HEAD_DIM
    kw = ATT_KV_HEADS * HEAD_DIM
    qi = np.arange(BLOCK)[:, None]
    kj = np.arange(BLOCK)[None, :]
    variants = []
    for v in range(4):
        prev = np.where((kj >= qi) & bool(v & 1), 0.0, NEG)
        nxt = np.where((kj <= qi) & bool(v & 2), 0.0, NEG)
        variants.append(np.concatenate([prev, np.zeros((BLOCK, BLOCK)), nxt, np.zeros((BLOCK, c_len))], axis=1))
    bias = jnp.asarray(np.stack(variants), F32)
    ncol = 3 * BLOCK + c_len
    nstep = nb // ATT_QSUB
    tq = ATT_QSUB * BLOCK

    return pl.pallas_call(
        _attn_kernel,
        grid=(bsz, nstep),
        in_specs=[pl.BlockSpec(memory_space=pltpu.SMEM),
                  pl.BlockSpec((tq, aw), lambda b, i: (b * nstep + i, cols["q"])),
                  pl.BlockSpec((s_len, kw), lambda b, i: (b, cols["k"])),
                  pl.BlockSpec((s_len, kw), lambda b, i: (b, cols["v"])),
                  pl.BlockSpec((c_len, kw), lambda b, i: (b, cols["k"])),
                  pl.BlockSpec((c_len, kw), lambda b, i: (b, cols["v"])),
                  pl.BlockSpec((s_len, LANES), lambda b, i: (0, 0)),
                  pl.BlockSpec((s_len, LANES), lambda b, i: (0, 0)),
                  pl.BlockSpec((4, BLOCK, ncol), lambda b, i: (0, 0, 0))],
        out_specs=pl.BlockSpec((tq, aw), lambda b, i: (b * nstep + i, 0)),
        out_shape=jax.ShapeDtypeStruct((proj.shape[0], aw), BF16),
        compiler_params=_params(("parallel", "arbitrary")),
        name="window_attention",
    )(sink, proj, proj, proj, projc, projc, cos_t, sin_t, bias)


def _ctx_attn_kernel(sink_ref, q_ref, k_ref, v_ref, o_ref):
    group = ATT_HEADS // ATT_KV_HEADS
    scale = HEAD_DIM ** -0.5
    n = q_ref.shape[0]
    for h in range(ATT_KV_HEADS):
        ksl = slice(h * HEAD_DIM, (h + 1) * HEAD_DIM)
        qs, sinks = [], []
        for g in range(group):
            hh = h * group + g
            qs.append((q_ref[:, hh * HEAD_DIM:(hh + 1) * HEAD_DIM].astype(F32) * scale).astype(BF16))
            sinks.append(jnp.full((n, 1), sink_ref[hh], F32))
        q4 = jnp.concatenate(qs, axis=0)
        sk = jnp.concatenate(sinks, axis=0)
        s = _dot_nt(q4, k_ref[:, ksl])
        m = jnp.maximum(jnp.max(s, axis=-1, keepdims=True), sk)
        p = jnp.exp(s - m)
        denom = jnp.exp(sk - m) + jnp.sum(p, axis=-1, keepdims=True)
        o = _dot(p.astype(BF16), v_ref[:, ksl]) / denom
        for g in range(group):
            hh = h * group + g
            hs = slice(hh * HEAD_DIM, (hh + 1) * HEAD_DIM)
            o_ref[:, hs] = o[g * n:(g + 1) * n, :].astype(o_ref.dtype)


def _ctx_attention(sink, projc, bsz, cols):
    c_len = projc.shape[0] // bsz
    aw = ATT_HEADS * HEAD_DIM
    kw = ATT_KV_HEADS * HEAD_DIM
    return pl.pallas_call(
        _ctx_attn_kernel,
        grid=(bsz,),
        in_specs=[pl.BlockSpec(memory_space=pltpu.SMEM),
                  pl.BlockSpec((c_len, aw), lambda b: (b, cols["q"])),
                  pl.BlockSpec((c_len, kw), lambda b: (b, cols["k"])),
                  pl.BlockSpec((c_len, kw), lambda b: (b, cols["v"]))],
        out_specs=pl.BlockSpec((c_len, aw), lambda b: (b, 0)),
        out_shape=jax.ShapeDtypeStruct((projc.shape[0], aw), BF16),
        compiler_params=_params(("parallel",)),
        name="context_attention",
    )(sink, projc, projc, projc)


def _ret_kernel(dl_ref, q_ref, k_ref, v_ref, qc_ref, kc_ref, vc_ref, o_ref, oc_ref,
                ot_ref, att_ref, u_ref, st_ref, vt_ref, dm_ref):
    c = RET_CHUNK
    s_len = q_ref.shape[0]
    c_len = qc_ref.shape[0]
    lg = -_softplus(-dl_ref[0])
    lgf, lgb = lg[0:1, :], lg[1:2, :]
    wide = lambda t: jnp.concatenate([t] * (c // LANES), axis=1)
    kj = lax.broadcasted_iota(jnp.int32, (c, c), 0)
    qi = lax.broadcasted_iota(jnp.int32, (c, c), 1)
    diff = (qi - kj).astype(F32)
    dm_ref[...] = jnp.where(qi >= kj, jnp.exp(jnp.maximum(diff, 0.0) * wide(lgf)),
                            jnp.exp(jnp.maximum(-diff, 0.0) * wide(lgb)))
    idx = lax.broadcasted_iota(jnp.int32, (c, LANES), 0).astype(F32)
    qdec_f = jnp.exp((idx + 1.0) * lgf)
    kdec_f = jnp.exp((c - 1.0 - idx) * lgf)
    qdec_b = jnp.exp((c - idx) * lgb)
    kdec_b = jnp.exp(idx * lgb)
    cdec_f = jnp.exp(c * lgf)
    cdec_b = jnp.exp(c * lgb)

    def chunk(j):
        return pl.ds(pl.multiple_of(j * c, c), c)

    def transpose_v(vr, n):
        def body(j, _):
            vt_ref[:, chunk(j)] = vr[chunk(j), :].astype(F32).T.astype(BF16)
            return 0
        lax.fori_loop(0, n // c, body, 0, unroll=min(8, n // c))

    def scores(qr, kr, n):
        def body(j, _):
            att_ref[j] = (_dot_nt(kr[chunk(j), :], qr[chunk(j), :]) * dm_ref[...]).astype(BF16)
            return 0
        lax.fori_loop(0, n // c, body, 0, unroll=min(RET_UNROLL, n // c))

    def intra(kr, n):
        def body(j, _):
            vt = vt_ref[:, chunk(j)]
            ot_ref[:, chunk(j)] = _dot(vt, att_ref[j])
            kf = kr[chunk(j), :].astype(F32)
            kcat = jnp.concatenate([(kf * kdec_f).astype(BF16), (kf * kdec_b).astype(BF16)], axis=1)
            u_ref[j] = _dot(vt, kcat)
            return 0
        lax.fori_loop(0, n // c, body, 0, unroll=min(RET_UNROLL, n // c))

    def states(n, sf, sb):
        nch = n // c

        def fbody(j, s):
            st_ref[j, :, 0:RET_DK] = s.astype(BF16)
            return s * cdec_f + u_ref[j, :, 0:RET_DK]
        sf = lax.fori_loop(0, nch, fbody, sf)

        def bbody(jj, s):
            j = nch - 1 - jj
            st_ref[j, :, RET_DK:] = s.astype(BF16)
            return s * cdec_b + u_ref[j, :, RET_DK:]
        sb = lax.fori_loop(0, nch, bbody, sb)
        return sf, sb

    def cross(qr, outr, n):
        def body(j, _):
            qf = qr[chunk(j), :].astype(F32)
            qcat = jnp.concatenate([(qf * qdec_f).astype(BF16), (qf * qdec_b).astype(BF16)], axis=1)
            ot = ot_ref[:, chunk(j)] + _dot_nt(st_ref[j], qcat)
            o = ot.T * (RET_DK ** -0.5)
            outr[chunk(j), :] = (o * lax.rsqrt(jnp.mean(o * o, axis=-1, keepdims=True) + EPS)).astype(outr.dtype)
            return 0
        lax.fori_loop(0, n // c, body, 0, unroll=min(RET_UNROLL, n // c))

    def run(qr, kr, vr, outr, n, sf, sb):
        transpose_v(vr, n)
        scores(qr, kr, n)
        intra(kr, n)
        finals = states(n, sf, sb)
        cross(qr, outr, n)
        return finals

    zero = jnp.zeros((LANES, RET_DK), F32)
    sf, sb = run(qc_ref, kc_ref, vc_ref, oc_ref, c_len, zero, zero)
    run(q_ref, k_ref, v_ref, o_ref, s_len, sf, sb)


def _retention(dl, proj, projc, bsz, cols):
    s_len = proj.shape[0] // bsz
    c_len = projc.shape[0] // bsz
    hb = lambda n, col: pl.BlockSpec((n, LANES), lambda b, h: (b, col + h))
    return pl.pallas_call(
        _ret_kernel,
        grid=(bsz, RET_HEADS),
        in_specs=[pl.BlockSpec((1, 2, LANES), lambda b, h: (h, 0, 0)),
                  hb(s_len, cols["q"]), hb(s_len, cols["k"]), hb(s_len, cols["v"]),
                  hb(c_len, cols["q"]), hb(c_len, cols["k"]), hb(c_len, cols["v"])],
        out_specs=[pl.BlockSpec((s_len, LANES), lambda b, h: (b, h)),
                   pl.BlockSpec((c_len, LANES), lambda b, h: (b, h))],
        out_shape=[jax.ShapeDtypeStruct((proj.shape[0], RET_HEADS * LANES), BF16),
                   jax.ShapeDtypeStruct((projc.shape[0], RET_HEADS * LANES), BF16)],
        scratch_shapes=[pltpu.VMEM((LANES, s_len), F32),
                        pltpu.VMEM((s_len // RET_CHUNK, RET_CHUNK, RET_CHUNK), BF16),
                        pltpu.VMEM((s_len // RET_CHUNK, LANES, 2 * RET_DK), F32),
                        pltpu.VMEM((s_len // RET_CHUNK, LANES, 2 * RET_DK), BF16),
                        pltpu.VMEM((LANES, s_len), BF16),
                        pltpu.VMEM((RET_CHUNK, RET_CHUNK), F32)],
        compiler_params=_params(("parallel", "parallel")),
        name="retention",
    )(dl, proj, proj, proj, projc, projc, projc)


def _hy_hid_kernel(z_ref, w1_ref, b1_ref, w2_ref, b2_ref, f_ref, o_ref):
    f = f_ref[...]
    h = jnp.sin(f * (_dot(z_ref[...], w1_ref[...], precision=HIGHEST) + b1_ref[...]))
    o_ref[...] = jnp.sin(f * (_dot(h, w2_ref[...], precision=HIGHEST) + b2_ref[...]))


def _hy_hidden(zfull, w1p, b1, w2, b2, freq):
    n, e = zfull.shape
    fd = w2.shape[0]
    tr = min(n, 1024)
    full = lambda shp: pl.BlockSpec(shp, lambda i: (0, 0))
    return pl.pallas_call(
        _hy_hid_kernel,
        grid=(n // tr,),
        in_specs=[pl.BlockSpec((tr, e), lambda i: (i, 0)), full((e, fd)), full((1, fd)), full((fd, fd)),
                  full((1, fd)), full((1, fd))],
        out_specs=pl.BlockSpec((tr, fd), lambda i: (i, 0)),
        out_shape=jax.ShapeDtypeStruct((n, fd), F32),
        compiler_params=_params(("parallel",)),
        name="hyena_filter_mlp",
    )(zfull, w1p, b1, w2, b2, freq)


def _hy_filt_kernel(hid_ref, w3f_ref, w3b_ref, tn_ref, dl_ref, o_ref):
    half = hid_ref.shape[0] // 2
    decay = jnp.exp(-tn_ref[...] * jnp.abs(dl_ref[...]))
    top = _dot(hid_ref[0:half, :], w3f_ref[...], precision=HIGHEST)
    bot = _dot(hid_ref[half:, :], w3b_ref[...], precision=HIGHEST)
    row = lax.broadcasted_iota(jnp.int32, bot.shape, 0)
    bot = jnp.where(row == 0, 0.0, bot)
    o_ref[0, 0:half, :] = top * decay[0:half, :]
    o_ref[0, half:, :] = bot * decay[half:, :]


def _hy_filters(hid, w3, tn_full, deltas):
    n, fd = hid.shape
    wch = deltas.shape[1]
    nsl = wch // LANES
    return pl.pallas_call(
        _hy_filt_kernel,
        grid=(HY_ORDER, nsl),
        in_specs=[pl.BlockSpec((n, fd), lambda o, j: (0, 0)),
                  pl.BlockSpec((fd, LANES), lambda o, j: (0, o * 2 * nsl + j)),
                  pl.BlockSpec((fd, LANES), lambda o, j: (0, o * 2 * nsl + nsl + j)),
                  pl.BlockSpec((n, LANES), lambda o, j: (0, 0)),
                  pl.BlockSpec((1, LANES), lambda o, j: (0, j))],
        out_specs=pl.BlockSpec((1, n, LANES), lambda o, j: (o, 0, j)),
        out_shape=jax.ShapeDtypeStruct((HY_ORDER, n, wch), F32),
        compiler_params=_params(("parallel", "parallel")),
        name="hyena_filter",
    )(hid, w3, w3, tn_full, deltas)


def _hy_kfft_kernel(kern_ref, f1_ref, f2_ref, o_ref, g_ref, *, n1, n2):
    pg = 2 * n1 + HY_PITCH_PAD

    def stage1(i2, _):
        x = kern_ref[pl.ds(i2, n1, stride=n2), :].astype(BF16)
        g_ref[pl.ds(pl.multiple_of(i2 * pg, 8), 2 * n1), :] = _dot(f1_ref[i2], x)
        return 0
    lax.fori_loop(0, n2, stage1, 0, unroll=HY_UNROLL)

    def stage2(k1, _):
        x = jnp.concatenate([g_ref[pl.ds(k1, n2, stride=pg), :],
                             g_ref[pl.ds(n1 + k1, n2, stride=pg), :]], axis=0).astype(BF16)
        o_ref[pl.ds(pl.multiple_of(k1 * 2 * n2, 2 * n2), 2 * n2), :] = _dot(f2_ref[...], x)
        return 0
    lax.fori_loop(0, n1, stage2, 0, unroll=HY_UNROLL // 2)


def _hy_kfft(kern, f1k, f2, n1, n2):
    orders, n, wch = kern.shape
    nsl = wch // LANES
    return pl.pallas_call(
        functools.partial(_hy_kfft_kernel, n1=n1, n2=n2),
        grid=(orders, nsl),
        in_specs=[pl.BlockSpec((None, n, LANES), lambda o, j: (o, 0, j)),
                  pl.BlockSpec(f1k.shape, lambda o, j: (0, 0, 0)),
                  pl.BlockSpec(f2.shape, lambda o, j: (0, 0))],
        out_specs=pl.BlockSpec((None, None, 2 * n, LANES), lambda o, j: (o, j, 0, 0)),
        out_shape=jax.ShapeDtypeStruct((orders, nsl, 2 * n, LANES), F32),
        scratch_shapes=[pltpu.VMEM((n2 * (2 * n1 + HY_PITCH_PAD), LANES), F32)],
        compiler_params=_params(("parallel", "parallel")),
        name="hyena_filter_fft",
    )(kern, f1k, f2)


def _hy_prep_kernel(z_ref, w_ref, b_ref, o_ref, zp_ref, *, n2, nseq):
    l_len = z_ref.shape[0] // nseq
    h1 = l_len // n2
    halo = HY_HALO
    pitch = n2 + 2 * halo + 8
    zeros_h = jnp.zeros((halo, LANES), F32)
    for s in range(nseq):
        for i1 in range(h1):
            base = (s * h1 + i1) * pitch
            lo, hi = i1 * n2 - halo, (i1 + 1) * n2 + halo
            if lo < 0:
                zp_ref[base:base + halo, :] = zeros_h
            if hi > l_len:
                zp_ref[base + n2 + halo:base + n2 + 2 * halo, :] = zeros_h
            lo_c, hi_c = max(lo, 0), min(hi, l_len)
            zp_ref[base + (lo_c - lo):base + (hi_c - lo), :] = (
                z_ref[s * l_len + lo_c:s * l_len + hi_c, :].astype(F32))
    w0, w1, w2, b = w_ref[0:1, :], w_ref[1:2, :], w_ref[2:3, :], b_ref[...]

    def tap(j):
        return zp_ref[pl.ds(halo + j, nseq * h1, stride=pitch), :]

    def body(i2, carry):
        zm, z0 = carry
        zn = tap(i2 + 1)
        val = (b + w0 * zm + w1 * z0 + w2 * zn).astype(o_ref.dtype)
        for s in range(nseq):
            o_ref[pl.ds(pl.multiple_of(s * l_len + i2 * h1, h1), h1), :] = val[s * h1:(s + 1) * h1, :]
        return z0, zn
    lax.fori_loop(0, n2, body, (tap(-1), tap(0)), unroll=HY_UNROLL if nseq <= 2 else 1)


def _hy_stage_dtype(l_len, n2):
    return BF16 if (l_len // n2) % 16 == 0 else F32


def _hy_prep(proj, z_col, nz, bsz, conv_w, conv_b, n2):
    l_len = proj.shape[0] // bsz
    h1 = l_len // n2
    nseq = max(1, min(bsz, HY_PREP_ROWS // l_len))
    while bsz % nseq:
        nseq -= 1
    return pl.pallas_call(
        functools.partial(_hy_prep_kernel, n2=n2, nseq=nseq),
        grid=(bsz // nseq, nz),
        in_specs=[pl.BlockSpec((nseq * l_len, LANES), lambda b, j: (b, z_col + j)),
                  pl.BlockSpec((3, LANES), lambda b, j: (0, j)),
                  pl.BlockSpec((1, LANES), lambda b, j: (0, j))],
        out_specs=pl.BlockSpec((nseq * l_len, LANES), lambda b, j: (b, j)),
        out_shape=jax.ShapeDtypeStruct((proj.shape[0], nz * LANES), _hy_stage_dtype(l_len, n2)),
        scratch_shapes=[pltpu.VMEM((nseq * h1 * (n2 + 2 * HY_HALO + 8), LANES), F32)],
        compiler_params=_params(("parallel", "parallel")),
        name="hyena_short_conv",
    )(proj, conv_w, conv_b)


def _pack2(re, im):
    r = lax.bitcast_convert_type(re, jnp.uint32) + jnp.uint32(0x8000)
    i = lax.bitcast_convert_type(im, jnp.uint32) + jnp.uint32(0x8000)
    return (r & jnp.uint32(0xFFFF0000)) | (i >> 16)


def _unpack2(w):
    re = lax.bitcast_convert_type(w & jnp.uint32(0xFFFF0000), F32)
    im = lax.bitcast_convert_type(w << 16, F32)
    return jnp.concatenate([re, im], axis=0).astype(BF16)


def _hy_conv_kernel(y_ref, x_ref, kh_ref, bias_ref, f1_ref, f2_ref, f2i_ref, f1i_ref, o_ref, g_ref,
                    *, n1, n2, natural_out):
    h1 = n1 // 2
    l_len = h1 * n2
    pg = n1 + HY_PITCH_PAD
    bias = bias_ref[...]
    unroll = min(HY_UNROLL, n1)

    def rows(seq, i2):
        return pl.ds(pl.multiple_of(seq * l_len + i2 * h1, h1), h1)

    def lanes2(a, b):
        return jnp.concatenate([a, b], axis=1)

    def stage1(i2, _):
        x = lanes2(jnp.concatenate([y_ref[rows(0, i2), :], y_ref[rows(1, i2), :]], axis=0),
                   jnp.concatenate([y_ref[rows(2, i2), :], y_ref[rows(3, i2), :]], axis=0)).astype(BF16)
        a = _dot(f1_ref[i2], x)
        w = _pack2(a[0:n1, :], a[n1:, :])
        blk = pl.ds(pl.multiple_of(i2 * pg, 8), n1)
        g_ref[0, blk, :] = w[:, 0:LANES]
        g_ref[1, blk, :] = w[:, LANES:]
        return 0
    lax.fori_loop(0, n2, stage1, 0, unroll=2 * unroll)

    def stage23(t, _):
        outs = []
        for u in range(unroll):
            k1 = t * unroll + u
            col = pl.ds(k1, n2, stride=pg)
            x = lanes2(_unpack2(g_ref[0, col, :]), _unpack2(g_ref[1, col, :]))
            yh = _dot(f2_ref[...], x)
            base = pl.multiple_of(k1 * 2 * n2, 2 * n2)
            kr = kh_ref[pl.ds(base, n2), :]
            ki = kh_ref[pl.ds(base + n2, n2), :]
            kr, ki = lanes2(kr, kr), lanes2(ki, ki)
            yr, yi = yh[0:n2, :], yh[n2:, :]
            z = jnp.concatenate([yr * kr - yi * ki, yr * ki + yi * kr], axis=0).astype(BF16)
            c = _dot(f2i_ref[...], z)
            outs.append((col, _pack2(c[0:n2, :], c[n2:, :])))
        for col, w in outs:
            g_ref[0, col, :] = w[:, 0:LANES]
            g_ref[1, col, :] = w[:, LANES:]
        return 0
    lax.fori_loop(0, n1 // unroll, stage23, 0)

    def stage4(i2, _):
        blk = pl.ds(pl.multiple_of(i2 * pg, 8), n1)
        x = lanes2(_unpack2(g_ref[0, blk, :]), _unpack2(g_ref[1, blk, :]))
        yc = _dot(f1i_ref[i2], x)
        for seq in range(4):
            pair, half = seq // 2, seq % 2
            sl = rows(seq, i2)
            conv = yc[half * h1:(half + 1) * h1, pair * LANES:(pair + 1) * LANES]
            val = x_ref[sl, :].astype(F32) * (conv + bias * y_ref[sl, :].astype(F32))
            if natural_out:
                o_ref[pl.ds(seq * l_len + i2, h1, stride=n2), :] = val
            else:
                o_ref[sl, :] = val.astype(o_ref.dtype)
        return 0
    lax.fori_loop(0, n2, stage4, 0, unroll=unroll)


def _hy_conv(ysrc, ycol, xsrc, xcol, khat, order, bias, mats, bsz, n1, n2, natural_out):
    l_len = ysrc.shape[0] // bsz
    nsl = khat.shape[1]
    f1, f2, f2i, f1i = mats
    once = pl.Buffered(1)
    cst2 = lambda a: pl.BlockSpec(a.shape, lambda j, p: (0, 0), pipeline_mode=once)
    cst3 = lambda a: pl.BlockSpec(a.shape, lambda j, p: (0, 0, 0), pipeline_mode=once)
    return pl.pallas_call(
        functools.partial(_hy_conv_kernel, n1=n1, n2=n2, natural_out=natural_out),
        grid=(nsl, bsz // 4),
        in_specs=[pl.BlockSpec((4 * l_len, LANES), lambda j, p: (p, ycol + j)),
                  pl.BlockSpec((4 * l_len, LANES), lambda j, p: (p, xcol + j)),
                  pl.BlockSpec((None, None, khat.shape[2], LANES), lambda j, p: (order, j, 0, 0),
                               pipeline_mode=once),
                  pl.BlockSpec((1, LANES), lambda j, p: (0, j)),
                  cst3(f1), cst2(f2), cst2(f2i), cst3(f1i)],
        out_specs=pl.BlockSpec((4 * l_len, LANES), lambda j, p: (p, j)),
        out_shape=jax.ShapeDtypeStruct((ysrc.shape[0], nsl * LANES),
                                       F32 if natural_out else _hy_stage_dtype(l_len, n2)),
        scratch_shapes=[pltpu.VMEM((2, n2 * (n1 + HY_PITCH_PAD), LANES), jnp.uint32)],
        compiler_params=_params(("parallel", "arbitrary")),
        name="hyena_long_conv",
    )(ysrc, xsrc, khat, bias[order][None], f1, f2, f2i, f1i)


def _dft_tables(l_len):
    n = 2 * l_len
    n2 = 128 if l_len >= 1024 else 32
    n1 = n // n2
    h1 = n1 // 2
    i1 = np.arange(n1)[None, None, :]
    k1 = np.arange(n1)[None, :, None]
    i2 = np.arange(n2)[:, None, None]
    ph = 2 * np.pi * (i1 * k1 / n1 + i2 * k1 / n)
    c, s = np.cos(ph), np.sin(ph)
    ch, sh = c[:, :, :h1], s[:, :, :h1]
    f1 = np.concatenate([np.concatenate([ch, sh], 2), np.concatenate([-sh, ch], 2)], 1)
    f1k = np.concatenate([c, -s], 1)
    ct, st = np.swapaxes(ch, 1, 2), np.swapaxes(sh, 1, 2)
    f1i = np.concatenate([np.concatenate([ct, -st], 2), np.concatenate([st, ct], 2)], 1) / n
    a = np.arange(n2)
    ph2 = 2 * np.pi * np.outer(a, a) / n2
    c2, s2 = np.cos(ph2), np.sin(ph2)
    f2 = np.block([[c2, s2], [-s2, c2]])
    f2i = np.block([[c2, -s2], [s2, c2]])
    bf = lambda m: jnp.asarray(m, dtype=F32).astype(BF16)
    return n1, n2, (bf(f1), bf(f2), bf(f2i), bf(f1i)), bf(f1k)


def _filter_positions(l_len, width):
    lag = np.concatenate([np.arange(l_len), l_len - np.arange(l_len)]).astype(np.float64)
    t = lag / (l_len - 1)
    bands = (HY_EMB - 1) // 2
    w = 2.0 * np.pi * lag / l_len
    f = np.linspace(1e-4, bands - 1, bands)[None]
    z = np.concatenate([t[:, None], np.cos(f * w[:, None]), -np.sin(f * w[:, None])], axis=-1)
    z = np.pad(z, ((0, 0), (0, HY_EMB_PAD - HY_EMB)))
    tn = np.repeat(t[:, None], LANES, axis=1)
    max_decay = math.log(HY_DECAY_TARGET) / HY_FAST_PCT
    min_decay = math.log(HY_DECAY_TARGET) / HY_SLOW_PCT
    deltas = np.linspace(min_decay, max_decay, width)[None]
    return jnp.asarray(z, F32), jnp.asarray(tn, F32), jnp.asarray(deltas, F32)


def _rope_tables(seq):
    n_rows = seq // GRID_W
    row = np.repeat(np.arange(n_rows), GRID_W).astype(np.float64)
    col = np.tile(np.arange(GRID_W), n_rows).astype(np.float64)
    half = HEAD_DIM // 2
    inv = ROPE_BASE ** (-np.arange(0, half, 2, dtype=np.float64) / half)
    ar, ac = row[:, None] * inv, col[:, None] * inv
    cos = np.concatenate([np.cos(ar), np.cos(ar), np.cos(ac), np.cos(ac)], axis=1)
    sin = np.concatenate([-np.sin(ar), np.sin(ar), -np.sin(ac), np.sin(ac)], axis=1)
    return jnp.asarray(cos, F32), jnp.asarray(sin, F32)


def _hyena_spectra(l_len, width, w1, b1, w2, b2, w3, freq, tables):
    n1, n2, mats, f1k = tables
    zfull, tn_full, deltas = _filter_positions(l_len, width)
    w1p = jnp.pad(w1, ((0, HY_EMB_PAD - HY_EMB), (0, 0)))
    hid = _hy_hidden(zfull, w1p, b1[None], w2, b2[None], freq[None])
    kern = _hy_filters(hid, w3, tn_full, deltas)
    return _hy_kfft(kern, f1k, mats[1], n1, n2)


def _hyena_seq(zc, bsz, khat, bias, tables):
    n1, n2, mats, _ = tables
    nsl = khat.shape[1]
    y1 = _hy_conv(zc, 0, zc, nsl, khat, 0, bias, mats, bsz, n1, n2, False)
    return _hy_conv(y1, 0, zc, 2 * nsl, khat, 1, bias, mats, bsz, n1, n2, True)


def kernel(x, c, ctx, c_ctx, mod_w, mod_b, norm_pre, norm_post, ev_w_in, ev_w_out, lru_conv_w, lru_conv_b, lru_wa, lru_ba, lru_wx, lru_bx, lru_lambda, attn_sink, od_w_in, od_w_out, hy_conv_w, hy_conv_b, hy_w1, hy_b1, hy_w2, hy_b2, hy_w3, hy_freq, hy_bias, ret_decay_logit):
    bsz, s_len, d = x.shape
    c_len = ctx.shape[1]
    depth = mod_w.shape[0]
    assert bsz % 4 == 0 and bsz <= 16 and s_len % 1024 == 0 and c_len % 256 == 0 and c_len <= s_len

    crows = jnp.concatenate([c, c_ctx[None], jnp.zeros((24 - bsz - 1, d), F32)], axis=0)
    mod = _modulation(crows, mod_w, mod_b)

    cos_t, sin_t = _rope_tables(s_len)
    tab_l = _dft_tables(s_len)
    tab_c = _dft_tables(c_len)

    x2 = x.reshape(bsz * s_len, d)
    ctx2 = ctx.reshape(bsz * c_len, d)
    tm = ROW_TILE

    for l in range(depth):
        need_ctx = l < depth - 1
        shift, scale, gate = (mod[l, :bsz, i * d:(i + 1) * d].reshape(bsz, 1, d) for i in range(3))
        shift_c, scale_c, gate_c = (mod[l, bsz:bsz + 1, i * d:(i + 1) * d].reshape(1, 1, d) for i in range(3))
        g_pre = norm_pre[l][None]
        g_post = norm_post[l][None]

        def proj(w, tn):
            wb = w.astype(BF16)
            p = _in_proj(x2, g_pre, scale, shift, s_len, wb, BF16, IN_ROW_TILE, tn)
            pc = _in_proj(ctx2, g_pre, scale_c, shift_c, bsz * c_len, wb, BF16, IN_ROW_TILE, tn)
            return p, pc

        if l % 2 == 0:
            e = l // 2
            w_in = ev_w_in[e]
            w_ord = jnp.concatenate([w_in[:, 1024:2048], w_in[:, 2048:3072], w_in[:, 3584:4608],
                                     w_in[:, 3072:3584], w_in[:, 0:1024]], axis=1)
            pr, prc = proj(w_ord, EVEN_COL_TILE)
            cols = {"q": 1, "gb": 2, "k": 12, "v": 13}
            xa_col = 3584 // LANES
            nblk = lru_wa.shape[2]
            wg = jnp.stack([lru_wa[e, 0], lru_wx[e, 0], lru_wa[e, 1], lru_wx[e, 1]], axis=1)
            wg = jnp.transpose(wg, (0, 2, 1, 3)).reshape(nblk, LRU_BLOCK_W, 4 * LRU_BLOCK_W)
            bg = jnp.stack([lru_ba[e, 0], lru_bx[e, 0], lru_ba[e, 1], lru_bx[e, 1]], axis=0)
            bg = jnp.transpose(bg.reshape(4, nblk, LRU_BLOCK_W), (1, 0, 2)).reshape(nblk, 1, 4 * LRU_BLOCK_W)
            wg, bg = (0.5 * wg).astype(BF16), 0.5 * bg
            ya, yac = _lru(pr, prc, xa_col, bsz, lru_conv_w[e], lru_conv_b[e][None], wg, bg, lru_lambda[e])
            yb = _attention(attn_sink[e], pr, prc, bsz, cols, cos_t, sin_t)
            w_out = ev_w_out[e].astype(BF16)
            x2_new = _out_proj(ya, pr, 0, yb, pr, cols["gb"], w_out, x2, g_post, gate, s_len, tm)
            if need_ctx:
                ybc = _ctx_attention(attn_sink[e], prc, bsz, cols)
                ctx2 = _out_proj(yac, prc, 0, ybc, prc, cols["gb"], w_out, ctx2, g_post, gate_c, bsz * c_len, tm)
            x2 = x2_new
        else:
            o = l // 2
            w_in = od_w_in[o]
            w_ord = jnp.concatenate([w_in[:, 3072:8192], w_in[:, 0:3072]], axis=1)
            pr, prc = proj(w_ord, ODD_COL_TILE)
            cols = {"q": 8, "k": 16, "v": 24}
            z_col = 5120 // LANES
            width = hy_bias.shape[2]
            nz = (HY_ORDER + 1) * width // LANES
            fargs = (hy_w1[o], hy_b1[o], hy_w2[o], hy_b2[o], hy_w3[o], hy_freq[o])
            khat = _hyena_spectra(s_len, width, *fargs, tab_l)
            zconv = _hy_prep(pr, z_col, nz, bsz, hy_conv_w[o], hy_conv_b[o][None], tab_l[1])
            yh = _hyena_seq(zconv, bsz, khat, hy_bias[o], tab_l)
            dl = jnp.broadcast_to(jnp.transpose(ret_decay_logit[o])[:, :, None], (RET_HEADS, 2, LANES))
            yd, ydc = _retention(dl, pr, prc, bsz, cols)
            w_out = od_w_out[o].astype(BF16)
            x2_new = _out_proj(yh, pr, 0, yd, pr, 4, w_out, x2, g_post, gate, s_len, tm)
            if need_ctx:
                khat_c = _hyena_spectra(c_len, width, *fargs, tab_c)
                zcconv = _hy_prep(prc, z_col, nz, bsz, hy_conv_w[o], hy_conv_b[o][None], tab_c[1])
                yhc = _hyena_seq(zcconv, bsz, khat_c, hy_bias[o], tab_c)
                ctx2 = _out_proj(yhc, prc, 0, ydc, prc, 4, w_out, ctx2, g_post, gate_c, bsz * c_len, tm)
            x2 = x2_new
    return x2.reshape(bsz, s_len, d)
```

```python
import functools
import math

import numpy as np
import jax
import jax.numpy as jnp
from jax import lax
from jax.experimental import pallas as pl
from jax.experimental.pallas import tpu as pltpu

F32 = jnp.float32
BF16 = jnp.bfloat16
HIGHEST = lax.Precision.HIGHEST

EPS = 1e-6
GRID_W = 64
LANES = 128
LRU_BLOCK_W = 128
LRU_CONV = 4
LRU_C = 8.0
LRU_SEG = 16
LRU_POS = 64
LRU_TINY = 1e-30
LRU_HALO = 16
ATT_HEADS = 8
ATT_KV_HEADS = 2
HEAD_DIM = 128
BLOCK = 128
ATT_QSUB = 4
ROPE_BASE = 10000.0
HY_ORDER = 2
HY_EMB = 33
HY_EMB_PAD = 40
HY_DECAY_TARGET = 1e-2
HY_FAST_PCT = 0.3
HY_SLOW_PCT = 1.5
HY_UNROLL = 32
HY_PREP_ROWS = 16384
HY_HALO = 16
HY_PITCH_PAD = 8
RET_HEADS = 8
RET_DK = 128
RET_CHUNK = 256
RET_UNROLL = 8

V7X_VMEM_BYTES = 64 * 1024 * 1024
VMEM_LIMIT = V7X_VMEM_BYTES - 8 * 1024 * 1024
ROW_TILE = 1024
IN_ROW_TILE = 2048
OUT_SPLIT = 4
EVEN_COL_TILE = 1536
ODD_COL_TILE = 2048
NEG = -1e30


def _params(sem, vmem=VMEM_LIMIT, **kw):
    return pltpu.CompilerParams(dimension_semantics=sem, vmem_limit_bytes=vmem, **kw)


def _sigmoid(v):
    return 0.5 * (jnp.tanh(0.5 * v) + 1.0)


def _silu(v):
    return v * _sigmoid(v)


def _softplus(v):
    return jnp.maximum(v, 0.0) + jnp.log(1.0 + jnp.exp(-jnp.abs(v)))


def _dot(a, b, **kw):
    return jnp.dot(a, b, preferred_element_type=F32, **kw)


def _dot_nt(a, b):
    return lax.dot_general(a, b, (((1,), (1,)), ((), ())), preferred_element_type=F32)


def _mod_kernel(c_ref, w_ref, b_ref, o_ref):
    s = _silu(c_ref[...])
    o_ref[0] = _dot(s, w_ref[0], precision=HIGHEST) + b_ref[0]


def _modulation(crows, mod_w, mod_b):
    depth, d, n3 = mod_w.shape
    r = crows.shape[0]
    tn = 1024
    return pl.pallas_call(
        _mod_kernel,
        grid=(depth, n3 // tn),
        in_specs=[pl.BlockSpec((r, d), lambda l, j: (0, 0)),
                  pl.BlockSpec((1, d, tn), lambda l, j: (l, 0, j)),
                  pl.BlockSpec((1, 1, tn), lambda l, j: (l, 0, j))],
        out_specs=pl.BlockSpec((1, r, tn), lambda l, j: (l, 0, j)),
        out_shape=jax.ShapeDtypeStruct((depth, r, n3), F32),
        compiler_params=_params(("parallel", "parallel")),
        name="modulation",
    )(crows, mod_w, mod_b.reshape(depth, 1, n3))


def _rope(v, cos, sin, lane):
    swapped = jnp.where((lane & 63) < 32, pltpu.roll(v, LANES - 32, 1), pltpu.roll(v, 32, 1))
    return v * cos + swapped * sin


def _inproj_kernel(x_ref, g_ref, sc_ref, sh_ref, w_ref, o_ref, h_ref):
    @pl.when(pl.program_id(1) == 0)
    def _():
        x = x_ref[...]
        y = x * lax.rsqrt(jnp.mean(x * x, axis=-1, keepdims=True) + EPS) * g_ref[...]
        h_ref[...] = (y * (1.0 + sc_ref[0]) + sh_ref[0]).astype(BF16)

    o_ref[...] = _dot(h_ref[...], w_ref[...]).astype(o_ref.dtype)


def _in_proj(x2, g, scale, shift, rows_per_group, w, out_dtype, tm, tn):
    m, d = x2.shape
    n = w.shape[1]
    tpg = rows_per_group // tm
    return pl.pallas_call(
        _inproj_kernel,
        grid=(m // tm, n // tn),
        in_specs=[pl.BlockSpec((tm, d), lambda i, j: (i, 0)),
                  pl.BlockSpec((1, d), lambda i, j: (0, 0)),
                  pl.BlockSpec((1, 1, d), lambda i, j: (i // tpg, 0, 0)),
                  pl.BlockSpec((1, 1, d), lambda i, j: (i // tpg, 0, 0)),
                  pl.BlockSpec((d, tn), lambda i, j: (0, j))],
        out_specs=pl.BlockSpec((tm, tn), lambda i, j: (i, j)),
        out_shape=jax.ShapeDtypeStruct((m, n), out_dtype),
        scratch_shapes=[pltpu.VMEM((tm, d), BF16)],
        compiler_params=_params(("parallel", "arbitrary")),
        name="in_proj",
    )(x2, g, scale, shift, w)


def _outproj_kernel(a_ref, ga_ref, b_ref, gb_ref, w_ref, x_ref, g_ref, gate_ref, o_ref):
    wa = w_ref.shape[0] // 2
    rows = a_ref.shape[0] // OUT_SPLIT
    for part in range(OUT_SPLIT):
        sl = slice(part * rows, (part + 1) * rows)
        a = a_ref[sl, :].astype(BF16) * _silu(ga_ref[sl, :])
        b = b_ref[sl, :].astype(BF16) * _silu(gb_ref[sl, :])
        y = _dot(a, w_ref[0:wa, :]) + _dot(b, w_ref[wa:, :])
        yn = y * lax.rsqrt(jnp.mean(y * y, axis=-1, keepdims=True) + EPS) * g_ref[...]
        o_ref[sl, :] = x_ref[sl, :] + gate_ref[0] * yn


def _out_proj(a, ga, ga_col, b, gb, gb_col, w, x2, g, gate, rows_per_group, tm):
    m, d = x2.shape
    wa = w.shape[0] // 2
    tpg = rows_per_group // tm
    return pl.pallas_call(
        _outproj_kernel,
        grid=(m // tm,),
        in_specs=[pl.BlockSpec((tm, wa), lambda i: (i, 0)),
                  pl.BlockSpec((tm, wa), lambda i: (i, ga_col)),
                  pl.BlockSpec((tm, wa), lambda i: (i, 0)),
                  pl.BlockSpec((tm, wa), lambda i: (i, gb_col)),
                  pl.BlockSpec(w.shape, lambda i: (0, 0)),
                  pl.BlockSpec((tm, d), lambda i: (i, 0)),
                  pl.BlockSpec((1, d), lambda i: (0, 0)),
                  pl.BlockSpec((1, 1, d), lambda i: (i // tpg, 0, 0))],
        out_specs=pl.BlockSpec((tm, d), lambda i: (i, 0)),
        out_shape=jax.ShapeDtypeStruct((m, d), F32),
        compiler_params=_params(("parallel",)),
        name="out_proj",
    )(a, ga, b, gb, w, x2, g, gate)


def _lru_kernel(xa_ref, xac_ref, cw_ref, cb_ref, wg_ref, bg_ref, lam_ref, y_ref, yc_ref,
                xp_ref, hf_ref, pf_ref, hb_ref, pb_ref):
    nseg = LRU_SEG
    c8h = -0.5 * LRU_C * _softplus(-lam_ref[...])
    left = LRU_CONV // 2
    cw = [cw_ref[k:k + 1, :] for k in range(LRU_CONV)]
    cb = cb_ref[...]
    zeros_h = jnp.zeros((LRU_HALO, LANES), F32)
    zeros_s = jnp.zeros((nseg, LANES), F32)
    ones_s = jnp.ones((nseg, LANES), F32)

    def run(src_ref, out_ref, n, c0f, c0b):
        seg = n // nseg
        npos = min(LRU_POS, seg)
        nblk = seg // npos
        halo = LRU_HALO
        pitch = seg + 2 * halo + 8
        for j in range(nseg):
            lo, hi = j * seg - halo, (j + 1) * seg + halo
            if lo < 0:
                xp_ref[j * pitch:j * pitch + halo, :] = zeros_h
            if hi > n:
                xp_ref[j * pitch + seg + halo:j * pitch + seg + 2 * halo, :] = zeros_h
            lo_c, hi_c = max(lo, 0), min(hi, n)
            xp_ref[j * pitch + (lo_c - lo):j * pitch + (hi_c - lo), :] = src_ref[lo_c:hi_c, :].astype(F32)

        def gather(q):
            return jnp.concatenate([xp_ref[pl.ds(halo + q + c * 8 * pitch, 8, stride=pitch), :]
                                    for c in range(nseg // 8)], axis=0)

        def blk(i):
            return pl.ds(pl.multiple_of(i * npos * nseg, npos * nseg), npos * nseg)

        def pos(v, p):
            return v[p * nseg:(p + 1) * nseg, :]

        def fwd_body(i, carry):
            h, pc = carry
            p0 = i * npos
            xs = [gather(p0 + q - left) for q in range(npos + LRU_CONV - 1)]
            us = []
            for p in range(npos):
                u = cb + cw[0] * xs[p]
                for k in range(1, LRU_CONV):
                    u = u + cw[k] * xs[p + k]
                us.append(u)
            u = jnp.concatenate(us, axis=0)
            g = _dot(u.astype(BF16), wg_ref[0]) + bg_ref[0]
            hu = 0.5 * u
            coef = []
            for d in range(2):
                tr = jnp.tanh(g[:, (2 * d) * LANES:(2 * d + 1) * LANES])
                ti = jnp.tanh(g[:, (2 * d + 1) * LANES:(2 * d + 2) * LANES])
                a = jnp.exp(c8h[d:d + 1, :] * tr + c8h[d:d + 1, :])
                om = 1.0 - a * a
                coef.append((a, om * lax.rsqrt(jnp.maximum(om, LRU_TINY)) * ((ti + 1.0) * hu)))
            (af, bf), (ab, bb) = coef
            hb_ref[blk(i), :] = ab
            pb_ref[blk(i), :] = bb
            hs, ps = [], []
            for p in range(npos):
                a_p = pos(af, p)
                h = a_p * h + pos(bf, p)
                pc = a_p * pc
                hs.append(h)
                ps.append(pc)
            hf_ref[blk(i), :] = jnp.concatenate(hs, axis=0)
            pf_ref[blk(i), :] = jnp.concatenate(ps, axis=0)
            return h, pc

        hef, pef = lax.fori_loop(0, nblk, fwd_body, (zeros_s, ones_s))

        def bwd_body(ii, carry):
            h, pc = carry
            i = nblk - 1 - ii
            ab = hb_ref[blk(i), :]
            bb = pb_ref[blk(i), :]
            hs, ps = [None] * npos, [None] * npos
            for p in reversed(range(npos)):
                a_p = pos(ab, p)
                h = a_p * h + pos(bb, p)
                pc = a_p * pc
                hs[p] = h
                ps[p] = pc
            hb_ref[blk(i), :] = jnp.concatenate(hs, axis=0)
            pb_ref[blk(i), :] = jnp.concatenate(ps, axis=0)
            return h, pc

        heb, peb = lax.fori_loop(0, nblk, bwd_body, (zeros_s, ones_s))

        rows_f, c = [], c0f
        for j in range(nseg):
            rows_f.append(c)
            c = hef[j:j + 1, :] + pef[j:j + 1, :] * c
        final_f = c
        rows_b, c = [None] * nseg, c0b
        for j in reversed(range(nseg)):
            rows_b[j] = c
            c = heb[j:j + 1, :] + peb[j:j + 1, :] * c
        final_b = c
        cf = jnp.concatenate(rows_f, axis=0)
        cbk = jnp.concatenate(rows_b, axis=0)

        opitch = seg + 8

        def out_body(p, _):
            rs = pl.ds(pl.multiple_of(p * nseg, nseg), nseg)
            y = hf_ref[rs, :] + pf_ref[rs, :] * cf + hb_ref[rs, :] + pb_ref[rs, :] * cbk
            for c in range(nseg // 8):
                xp_ref[pl.ds(p + c * 8 * opitch, 8, stride=opitch), :] = y[c * 8:(c + 1) * 8, :]
            return 0
        lax.fori_loop(0, seg, out_body, 0, unroll=8)
        for j in range(nseg):
            out_ref[j * seg:(j + 1) * seg, :] = xp_ref[j * opitch:j * opitch + seg, :].astype(out_ref.dtype)
        return final_f, final_b

    zero = jnp.zeros((1, LANES), F32)
    ff, fb = run(xac_ref, yc_ref, xac_ref.shape[0], zero, zero)
    run(xa_ref, y_ref, xa_ref.shape[0], ff, fb)


def _lru(proj, projc, xa_col, bsz, conv_w, conv_b, wg, bg, lam):
    w = conv_w.shape[1]
    s_len = proj.shape[0] // bsz
    c_len = projc.shape[0] // bsz
    nblk = w // LANES
    seq = lambda n: pl.BlockSpec((n, LANES), lambda b, j: (b, j))
    src = lambda n: pl.BlockSpec((n, LANES), lambda b, j: (b, xa_col + j))
    return pl.pallas_call(
        _lru_kernel,
        grid=(bsz, nblk),
        in_specs=[src(s_len), src(c_len),
                  pl.BlockSpec((LRU_CONV, LANES), lambda b, j: (0, j)),
                  pl.BlockSpec((1, LANES), lambda b, j: (0, j)),
                  pl.BlockSpec((1, LANES, 4 * LANES), lambda b, j: (j, 0, 0)),
                  pl.BlockSpec((1, 1, 4 * LANES), lambda b, j: (j, 0, 0)),
                  pl.BlockSpec((2, LANES), lambda b, j: (0, j))],
        out_specs=[seq(s_len), seq(c_len)],
        out_shape=[jax.ShapeDtypeStruct((proj.shape[0], w), BF16), jax.ShapeDtypeStruct((projc.shape[0], w), BF16)],
        scratch_shapes=[pltpu.VMEM((s_len + (2 * LRU_HALO + 8) * LRU_SEG, LANES), F32)]
        + [pltpu.VMEM((s_len, LANES), F32)] * 4,
        compiler_params=_params(("parallel", "parallel")),
        name="rglru",
    )(proj, projc, conv_w, conv_b, wg, bg, lam)


def _attn_kernel(sink_ref, q_ref, k_ref, v_ref, kc_ref, vc_ref, cos_ref, sin_ref, bias_ref, o_ref):
    nb = pl.num_programs(1) * ATT_QSUB
    group = ATT_HEADS // ATT_KV_HEADS
    scale = HEAD_DIM ** -0.5
    lane = lax.broadcasted_iota(jnp.int32, (BLOCK, LANES), 1)
    ones = jnp.ones((bias_ref.shape[2], HEAD_DIM), BF16)

    def blk(i):
        return pl.ds(pl.multiple_of(i * BLOCK, BLOCK), BLOCK)

    def rot(x, tab):
        return _rope(x.astype(F32), tab[0], tab[1], lane).astype(BF16)

    for sub in range(ATT_QSUB):
        qb = pl.program_id(1) * ATT_QSUB + sub
        qrows = slice(sub * BLOCK, (sub + 1) * BLOCK)
        ip = jnp.maximum(qb - 1, 0)
        inx = jnp.minimum(qb + 1, nb - 1)
        tab_p = (cos_ref[blk(ip), :], sin_ref[blk(ip), :])
        tab_o = (cos_ref[blk(qb), :], sin_ref[blk(qb), :])
        tab_n = (cos_ref[blk(inx), :], sin_ref[blk(inx), :])
        tab_q = (tab_o[0] * scale, tab_o[1] * scale)
        variant = jnp.where(qb > 0, 1, 0) + jnp.where(qb < nb - 1, 2, 0)
        bias = jnp.concatenate([bias_ref[variant]] * group, axis=0)

        for h in range(ATT_KV_HEADS):
            ksl = slice(h * HEAD_DIM, (h + 1) * HEAD_DIM)
            kcat = jnp.concatenate([rot(k_ref[blk(ip), ksl], tab_p), rot(k_ref[blk(qb), ksl], tab_o),
                                    rot(k_ref[blk(inx), ksl], tab_n), kc_ref[:, ksl]], axis=0)
            vcat = jnp.concatenate([v_ref[blk(ip), ksl], v_ref[blk(qb), ksl], v_ref[blk(inx), ksl],
                                    vc_ref[:, ksl]], axis=0)
            vext = jnp.concatenate([vcat, ones], axis=1)
            heads = [h * group + g for g in range(group)]
            qs = jnp.concatenate([rot(q_ref[qrows, hh * HEAD_DIM:(hh + 1) * HEAD_DIM], tab_q) for hh in heads],
                                 axis=0)
            sk = jnp.concatenate([jnp.full((BLOCK, 1), sink_ref[hh], F32) for hh in heads], axis=0)
            s = _dot_nt(qs, kcat) + bias
            m = jnp.maximum(jnp.max(s, axis=-1, keepdims=True), sk)
            p = jnp.exp((s - m).astype(BF16))
            oe = _dot(p, vext)
            o = oe[:, 0:HEAD_DIM] / (jnp.exp(sk - m) + oe[:, HEAD_DIM:HEAD_DIM + 1])
            for g, hh in enumerate(heads):
                o_ref[qrows, hh * HEAD_DIM:(hh + 1) * HEAD_DIM] = o[g * BLOCK:(g + 1) * BLOCK, :].astype(o_ref.dtype)


def _attention(sink, proj, projc, bsz, cols, cos_t, sin_t):
    s_len = proj.shape[0] // bsz
    c_len = projc.shape[0] // bsz
    nb = s_len // BLOCK
    aw = ATT_HEADS * HEAD_DIM
    kw = ATT_KV_HEADS * HEAD_DIM
    qi = np.arange(BLOCK)[:, None]
    kj = np.arange(BLOCK)[None, :]
    variants = []
    for v in range(4):
        prev = np.where((kj >= qi) & bool(v & 1), 0.0, NEG)
        nxt = np.where((kj <= qi) & bool(v & 2), 0.0, NEG)
        variants.append(np.concatenate([prev, np.zeros((BLOCK, BLOCK)), nxt, np.zeros((BLOCK, c_len))], axis=1))
    bias = jnp.asarray(np.stack(variants), F32)
    ncol = 3 * BLOCK + c_len
    nstep = nb // ATT_QSUB
    tq = ATT_QSUB * BLOCK

    return pl.pallas_call(
        _attn_kernel,
        grid=(bsz, nstep),
        in_specs=[pl.BlockSpec(memory_space=pltpu.SMEM),
                  pl.BlockSpec((tq, aw), lambda b, i: (b * nstep + i, cols["q"])),
                  pl.BlockSpec((s_len, kw), lambda b, i: (b, cols["k"])),
                  pl.BlockSpec((s_len, kw), lambda b, i: (b, cols["v"])),
                  pl.BlockSpec((c_len, kw), lambda b, i: (b, cols["k"])),
                  pl.BlockSpec((c_len, kw), lambda b, i: (b, cols["v"])),
                  pl.BlockSpec((s_len, LANES), lambda b, i: (0, 0)),
                  pl.BlockSpec((s_len, LANES), lambda b, i: (0, 0)),
                  pl.BlockSpec((4, BLOCK, ncol), lambda b, i: (0, 0, 0))],
        out_specs=pl.BlockSpec((tq, aw), lambda b, i: (b * nstep + i, 0)),
        out_shape=jax.ShapeDtypeStruct((proj.shape[0], aw), BF16),
        compiler_params=_params(("parallel", "arbitrary")),
        name="window_attention",
    )(sink, proj, proj, proj, projc, projc, cos_t, sin_t, bias)


def _ctx_attn_kernel(sink_ref, q_ref, k_ref, v_ref, o_ref):
    group = ATT_HEADS // ATT_KV_HEADS
    scale = HEAD_DIM ** -0.5
    n = q_ref.shape[0]
    for h in range(ATT_KV_HEADS):
        ksl = slice(h * HEAD_DIM, (h + 1) * HEAD_DIM)
        qs, sinks = [], []
        for g in range(group):
            hh = h * group + g
            qs.append((q_ref[:, hh * HEAD_DIM:(hh + 1) * HEAD_DIM].astype(F32) * scale).astype(BF16))
            sinks.append(jnp.full((n, 1), sink_ref[hh], F32))
        q4 = jnp.concatenate(qs, axis=0)
        sk = jnp.concatenate(sinks, axis=0)
        s = _dot_nt(q4, k_ref[:, ksl])
        m = jnp.maximum(jnp.max(s, axis=-1, keepdims=True), sk)
        p = jnp.exp(s - m)
        denom = jnp.exp(sk - m) + jnp.sum(p, axis=-1, keepdims=True)
        o = _dot(p.astype(BF16), v_ref[:, ksl]) / denom
        for g in range(group):
            hh = h * group + g
            hs = slice(hh * HEAD_DIM, (hh + 1) * HEAD_DIM)
            o_ref[:, hs] = o[g * n:(g + 1) * n, :].astype(o_ref.dtype)


def _ctx_attention(sink, projc, bsz, cols):
    c_len = projc.shape[0] // bsz
    aw = ATT_HEADS * HEAD_DIM
    kw = ATT_KV_HEADS * HEAD_DIM
    return pl.pallas_call(
        _ctx_attn_kernel,
        grid=(bsz,),
        in_specs=[pl.BlockSpec(memory_space=pltpu.SMEM),
                  pl.BlockSpec((c_len, aw), lambda b: (b, cols["q"])),
                  pl.BlockSpec((c_len, kw), lambda b: (b, cols["k"])),
                  pl.BlockSpec((c_len, kw), lambda b: (b, cols["v"]))],
        out_specs=pl.BlockSpec((c_len, aw), lambda b: (b, 0)),
        out_shape=jax.ShapeDtypeStruct((projc.shape[0], aw), BF16),
        compiler_params=_params(("parallel",)),
        name="context_attention",
    )(sink, projc, projc, projc)


def _ret_kernel(dl_ref, q_ref, k_ref, v_ref, qc_ref, kc_ref, vc_ref, o_ref, oc_ref,
                ot_ref, att_ref, u_ref, st_ref, vt_ref, dm_ref):
    c = RET_CHUNK
    s_len = q_ref.shape[0]
    c_len = qc_ref.shape[0]
    lg = -_softplus(-dl_ref[0])
    lgf, lgb = lg[0:1, :], lg[1:2, :]
    wide = lambda t: jnp.concatenate([t] * (c // LANES), axis=1)
    kj = lax.broadcasted_iota(jnp.int32, (c, c), 0)
    qi = lax.broadcasted_iota(jnp.int32, (c, c), 1)
    diff = (qi - kj).astype(F32)
    dm_ref[...] = jnp.where(qi >= kj, jnp.exp(jnp.maximum(diff, 0.0) * wide(lgf)),
                            jnp.exp(jnp.maximum(-diff, 0.0) * wide(lgb)))
    idx = lax.broadcasted_iota(jnp.int32, (c, LANES), 0).astype(F32)
    qdec_f = jnp.exp((idx + 1.0) * lgf)
    kdec_f = jnp.exp((c - 1.0 - idx) * lgf)
    qdec_b = jnp.exp((c - idx) * lgb)
    kdec_b = jnp.exp(idx * lgb)
    cdec_f = jnp.exp(c * lgf)
    cdec_b = jnp.exp(c * lgb)

    def chunk(j):
        return pl.ds(pl.multiple_of(j * c, c), c)

    def transpose_v(vr, n):
        def body(j, _):
            vt_ref[:, chunk(j)] = vr[chunk(j), :].astype(F32).T.astype(BF16)
            return 0
        lax.fori_loop(0, n // c, body, 0, unroll=min(8, n // c))

    def scores(qr, kr, n):
        def body(j, _):
            att_ref[j] = (_dot_nt(kr[chunk(j), :], qr[chunk(j), :]) * dm_ref[...]).astype(BF16)
            return 0
        lax.fori_loop(0, n // c, body, 0, unroll=min(RET_UNROLL, n // c))

    def intra(kr, n):
        def body(j, _):
            vt = vt_ref[:, chunk(j)]
            ot_ref[:, chunk(j)] = _dot(vt, att_ref[j])
            kf = kr[chunk(j), :].astype(F32)
            kcat = jnp.concatenate([(kf * kdec_f).astype(BF16), (kf * kdec_b).astype(BF16)], axis=1)
            u_ref[j] = _dot(vt, kcat)
            return 0
        lax.fori_loop(0, n // c, body, 0, unroll=min(RET_UNROLL, n // c))

    def states(n, sf, sb):
        nch = n // c

        def fbody(j, s):
            st_ref[j, :, 0:RET_DK] = s.astype(BF16)
            return s * cdec_f + u_ref[j, :, 0:RET_DK]
        sf = lax.fori_loop(0, nch, fbody, sf)

        def bbody(jj, s):
            j = nch - 1 - jj
            st_ref[j, :, RET_DK:] = s.astype(BF16)
            return s * cdec_b + u_ref[j, :, RET_DK:]
        sb = lax.fori_loop(0, nch, bbody, sb)
        return sf, sb

    def cross(qr, outr, n):
        def body(j, _):
            qf = qr[chunk(j), :].astype(F32)
            qcat = jnp.concatenate([(qf * qdec_f).astype(BF16), (qf * qdec_b).astype(BF16)], axis=1)
            ot = (ot_ref[:, chunk(j)] + _dot_nt(st_ref[j], qcat)) * (RET_DK ** -0.5)
            on = ot * lax.rsqrt(jnp.mean(ot * ot, axis=0, keepdims=True) + EPS)
            outr[chunk(j), :] = on.T.astype(outr.dtype)
            return 0
        lax.fori_loop(0, n // c, body, 0, unroll=min(RET_UNROLL, n // c))

    def run(qr, kr, vr, outr, n, sf, sb):
        transpose_v(vr, n)
        scores(qr, kr, n)
        intra(kr, n)
        finals = states(n, sf, sb)
        cross(qr, outr, n)
        return finals

    zero = jnp.zeros((LANES, RET_DK), F32)
    sf, sb = run(qc_ref, kc_ref, vc_ref, oc_ref, c_len, zero, zero)
    run(q_ref, k_ref, v_ref, o_ref, s_len, sf, sb)


def _retention(dl, proj, projc, bsz, cols):
    s_len = proj.shape[0] // bsz
    c_len = projc.shape[0] // bsz
    hb = lambda n, col: pl.BlockSpec((n, LANES), lambda b, h: (b, col + h))
    return pl.pallas_call(
        _ret_kernel,
        grid=(bsz, RET_HEADS),
        in_specs=[pl.BlockSpec((1, 2, LANES), lambda b, h: (h, 0, 0)),
                  hb(s_len, cols["q"]), hb(s_len, cols["k"]), hb(s_len, cols["v"]),
                  hb(c_len, cols["q"]), hb(c_len, cols["k"]), hb(c_len, cols["v"])],
        out_specs=[pl.BlockSpec((s_len, LANES), lambda b, h: (b, h)),
                   pl.BlockSpec((c_len, LANES), lambda b, h: (b, h))],
        out_shape=[jax.ShapeDtypeStruct((proj.shape[0], RET_HEADS * LANES), BF16),
                   jax.ShapeDtypeStruct((projc.shape[0], RET_HEADS * LANES), BF16)],
        scratch_shapes=[pltpu.VMEM((LANES, s_len), F32),
                        pltpu.VMEM((s_len // RET_CHUNK, RET_CHUNK, RET_CHUNK), BF16),
                        pltpu.VMEM((s_len // RET_CHUNK, LANES, 2 * RET_DK), F32),
                        pltpu.VMEM((s_len // RET_CHUNK, LANES, 2 * RET_DK), BF16),
                        pltpu.VMEM((LANES, s_len), BF16),
                        pltpu.VMEM((RET_CHUNK, RET_CHUNK), F32)],
        compiler_params=_params(("parallel", "parallel")),
        name="retention",
    )(dl, proj, proj, proj, projc, projc, projc)


def _hy_hid_kernel(z_ref, w1_ref, b1_ref, w2_ref, b2_ref, f_ref, o_ref):
    f = f_ref[...]
    h = jnp.sin(f * (_dot(z_ref[...], w1_ref[...], precision=HIGHEST) + b1_ref[...]))
    o_ref[...] = jnp.sin(f * (_dot(h, w2_ref[...], precision=HIGHEST) + b2_ref[...]))


def _hy_hidden(zfull, w1p, b1, w2, b2, freq):
    n, e = zfull.shape
    fd = w2.shape[0]
    tr = min(n, 1024)
    full = lambda shp: pl.BlockSpec(shp, lambda i: (0, 0))
    return pl.pallas_call(
        _hy_hid_kernel,
        grid=(n // tr,),
        in_specs=[pl.BlockSpec((tr, e), lambda i: (i, 0)), full((e, fd)), full((1, fd)), full((fd, fd)),
                  full((1, fd)), full((1, fd))],
        out_specs=pl.BlockSpec((tr, fd), lambda i: (i, 0)),
        out_shape=jax.ShapeDtypeStruct((n, fd), F32),
        compiler_params=_params(("parallel",)),
        name="hyena_filter_mlp",
    )(zfull, w1p, b1, w2, b2, freq)


def _hy_filt_kernel(hid_ref, w3f_ref, w3b_ref, tn_ref, dl_ref, o_ref):
    half = hid_ref.shape[0] // 2
    decay = jnp.exp(-tn_ref[...] * jnp.abs(dl_ref[...]))
    top = _dot(hid_ref[0:half, :], w3f_ref[...], precision=HIGHEST)
    bot = _dot(hid_ref[half:, :], w3b_ref[...], precision=HIGHEST)
    row = lax.broadcasted_iota(jnp.int32, bot.shape, 0)
    bot = jnp.where(row == 0, 0.0, bot)
    o_ref[0, 0:half, :] = top * decay[0:half, :]
    o_ref[0, half:, :] = bot * decay[half:, :]


def _hy_filters(hid, w3, tn_full, deltas):
    n, fd = hid.shape
    wch = deltas.shape[1]
    nsl = wch // LANES
    return pl.pallas_call(
        _hy_filt_kernel,
        grid=(HY_ORDER, nsl),
        in_specs=[pl.BlockSpec((n, fd), lambda o, j: (0, 0)),
                  pl.BlockSpec((fd, LANES), lambda o, j: (0, o * 2 * nsl + j)),
                  pl.BlockSpec((fd, LANES), lambda o, j: (0, o * 2 * nsl + nsl + j)),
                  pl.BlockSpec((n, LANES), lambda o, j: (0, 0)),
                  pl.BlockSpec((1, LANES), lambda o, j: (0, j))],
        out_specs=pl.BlockSpec((1, n, LANES), lambda o, j: (o, 0, j)),
        out_shape=jax.ShapeDtypeStruct((HY_ORDER, n, wch), F32),
        compiler_params=_params(("parallel", "parallel")),
        name="hyena_filter",
    )(hid, w3, w3, tn_full, deltas)


def _hy_kfft_kernel(kern_ref, f1_ref, f2_ref, o_ref, g_ref, *, n1, n2):
    pg = 2 * n1 + HY_PITCH_PAD

    def stage1(i2, _):
        x = kern_ref[pl.ds(i2, n1, stride=n2), :].astype(BF16)
        g_ref[pl.ds(pl.multiple_of(i2 * pg, 8), 2 * n1), :] = _dot(f1_ref[i2], x)
        return 0
    lax.fori_loop(0, n2, stage1, 0, unroll=HY_UNROLL)

    def stage2(k1, _):
        x = jnp.concatenate([g_ref[pl.ds(k1, n2, stride=pg), :],
                             g_ref[pl.ds(n1 + k1, n2, stride=pg), :]], axis=0).astype(BF16)
        o_ref[pl.ds(pl.multiple_of(k1 * 2 * n2, 2 * n2), 2 * n2), :] = _dot(f2_ref[...], x)
        return 0
    lax.fori_loop(0, n1, stage2, 0, unroll=HY_UNROLL // 2)


def _hy_kfft(kern, f1k, f2, n1, n2):
    orders, n, wch = kern.shape
    nsl = wch // LANES
    return pl.pallas_call(
        functools.partial(_hy_kfft_kernel, n1=n1, n2=n2),
        grid=(orders, nsl),
        in_specs=[pl.BlockSpec((None, n, LANES), lambda o, j: (o, 0, j)),
                  pl.BlockSpec(f1k.shape, lambda o, j: (0, 0, 0)),
                  pl.BlockSpec(f2.shape, lambda o, j: (0, 0))],
        out_specs=pl.BlockSpec((None, None, 2 * n, LANES), lambda o, j: (o, j, 0, 0)),
        out_shape=jax.ShapeDtypeStruct((orders, nsl, 2 * n, LANES), F32),
        scratch_shapes=[pltpu.VMEM((n2 * (2 * n1 + HY_PITCH_PAD), LANES), F32)],
        compiler_params=_params(("parallel", "parallel")),
        name="hyena_filter_fft",
    )(kern, f1k, f2)


def _hy_prep_kernel(z_ref, w_ref, b_ref, o_ref, zp_ref, *, n2, nseq):
    l_len = z_ref.shape[0] // nseq
    h1 = l_len // n2
    halo = HY_HALO
    pitch = n2 + 2 * halo + 8
    zeros_h = jnp.zeros((halo, LANES), F32)
    for s in range(nseq):
        for i1 in range(h1):
            base = (s * h1 + i1) * pitch
            lo, hi = i1 * n2 - halo, (i1 + 1) * n2 + halo
            if lo < 0:
                zp_ref[base:base + halo, :] = zeros_h
            if hi > l_len:
                zp_ref[base + n2 + halo:base + n2 + 2 * halo, :] = zeros_h
            lo_c, hi_c = max(lo, 0), min(hi, l_len)
            zp_ref[base + (lo_c - lo):base + (hi_c - lo), :] = (
                z_ref[s * l_len + lo_c:s * l_len + hi_c, :].astype(F32))
    w0, w1, w2, b = w_ref[0:1, :], w_ref[1:2, :], w_ref[2:3, :], b_ref[...]

    def tap(j):
        return zp_ref[pl.ds(halo + j, nseq * h1, stride=pitch), :]

    def body(i2, carry):
        zm, z0 = carry
        zn = tap(i2 + 1)
        val = (b + w0 * zm + w1 * z0 + w2 * zn).astype(o_ref.dtype)
        for s in range(nseq):
            o_ref[pl.ds(pl.multiple_of(s * l_len + i2 * h1, h1), h1), :] = val[s * h1:(s + 1) * h1, :]
        return z0, zn
    lax.fori_loop(0, n2, body, (tap(-1), tap(0)), unroll=HY_UNROLL if nseq <= 2 else 1)


def _hy_stage_dtype(l_len, n2):
    return BF16 if (l_len // n2) % 16 == 0 else F32


def _hy_prep(proj, z_col, nz, bsz, conv_w, conv_b, n2):
    l_len = proj.shape[0] // bsz
    h1 = l_len // n2
    nseq = max(1, min(bsz, HY_PREP_ROWS // l_len))
    while bsz % nseq:
        nseq -= 1
    return pl.pallas_call(
        functools.partial(_hy_prep_kernel, n2=n2, nseq=nseq),
        grid=(bsz // nseq, nz),
        in_specs=[pl.BlockSpec((nseq * l_len, LANES), lambda b, j: (b, z_col + j)),
                  pl.BlockSpec((3, LANES), lambda b, j: (0, j)),
                  pl.BlockSpec((1, LANES), lambda b, j: (0, j))],
        out_specs=pl.BlockSpec((nseq * l_len, LANES), lambda b, j: (b, j)),
        out_shape=jax.ShapeDtypeStruct((proj.shape[0], nz * LANES), _hy_stage_dtype(l_len, n2)),
        scratch_shapes=[pltpu.VMEM((nseq * h1 * (n2 + 2 * HY_HALO + 8), LANES), F32)],
        compiler_params=_params(("parallel", "parallel")),
        name="hyena_short_conv",
    )(proj, conv_w, conv_b)


def _pack2(re, im):
    r = lax.bitcast_convert_type(re, jnp.uint32) + jnp.uint32(0x8000)
    i = lax.bitcast_convert_type(im, jnp.uint32) + jnp.uint32(0x8000)
    return (r & jnp.uint32(0xFFFF0000)) | (i >> 16)


def _unpack2(w):
    re = lax.bitcast_convert_type(w & jnp.uint32(0xFFFF0000), F32)
    im = lax.bitcast_convert_type(w << 16, F32)
    return jnp.concatenate([re, im], axis=0).astype(BF16)


def _hy_conv_kernel(y_ref, x_ref, kh_ref, bias_ref, f1_ref, f2_ref, f2i_ref, f1i_ref, o_ref, g_ref,
                    *, n1, n2, natural_out):
    h1 = n1 // 2
    l_len = h1 * n2
    pg = n1 + HY_PITCH_PAD
    bias = bias_ref[...]
    unroll = min(HY_UNROLL, n1)

    def rows(seq, i2):
        return pl.ds(pl.multiple_of(seq * l_len + i2 * h1, h1), h1)

    def lanes2(a, b):
        return jnp.concatenate([a, b], axis=1)

    def stage1(i2, _):
        x = lanes2(jnp.concatenate([y_ref[rows(0, i2), :], y_ref[rows(1, i2), :]], axis=0),
                   jnp.concatenate([y_ref[rows(2, i2), :], y_ref[rows(3, i2), :]], axis=0)).astype(BF16)
        a = _dot(f1_ref[i2], x)
        w = _pack2(a[0:n1, :], a[n1:, :])
        blk = pl.ds(pl.multiple_of(i2 * pg, 8), n1)
        g_ref[0, blk, :] = w[:, 0:LANES]
        g_ref[1, blk, :] = w[:, LANES:]
        return 0
    lax.fori_loop(0, n2, stage1, 0, unroll=2 * unroll)

    def stage23(t, _):
        outs = []
        for u in range(unroll):
            k1 = t * unroll + u
            col = pl.ds(k1, n2, stride=pg)
            x = lanes2(_unpack2(g_ref[0, col, :]), _unpack2(g_ref[1, col, :]))
            yh = _dot(f2_ref[...], x)
            base = pl.multiple_of(k1 * 2 * n2, 2 * n2)
            kr = kh_ref[pl.ds(base, n2), :]
            ki = kh_ref[pl.ds(base + n2, n2), :]
            kr, ki = lanes2(kr, kr), lanes2(ki, ki)
            yr, yi = yh[0:n2, :], yh[n2:, :]
            z = jnp.concatenate([yr * kr - yi * ki, yr * ki + yi * kr], axis=0).astype(BF16)
            c = _dot(f2i_ref[...], z)
            outs.append((col, _pack2(c[0:n2, :], c[n2:, :])))
        for col, w in outs:
            g_ref[0, col, :] = w[:, 0:LANES]
            g_ref[1, col, :] = w[:, LANES:]
        return 0
    lax.fori_loop(0, n1 // unroll, stage23, 0)

    def stage4(i2, _):
        blk = pl.ds(pl.multiple_of(i2 * pg, 8), n1)
        x = lanes2(_unpack2(g_ref[0, blk, :]), _unpack2(g_ref[1, blk, :]))
        yc = _dot(f1i_ref[i2], x)
        for seq in range(4):
            pair, half = seq // 2, seq % 2
            sl = rows(seq, i2)
            conv = yc[half * h1:(half + 1) * h1, pair * LANES:(pair + 1) * LANES]
            val = x_ref[sl, :].astype(F32) * (conv + bias * y_ref[sl, :].astype(F32))
            if natural_out:
                o_ref[pl.ds(seq * l_len + i2, h1, stride=n2), :] = val
            else:
                o_ref[sl, :] = val.astype(o_ref.dtype)
        return 0
    lax.fori_loop(0, n2, stage4, 0, unroll=unroll)


def _hy_conv(ysrc, ycol, xsrc, xcol, khat, order, bias, mats, bsz, n1, n2, natural_out):
    l_len = ysrc.shape[0] // bsz
    nsl = khat.shape[1]
    f1, f2, f2i, f1i = mats
    once = pl.Buffered(1)
    cst2 = lambda a: pl.BlockSpec(a.shape, lambda j, p: (0, 0), pipeline_mode=once)
    cst3 = lambda a: pl.BlockSpec(a.shape, lambda j, p: (0, 0, 0), pipeline_mode=once)
    return pl.pallas_call(
        functools.partial(_hy_conv_kernel, n1=n1, n2=n2, natural_out=natural_out),
        grid=(nsl, bsz // 4),
        in_specs=[pl.BlockSpec((4 * l_len, LANES), lambda j, p: (p, ycol + j)),
                  pl.BlockSpec((4 * l_len, LANES), lambda j, p: (p, xcol + j)),
                  pl.BlockSpec((None, None, khat.shape[2], LANES), lambda j, p: (order, j, 0, 0),
                               pipeline_mode=once),
                  pl.BlockSpec((1, LANES), lambda j, p: (0, j)),
                  cst3(f1), cst2(f2), cst2(f2i), cst3(f1i)],
        out_specs=pl.BlockSpec((4 * l_len, LANES), lambda j, p: (p, j)),
        out_shape=jax.ShapeDtypeStruct((ysrc.shape[0], nsl * LANES),
                                       F32 if natural_out else _hy_stage_dtype(l_len, n2)),
        scratch_shapes=[pltpu.VMEM((2, n2 * (n1 + HY_PITCH_PAD), LANES), jnp.uint32)],
        compiler_params=_params(("parallel", "arbitrary")),
        name="hyena_long_conv",
    )(ysrc, xsrc, khat, bias[order][None], f1, f2, f2i, f1i)


def _dft_tables(l_len):
    n = 2 * l_len
    n2 = 128 if l_len >= 1024 else 32
    n1 = n // n2
    h1 = n1 // 2
    i1 = np.arange(n1)[None, None, :]
    k1 = np.arange(n1)[None, :, None]
    i2 = np.arange(n2)[:, None, None]
    ph = 2 * np.pi * (i1 * k1 / n1 + i2 * k1 / n)
    c, s = np.cos(ph), np.sin(ph)
    ch, sh = c[:, :, :h1], s[:, :, :h1]
    f1 = np.concatenate([np.concatenate([ch, sh], 2), np.concatenate([-sh, ch], 2)], 1)
    f1k = np.concatenate([c, -s], 1)
    ct, st = np.swapaxes(ch, 1, 2), np.swapaxes(sh, 1, 2)
    f1i = np.concatenate([np.concatenate([ct, -st], 2), np.concatenate([st, ct], 2)], 1) / n
    a = np.arange(n2)
    ph2 = 2 * np.pi * np.outer(a, a) / n2
    c2, s2 = np.cos(ph2), np.sin(ph2)
    f2 = np.block([[c2, s2], [-s2, c2]])
    f2i = np.block([[c2, -s2], [s2, c2]])
    bf = lambda m: jnp.asarray(m, dtype=F32).astype(BF16)
    return n1, n2, (bf(f1), bf(f2), bf(f2i), bf(f1i)), bf(f1k)


def _filter_positions(l_len, width):
    lag = np.concatenate([np.arange(l_len), l_len - np.arange(l_len)]).astype(np.float64)
    t = lag / (l_len - 1)
    bands = (HY_EMB - 1) // 2
    w = 2.0 * np.pi * lag / l_len
    f = np.linspace(1e-4, bands - 1, bands)[None]
    z = np.concatenate([t[:, None], np.cos(f * w[:, None]), -np.sin(f * w[:, None])], axis=-1)
    z = np.pad(z, ((0, 0), (0, HY_EMB_PAD - HY_EMB)))
    tn = np.repeat(t[:, None], LANES, axis=1)
    max_decay = math.log(HY_DECAY_TARGET) / HY_FAST_PCT
    min_decay = math.log(HY_DECAY_TARGET) / HY_SLOW_PCT
    deltas = np.linspace(min_decay, max_decay, width)[None]
    return jnp.asarray(z, F32), jnp.asarray(tn, F32), jnp.asarray(deltas, F32)


def _rope_tables(seq):
    n_rows = seq // GRID_W
    row = np.repeat(np.arange(n_rows), GRID_W).astype(np.float64)
    col = np.tile(np.arange(GRID_W), n_rows).astype(np.float64)
    half = HEAD_DIM // 2
    inv = ROPE_BASE ** (-np.arange(0, half, 2, dtype=np.float64) / half)
    ar, ac = row[:, None] * inv, col[:, None] * inv
    cos = np.concatenate([np.cos(ar), np.cos(ar), np.cos(ac), np.cos(ac)], axis=1)
    sin = np.concatenate([-np.sin(ar), np.sin(ar), -np.sin(ac), np.sin(ac)], axis=1)
    return jnp.asarray(cos, F32), jnp.asarray(sin, F32)


def _hyena_spectra(l_len, width, w1, b1, w2, b2, w3, freq, tables):
    n1, n2, mats, f1k = tables
    zfull, tn_full, deltas = _filter_positions(l_len, width)
    w1p = jnp.pad(w1, ((0, HY_EMB_PAD - HY_EMB), (0, 0)))
    hid = _hy_hidden(zfull, w1p, b1[None], w2, b2[None], freq[None])
    kern = _hy_filters(hid, w3, tn_full, deltas)
    return _hy_kfft(kern, f1k, mats[1], n1, n2)


def _hyena_seq(zc, bsz, khat, bias, tables):
    n1, n2, mats, _ = tables
    nsl = khat.shape[1]
    y1 = _hy_conv(zc, 0, zc, nsl, khat, 0, bias, mats, bsz, n1, n2, False)
    return _hy_conv(y1, 0, zc, 2 * nsl, khat, 1, bias, mats, bsz, n1, n2, True)


def kernel(x, c, ctx, c_ctx, mod_w, mod_b, norm_pre, norm_post, ev_w_in, ev_w_out, lru_conv_w, lru_conv_b, lru_wa, lru_ba, lru_wx, lru_bx, lru_lambda, attn_sink, od_w_in, od_w_out, hy_conv_w, hy_conv_b, hy_w1, hy_b1, hy_w2, hy_b2, hy_w3, hy_freq, hy_bias, ret_decay_logit):
    bsz, s_len, d = x.shape
    c_len = ctx.shape[1]
    depth = mod_w.shape[0]
    assert bsz % 4 == 0 and bsz <= 16 and s_len % 1024 == 0 and c_len % 256 == 0 and c_len <= s_len

    crows = jnp.concatenate([c, c_ctx[None], jnp.zeros((24 - bsz - 1, d), F32)], axis=0)
    mod = _modulation(crows, mod_w, mod_b)

    cos_t, sin_t = _rope_tables(s_len)
    tab_l = _dft_tables(s_len)
    tab_c = _dft_tables(c_len)

    x2 = x.reshape(bsz * s_len, d)
    ctx2 = ctx.reshape(bsz * c_len, d)
    tm = ROW_TILE

    for l in range(depth):
        need_ctx = l < depth - 1
        shift, scale, gate = (mod[l, :bsz, i * d:(i + 1) * d].reshape(bsz, 1, d) for i in range(3))
        shift_c, scale_c, gate_c = (mod[l, bsz:bsz + 1, i * d:(i + 1) * d].reshape(1, 1, d) for i in range(3))
        g_pre = norm_pre[l][None]
        g_post = norm_post[l][None]

        def proj(w, tn):
            wb = w.astype(BF16)
            p = _in_proj(x2, g_pre, scale, shift, s_len, wb, BF16, IN_ROW_TILE, tn)
            pc = _in_proj(ctx2, g_pre, scale_c, shift_c, bsz * c_len, wb, BF16, IN_ROW_TILE, tn)
            return p, pc

        if l % 2 == 0:
            e = l // 2
            w_in = ev_w_in[e]
            w_ord = jnp.concatenate([w_in[:, 1024:2048], w_in[:, 2048:3072], w_in[:, 3584:4608],
                                     w_in[:, 3072:3584], w_in[:, 0:1024]], axis=1)
            pr, prc = proj(w_ord, EVEN_COL_TILE)
            cols = {"q": 1, "gb": 2, "k": 12, "v": 13}
            xa_col = 3584 // LANES
            nblk = lru_wa.shape[2]
            wg = jnp.stack([lru_wa[e, 0], lru_wx[e, 0], lru_wa[e, 1], lru_wx[e, 1]], axis=1)
            wg = jnp.transpose(wg, (0, 2, 1, 3)).reshape(nblk, LRU_BLOCK_W, 4 * LRU_BLOCK_W)
            bg = jnp.stack([lru_ba[e, 0], lru_bx[e, 0], lru_ba[e, 1], lru_bx[e, 1]], axis=0)
            bg = jnp.transpose(bg.reshape(4, nblk, LRU_BLOCK_W), (1, 0, 2)).reshape(nblk, 1, 4 * LRU_BLOCK_W)
            wg, bg = (0.5 * wg).astype(BF16), 0.5 * bg
            ya, yac = _lru(pr, prc, xa_col, bsz, lru_conv_w[e], lru_conv_b[e][None], wg, bg, lru_lambda[e])
            yb = _attention(attn_sink[e], pr, prc, bsz, cols, cos_t, sin_t)
            w_out = ev_w_out[e].astype(BF16)
            x2_new = _out_proj(ya, pr, 0, yb, pr, cols["gb"], w_out, x2, g_post, gate, s_len, tm)
            if need_ctx:
                ybc = _ctx_attention(attn_sink[e], prc, bsz, cols)
                ctx2 = _out_proj(yac, prc, 0, ybc, prc, cols["gb"], w_out, ctx2, g_post, gate_c, bsz * c_len, tm)
            x2 = x2_new
        else:
            o = l // 2
            w_in = od_w_in[o]
            w_ord = jnp.concatenate([w_in[:, 3072:8192], w_in[:, 0:3072]], axis=1)
            pr, prc = proj(w_ord, ODD_COL_TILE)
            cols = {"q": 8, "k": 16, "v": 24}
            z_col = 5120 // LANES
            width = hy_bias.shape[2]
            nz = (HY_ORDER + 1) * width // LANES
            fargs = (hy_w1[o], hy_b1[o], hy_w2[o], hy_b2[o], hy_w3[o], hy_freq[o])
            khat = _hyena_spectra(s_len, width, *fargs, tab_l)
            zconv = _hy_prep(pr, z_col, nz, bsz, hy_conv_w[o], hy_conv_b[o][None], tab_l[1])
            yh = _hyena_seq(zconv, bsz, khat, hy_bias[o], tab_l)
            dl = jnp.broadcast_to(jnp.transpose(ret_decay_logit[o])[:, :, None], (RET_HEADS, 2, LANES))
            yd, ydc = _retention(dl, pr, prc, bsz, cols)
            w_out = od_w_out[o].astype(BF16)
            x2_new = _out_proj(yh, pr, 0, yd, pr, 4, w_out, x2, g_post, gate, s_len, tm)
            if need_ctx:
                khat_c = _hyena_spectra(c_len, width, *fargs, tab_c)
                zcconv = _hy_prep(prc, z_col, nz, bsz, hy_conv_w[o], hy_conv_b[o][None], tab_c[1])
                yhc = _hyena_seq(zcconv, bsz, khat_c, hy_bias[o], tab_c)
                ctx2 = _out_proj(yhc, prc, 0, ydc, prc, 4, w_out, ctx2, g_post, gate_c, bsz * c_len, tm)
            x2 = x2_new
    return x2.reshape(bsz, s_len, d)
```

```python
import functools
import math

import numpy as np
import jax
import jax.numpy as jnp
from jax import lax
from jax.experimental import pallas as pl
from jax.experimental.pallas import tpu as pltpu

F32 = jnp.float32
BF16 = jnp.bfloat16
HIGHEST = lax.Precision.HIGHEST

EPS = 1e-6
GRID_W = 64
LANES = 128
LRU_BLOCK_W = 128
LRU_CONV = 4
LRU_C = 8.0
LRU_SEG = 16
LRU_POS = 64
LRU_TINY = 1e-30
LRU_HALO = 16
ATT_HEADS = 8
ATT_KV_HEADS = 2
HEAD_DIM = 128
BLOCK = 128
ATT_QSUB = 4
ROPE_BASE = 10000.0
HY_ORDER = 2
HY_EMB = 33
HY_EMB_PAD = 40
HY_DECAY_TARGET = 1e-2
HY_FAST_PCT = 0.3
HY_SLOW_PCT = 1.5
HY_UNROLL = 32
HY_PREP_ROWS = 16384
HY_HALO = 16
HY_PITCH_PAD = 8
RET_HEADS = 8
RET_DK = 128
RET_CHUNK = 256
RET_UNROLL = 8

V7X_VMEM_BYTES = 64 * 1024 * 1024
VMEM_LIMIT = V7X_VMEM_BYTES - 6 * 1024 * 1024
ROW_TILE = 1024
IN_ROW_TILE = 2048
OUT_SPLIT = 4
EVEN_COL_TILE = 1536
ODD_COL_TILE = 2048
NEG = -1e30


def _params(sem, vmem=VMEM_LIMIT, **kw):
    return pltpu.CompilerParams(dimension_semantics=sem, vmem_limit_bytes=vmem, **kw)


def _sigmoid(v):
    return 0.5 * (jnp.tanh(0.5 * v) + 1.0)


def _silu(v):
    return v * _sigmoid(v)


def _softplus(v):
    return jnp.maximum(v, 0.0) + jnp.log(1.0 + jnp.exp(-jnp.abs(v)))


def _dot(a, b, **kw):
    return jnp.dot(a, b, preferred_element_type=F32, **kw)


def _dot_nt(a, b):
    return lax.dot_general(a, b, (((1,), (1,)), ((), ())), preferred_element_type=F32)


def _mod_kernel(c_ref, w_ref, b_ref, o_ref):
    s = _silu(c_ref[...])
    o_ref[0] = _dot(s, w_ref[0], precision=HIGHEST) + b_ref[0]


def _modulation(crows, mod_w, mod_b):
    depth, d, n3 = mod_w.shape
    r = crows.shape[0]
    tn = 1024
    return pl.pallas_call(
        _mod_kernel,
        grid=(depth, n3 // tn),
        in_specs=[pl.BlockSpec((r, d), lambda l, j: (0, 0)),
                  pl.BlockSpec((1, d, tn), lambda l, j: (l, 0, j)),
                  pl.BlockSpec((1, 1, tn), lambda l, j: (l, 0, j))],
        out_specs=pl.BlockSpec((1, r, tn), lambda l, j: (l, 0, j)),
        out_shape=jax.ShapeDtypeStruct((depth, r, n3), F32),
        compiler_params=_params(("parallel", "parallel")),
        name="modulation",
    )(crows, mod_w, mod_b.reshape(depth, 1, n3))


def _rope(v, cos, sin, lane):
    swapped = jnp.where((lane & 63) < 32, pltpu.roll(v, LANES - 32, 1), pltpu.roll(v, 32, 1))
    return v * cos + swapped * sin


def _inproj_kernel(x_ref, g_ref, sc_ref, sh_ref, w_ref, o_ref, h_ref):
    @pl.when(pl.program_id(1) == 0)
    def _():
        x = x_ref[...]
        y = x * lax.rsqrt(jnp.mean(x * x, axis=-1, keepdims=True) + EPS) * g_ref[...]
        h_ref[...] = (y * (1.0 + sc_ref[0]) + sh_ref[0]).astype(BF16)

    o_ref[...] = _dot(h_ref[...], w_ref[...]).astype(o_ref.dtype)


def _in_proj(x2, g, scale, shift, rows_per_group, w, out_dtype, tm, tn):
    m, d = x2.shape
    n = w.shape[1]
    tpg = rows_per_group // tm
    return pl.pallas_call(
        _inproj_kernel,
        grid=(m // tm, n // tn),
        in_specs=[pl.BlockSpec((tm, d), lambda i, j: (i, 0)),
                  pl.BlockSpec((1, d), lambda i, j: (0, 0)),
                  pl.BlockSpec((1, 1, d), lambda i, j: (i // tpg, 0, 0)),
                  pl.BlockSpec((1, 1, d), lambda i, j: (i // tpg, 0, 0)),
                  pl.BlockSpec((d, tn), lambda i, j: (0, j))],
        out_specs=pl.BlockSpec((tm, tn), lambda i, j: (i, j)),
        out_shape=jax.ShapeDtypeStruct((m, n), out_dtype),
        scratch_shapes=[pltpu.VMEM((tm, d), BF16)],
        compiler_params=_params(("parallel", "arbitrary")),
        name="in_proj",
    )(x2, g, scale, shift, w)


def _outproj_kernel(a_ref, ga_ref, b_ref, gb_ref, w_ref, x_ref, g_ref, gate_ref, o_ref):
    wa = w_ref.shape[0] // 2
    rows = a_ref.shape[0] // OUT_SPLIT
    for part in range(OUT_SPLIT):
        sl = slice(part * rows, (part + 1) * rows)
        a = a_ref[sl, :].astype(BF16) * _silu(ga_ref[sl, :])
        b = b_ref[sl, :].astype(BF16) * _silu(gb_ref[sl, :])
        y = _dot(a, w_ref[0:wa, :]) + _dot(b, w_ref[wa:, :])
        yn = y * lax.rsqrt(jnp.mean(y * y, axis=-1, keepdims=True) + EPS) * g_ref[...]
        o_ref[sl, :] = x_ref[sl, :] + gate_ref[0] * yn


def _out_proj(a, ga, ga_col, b, gb, gb_col, w, x2, g, gate, rows_per_group, tm):
    m, d = x2.shape
    wa = w.shape[0] // 2
    tpg = rows_per_group // tm
    return pl.pallas_call(
        _outproj_kernel,
        grid=(m // tm,),
        in_specs=[pl.BlockSpec((tm, wa), lambda i: (i, 0)),
                  pl.BlockSpec((tm, wa), lambda i: (i, ga_col)),
                  pl.BlockSpec((tm, wa), lambda i: (i, 0)),
                  pl.BlockSpec((tm, wa), lambda i: (i, gb_col)),
                  pl.BlockSpec(w.shape, lambda i: (0, 0)),
                  pl.BlockSpec((tm, d), lambda i: (i, 0)),
                  pl.BlockSpec((1, d), lambda i: (0, 0)),
                  pl.BlockSpec((1, 1, d), lambda i: (i // tpg, 0, 0))],
        out_specs=pl.BlockSpec((tm, d), lambda i: (i, 0)),
        out_shape=jax.ShapeDtypeStruct((m, d), F32),
        compiler_params=_params(("parallel",)),
        name="out_proj",
    )(a, ga, b, gb, w, x2, g, gate)


def _lru_kernel(xa_ref, xac_ref, cw_ref, cb_ref, wg_ref, bg_ref, lam_ref, y_ref, yc_ref,
                xp_ref, hf_ref, pf_ref, hb_ref, pb_ref):
    nseg = LRU_SEG
    c8h = -0.5 * LRU_C * _softplus(-lam_ref[...])
    left = LRU_CONV // 2
    cw = [cw_ref[k:k + 1, :] for k in range(LRU_CONV)]
    cb = cb_ref[...]
    zeros_h = jnp.zeros((LRU_HALO, LANES), F32)
    zeros_s = jnp.zeros((nseg, LANES), F32)
    ones_s = jnp.ones((nseg, LANES), F32)

    def run(src_ref, out_ref, n, c0f, c0b):
        seg = n // nseg
        npos = min(LRU_POS, seg)
        nblk = seg // npos
        halo = LRU_HALO
        pitch = seg + 2 * halo + 8
        for j in range(nseg):
            lo, hi = j * seg - halo, (j + 1) * seg + halo
            if lo < 0:
                xp_ref[j * pitch:j * pitch + halo, :] = zeros_h
            if hi > n:
                xp_ref[j * pitch + seg + halo:j * pitch + seg + 2 * halo, :] = zeros_h
            lo_c, hi_c = max(lo, 0), min(hi, n)
            xp_ref[j * pitch + (lo_c - lo):j * pitch + (hi_c - lo), :] = src_ref[lo_c:hi_c, :].astype(F32)

        def gather(q):
            return jnp.concatenate([xp_ref[pl.ds(halo + q + c * 8 * pitch, 8, stride=pitch), :]
                                    for c in range(nseg // 8)], axis=0)

        def blk(i):
            return pl.ds(pl.multiple_of(i * npos * nseg, npos * nseg), npos * nseg)

        def pos(v, p):
            return v[p * nseg:(p + 1) * nseg, :]

        def fwd_body(i, carry):
            h, pc = carry
            p0 = i * npos
            xs = [gather(p0 + q - left) for q in range(npos + LRU_CONV - 1)]
            us = []
            for p in range(npos):
                u = cb + cw[0] * xs[p]
                for k in range(1, LRU_CONV):
                    u = u + cw[k] * xs[p + k]
                us.append(u)
            u = jnp.concatenate(us, axis=0)
            g = _dot(u.astype(BF16), wg_ref[0]) + bg_ref[0]
            hu = 0.5 * u
            coef = []
            for d in range(2):
                tr = jnp.tanh(g[:, (2 * d) * LANES:(2 * d + 1) * LANES])
                ti = jnp.tanh(g[:, (2 * d + 1) * LANES:(2 * d + 2) * LANES])
                a = jnp.exp(c8h[d:d + 1, :] * tr + c8h[d:d + 1, :])
                om = 1.0 - a * a
                coef.append((a, om * lax.rsqrt(jnp.maximum(om, LRU_TINY)) * ((ti + 1.0) * hu)))
            (af, bf), (ab, bb) = coef
            hb_ref[blk(i), :] = ab
            pb_ref[blk(i), :] = bb
            hs, ps = [], []
            for p in range(npos):
                a_p = pos(af, p)
                h = a_p * h + pos(bf, p)
                pc = a_p * pc
                hs.append(h)
                ps.append(pc)
            hf_ref[blk(i), :] = jnp.concatenate(hs, axis=0)
            pf_ref[blk(i), :] = jnp.concatenate(ps, axis=0)
            return h, pc

        hef, pef = lax.fori_loop(0, nblk, fwd_body, (zeros_s, ones_s))

        def bwd_body(ii, carry):
            h, pc = carry
            i = nblk - 1 - ii
            ab = hb_ref[blk(i), :]
            bb = pb_ref[blk(i), :]
            hs, ps = [None] * npos, [None] * npos
            for p in reversed(range(npos)):
                a_p = pos(ab, p)
                h = a_p * h + pos(bb, p)
                pc = a_p * pc
                hs[p] = h
                ps[p] = pc
            hb_ref[blk(i), :] = jnp.concatenate(hs, axis=0)
            pb_ref[blk(i), :] = jnp.concatenate(ps, axis=0)
            return h, pc

        heb, peb = lax.fori_loop(0, nblk, bwd_body, (zeros_s, ones_s))

        rows_f, c = [], c0f
        for j in range(nseg):
            rows_f.append(c)
            c = hef[j:j + 1, :] + pef[j:j + 1, :] * c
        final_f = c
        rows_b, c = [None] * nseg, c0b
        for j in reversed(range(nseg)):
            rows_b[j] = c
            c = heb[j:j + 1, :] + peb[j:j + 1, :] * c
        final_b = c
        cf = jnp.concatenate(rows_f, axis=0)
        cbk = jnp.concatenate(rows_b, axis=0)

        opitch = seg + 8

        def out_body(p, _):
            rs = pl.ds(pl.multiple_of(p * nseg, nseg), nseg)
            y = hf_ref[rs, :] + pf_ref[rs, :] * cf + hb_ref[rs, :] + pb_ref[rs, :] * cbk
            for c in range(nseg // 8):
                xp_ref[pl.ds(p + c * 8 * opitch, 8, stride=opitch), :] = y[c * 8:(c + 1) * 8, :]
            return 0
        lax.fori_loop(0, seg, out_body, 0, unroll=8)
        for j in range(nseg):
            out_ref[j * seg:(j + 1) * seg, :] = xp_ref[j * opitch:j * opitch + seg, :].astype(out_ref.dtype)
        return final_f, final_b

    zero = jnp.zeros((1, LANES), F32)
    ff, fb = run(xac_ref, yc_ref, xac_ref.shape[0], zero, zero)
    run(xa_ref, y_ref, xa_ref.shape[0], ff, fb)


def _lru(proj, projc, xa_col, bsz, conv_w, conv_b, wg, bg, lam):
    w = conv_w.shape[1]
    s_len = proj.shape[0] // bsz
    c_len = projc.shape[0] // bsz
    nblk = w // LANES
    seq = lambda n: pl.BlockSpec((n, LANES), lambda b, j: (b, j))
    src = lambda n: pl.BlockSpec((n, LANES), lambda b, j: (b, xa_col + j))
    return pl.pallas_call(
        _lru_kernel,
        grid=(bsz, nblk),
        in_specs=[src(s_len), src(c_len),
                  pl.BlockSpec((LRU_CONV, LANES), lambda b, j: (0, j)),
                  pl.BlockSpec((1, LANES), lambda b, j: (0, j)),
                  pl.BlockSpec((1, LANES, 4 * LANES), lambda b, j: (j, 0, 0)),
                  pl.BlockSpec((1, 1, 4 * LANES), lambda b, j: (j, 0, 0)),
                  pl.BlockSpec((2, LANES), lambda b, j: (0, j))],
        out_specs=[seq(s_len), seq(c_len)],
        out_shape=[jax.ShapeDtypeStruct((proj.shape[0], w), BF16), jax.ShapeDtypeStruct((projc.shape[0], w), BF16)],
        scratch_shapes=[pltpu.VMEM((s_len + (2 * LRU_HALO + 8) * LRU_SEG, LANES), F32)]
        + [pltpu.VMEM((s_len, LANES), F32)] * 4,
        compiler_params=_params(("parallel", "parallel")),
        name="rglru",
    )(proj, projc, conv_w, conv_b, wg, bg, lam)


def _attn_kernel(sink_ref, q_ref, k_ref, v_ref, kc_ref, vc_ref, cos_ref, sin_ref, bias_ref, o_ref):
    nb = pl.num_programs(1) * ATT_QSUB
    group = ATT_HEADS // ATT_KV_HEADS
    scale = HEAD_DIM ** -0.5
    lane = lax.broadcasted_iota(jnp.int32, (BLOCK, LANES), 1)
    ones = jnp.ones((bias_ref.shape[2], HEAD_DIM), BF16)

    def blk(i):
        return pl.ds(pl.multiple_of(i * BLOCK, BLOCK), BLOCK)

    def rot(x, tab):
        return _rope(x.astype(F32), tab[0], tab[1], lane).astype(BF16)

    for sub in range(ATT_QSUB):
        qb = pl.program_id(1) * ATT_QSUB + sub
        qrows = slice(sub * BLOCK, (sub + 1) * BLOCK)
        ip = jnp.maximum(qb - 1, 0)
        inx = jnp.minimum(qb + 1, nb - 1)
        tab_p = (cos_ref[blk(ip), :], sin_ref[blk(ip), :])
        tab_o = (cos_ref[blk(qb), :], sin_ref[blk(qb), :])
        tab_n = (cos_ref[blk(inx), :], sin_ref[blk(inx), :])
        tab_q = (tab_o[0] * scale, tab_o[1] * scale)
        variant = jnp.where(qb > 0, 1, 0) + jnp.where(qb < nb - 1, 2, 0)
        bias = jnp.concatenate([bias_ref[variant]] * group, axis=0)

        for h in range(ATT_KV_HEADS):
            ksl = slice(h * HEAD_DIM, (h + 1) * HEAD_DIM)
            kcat = jnp.concatenate([rot(k_ref[blk(ip), ksl], tab_p), rot(k_ref[blk(qb), ksl], tab_o),
                                    rot(k_ref[blk(inx), ksl], tab_n), kc_ref[:, ksl]], axis=0)
            vcat = jnp.concatenate([v_ref[blk(ip), ksl], v_ref[blk(qb), ksl], v_ref[blk(inx), ksl],
                                    vc_ref[:, ksl]], axis=0)
            vext = jnp.concatenate([vcat, ones], axis=1)
            heads = [h * group + g for g in range(group)]
            qs = jnp.concatenate([rot(q_ref[qrows, hh * HEAD_DIM:(hh + 1) * HEAD_DIM], tab_q) for hh in heads],
                                 axis=0)
            sk = jnp.concatenate([jnp.full((BLOCK, 1), sink_ref[hh], F32) for hh in heads], axis=0)
            s = _dot_nt(qs, kcat) + bias
            m = jnp.maximum(jnp.max(s, axis=-1, keepdims=True), sk)
            p = jnp.exp((s - m).astype(BF16))
            oe = _dot(p, vext)
            o = oe[:, 0:HEAD_DIM] / (jnp.exp(sk - m) + oe[:, HEAD_DIM:HEAD_DIM + 1])
            for g, hh in enumerate(heads):
                o_ref[qrows, hh * HEAD_DIM:(hh + 1) * HEAD_DIM] = o[g * BLOCK:(g + 1) * BLOCK, :].astype(o_ref.dtype)


def _attention(sink, proj, projc, bsz, cols, cos_t, sin_t):
    s_len = proj.shape[0] // bsz
    c_len = projc.shape[0] // bsz
    nb = s_len // BLOCK
    aw = ATT_HEADS * HEAD_DIM
    kw = ATT_KV_HEADS * HEAD_DIM
    qi = np.arange(BLOCK)[:, None]
    kj = np.arange(BLOCK)[None, :]
    variants = []
    for v in range(4):
        prev = np.where((kj >= qi) & bool(v & 1), 0.0, NEG)
        nxt = np.where((kj <= qi) & bool(v & 2), 0.0, NEG)
        variants.append(np.concatenate([prev, np.zeros((BLOCK, BLOCK)), nxt, np.zeros((BLOCK, c_len))], axis=1))
    bias = jnp.asarray(np.stack(variants), F32)
    ncol = 3 * BLOCK + c_len
    nstep = nb // ATT_QSUB
    tq = ATT_QSUB * BLOCK

    return pl.pallas_call(
        _attn_kernel,
        grid=(bsz, nstep),
        in_specs=[pl.BlockSpec(memory_space=pltpu.SMEM),
                  pl.BlockSpec((tq, aw), lambda b, i: (b * nstep + i, cols["q"])),
                  pl.BlockSpec((s_len, kw), lambda b, i: (b, cols["k"])),
                  pl.BlockSpec((s_len, kw), lambda b, i: (b, cols["v"])),
                  pl.BlockSpec((c_len, kw), lambda b, i: (b, cols["k"])),
                  pl.BlockSpec((c_len, kw), lambda b, i: (b, cols["v"])),
                  pl.BlockSpec((s_len, LANES), lambda b, i: (0, 0)),
                  pl.BlockSpec((s_len, LANES), lambda b, i: (0, 0)),
                  pl.BlockSpec((4, BLOCK, ncol), lambda b, i: (0, 0, 0))],
        out_specs=pl.BlockSpec((tq, aw), lambda b, i: (b * nstep + i, 0)),
        out_shape=jax.ShapeDtypeStruct((proj.shape[0], aw), BF16),
        compiler_params=_params(("parallel", "arbitrary")),
        name="window_attention",
    )(sink, proj, proj, proj, projc, projc, cos_t, sin_t, bias)


def _ctx_attn_kernel(sink_ref, q_ref, k_ref, v_ref, o_ref):
    group = ATT_HEADS // ATT_KV_HEADS
    scale = HEAD_DIM ** -0.5
    n = q_ref.shape[0]
    for h in range(ATT_KV_HEADS):
        ksl = slice(h * HEAD_DIM, (h + 1) * HEAD_DIM)
        qs, sinks = [], []
        for g in range(group):
            hh = h * group + g
            qs.append((q_ref[:, hh * HEAD_DIM:(hh + 1) * HEAD_DIM].astype(F32) * scale).astype(BF16))
            sinks.append(jnp.full((n, 1), sink_ref[hh], F32))
        q4 = jnp.concatenate(qs, axis=0)
        sk = jnp.concatenate(sinks, axis=0)
        s = _dot_nt(q4, k_ref[:, ksl])
        m = jnp.maximum(jnp.max(s, axis=-1, keepdims=True), sk)
        p = jnp.exp(s - m)
        denom = jnp.exp(sk - m) + jnp.sum(p, axis=-1, keepdims=True)
        o = _dot(p.astype(BF16), v_ref[:, ksl]) / denom
        for g in range(group):
            hh = h * group + g
            hs = slice(hh * HEAD_DIM, (hh + 1) * HEAD_DIM)
            o_ref[:, hs] = o[g * n:(g + 1) * n, :].astype(o_ref.dtype)


def _ctx_attention(sink, projc, bsz, cols):
    c_len = projc.shape[0] // bsz
    aw = ATT_HEADS * HEAD_DIM
    kw = ATT_KV_HEADS * HEAD_DIM
    return pl.pallas_call(
        _ctx_attn_kernel,
        grid=(bsz,),
        in_specs=[pl.BlockSpec(memory_space=pltpu.SMEM),
                  pl.BlockSpec((c_len, aw), lambda b: (b, cols["q"])),
                  pl.BlockSpec((c_len, kw), lambda b: (b, cols["k"])),
                  pl.BlockSpec((c_len, kw), lambda b: (b, cols["v"]))],
        out_specs=pl.BlockSpec((c_len, aw), lambda b: (b, 0)),
        out_shape=jax.ShapeDtypeStruct((projc.shape[0], aw), BF16),
        compiler_params=_params(("parallel",)),
        name="context_attention",
    )(sink, projc, projc, projc)


def _ret_kernel(dl_ref, q_ref, k_ref, v_ref, qc_ref, kc_ref, vc_ref, o_ref, oc_ref,
                ot_ref, att_ref, u_ref, st_ref, vt_ref, dm_ref):
    c = RET_CHUNK
    s_len = q_ref.shape[0]
    c_len = qc_ref.shape[0]
    lg = -_softplus(-dl_ref[0])
    lgf, lgb = lg[0:1, :], lg[1:2, :]
    wide = lambda t: jnp.concatenate([t] * (c // LANES), axis=1)
    kj = lax.broadcasted_iota(jnp.int32, (c, c), 0)
    qi = lax.broadcasted_iota(jnp.int32, (c, c), 1)
    diff = (qi - kj).astype(F32)
    dm_ref[...] = jnp.where(qi >= kj, jnp.exp(jnp.maximum(diff, 0.0) * wide(lgf)),
                            jnp.exp(jnp.maximum(-diff, 0.0) * wide(lgb)))
    idx = lax.broadcasted_iota(jnp.int32, (c, LANES), 0).astype(F32)
    qdec_f = jnp.exp((idx + 1.0) * lgf)
    kdec_f = jnp.exp((c - 1.0 - idx) * lgf)
    qdec_b = jnp.exp((c - idx) * lgb)
    kdec_b = jnp.exp(idx * lgb)
    cdec_f = jnp.exp(c * lgf)
    cdec_b = jnp.exp(c * lgb)

    def chunk(j):
        return pl.ds(pl.multiple_of(j * c, c), c)

    def transpose_v(vr, n):
        def body(j, _):
            vt_ref[:, chunk(j)] = vr[chunk(j), :].astype(F32).T.astype(BF16)
            return 0
        lax.fori_loop(0, n // c, body, 0, unroll=min(8, n // c))

    def scores(qr, kr, n):
        def body(j, _):
            att_ref[j] = (_dot_nt(kr[chunk(j), :], qr[chunk(j), :]) * dm_ref[...]).astype(BF16)
            return 0
        lax.fori_loop(0, n // c, body, 0, unroll=min(RET_UNROLL, n // c))

    def intra(kr, n):
        def body(j, _):
            vt = vt_ref[:, chunk(j)]
            ot_ref[:, chunk(j)] = _dot(vt, att_ref[j])
            kf = kr[chunk(j), :].astype(F32)
            kcat = jnp.concatenate([(kf * kdec_f).astype(BF16), (kf * kdec_b).astype(BF16)], axis=1)
            u_ref[j] = _dot(vt, kcat)
            return 0
        lax.fori_loop(0, n // c, body, 0, unroll=min(RET_UNROLL, n // c))

    def states(n, sf, sb):
        nch = n // c

        def fbody(j, s):
            st_ref[j, :, 0:RET_DK] = s.astype(BF16)
            return s * cdec_f + u_ref[j, :, 0:RET_DK]
        sf = lax.fori_loop(0, nch, fbody, sf)

        def bbody(jj, s):
            j = nch - 1 - jj
            st_ref[j, :, RET_DK:] = s.astype(BF16)
            return s * cdec_b + u_ref[j, :, RET_DK:]
        sb = lax.fori_loop(0, nch, bbody, sb)
        return sf, sb

    def cross(qr, outr, n):
        def body(j, _):
            qf = qr[chunk(j), :].astype(F32)
            qcat = jnp.concatenate([(qf * qdec_f).astype(BF16), (qf * qdec_b).astype(BF16)], axis=1)
            ot = (ot_ref[:, chunk(j)] + _dot_nt(st_ref[j], qcat)) * (RET_DK ** -0.5)
            on = ot * lax.rsqrt(jnp.mean(ot * ot, axis=0, keepdims=True) + EPS)
            outr[chunk(j), :] = on.T.astype(outr.dtype)
            return 0
        lax.fori_loop(0, n // c, body, 0, unroll=min(RET_UNROLL, n // c))

    def run(qr, kr, vr, outr, n, sf, sb):
        transpose_v(vr, n)
        scores(qr, kr, n)
        intra(kr, n)
        finals = states(n, sf, sb)
        cross(qr, outr, n)
        return finals

    zero = jnp.zeros((LANES, RET_DK), F32)
    sf, sb = run(qc_ref, kc_ref, vc_ref, oc_ref, c_len, zero, zero)
    run(q_ref, k_ref, v_ref, o_ref, s_len, sf, sb)


def _retention(dl, proj, projc, bsz, cols):
    s_len = proj.shape[0] // bsz
    c_len = projc.shape[0] // bsz
    hb = lambda n, col: pl.BlockSpec((n, LANES), lambda b, h: (b, col + h))
    return pl.pallas_call(
        _ret_kernel,
        grid=(bsz, RET_HEADS),
        in_specs=[pl.BlockSpec((1, 2, LANES), lambda b, h: (h, 0, 0)),
                  hb(s_len, cols["q"]), hb(s_len, cols["k"]), hb(s_len, cols["v"]),
                  hb(c_len, cols["q"]), hb(c_len, cols["k"]), hb(c_len, cols["v"])],
        out_specs=[pl.BlockSpec((s_len, LANES), lambda b, h: (b, h)),
                   pl.BlockSpec((c_len, LANES), lambda b, h: (b, h))],
        out_shape=[jax.ShapeDtypeStruct((proj.shape[0], RET_HEADS * LANES), BF16),
                   jax.ShapeDtypeStruct((projc.shape[0], RET_HEADS * LANES), BF16)],
        scratch_shapes=[pltpu.VMEM((LANES, s_len), F32),
                        pltpu.VMEM((s_len // RET_CHUNK, RET_CHUNK, RET_CHUNK), BF16),
                        pltpu.VMEM((s_len // RET_CHUNK, LANES, 2 * RET_DK), F32),
                        pltpu.VMEM((s_len // RET_CHUNK, LANES, 2 * RET_DK), BF16),
                        pltpu.VMEM((LANES, s_len), BF16),
                        pltpu.VMEM((RET_CHUNK, RET_CHUNK), F32)],
        compiler_params=_params(("parallel", "parallel")),
        name="retention",
    )(dl, proj, proj, proj, projc, projc, projc)


def _hy_hid_kernel(z_ref, w1_ref, b1_ref, w2_ref, b2_ref, f_ref, o_ref):
    f = f_ref[...]
    h = jnp.sin(f * (_dot(z_ref[...], w1_ref[...], precision=HIGHEST) + b1_ref[...]))
    o_ref[...] = jnp.sin(f * (_dot(h, w2_ref[...], precision=HIGHEST) + b2_ref[...]))


def _hy_hidden(zfull, w1p, b1, w2, b2, freq):
    n, e = zfull.shape
    fd = w2.shape[0]
    tr = min(n, 1024)
    full = lambda shp: pl.BlockSpec(shp, lambda i: (0, 0))
    return pl.pallas_call(
        _hy_hid_kernel,
        grid=(n // tr,),
        in_specs=[pl.BlockSpec((tr, e), lambda i: (i, 0)), full((e, fd)), full((1, fd)), full((fd, fd)),
                  full((1, fd)), full((1, fd))],
        out_specs=pl.BlockSpec((tr, fd), lambda i: (i, 0)),
        out_shape=jax.ShapeDtypeStruct((n, fd), F32),
        compiler_params=_params(("parallel",)),
        name="hyena_filter_mlp",
    )(zfull, w1p, b1, w2, b2, freq)


def _hy_filt_kernel(hid_ref, w3f_ref, w3b_ref, tn_ref, dl_ref, o_ref):
    half = hid_ref.shape[0] // 2
    decay = jnp.exp(-tn_ref[...] * jnp.abs(dl_ref[...]))
    top = _dot(hid_ref[0:half, :], w3f_ref[...], precision=HIGHEST)
    bot = _dot(hid_ref[half:, :], w3b_ref[...], precision=HIGHEST)
    row = lax.broadcasted_iota(jnp.int32, bot.shape, 0)
    bot = jnp.where(row == 0, 0.0, bot)
    o_ref[0, 0:half, :] = top * decay[0:half, :]
    o_ref[0, half:, :] = bot * decay[half:, :]


def _hy_filters(hid, w3, tn_full, deltas):
    n, fd = hid.shape
    wch = deltas.shape[1]
    nsl = wch // LANES
    return pl.pallas_call(
        _hy_filt_kernel,
        grid=(HY_ORDER, nsl),
        in_specs=[pl.BlockSpec((n, fd), lambda o, j: (0, 0)),
                  pl.BlockSpec((fd, LANES), lambda o, j: (0, o * 2 * nsl + j)),
                  pl.BlockSpec((fd, LANES), lambda o, j: (0, o * 2 * nsl + nsl + j)),
                  pl.BlockSpec((n, LANES), lambda o, j: (0, 0)),
                  pl.BlockSpec((1, LANES), lambda o, j: (0, j))],
        out_specs=pl.BlockSpec((1, n, LANES), lambda o, j: (o, 0, j)),
        out_shape=jax.ShapeDtypeStruct((HY_ORDER, n, wch), F32),
        compiler_params=_params(("parallel", "parallel")),
        name="hyena_filter",
    )(hid, w3, w3, tn_full, deltas)


def _hy_kfft_kernel(kern_ref, f1_ref, f2_ref, o_ref, g_ref, *, n1, n2):
    pg = 2 * n1 + HY_PITCH_PAD

    def stage1(i2, _):
        x = kern_ref[pl.ds(i2, n1, stride=n2), :].astype(BF16)
        g_ref[pl.ds(pl.multiple_of(i2 * pg, 8), 2 * n1), :] = _dot(f1_ref[i2], x)
        return 0
    lax.fori_loop(0, n2, stage1, 0, unroll=HY_UNROLL)

    def stage2(k1, _):
        x = jnp.concatenate([g_ref[pl.ds(k1, n2, stride=pg), :],
                             g_ref[pl.ds(n1 + k1, n2, stride=pg), :]], axis=0).astype(BF16)
        o_ref[pl.ds(pl.multiple_of(k1 * 2 * n2, 2 * n2), 2 * n2), :] = _dot(f2_ref[...], x)
        return 0
    lax.fori_loop(0, n1, stage2, 0, unroll=HY_UNROLL // 2)


def _hy_kfft(kern, f1k, f2, n1, n2):
    orders, n, wch = kern.shape
    nsl = wch // LANES
    return pl.pallas_call(
        functools.partial(_hy_kfft_kernel, n1=n1, n2=n2),
        grid=(orders, nsl),
        in_specs=[pl.BlockSpec((None, n, LANES), lambda o, j: (o, 0, j)),
                  pl.BlockSpec(f1k.shape, lambda o, j: (0, 0, 0)),
                  pl.BlockSpec(f2.shape, lambda o, j: (0, 0))],
        out_specs=pl.BlockSpec((None, None, 2 * n, LANES), lambda o, j: (o, j, 0, 0)),
        out_shape=jax.ShapeDtypeStruct((orders, nsl, 2 * n, LANES), F32),
        scratch_shapes=[pltpu.VMEM((n2 * (2 * n1 + HY_PITCH_PAD), LANES), F32)],
        compiler_params=_params(("parallel", "parallel")),
        name="hyena_filter_fft",
    )(kern, f1k, f2)


def _hy_prep_kernel(z_ref, w_ref, b_ref, o_ref, zp_ref, *, n2, nseq):
    l_len = z_ref.shape[0] // nseq
    h1 = l_len // n2
    halo = HY_HALO
    pitch = n2 + 2 * halo + 8
    zeros_h = jnp.zeros((halo, LANES), F32)
    for s in range(nseq):
        for i1 in range(h1):
            base = (s * h1 + i1) * pitch
            lo, hi = i1 * n2 - halo, (i1 + 1) * n2 + halo
            if lo < 0:
                zp_ref[base:base + halo, :] = zeros_h
            if hi > l_len:
                zp_ref[base + n2 + halo:base + n2 + 2 * halo, :] = zeros_h
            lo_c, hi_c = max(lo, 0), min(hi, l_len)
            zp_ref[base + (lo_c - lo):base + (hi_c - lo), :] = (
                z_ref[s * l_len + lo_c:s * l_len + hi_c, :].astype(F32))
    w0, w1, w2, b = w_ref[0:1, :], w_ref[1:2, :], w_ref[2:3, :], b_ref[...]

    def tap(j):
        return zp_ref[pl.ds(halo + j, nseq * h1, stride=pitch), :]

    def body(i2, carry):
        zm, z0 = carry
        zn = tap(i2 + 1)
        val = (b + w0 * zm + w1 * z0 + w2 * zn).astype(o_ref.dtype)
        for s in range(nseq):
            o_ref[pl.ds(pl.multiple_of(s * l_len + i2 * h1, h1), h1), :] = val[s * h1:(s + 1) * h1, :]
        return z0, zn
    lax.fori_loop(0, n2, body, (tap(-1), tap(0)), unroll=HY_UNROLL if nseq <= 2 else 1)


def _hy_stage_dtype(l_len, n2):
    return BF16 if (l_len // n2) % 16 == 0 else F32


def _hy_prep(proj, z_col, nz, bsz, conv_w, conv_b, n2):
    l_len = proj.shape[0] // bsz
    h1 = l_len // n2
    nseq = max(1, min(bsz, HY_PREP_ROWS // l_len))
    while bsz % nseq:
        nseq -= 1
    return pl.pallas_call(
        functools.partial(_hy_prep_kernel, n2=n2, nseq=nseq),
        grid=(bsz // nseq, nz),
        in_specs=[pl.BlockSpec((nseq * l_len, LANES), lambda b, j: (b, z_col + j)),
                  pl.BlockSpec((3, LANES), lambda b, j: (0, j)),
                  pl.BlockSpec((1, LANES), lambda b, j: (0, j))],
        out_specs=pl.BlockSpec((nseq * l_len, LANES), lambda b, j: (b, j)),
        out_shape=jax.ShapeDtypeStruct((proj.shape[0], nz * LANES), _hy_stage_dtype(l_len, n2)),
        scratch_shapes=[pltpu.VMEM((nseq * h1 * (n2 + 2 * HY_HALO + 8), LANES), F32)],
        compiler_params=_params(("parallel", "parallel")),
        name="hyena_short_conv",
    )(proj, conv_w, conv_b)


def _pack2(re, im):
    r = lax.bitcast_convert_type(re, jnp.uint32) + jnp.uint32(0x8000)
    i = lax.bitcast_convert_type(im, jnp.uint32) + jnp.uint32(0x8000)
    return (r & jnp.uint32(0xFFFF0000)) | (i >> 16)


def _unpack2(w):
    re = lax.bitcast_convert_type(w & jnp.uint32(0xFFFF0000), F32)
    im = lax.bitcast_convert_type(w << 16, F32)
    return jnp.concatenate([re, im], axis=0).astype(BF16)


def _hy_conv_kernel(y_ref, x_ref, kh_ref, bias_ref, f1_ref, f2_ref, f2i_ref, f1i_ref, o_ref, g_ref, *t_refs,
                    n1, n2, natural_out):
    h1 = n1 // 2
    l_len = h1 * n2
    pg = n1 + HY_PITCH_PAD
    bias = bias_ref[...]
    unroll = min(HY_UNROLL, n1)
    if natural_out:
        (t_ref,) = t_refs
        tp = n2 + HY_PITCH_PAD

    def rows(seq, i2):
        return pl.ds(pl.multiple_of(seq * l_len + i2 * h1, h1), h1)

    def lanes2(a, b):
        return jnp.concatenate([a, b], axis=1)

    def stage1(i2, _):
        x = lanes2(jnp.concatenate([y_ref[rows(0, i2), :], y_ref[rows(1, i2), :]], axis=0),
                   jnp.concatenate([y_ref[rows(2, i2), :], y_ref[rows(3, i2), :]], axis=0)).astype(BF16)
        a = _dot(f1_ref[i2], x)
        w = _pack2(a[0:n1, :], a[n1:, :])
        blk = pl.ds(pl.multiple_of(i2 * pg, 8), n1)
        g_ref[0, blk, :] = w[:, 0:LANES]
        g_ref[1, blk, :] = w[:, LANES:]
        return 0
    lax.fori_loop(0, n2, stage1, 0, unroll=2 * unroll)

    def stage23(t, _):
        outs = []
        for u in range(unroll):
            k1 = t * unroll + u
            col = pl.ds(k1, n2, stride=pg)
            x = lanes2(_unpack2(g_ref[0, col, :]), _unpack2(g_ref[1, col, :]))
            yh = _dot(f2_ref[...], x)
            base = pl.multiple_of(k1 * 2 * n2, 2 * n2)
            kr = kh_ref[pl.ds(base, n2), :]
            ki = kh_ref[pl.ds(base + n2, n2), :]
            kr, ki = lanes2(kr, kr), lanes2(ki, ki)
            yr, yi = yh[0:n2, :], yh[n2:, :]
            z = jnp.concatenate([yr * kr - yi * ki, yr * ki + yi * kr], axis=0).astype(BF16)
            c = _dot(f2i_ref[...], z)
            outs.append((col, _pack2(c[0:n2, :], c[n2:, :])))
        for col, w in outs:
            g_ref[0, col, :] = w[:, 0:LANES]
            g_ref[1, col, :] = w[:, LANES:]
        return 0
    lax.fori_loop(0, n1 // unroll, stage23, 0)

    def stage4(i2, _):
        blk = pl.ds(pl.multiple_of(i2 * pg, 8), n1)
        x = lanes2(_unpack2(g_ref[0, blk, :]), _unpack2(g_ref[1, blk, :]))
        yc = _dot(f1i_ref[i2], x)
        for seq in range(4):
            pair, half = seq // 2, seq % 2
            sl = rows(seq, i2)
            conv = yc[half * h1:(half + 1) * h1, pair * LANES:(pair + 1) * LANES]
            val = x_ref[sl, :].astype(F32) * (conv + bias * y_ref[sl, :].astype(F32))
            if natural_out:
                t_ref[pl.ds(seq * h1 * tp + i2, h1, stride=tp), :] = val
            else:
                o_ref[sl, :] = val.astype(o_ref.dtype)
        return 0
    lax.fori_loop(0, n2, stage4, 0, unroll=unroll)
    if natural_out:
        for blk in range(4 * h1):
            o_ref[blk * n2:(blk + 1) * n2, :] = t_ref[blk * tp:blk * tp + n2, :].astype(o_ref.dtype)


def _hy_conv(ysrc, ycol, xsrc, xcol, khat, order, bias, mats, bsz, n1, n2, natural_out):
    l_len = ysrc.shape[0] // bsz
    nsl = khat.shape[1]
    f1, f2, f2i, f1i = mats
    once = pl.Buffered(1)
    cst2 = lambda a: pl.BlockSpec(a.shape, lambda j, p: (0, 0), pipeline_mode=once)
    cst3 = lambda a: pl.BlockSpec(a.shape, lambda j, p: (0, 0, 0), pipeline_mode=once)
    return pl.pallas_call(
        functools.partial(_hy_conv_kernel, n1=n1, n2=n2, natural_out=natural_out),
        grid=(nsl, bsz // 4),
        in_specs=[pl.BlockSpec((4 * l_len, LANES), lambda j, p: (p, ycol + j)),
                  pl.BlockSpec((4 * l_len, LANES), lambda j, p: (p, xcol + j)),
                  pl.BlockSpec((None, None, khat.shape[2], LANES), lambda j, p: (order, j, 0, 0),
                               pipeline_mode=once),
                  pl.BlockSpec((1, LANES), lambda j, p: (0, j)),
                  cst3(f1), cst2(f2), cst2(f2i), cst3(f1i)],
        out_specs=pl.BlockSpec((4 * l_len, LANES), lambda j, p: (p, j)),
        out_shape=jax.ShapeDtypeStruct((ysrc.shape[0], nsl * LANES),
                                       BF16 if natural_out else _hy_stage_dtype(l_len, n2)),
        scratch_shapes=[pltpu.VMEM((2, n2 * (n1 + HY_PITCH_PAD), LANES), jnp.uint32)]
        + ([pltpu.VMEM((4 * (n1 // 2) * (n2 + HY_PITCH_PAD), LANES), F32)] if natural_out else []),
        compiler_params=_params(("parallel", "arbitrary")),
        name="hyena_long_conv",
    )(ysrc, xsrc, khat, bias[order][None], f1, f2, f2i, f1i)


def _dft_tables(l_len):
    n = 2 * l_len
    n2 = 128 if l_len >= 1024 else 32
    n1 = n // n2
    h1 = n1 // 2
    i1 = np.arange(n1)[None, None, :]
    k1 = np.arange(n1)[None, :, None]
    i2 = np.arange(n2)[:, None, None]
    ph = 2 * np.pi * (i1 * k1 / n1 + i2 * k1 / n)
    c, s = np.cos(ph), np.sin(ph)
    ch, sh = c[:, :, :h1], s[:, :, :h1]
    f1 = np.concatenate([np.concatenate([ch, sh], 2), np.concatenate([-sh, ch], 2)], 1)
    f1k = np.concatenate([c, -s], 1)
    ct, st = np.swapaxes(ch, 1, 2), np.swapaxes(sh, 1, 2)
    f1i = np.concatenate([np.concatenate([ct, -st], 2), np.concatenate([st, ct], 2)], 1) / n
    a = np.arange(n2)
    ph2 = 2 * np.pi * np.outer(a, a) / n2
    c2, s2 = np.cos(ph2), np.sin(ph2)
    f2 = np.block([[c2, s2], [-s2, c2]])
    f2i = np.block([[c2, -s2], [s2, c2]])
    bf = lambda m: jnp.asarray(m, dtype=F32).astype(BF16)
    return n1, n2, (bf(f1), bf(f2), bf(f2i), bf(f1i)), bf(f1k)


def _filter_positions(l_len, width):
    lag = np.concatenate([np.arange(l_len), l_len - np.arange(l_len)]).astype(np.float64)
    t = lag / (l_len - 1)
    bands = (HY_EMB - 1) // 2
    w = 2.0 * np.pi * lag / l_len
    f = np.linspace(1e-4, bands - 1, bands)[None]
    z = np.concatenate([t[:, None], np.cos(f * w[:, None]), -np.sin(f * w[:, None])], axis=-1)
    z = np.pad(z, ((0, 0), (0, HY_EMB_PAD - HY_EMB)))
    tn = np.repeat(t[:, None], LANES, axis=1)
    max_decay = math.log(HY_DECAY_TARGET) / HY_FAST_PCT
    min_decay = math.log(HY_DECAY_TARGET) / HY_SLOW_PCT
    deltas = np.linspace(min_decay, max_decay, width)[None]
    return jnp.asarray(z, F32), jnp.asarray(tn, F32), jnp.asarray(deltas, F32)


def _rope_tables(seq):
    n_rows = seq // GRID_W
    row = np.repeat(np.arange(n_rows), GRID_W).astype(np.float64)
    col = np.tile(np.arange(GRID_W), n_rows).astype(np.float64)
    half = HEAD_DIM // 2
    inv = ROPE_BASE ** (-np.arange(0, half, 2, dtype=np.float64) / half)
    ar, ac = row[:, None] * inv, col[:, None] * inv
    cos = np.concatenate([np.cos(ar), np.cos(ar), np.cos(ac), np.cos(ac)], axis=1)
    sin = np.concatenate([-np.sin(ar), np.sin(ar), -np.sin(ac), np.sin(ac)], axis=1)
    return jnp.asarray(cos, F32), jnp.asarray(sin, F32)


def _hyena_spectra(l_len, width, w1, b1, w2, b2, w3, freq, tables):
    n1, n2, mats, f1k = tables
    zfull, tn_full, deltas = _filter_positions(l_len, width)
    w1p = jnp.pad(w1, ((0, HY_EMB_PAD - HY_EMB), (0, 0)))
    hid = _hy_hidden(zfull, w1p, b1[None], w2, b2[None], freq[None])
    kern = _hy_filters(hid, w3, tn_full, deltas)
    return _hy_kfft(kern, f1k, mats[1], n1, n2)


def _hyena_seq(zc, bsz, khat, bias, tables):
    n1, n2, mats, _ = tables
    nsl = khat.shape[1]
    y1 = _hy_conv(zc, 0, zc, nsl, khat, 0, bias, mats, bsz, n1, n2, False)
    return _hy_conv(y1, 0, zc, 2 * nsl, khat, 1, bias, mats, bsz, n1, n2, True)


def kernel(x, c, ctx, c_ctx, mod_w, mod_b, norm_pre, norm_post, ev_w_in, ev_w_out, lru_conv_w, lru_conv_b, lru_wa, lru_ba, lru_wx, lru_bx, lru_lambda, attn_sink, od_w_in, od_w_out, hy_conv_w, hy_conv_b, hy_w1, hy_b1, hy_w2, hy_b2, hy_w3, hy_freq, hy_bias, ret_decay_logit):
    bsz, s_len, d = x.shape
    c_len = ctx.shape[1]
    depth = mod_w.shape[0]
    assert bsz % 4 == 0 and bsz <= 16 and s_len % 1024 == 0 and c_len % 256 == 0 and c_len <= s_len

    crows = jnp.concatenate([c, c_ctx[None], jnp.zeros((24 - bsz - 1, d), F32)], axis=0)
    mod = _modulation(crows, mod_w, mod_b)

    cos_t, sin_t = _rope_tables(s_len)
    tab_l = _dft_tables(s_len)
    tab_c = _dft_tables(c_len)

    x2 = x.reshape(bsz * s_len, d)
    ctx2 = ctx.reshape(bsz * c_len, d)
    tm = ROW_TILE

    for l in range(depth):
        need_ctx = l < depth - 1
        shift, scale, gate = (mod[l, :bsz, i * d:(i + 1) * d].reshape(bsz, 1, d) for i in range(3))
        shift_c, scale_c, gate_c = (mod[l, bsz:bsz + 1, i * d:(i + 1) * d].reshape(1, 1, d) for i in range(3))
        g_pre = norm_pre[l][None]
        g_post = norm_post[l][None]

        def proj(w, tn):
            wb = w.astype(BF16)
            p = _in_proj(x2, g_pre, scale, shift, s_len, wb, BF16, IN_ROW_TILE, tn)
            pc = _in_proj(ctx2, g_pre, scale_c, shift_c, bsz * c_len, wb, BF16, IN_ROW_TILE, tn)
            return p, pc

        if l % 2 == 0:
            e = l // 2
            w_in = ev_w_in[e]
            w_ord = jnp.concatenate([w_in[:, 1024:2048], w_in[:, 2048:3072], w_in[:, 3584:4608],
                                     w_in[:, 3072:3584], w_in[:, 0:1024]], axis=1)
            pr, prc = proj(w_ord, EVEN_COL_TILE)
            cols = {"q": 1, "gb": 2, "k": 12, "v": 13}
            xa_col = 3584 // LANES
            nblk = lru_wa.shape[2]
            wg = jnp.stack([lru_wa[e, 0], lru_wx[e, 0], lru_wa[e, 1], lru_wx[e, 1]], axis=1)
            wg = jnp.transpose(wg, (0, 2, 1, 3)).reshape(nblk, LRU_BLOCK_W, 4 * LRU_BLOCK_W)
            bg = jnp.stack([lru_ba[e, 0], lru_bx[e, 0], lru_ba[e, 1], lru_bx[e, 1]], axis=0)
            bg = jnp.transpose(bg.reshape(4, nblk, LRU_BLOCK_W), (1, 0, 2)).reshape(nblk, 1, 4 * LRU_BLOCK_W)
            wg, bg = (0.5 * wg).astype(BF16), 0.5 * bg
            ya, yac = _lru(pr, prc, xa_col, bsz, lru_conv_w[e], lru_conv_b[e][None], wg, bg, lru_lambda[e])
            yb = _attention(attn_sink[e], pr, prc, bsz, cols, cos_t, sin_t)
            w_out = ev_w_out[e].astype(BF16)
            x2_new = _out_proj(ya, pr, 0, yb, pr, cols["gb"], w_out, x2, g_post, gate, s_len, tm)
            if need_ctx:
                ybc = _ctx_attention(attn_sink[e], prc, bsz, cols)
                ctx2 = _out_proj(yac, prc, 0, ybc, prc, cols["gb"], w_out, ctx2, g_post, gate_c, bsz * c_len, tm)
            x2 = x2_new
        else:
            o = l // 2
            w_in = od_w_in[o]
            w_ord = jnp.concatenate([w_in[:, 3072:8192], w_in[:, 0:3072]], axis=1)
            pr, prc = proj(w_ord, ODD_COL_TILE)
            cols = {"q": 8, "k": 16, "v": 24}
            z_col = 5120 // LANES
            width = hy_bias.shape[2]
            nz = (HY_ORDER + 1) * width // LANES
            fargs = (hy_w1[o], hy_b1[o], hy_w2[o], hy_b2[o], hy_w3[o], hy_freq[o])
            khat = _hyena_spectra(s_len, width, *fargs, tab_l)
            zconv = _hy_prep(pr, z_col, nz, bsz, hy_conv_w[o], hy_conv_b[o][None], tab_l[1])
            yh = _hyena_seq(zconv, bsz, khat, hy_bias[o], tab_l)
            dl = jnp.broadcast_to(jnp.transpose(ret_decay_logit[o])[:, :, None], (RET_HEADS, 2, LANES))
            yd, ydc = _retention(dl, pr, prc, bsz, cols)
            w_out = od_w_out[o].astype(BF16)
            x2_new = _out_proj(yh, pr, 0, yd, pr, 4, w_out, x2, g_post, gate, s_len, tm)
            if need_ctx:
                khat_c = _hyena_spectra(c_len, width, *fargs, tab_c)
                zcconv = _hy_prep(prc, z_col, nz, bsz, hy_conv_w[o], hy_conv_b[o][None], tab_c[1])
                yhc = _hyena_seq(zcconv, bsz, khat_c, hy_bias[o], tab_c)
                ctx2 = _out_proj(yhc, prc, 0, ydc, prc, 4, w_out, ctx2, g_post, gate_c, bsz * c_len, tm)
            x2 = x2_new
    return x2.reshape(bsz, s_len, d)
```
